```python
import numpy as np
import jax
import jax.numpy as jnp
from jax import lax

D_MODEL = 1024
BATCH = 4
SEQ = 4096
DEPTH = 1

NSA_HEADS = 8
NSA_HEAD_DIM = 64
NSA_KV_GROUPS = 2
NSA_HEADS_PER_GROUP = NSA_HEADS // NSA_KV_GROUPS
NSA_KV_DIM = NSA_KV_GROUPS * NSA_HEAD_DIM
CMP_BLOCK = 32
CMP_STRIDE = 16
CMP_HIDDEN = 256
SLC_BLOCK = 64
SLC_TOPK = 16
WINDOW = 512
SLC_Q_BLOCK = 64
WIN_Q_BLOCK = 128
RET_HEADS = 4
RET_DK = 128
RET_DV = 256
RET_V_DIM = RET_HEADS * RET_DV
RET_CHUNK = 128
MOE_GROUPS = 4
EXPERTS_PER_GROUP = 8
N_EXPERTS = MOE_GROUPS * EXPERTS_PER_GROUP
D_FF_EXPERT = 512
INNER_TOP_K = 2
ROPE_THETA = 10000.0
LN_EPS = 1e-5
GN_EPS = 1e-5
NEG_INF = -1e30
DEEPNORM_ALPHA = (2.0 * DEPTH) ** 0.25
DEEPNORM_BETA = (8.0 * DEPTH) ** -0.25

IN_LAYOUT = (
    ("nsa_q", NSA_HEADS * NSA_HEAD_DIM),
    ("cmp_k", NSA_KV_DIM), ("cmp_v", NSA_KV_DIM),
    ("slc_k", NSA_KV_DIM), ("slc_v", NSA_KV_DIM),
    ("win_k", NSA_KV_DIM), ("win_v", NSA_KV_DIM),
    ("nsa_gate", 3 * NSA_HEADS),
    ("ret_q", RET_HEADS * RET_DK), ("ret_k", RET_HEADS * RET_DK),
    ("ret_v", RET_V_DIM), ("ret_gate", RET_V_DIM),
    ("merge_gate", 2 * D_MODEL),
)
IN_DIM = sum(w for _, w in IN_LAYOUT)
VALUE_SLOTS = ("cmp_v", "slc_v", "win_v", "ret_v")

kernel_name = "hybrid_nsa_retention_hmoe_deepnorm"


def split_projection(h):
    out = {}
    off = 0
    for name, width in IN_LAYOUT:
        out[name] = h[..., off:off + width]
        off += width
    return out


def layer_norm(x, g, b):
    x32 = x.astype(jnp.float32)
    mu = jnp.mean(x32, -1, keepdims=True)
    var = jnp.mean(jnp.square(x32 - mu), -1, keepdims=True)
    return ((x32 - mu) * lax.rsqrt(var + LN_EPS) * g + b).astype(x.dtype)


def rope(x, pos):
    d = x.shape[-1]
    half = d // 2
    inv_freq = ROPE_THETA ** (-jnp.arange(half, dtype=jnp.float32) * 2.0 / d)
    ang = pos.astype(jnp.float32)[:, None] * inv_freq[None, :]
    cos = jnp.cos(ang)[:, None, :]
    sin = jnp.sin(ang)[:, None, :]
    x32 = x.astype(jnp.float32)
    x1, x2 = x32[..., :half], x32[..., half:]
    return jnp.concatenate([x1 * cos - x2 * sin, x1 * sin + x2 * cos], -1).astype(x.dtype)


def masked_softmax(s, mask):
    s = jnp.where(mask, s.astype(jnp.float32), NEG_INF)
    return jnp.where(mask, jax.nn.softmax(s, axis=-1), 0.0)


def compress_blocks(t, pos_emb, w1, b1, w2, b2):
    B, T, G, hd = t.shape
    n_cmp = (T - CMP_BLOCK) // CMP_STRIDE + 1
    tok = np.arange(n_cmp)[:, None] * CMP_STRIDE + np.arange(CMP_BLOCK)[None, :]
    blk = t[:, tok] + pos_emb[None, None, :, None, :]
    blk = blk.transpose(0, 1, 3, 2, 4).reshape(B, n_cmp, G, CMP_BLOCK * hd)
    return jax.nn.gelu(blk @ w1 + b1) @ w2 + b2


def nsa_attention(q, k_c, v_c, k_s, v_s, k_w, v_w, gate_logits, cmp_k_params, cmp_v_params):
    B, T = q.shape[:2]
    G, HPG, hd = NSA_KV_GROUPS, NSA_HEADS_PER_GROUP, NSA_HEAD_DIM
    pos = jnp.arange(T)
    qg = (rope(q, pos) * hd ** -0.5).reshape(B, T, G, HPG, hd).transpose(0, 2, 3, 1, 4)
    k_s = rope(k_s, pos)
    k_w = rope(k_w, pos)

    n_cmp = (T - CMP_BLOCK) // CMP_STRIDE + 1
    cmp_start = np.arange(n_cmp) * CMP_STRIDE
    cmp_end = cmp_start + CMP_BLOCK - 1
    k_cmp = rope(compress_blocks(k_c, *cmp_k_params), jnp.asarray(cmp_end))
    v_cmp = compress_blocks(v_c, *cmp_v_params)
    s_cmp = jnp.einsum('bghtd,bngd->bghtn', qg, k_cmp)
    mask_cmp = jnp.asarray(cmp_end)[None, :] <= pos[:, None]
    p_cmp = masked_softmax(s_cmp, mask_cmp)
    o_cmp = jnp.einsum('bghtn,bngd->bghtd', p_cmp.astype(v_cmp.dtype), v_cmp)

    n_sel = T // SLC_BLOCK
    k_top = min(SLC_TOPK, n_sel)
    sel_start = np.arange(n_sel) * SLC_BLOCK
    overlap = np.clip(np.minimum(cmp_start[None, :] + CMP_BLOCK, sel_start[:, None] + SLC_BLOCK)
                      - np.maximum(cmp_start[None, :], sel_start[:, None]), 0, None)
    overlap = jnp.asarray(overlap.astype(np.float32) / CMP_STRIDE)
    importance = jnp.einsum('bghtn,sn->bgts', p_cmp, overlap)
    blk_t = pos // SLC_BLOCK
    j = jnp.arange(n_sel)
    forced = (j[None, :] == 0) | (j[None, :] == blk_t[:, None]) | (j[None, :] == blk_t[:, None] - 1)
    causal = j[None, :] <= blk_t[:, None]
    score = jnp.where(forced, jnp.float32(1e9), jnp.where(causal, importance, jnp.float32(-1e9)))
    _, sel_idx = lax.top_k(score, k_top)

    k_blocks = k_s.reshape(B, n_sel, SLC_BLOCK, G, hd).transpose(0, 3, 1, 2, 4)
    v_blocks = v_s.reshape(B, n_sel, SLC_BLOCK, G, hd).transpose(0, 3, 1, 2, 4)
    nq = T // SLC_Q_BLOCK
    q_b = qg.reshape(B, G, HPG, nq, SLC_Q_BLOCK, hd).transpose(3, 0, 1, 2, 4, 5)
    idx_b = sel_idx.reshape(B, G, nq, SLC_Q_BLOCK, k_top).transpose(2, 0, 1, 3, 4)
    t_b = pos.reshape(nq, SLC_Q_BLOCK)
    bi = jnp.arange(B)[:, None, None, None]
    gi = jnp.arange(G)[None, :, None, None]

    def slc_block(args):
        qb, ib, tb = args
        kg = k_blocks[bi, gi, ib]
        vg = v_blocks[bi, gi, ib]
        s = jnp.einsum('bghqd,bgqskd->bghqsk', qb, kg)
        key_pos = ib[..., None] * SLC_BLOCK + jnp.arange(SLC_BLOCK)
        mask = (key_pos <= tb[:, None, None])[:, :, None]
        p = masked_softmax(s.reshape(B, G, HPG, SLC_Q_BLOCK, -1),
                           mask.reshape(B, G, 1, SLC_Q_BLOCK, -1)).reshape(s.shape)
        return jnp.einsum('bghqsk,bgqskd->bghqd', p.astype(vg.dtype), vg)

    o_slc = lax.map(slc_block, (q_b, idx_b, t_b))
    o_slc = o_slc.transpose(1, 2, 3, 0, 4, 5).reshape(B, G, HPG, T, hd)

    k_wp = jnp.pad(k_w, ((0, 0), (WINDOW, 0), (0, 0), (0, 0)))
    v_wp = jnp.pad(v_w, ((0, 0), (WINDOW, 0), (0, 0), (0, 0)))
    nqw = T // WIN_Q_BLOCK

    def win_block(n):
        start = n * WIN_Q_BLOCK
        qb = lax.dynamic_slice_in_dim(qg, start, WIN_Q_BLOCK, axis=3)
        kb = lax.dynamic_slice_in_dim(k_wp, start, WINDOW + WIN_Q_BLOCK, axis=1)
        vb = lax.dynamic_slice_in_dim(v_wp, start, WINDOW + WIN_Q_BLOCK, axis=1)
        s = jnp.einsum('bghqd,bkgd->bghqk', qb, kb)
        tq = start + jnp.arange(WIN_Q_BLOCK)
        tk = start - WINDOW + jnp.arange(WINDOW + WIN_Q_BLOCK)
        mask = (tk[None, :] <= tq[:, None]) & (tq[:, None] - tk[None, :] < WINDOW) & (tk[None, :] >= 0)
        p = masked_softmax(s, mask)
        return jnp.einsum('bghqk,bkgd->bghqd', p.astype(vb.dtype), vb)

    o_win = lax.map(win_block, jnp.arange(nqw))
    o_win = o_win.transpose(1, 2, 3, 0, 4, 5).reshape(B, G, HPG, T, hd)

    gates = jax.nn.sigmoid(gate_logits.astype(jnp.float32)).reshape(B, T, G, HPG, 3).transpose(0, 2, 3, 1, 4)
    o = gates[..., 0:1] * o_cmp + gates[..., 1:2] * o_slc + gates[..., 2:3] * o_win
    return o.transpose(0, 3, 1, 2, 4).reshape(B, T, NSA_HEADS * hd).astype(q.dtype)


def retention(q, k, v, gn_g, gn_b):
    B, T = q.shape[:2]
    C = RET_CHUNK
    N = T // C
    pos = jnp.arange(T)
    q = rope(q, pos).astype(jnp.float32)
    k = (rope(k, pos).astype(jnp.float32)) * RET_DK ** -0.5
    v = v.astype(jnp.float32)
    qc = q.reshape(B, N, C, RET_HEADS, RET_DK).transpose(1, 0, 3, 2, 4)
    kc = k.reshape(B, N, C, RET_HEADS, RET_DK).transpose(1, 0, 3, 2, 4)
    vc = v.reshape(B, N, C, RET_HEADS, RET_DV).transpose(1, 0, 3, 2, 4)
    gamma = 1.0 - 2.0 ** (-5.0 - jnp.arange(RET_HEADS, dtype=jnp.float32))
    log_g = jnp.log(gamma)
    i = jnp.arange(C, dtype=jnp.float32)
    diff = i[:, None] - i[None, :]
    decay_intra = jnp.where(diff >= 0, jnp.exp(jnp.maximum(diff, 0.0) * log_g[:, None, None]), 0.0)
    xi = jnp.exp((i + 1.0) * log_g[:, None])
    zeta = jnp.exp((C - 1.0 - i) * log_g[:, None])
    chunk_decay = jnp.exp(C * log_g)

    scores = jnp.einsum('nbhcd,nbhmd->nbhcm', qc, kc) * decay_intra
    intra = jnp.einsum('nbhcm,nbhme->nbhce', scores, vc)

    def step(R, xs):
        qn, kn, vn = xs
        cross = jnp.einsum('bhcd,bhde->bhce', qn, R) * xi[None, :, :, None]
        R_new = R * chunk_decay[None, :, None, None] + jnp.einsum('bhmd,bhme->bhde', kn * zeta[None, :, :, None], vn)
        return R_new, cross

    R0 = jnp.zeros((B, RET_HEADS, RET_DK, RET_DV), jnp.float32)
    _, cross = lax.scan(step, R0, (qc, kc, vc))
    o = (intra + cross).transpose(1, 0, 3, 2, 4).reshape(B, T, RET_HEADS, RET_DV)
    mu = jnp.mean(o, -1, keepdims=True)
    var = jnp.mean(jnp.square(o - mu), -1, keepdims=True)
    o = (o - mu) * lax.rsqrt(var + GN_EPS)
    o = o * gn_g.astype(jnp.float32).reshape(RET_HEADS, RET_DV) + gn_b.astype(jnp.float32).reshape(RET_HEADS, RET_DV)
    return o.reshape(B, T, RET_V_DIM)


def hier_moe(xt, rg_w, rg_b, ri_w, ri_b, w_gate, w_up, w_down):
    group_logits = (xt @ rg_w).astype(jnp.float32) + rg_b.astype(jnp.float32)
    p_group = jax.nn.softmax(group_logits, axis=-1)
    g_prob, g_idx = lax.top_k(p_group, 1)
    g_prob, g_idx = g_prob[:, 0], g_idx[:, 0]
    inner_all = jnp.einsum('md,gde->mge', xt, ri_w).astype(jnp.float32) + ri_b.astype(jnp.float32)
    inner = jnp.take_along_axis(inner_all, g_idx[:, None, None], axis=1)[:, 0]
    e_logit, e_idx = lax.top_k(inner, INNER_TOP_K)
    w = jax.nn.softmax(e_logit, axis=-1) * g_prob[:, None]
    expert_id = g_idx[:, None] * EXPERTS_PER_GROUP + e_idx
    combine = jnp.sum(jax.nn.one_hot(expert_id, N_EXPERTS, dtype=jnp.float32) * w[..., None], axis=1)
    out = jnp.zeros(xt.shape, jnp.float32)
    for g in range(MOE_GROUPS):
        sl = slice(g * EXPERTS_PER_GROUP, (g + 1) * EXPERTS_PER_GROUP)
        hg = jax.nn.silu(jnp.einsum('md,edf->mef', xt, w_gate[sl])) * jnp.einsum('md,edf->mef', xt, w_up[sl])
        hg = hg * combine[:, sl, None].astype(hg.dtype)
        out = out + jnp.einsum('mef,efd->md', hg, w_down[sl])
    return out


def setup_inputs(seed: int = 0) -> dict:
    key = jax.random.key(seed)
    ks = jax.random.split(key, 32)
    f32 = jnp.float32
    L, D, hd = DEPTH, D_MODEL, NSA_HEAD_DIM

    def nrm(k, shape, scale):
        return jax.random.normal(k, shape, f32) * scale

    col_scale = np.concatenate([np.full((w,), DEEPNORM_BETA if name in VALUE_SLOTS else 1.0, np.float32)
                                for name, w in IN_LAYOUT])
    cmp_in = CMP_BLOCK * hd
    return {
        "x": nrm(ks[0], (BATCH, SEQ, D), 1.0),
        "w_in": nrm(ks[1], (L, D, IN_DIM), D ** -0.5) * jnp.asarray(col_scale),
        "cmp_pos_k": nrm(ks[2], (L, CMP_BLOCK, hd), 0.1),
        "cmp_k_w1": nrm(ks[3], (L, cmp_in, CMP_HIDDEN), cmp_in ** -0.5),
        "cmp_k_b1": nrm(ks[4], (L, CMP_HIDDEN), 0.01),
        "cmp_k_w2": nrm(ks[5], (L, CMP_HIDDEN, hd), CMP_HIDDEN ** -0.5),
        "cmp_k_b2": nrm(ks[6], (L, hd), 0.01),
        "cmp_pos_v": nrm(ks[7], (L, CMP_BLOCK, hd), 0.1),
        "cmp_v_w1": nrm(ks[8], (L, cmp_in, CMP_HIDDEN), cmp_in ** -0.5),
        "cmp_v_b1": nrm(ks[9], (L, CMP_HIDDEN), 0.01),
        "cmp_v_w2": nrm(ks[10], (L, CMP_HIDDEN, hd), CMP_HIDDEN ** -0.5),
        "cmp_v_b2": nrm(ks[11], (L, hd), 0.01),
        "ret_gn_g": 1.0 + nrm(ks[12], (L, RET_V_DIM), 0.05),
        "ret_gn_b": nrm(ks[13], (L, RET_V_DIM), 0.01),
        "w_up_attn": nrm(ks[14], (L, NSA_HEADS * hd, D), (NSA_HEADS * hd) ** -0.5 * DEEPNORM_BETA),
        "w_up_ret": nrm(ks[15], (L, RET_V_DIM, D), RET_V_DIM ** -0.5 * DEEPNORM_BETA),
        "w_out": nrm(ks[16], (L, D, D), D ** -0.5 * DEEPNORM_BETA),
        "ln1_g": 1.0 + nrm(ks[17], (L, D), 0.05),
        "ln1_b": nrm(ks[18], (L, D), 0.01),
        "router_group_w": nrm(ks[19], (L, D, MOE_GROUPS), D ** -0.5),
        "router_group_b": nrm(ks[20], (L, MOE_GROUPS), 0.01),
        "router_inner_w": nrm(ks[21], (L, MOE_GROUPS, D, EXPERTS_PER_GROUP), D ** -0.5),
        "router_inner_b": nrm(ks[22], (L, MOE_GROUPS, EXPERTS_PER_GROUP), 0.01),
        "expert_w_gate": nrm(ks[23], (L, N_EXPERTS, D, D_FF_EXPERT), D ** -0.5),
        "expert_w_up": nrm(ks[24], (L, N_EXPERTS, D, D_FF_EXPERT), D ** -0.5 * DEEPNORM_BETA),
        "expert_w_down": nrm(ks[25], (L, N_EXPERTS, D_FF_EXPERT, D), D_FF_EXPERT ** -0.5 * DEEPNORM_BETA),
        "ln2_g": 1.0 + nrm(ks[26], (L, D), 0.05),
        "ln2_b": nrm(ks[27], (L, D), 0.01),
    }


def reference(x, w_in, cmp_pos_k, cmp_k_w1, cmp_k_b1, cmp_k_w2, cmp_k_b2,
              cmp_pos_v, cmp_v_w1, cmp_v_b1, cmp_v_w2, cmp_v_b2,
              ret_gn_g, ret_gn_b, w_up_attn, w_up_ret, w_out, ln1_g, ln1_b,
              router_group_w, router_group_b, router_inner_w, router_inner_b,
              expert_w_gate, expert_w_up, expert_w_down, ln2_g, ln2_b):
    B, T, D = x.shape
    G, hd = NSA_KV_GROUPS, NSA_HEAD_DIM
    for l in range(DEPTH):
        p = split_projection(x @ w_in[l])
        o_attn = nsa_attention(
            p["nsa_q"].reshape(B, T, NSA_HEADS, hd),
            p["cmp_k"].reshape(B, T, G, hd), p["cmp_v"].reshape(B, T, G, hd),
            p["slc_k"].reshape(B, T, G, hd), p["slc_v"].reshape(B, T, G, hd),
            p["win_k"].reshape(B, T, G, hd), p["win_v"].reshape(B, T, G, hd),
            p["nsa_gate"],
            (cmp_pos_k[l], cmp_k_w1[l], cmp_k_b1[l], cmp_k_w2[l], cmp_k_b2[l]),
            (cmp_pos_v[l], cmp_v_w1[l], cmp_v_b1[l], cmp_v_w2[l], cmp_v_b2[l]))
        o_ret = retention(p["ret_q"].reshape(B, T, RET_HEADS, RET_DK),
                          p["ret_k"].reshape(B, T, RET_HEADS, RET_DK),
                          p["ret_v"].reshape(B, T, RET_HEADS, RET_DV),
                          ret_gn_g[l], ret_gn_b[l])
        o_ret = (o_ret * jax.nn.silu(p["ret_gate"].astype(jnp.float32))).astype(x.dtype)
        gate = jax.nn.sigmoid(p["merge_gate"].astype(jnp.float32)).reshape(B, T, 2, D)
        merged = gate[:, :, 0] * (o_attn @ w_up_attn[l]) + gate[:, :, 1] * (o_ret @ w_up_ret[l])
        mix = merged.astype(x.dtype) @ w_out[l]
        x = layer_norm(DEEPNORM_ALPHA * x + mix.astype(x.dtype), ln1_g[l], ln1_b[l])
        moe = hier_moe(x.reshape(B * T, D), router_group_w[l], router_group_b[l],
                       router_inner_w[l], router_inner_b[l],
                       expert_w_gate[l], expert_w_up[l], expert_w_down[l]).reshape(B, T, D)
        x = layer_norm(DEEPNORM_ALPHA * x + moe.astype(x.dtype), ln2_g[l], ln2_b[l])
    return x
```

```python
import functools

import numpy as np
import jax
import jax.numpy as jnp
from jax import lax
from jax.experimental import pallas as pl
from jax.experimental.pallas import tpu as pltpu

F32 = jnp.float32
BF16 = jnp.bfloat16
I32 = jnp.int32

D_MODEL = 1024
NSA_HEADS = 8
NSA_HD = 64
NSA_GROUPS = 2
NSA_HPG = NSA_HEADS // NSA_GROUPS
CMP_BLOCK = 32
CMP_STRIDE = 16
CMP_HIDDEN = 256
SLC_BLOCK = 64
SLC_TOPK = 16
WINDOW = 512
RET_HEADS = 4
RET_DK = 128
RET_DV = 256
RET_CHUNK = 128
MOE_GROUPS = 4
EXPERTS_PER_GROUP = 8
N_EXPERTS = MOE_GROUPS * EXPERTS_PER_GROUP
D_FF = 512
ROPE_THETA = 10000.0
LN_EPS = 1e-5
GN_EPS = 1e-5
NEG_INF = -1e30

VMEM_LIMIT_V7X = 56 * 1024 * 1024

_OFF = {}
_o = 0
for _n, _w in (("nsa_q", 512), ("cmp_k", 128), ("cmp_v", 128), ("slc_k", 128), ("slc_v", 128),
               ("win_k", 128), ("win_v", 128), ("nsa_gate", 24), ("ret_q", 512), ("ret_k", 512),
               ("ret_v", 1024), ("ret_gate", 1024), ("merge_gate", 2048)):
    _OFF[_n] = (_o, _o + _w)
    _o += _w

PROJ_TM = 512
NSA_TQ = 256
NSA_KC = 256
MERGE_TM = 512
RANK_TB = 512
DISPATCH_TM = 512
EXPERT_TM = 256
COMBINE_TM = 256


def _cparams(sem, **kw):
    return pltpu.CompilerParams(dimension_semantics=sem, vmem_limit_bytes=VMEM_LIMIT_V7X, **kw)


def _nt(a, b):
    return lax.dot_general(a, b, (((1,), (1,)), ((), ())), preferred_element_type=F32)


def _nn(a, b):
    return jnp.dot(a, b, preferred_element_type=F32)


def _proj_body(x_ref, wn_ref, wt_ref, c64_ref, s64_ref, c128_ref, s128_ref,
               ct64_ref, st64_ref, ct128_ref, st128_ref,
               qT_ref, ckv_ref, sk_ref, wk_ref, svT_ref, wvT_ref, gT_ref, rq_ref, rkT_ref, rv_ref):
    tm = PROJ_TM
    xb = x_ref[0].astype(BF16)

    def nn(a, b):
        return _nn(xb, wn_ref[:, a:b])

    def nt(a, b):
        return _nt(wt_ref[a:b, :], xb)

    ckv = nn(0, 256)
    for j in range(4):
        ckv_ref[0, j] = ckv[:, 64 * j:64 * (j + 1)]

    lane = lax.broadcasted_iota(I32, (tm, 128), 1)
    first = (lane % 64) < 32
    c64 = c64_ref[...]
    s64 = s64_ref[...]

    def rope64(k):
        rot = jnp.where(first, pltpu.roll(k, 96, 1), pltpu.roll(k, 32, 1))
        return k * c64 + rot * s64

    sk = rope64(nn(256, 384)).astype(BF16)
    sk_ref[0, 0] = sk[:, :64]
    sk_ref[0, 1] = sk[:, 64:]
    wk = rope64(nn(384, 512)).astype(BF16)
    wk_ref[0, 0] = wk[:, :64]
    wk_ref[0, 1] = wk[:, 64:]

    c128 = c128_ref[...]
    s128 = s128_ref[...]
    rq = nn(512, 1024)
    for h in range(RET_HEADS):
        ch = rq[:, 128 * h:128 * (h + 1)]
        rq_ref[0, :, 128 * h:128 * (h + 1)] = (ch * c128 + pltpu.roll(ch, 64, 1) * s128).astype(BF16)
    rv_ref[0] = nn(1024, 2048).astype(BF16)

    ct = ct64_ref[...]
    st = st64_ref[...]
    qT = nt(0, 512)
    scale_q = NSA_HD ** -0.5
    for h in range(NSA_HEADS):
        x1 = qT[64 * h:64 * h + 32]
        x2 = qT[64 * h + 32:64 * h + 64]
        qT_ref[0, 64 * h:64 * h + 32, :] = ((x1 * ct - x2 * st) * scale_q).astype(BF16)
        qT_ref[0, 64 * h + 32:64 * h + 64, :] = ((x1 * st + x2 * ct) * scale_q).astype(BF16)

    svT = nt(512, 640).astype(BF16)
    wvT = nt(640, 768).astype(BF16)
    for c in range(tm // NSA_KC):
        svT_ref[0, c] = svT[:, NSA_KC * c:NSA_KC * (c + 1)]
        wvT_ref[0, c] = wvT[:, NSA_KC * c:NSA_KC * (c + 1)]

    ct2 = ct128_ref[...]
    st2 = st128_ref[...]
    rkT = nt(768, 1280)
    scale_k = RET_DK ** -0.5
    for h in range(RET_HEADS):
        x1 = rkT[128 * h:128 * h + 64]
        x2 = rkT[128 * h + 64:128 * h + 128]
        o1 = (x1 * ct2 - x2 * st2) * scale_k
        o2 = (x1 * st2 + x2 * ct2) * scale_k
        for c in range(tm // RET_CHUNK):
            rkT_ref[0, c, 128 * h:128 * h + 64, :] = o1[:, 128 * c:128 * (c + 1)]
            rkT_ref[0, c, 128 * h + 64:128 * h + 128, :] = o2[:, 128 * c:128 * (c + 1)]

    gT_ref[0] = jax.nn.sigmoid(nt(1280, 1312))


def _proj(x, wn, wt, tabs):
    B, T, D = x.shape
    tm = PROJ_TM
    c64, s64, c128, s128, ct64, st64, ct128, st128 = tabs
    const = lambda b, i: (0, 0)
    in_specs = [
        pl.BlockSpec((1, tm, D), lambda b, i: (b, i, 0)),
        pl.BlockSpec(wn.shape, const),
        pl.BlockSpec(wt.shape, const),
        pl.BlockSpec((tm, 128), lambda b, i: (i, 0)),
        pl.BlockSpec((tm, 128), lambda b, i: (i, 0)),
        pl.BlockSpec((tm, 128), lambda b, i: (i, 0)),
        pl.BlockSpec((tm, 128), lambda b, i: (i, 0)),
        pl.BlockSpec((32, tm), lambda b, i: (0, i)),
        pl.BlockSpec((32, tm), lambda b, i: (0, i)),
        pl.BlockSpec((64, tm), lambda b, i: (0, i)),
        pl.BlockSpec((64, tm), lambda b, i: (0, i)),
    ]
    out_shape = [
        jax.ShapeDtypeStruct((B, 512, T), BF16),
        jax.ShapeDtypeStruct((B, 4, T, 64), F32),
        jax.ShapeDtypeStruct((B, 2, T, 64), BF16),
        jax.ShapeDtypeStruct((B, 2, T, 64), BF16),
        jax.ShapeDtypeStruct((B, T // NSA_KC, 128, NSA_KC), BF16),
        jax.ShapeDtypeStruct((B, T // NSA_KC, 128, NSA_KC), BF16),
        jax.ShapeDtypeStruct((B, 32, T), F32),
        jax.ShapeDtypeStruct((B, T, 512), BF16),
        jax.ShapeDtypeStruct((B, T // RET_CHUNK, 512, RET_CHUNK), F32),
        jax.ShapeDtypeStruct((B, T, 1024), BF16),
    ]
    out_specs = [
        pl.BlockSpec((1, 512, tm), lambda b, i: (b, 0, i)),
        pl.BlockSpec((1, 4, tm, 64), lambda b, i: (b, 0, i, 0)),
        pl.BlockSpec((1, 2, tm, 64), lambda b, i: (b, 0, i, 0)),
        pl.BlockSpec((1, 2, tm, 64), lambda b, i: (b, 0, i, 0)),
        pl.BlockSpec((1, tm // NSA_KC, 128, NSA_KC), lambda b, i: (b, i, 0, 0)),
        pl.BlockSpec((1, tm // NSA_KC, 128, NSA_KC), lambda b, i: (b, i, 0, 0)),
        pl.BlockSpec((1, 32, tm), lambda b, i: (b, 0, i)),
        pl.BlockSpec((1, tm, 512), lambda b, i: (b, i, 0)),
        pl.BlockSpec((1, tm // RET_CHUNK, 512, RET_CHUNK), lambda b, i: (b, i, 0, 0)),
        pl.BlockSpec((1, tm, 1024), lambda b, i: (b, i, 0)),
    ]
    return pl.pallas_call(
        _proj_body, out_shape=out_shape, grid=(B, T // tm), in_specs=in_specs, out_specs=out_specs,
        compiler_params=_cparams(("parallel", "parallel")), name="proj",
    )(x, wn, wt, c64, s64, c128, s128, ct64, st64, ct128, st128)


def _compress_body(uk_ref, uv_ref, posk_ref, posv_ref, w1k_ref, b1k_ref, w2k_ref, w2kr_ref, b2k_ref, b2kr_ref,
                   cc_ref, sc_ref, w1v_ref, b1v_ref, w2vT_ref, b2vT_ref, kc_ref, vcT_ref):
    half = CMP_STRIDE * NSA_HD

    def hidden(u_ref, pos_ref, w1_ref, b1_ref):
        u = u_ref[0, 0]
        top = (u + pos_ref[0:1, :]).astype(BF16)
        bot = (u + pos_ref[1:2, :]).astype(BF16)
        a = _nn(top, w1_ref[0:half, :].astype(BF16))
        bm = _nn(bot, w1_ref[half:2 * half, :].astype(BF16))
        n = bm.shape[0]
        return jax.nn.gelu(a + pltpu.roll(bm, n - 1, 0) + b1_ref[...])

    hk = hidden(uk_ref, posk_ref, w1k_ref, b1k_ref).astype(BF16)
    k = _nn(hk, w2k_ref[...].astype(BF16)) + b2k_ref[...]
    kr = _nn(hk, w2kr_ref[...].astype(BF16)) + b2kr_ref[...]
    kc_ref[0, 0] = (k * cc_ref[...] + kr * sc_ref[...]).astype(BF16)

    hv = hidden(uv_ref, posv_ref, w1v_ref, b1v_ref).astype(BF16)
    vcT_ref[0, 0] = (_nt(w2vT_ref[...].astype(BF16), hv) + b2vT_ref[...]).astype(BF16)


def _compress(ckv, posk, posv, w1k, b1k, w2k, w2kr, b2k, b2kr, cc, sc, w1v, b1v, w2vT, b2vT):
    B = ckv.shape[0]
    T = ckv.shape[2]
    nr = T // CMP_STRIDE
    u = ckv.reshape(B, 4, nr, CMP_STRIDE * NSA_HD)
    full = lambda a: pl.BlockSpec(a.shape, lambda b, g: (0,) * a.ndim)
    in_specs = [
        pl.BlockSpec((1, 1, nr, 1024), lambda b, g: (b, g, 0, 0)),
        pl.BlockSpec((1, 1, nr, 1024), lambda b, g: (b, g + 2, 0, 0)),
    ] + [full(a) for a in (posk, posv, w1k, b1k, w2k, w2kr, b2k, b2kr, cc, sc, w1v, b1v, w2vT, b2vT)]
    out_shape = [jax.ShapeDtypeStruct((B, NSA_GROUPS, nr, NSA_HD), BF16),
                 jax.ShapeDtypeStruct((B, NSA_GROUPS, NSA_HD, nr), BF16)]
    out_specs = [pl.BlockSpec((1, 1, nr, NSA_HD), lambda b, g: (b, g, 0, 0)),
                 pl.BlockSpec((1, 1, NSA_HD, nr), lambda b, g: (b, g, 0, 0))]
    return pl.pallas_call(
        _compress_body, out_shape=out_shape, grid=(B, NSA_GROUPS), in_specs=in_specs, out_specs=out_specs,
        compiler_params=_cparams(("parallel", "parallel")), name="compress",
    )(u, u, posk, posv, w1k, b1k, w2k, w2kr, b2k, b2kr, cc, sc, w1v, b1v, w2vT, b2vT)


def _nsa_body(qT_ref, kc_ref, vcT_ref, ov_ref, sk_ref, svT_ref, wk_ref, wvT_ref, gT_ref, o_ref,
              score_ref, bias_ref, m_ref, l_ref, acc_ref, tot_ref):
    tq = NSA_TQ
    kc_n = NSA_KC
    g = pl.program_id(1)
    i = pl.program_id(2)
    t0 = i * tq
    t_row = t0 + lax.broadcasted_iota(I32, (1, tq), 1)
    n_cmp = kc_ref.shape[2]
    n_sel = ov_ref.shape[0]

    def gate(h, br):
        return gT_ref[0, pl.ds((g * NSA_HPG + h) * 3 + br, 1), :]

    cmp_end = lax.broadcasted_iota(I32, (n_cmp, 1), 0) * CMP_STRIDE + (CMP_BLOCK - 1)
    mask_c = cmp_end <= t_row
    kc = kc_ref[0, 0]
    vcT = vcT_ref[0, 0]
    psum = jnp.zeros((n_cmp, tq), F32)
    for h in range(NSA_HPG):
        qh = qT_ref[0, 64 * h:64 * (h + 1), :]
        s = jnp.where(mask_c, _nn(kc, qh), NEG_INF)
        m = jnp.max(s, axis=0, keepdims=True)
        e = jnp.exp(s - m)
        p = jnp.where(mask_c, e / jnp.sum(e, axis=0, keepdims=True), 0.0)
        psum = psum + p
        tot_ref[64 * h:64 * (h + 1), :] = gate(h, 0) * _nn(vcT, p.astype(BF16))

    p_hi = psum.astype(BF16)
    p_lo = (psum - p_hi.astype(F32)).astype(BF16)
    ov = ov_ref[...]
    imp = _nn(ov, p_hi) + _nn(ov, p_lo)
    j_col = lax.broadcasted_iota(I32, (n_sel, 1), 0)
    blk_t = t_row // SLC_BLOCK
    forced = (j_col == 0) | (j_col == blk_t) | (j_col == blk_t - 1)
    score = jnp.where(forced, 1e9, jnp.where(j_col <= blk_t, imp, -1e9)).astype(F32)
    score_ref[...] = score
    sub = lax.broadcasted_iota(I32, (8, 1), 0)
    for v in range(n_sel // 8):
        sc_v = score[8 * v:8 * (v + 1)]
        cnt = jnp.zeros((8, tq), F32)
        for r in range(n_sel):
            row = score_ref[r:r + 1, :]
            if r < 8 * v:
                cnt = cnt + jnp.where(row >= sc_v, 1.0, 0.0)
            elif r >= 8 * (v + 1):
                cnt = cnt + jnp.where(row > sc_v, 1.0, 0.0)
            else:
                cnt = cnt + jnp.where(sub + 8 * v > r, jnp.where(row >= sc_v, 1.0, 0.0),
                                      jnp.where(row > sc_v, 1.0, 0.0))
        bias_ref[8 * v:8 * (v + 1), :] = jnp.where(cnt < float(SLC_TOPK), 0.0, NEG_INF)

    k_iota = lax.broadcasted_iota(I32, (kc_n, 1), 0)

    def flash(k_ref, vT_ref, c_lo, c_hi, sel, br):
        m_ref[...] = jnp.full(m_ref.shape, NEG_INF, F32)
        l_ref[...] = jnp.zeros(l_ref.shape, F32)
        acc_ref[...] = jnp.zeros(acc_ref.shape, F32)

        def chunk(c, carry):
            k_c = k_ref[0, 0, pl.ds(pl.multiple_of(c * kc_n, kc_n), kc_n), :]
            vT_c = vT_ref[0, c]
            kpos = c * kc_n + k_iota
            valid = kpos <= t_row
            if not sel:
                valid = valid & ((t_row - kpos) < WINDOW)
            for h in range(NSA_HPG):
                qh = qT_ref[0, 64 * h:64 * (h + 1), :]
                s = _nn(k_c, qh)
                if sel:
                    nb = kc_n // SLC_BLOCK
                    s = jnp.concatenate(
                        [s[SLC_BLOCK * b:SLC_BLOCK * (b + 1)] + bias_ref[pl.ds(c * nb + b, 1), :]
                         for b in range(nb)], axis=0)
                s = jnp.where(valid, s, NEG_INF)
                m_old = m_ref[h:h + 1, :]
                m_new = jnp.maximum(m_old, jnp.max(s, axis=0, keepdims=True))
                alpha = jnp.exp(m_old - m_new)
                p = jnp.exp(s - m_new)
                l_ref[h:h + 1, :] = alpha * l_ref[h:h + 1, :] + jnp.sum(p, axis=0, keepdims=True)
                acc_ref[64 * h:64 * (h + 1), :] = (alpha * acc_ref[64 * h:64 * (h + 1), :]
                                                   + _nn(vT_c, p.astype(BF16)))
                m_ref[h:h + 1, :] = m_new
            return carry

        lax.fori_loop(c_lo, c_hi, chunk, 0)
        for h in range(NSA_HPG):
            tot_ref[64 * h:64 * (h + 1), :] = (tot_ref[64 * h:64 * (h + 1), :]
                                               + gate(h, br) * (acc_ref[64 * h:64 * (h + 1), :] / l_ref[h:h + 1, :]))

    q_chunks = tq // kc_n
    flash(sk_ref, svT_ref, 0, (i + 1) * q_chunks, True, 1)
    w_lo = jnp.maximum(i * q_chunks - WINDOW // kc_n, 0)
    flash(wk_ref, wvT_ref, w_lo, (i + 1) * q_chunks, False, 2)

    o_ref[0] = tot_ref[...].T.astype(BF16)


def _nsa(qT, kc, vcT, ov, sk, svT, wk, wvT, gT):
    B, _, T = qT.shape
    tq = NSA_TQ
    nr = kc.shape[2]
    nch = T // NSA_KC
    in_specs = [
        pl.BlockSpec((1, 256, tq), lambda b, g, i: (b, g, i)),
        pl.BlockSpec((1, 1, nr, NSA_HD), lambda b, g, i: (b, g, 0, 0)),
        pl.BlockSpec((1, 1, NSA_HD, nr), lambda b, g, i: (b, g, 0, 0)),
        pl.BlockSpec(ov.shape, lambda b, g, i: (0, 0)),
        pl.BlockSpec((1, 1, T, NSA_HD), lambda b, g, i: (b, g, 0, 0)),
        pl.BlockSpec((1, nch, NSA_HD, NSA_KC), lambda b, g, i: (b, 0, g, 0)),
        pl.BlockSpec((1, 1, T, NSA_HD), lambda b, g, i: (b, g, 0, 0)),
        pl.BlockSpec((1, nch, NSA_HD, NSA_KC), lambda b, g, i: (b, 0, g, 0)),
        pl.BlockSpec((1, 32, tq), lambda b, g, i: (b, 0, i)),
    ]
    n_sel = T // SLC_BLOCK
    return pl.pallas_call(
        _nsa_body, out_shape=jax.ShapeDtypeStruct((B, T, 512), BF16),
        grid=(B, NSA_GROUPS, T // tq), in_specs=in_specs,
        out_specs=pl.BlockSpec((1, tq, 256), lambda b, g, i: (b, i, g)),
        scratch_shapes=[pltpu.VMEM((n_sel, tq), F32), pltpu.VMEM((n_sel, tq), F32),
                        pltpu.VMEM((8, tq), F32), pltpu.VMEM((8, tq), F32),
                        pltpu.VMEM((256, tq), F32), pltpu.VMEM((256, tq), F32)],
        compiler_params=_cparams(("parallel", "parallel", "parallel")), name="nsa",
    )(qT, kc, vcT, ov, sk, svT, wk, wvT, gT)


def _ret_body(q_ref, kT_ref, v_ref, dec_ref, xi_ref, zeta_ref, cd_ref, gg_ref, gb_ref, o_ref, r_ref):
    C = RET_CHUNK
    n_chunks = q_ref.shape[1] // C
    r_ref[...] = jnp.zeros(r_ref.shape, F32)
    dec = dec_ref[0]
    xi = xi_ref[0]
    zeta = zeta_ref[0]
    cd = cd_ref[0]
    gg = gg_ref[...]
    gb = gb_ref[...]

    def chunk(n, carry):
        rows = pl.ds(pl.multiple_of(n * C, C), C)
        qc = q_ref[0, rows, :]
        kT = kT_ref[0, n]
        vc = v_ref[0, rows, :]
        r_old = r_ref[...]
        s = _nn(qc, kT.astype(BF16)) * dec
        o = _nn(s.astype(BF16), vc) + _nn(qc, r_old.astype(BF16)) * xi
        r_ref[...] = r_old * cd + _nn((kT * zeta).astype(BF16), vc)
        mu = jnp.mean(o, axis=-1, keepdims=True)
        var = jnp.mean(jnp.square(o - mu), axis=-1, keepdims=True)
        o_ref[0, rows, :] = (o - mu) * lax.rsqrt(var + GN_EPS) * gg + gb
        return carry

    lax.fori_loop(0, n_chunks, chunk, 0)


def _retention(rq, rkT, rv, dec, xi, zeta, cd, gn_g, gn_b):
    B, T, _ = rq.shape
    nch = T // RET_CHUNK
    in_specs = [
        pl.BlockSpec((1, T, RET_DK), lambda b, h: (b, 0, h)),
        pl.BlockSpec((1, nch, RET_DK, RET_CHUNK), lambda b, h: (b, 0, h, 0)),
        pl.BlockSpec((1, T, RET_DV), lambda b, h: (b, 0, h)),
        pl.BlockSpec((1, RET_CHUNK, RET_CHUNK), lambda b, h: (h, 0, 0)),
        pl.BlockSpec((1, RET_CHUNK, RET_DV), lambda b, h: (h, 0, 0)),
        pl.BlockSpec((1, 1, RET_CHUNK), lambda b, h: (h, 0, 0)),
        pl.BlockSpec((1, 1, RET_DV), lambda b, h: (h, 0, 0)),
        pl.BlockSpec((1, RET_DV), lambda b, h: (0, h)),
        pl.BlockSpec((1, RET_DV), lambda b, h: (0, h)),
    ]
    return pl.pallas_call(
        _ret_body, out_shape=jax.ShapeDtypeStruct((B, T, RET_HEADS * RET_DV), F32),
        grid=(B, RET_HEADS), in_specs=in_specs,
        out_specs=pl.BlockSpec((1, T, RET_DV), lambda b, h: (b, 0, h)),
        scratch_shapes=[pltpu.VMEM((RET_DK, RET_DV), F32)],
        compiler_params=_cparams(("parallel", "parallel")), name="retention",
    )(rq, rkT, rv, dec, xi, zeta, cd, gn_g, gn_b)


def _layer_norm(y, g, b):
    mu = jnp.mean(y, axis=-1, keepdims=True)
    var = jnp.mean(jnp.square(y - mu), axis=-1, keepdims=True)
    return (y - mu) * lax.rsqrt(var + LN_EPS) * g + b


def _merge_body(alpha, x_ref, oa_ref, or_ref, wrg_ref, wmg_ref, wua_ref, wur_ref, wo_ref, g1_ref, b1_ref,
                wrh_ref, wrl_ref, rb_ref, x1_ref, eid_ref, wgt_ref):
    x = x_ref[...]
    xb = x.astype(BF16)
    rgate = jax.nn.silu(_nn(xb, wrg_ref[...]))
    o_ret = (or_ref[...] * rgate).astype(BF16)
    a = _nn(oa_ref[...], wua_ref[...])
    r = _nn(o_ret, wur_ref[...])
    mg = jax.nn.sigmoid(_nn(xb, wmg_ref[...]))
    merged = mg[:, :D_MODEL] * a + mg[:, D_MODEL:] * r
    mix = _nn(merged.astype(BF16), wo_ref[...])
    x1 = _layer_norm(alpha * x + mix, g1_ref[...], b1_ref[...])
    x1_ref[...] = x1

    xh = x1.astype(BF16)
    xl = (x1 - xh.astype(F32)).astype(BF16)
    wh = wrh_ref[...]
    lg = _nt(wh, xh) + _nt(wh, xl) + _nt(wrl_ref[...], xh) + rb_ref[...]
    ne = N_EXPERTS
    gl = lg[ne:ne + MOE_GROUPS]
    ge = jnp.exp(gl - jnp.max(gl, axis=0, keepdims=True))
    pg = ge / jnp.sum(ge, axis=0, keepdims=True)
    g_prob = jnp.max(pg, axis=0, keepdims=True)
    gi = lax.broadcasted_iota(I32, pg.shape, 0)
    g_idx = jnp.min(jnp.where(pg == g_prob, gi, MOE_GROUPS), axis=0, keepdims=True)
    inner = jnp.zeros((EXPERTS_PER_GROUP, lg.shape[1]), F32)
    for gq in range(MOE_GROUPS):
        inner = inner + jnp.where(g_idx == gq, lg[8 * gq:8 * (gq + 1)], 0.0)
    ei = lax.broadcasted_iota(I32, inner.shape, 0)
    m1 = jnp.max(inner, axis=0, keepdims=True)
    i1 = jnp.min(jnp.where(inner == m1, ei, EXPERTS_PER_GROUP), axis=0, keepdims=True)
    rest = jnp.where(ei == i1, -jnp.inf, inner)
    m2 = jnp.max(rest, axis=0, keepdims=True)
    i2 = jnp.min(jnp.where(rest == m2, ei, EXPERTS_PER_GROUP), axis=0, keepdims=True)
    e2 = jnp.exp(m2 - m1)
    den = 1.0 + e2
    w1 = (1.0 / den) * g_prob
    w2 = (e2 / den) * g_prob
    zi = jnp.zeros((6, lg.shape[1]), I32)
    eid_ref[...] = jnp.concatenate([g_idx * EXPERTS_PER_GROUP + i1, g_idx * EXPERTS_PER_GROUP + i2, zi], axis=0)
    wgt_ref[...] = jnp.concatenate([w1, w2, jnp.zeros((6, lg.shape[1]), F32)], axis=0)


def _merge(alpha, x2, oa2, or2, wrg, wmg, wua, wur, wo, g1, b1, wrh, wrl, rb):
    M, D = x2.shape
    tm = MERGE_TM
    full = lambda a: pl.BlockSpec(a.shape, lambda i: (0,) * a.ndim)
    in_specs = [pl.BlockSpec((tm, D), lambda i: (i, 0)),
                pl.BlockSpec((tm, oa2.shape[1]), lambda i: (i, 0)),
                pl.BlockSpec((tm, or2.shape[1]), lambda i: (i, 0))] + [
        full(a) for a in (wrg, wmg, wua, wur, wo, g1, b1, wrh, wrl, rb)]
    out_shape = [jax.ShapeDtypeStruct((M, D), F32),
                 jax.ShapeDtypeStruct((8, M), I32),
                 jax.ShapeDtypeStruct((8, M), F32)]
    out_specs = [pl.BlockSpec((tm, D), lambda i: (i, 0)),
                 pl.BlockSpec((8, tm), lambda i: (0, i)),
                 pl.BlockSpec((8, tm), lambda i: (0, i))]
    return pl.pallas_call(
        functools.partial(_merge_body, alpha), out_shape=out_shape, grid=(M // tm,),
        in_specs=in_specs, out_specs=out_specs,
        compiler_params=_cparams(("parallel",)), name="merge",
    )(x2, oa2, or2, wrg, wmg, wua, wur, wo, g1, b1, wrh, wrl, rb)


def _rank_body(eid_ref, tri_ref, rank_ref, cnt_ref, carry_ref):
    j = pl.program_id(0)

    @pl.when(j == 0)
    def _():
        carry_ref[...] = jnp.zeros(carry_ref.shape, F32)

    e = eid_ref[...]
    rows = lax.broadcasted_iota(I32, (N_EXPERTS, e.shape[1]), 0)
    hit = rows == e
    onehot = jnp.where(hit, 1.0, 0.0).astype(BF16)
    incl = _nn(onehot, tri_ref[...])
    carry = carry_ref[:, 0:1]
    rank = jnp.sum(jnp.where(hit, incl - 1.0 + carry, 0.0), axis=0, keepdims=True)
    rank_ref[...] = rank.astype(I32)
    carry_ref[...] = carry_ref[...] + jnp.sum(jnp.where(hit, 1.0, 0.0), axis=1, keepdims=True)
    cnt_ref[...] = carry_ref[...]


def _rank(eid_flat, tri):
    n = eid_flat.shape[1]
    tb = RANK_TB
    return pl.pallas_call(
        _rank_body,
        out_shape=[jax.ShapeDtypeStruct((1, n), I32), jax.ShapeDtypeStruct((N_EXPERTS, 128), F32)],
        grid=(n // tb,),
        in_specs=[pl.BlockSpec((1, tb), lambda j: (0, j)), pl.BlockSpec((tb, tb), lambda j: (0, 0))],
        out_specs=[pl.BlockSpec((1, tb), lambda j: (0, j)), pl.BlockSpec((N_EXPERTS, 128), lambda j: (0, 0))],
        scratch_shapes=[pltpu.VMEM((N_EXPERTS, 128), F32)],
        compiler_params=_cparams(("arbitrary",)), name="rank",
    )(eid_flat, tri)


def _dispatch_body(pos_ref, x_ref, xs_in_ref, xs_ref, sem):
    del xs_in_ref
    tm = DISPATCH_TM
    m_tok = pos_ref.shape[0] // 2
    base = pl.program_id(0) * tm

    def row_copy(r, p):
        return pltpu.make_async_copy(x_ref.at[pl.ds(r, 1), :], xs_ref.at[pl.ds(p, 1), :], sem)

    def issue(r, carry):
        row_copy(r, pos_ref[base + r]).start()
        row_copy(r, pos_ref[m_tok + base + r]).start()
        return carry

    lax.fori_loop(0, tm, issue, 0)

    def drain(r, carry):
        row_copy(0, 0).wait()
        row_copy(0, 0).wait()
        return carry

    lax.fori_loop(0, tm, drain, 0)


def _dispatch(pos, x1, xs_init):
    M, D = x1.shape
    tm = DISPATCH_TM
    grid_spec = pltpu.PrefetchScalarGridSpec(
        num_scalar_prefetch=1, grid=(M // tm,),
        in_specs=[pl.BlockSpec((tm, D), lambda i, pos: (i, 0)),
                  pl.BlockSpec(memory_space=pl.ANY)],
        out_specs=pl.BlockSpec(memory_space=pl.ANY),
        scratch_shapes=[pltpu.SemaphoreType.DMA],
    )
    return pl.pallas_call(
        _dispatch_body, out_shape=jax.ShapeDtypeStruct(xs_init.shape, F32), grid_spec=grid_spec,
        input_output_aliases={2: 0},
        compiler_params=_cparams(("arbitrary",), has_side_effects=True), name="dispatch",
    )(pos, x1, xs_init)


def _experts_body(te_ref, nu_ref, xs_ref, wg_ref, wu_ref, wd_ref, ys_ref):
    i = pl.program_id(0)

    @pl.when(i < nu_ref[0])
    def _():
        xb = xs_ref[...].astype(BF16)
        hg = _nn(xb, wg_ref[0].astype(BF16))
        hu = _nn(xb, wu_ref[0].astype(BF16))
        h = (jax.nn.silu(hg) * hu).astype(BF16)
        ys_ref[...] = _nn(h, wd_ref[0].astype(BF16))

    @pl.when(i >= nu_ref[0])
    def _():
        ys_ref[...] = jnp.zeros(ys_ref.shape, F32)


def _experts(tile_expert, n_used, xs, wg, wu, wd):
    npad, D = xs.shape
    tm = EXPERT_TM
    grid_spec = pltpu.PrefetchScalarGridSpec(
        num_scalar_prefetch=2, grid=(npad // tm,),
        in_specs=[pl.BlockSpec((tm, D), lambda i, te, nu: (i, 0)),
                  pl.BlockSpec((1, D, D_FF), lambda i, te, nu: (te[i], 0, 0)),
                  pl.BlockSpec((1, D, D_FF), lambda i, te, nu: (te[i], 0, 0)),
                  pl.BlockSpec((1, D_FF, D), lambda i, te, nu: (te[i], 0, 0))],
        out_specs=pl.BlockSpec((tm, D), lambda i, te, nu: (i, 0)),
    )
    return pl.pallas_call(
        _experts_body, out_shape=jax.ShapeDtypeStruct((npad, D), F32), grid_spec=grid_spec,
        compiler_params=_cparams(("arbitrary",)), name="experts",
    )(tile_expert, n_used, xs, wg, wu, wd)


def _combine_body(alpha, pos_ref, x1_ref, w_ref, eye_ref, g2_ref, b2_ref, ys_ref, o_ref, buf_ref, sem):
    tm = COMBINE_TM
    m_tok = pos_ref.shape[0] // 2
    base = pl.program_id(0) * tm

    def row_copy(slot, r, p):
        return pltpu.make_async_copy(ys_ref.at[pl.ds(p, 1), :], buf_ref.at[slot, pl.ds(r, 1), :], sem)

    def issue(r, carry):
        row_copy(0, r, pos_ref[base + r]).start()
        row_copy(1, r, pos_ref[m_tok + base + r]).start()
        return carry

    lax.fori_loop(0, tm, issue, 0)

    w = w_ref[...]
    eye = eye_ref[...]
    w_a = w.astype(BF16)
    w_b = (w - w_a.astype(F32)).astype(BF16)
    w_c = (w - w_a.astype(F32) - w_b.astype(F32)).astype(BF16)
    wcol = _nt(eye, w_a) + _nt(eye, w_b) + _nt(eye, w_c)

    def drain(r, carry):
        row_copy(0, 0, 0).wait()
        row_copy(1, 0, 0).wait()
        return carry

    lax.fori_loop(0, tm, drain, 0)
    moe = buf_ref[0] * wcol[:, 0:1] + buf_ref[1] * wcol[:, 1:2]
    o_ref[...] = _layer_norm(alpha * x1_ref[...] + moe, g2_ref[...], b2_ref[...])


def _combine(alpha, pos, x1, wgt, eye, g2, b2, ys):
    M, D = x1.shape
    tm = COMBINE_TM
    grid_spec = pltpu.PrefetchScalarGridSpec(
        num_scalar_prefetch=1, grid=(M // tm,),
        in_specs=[pl.BlockSpec((tm, D), lambda i, pos: (i, 0)),
                  pl.BlockSpec((8, tm), lambda i, pos: (0, i)),
                  pl.BlockSpec((tm, tm), lambda i, pos: (0, 0)),
                  pl.BlockSpec((1, D), lambda i, pos: (0, 0)),
                  pl.BlockSpec((1, D), lambda i, pos: (0, 0)),
                  pl.BlockSpec(memory_space=pl.ANY)],
        out_specs=pl.BlockSpec((tm, D), lambda i, pos: (i, 0)),
        scratch_shapes=[pltpu.VMEM((2, tm, D), F32), pltpu.SemaphoreType.DMA],
    )
    return pl.pallas_call(
        functools.partial(_combine_body, alpha), out_shape=jax.ShapeDtypeStruct((M, D), F32),
        grid_spec=grid_spec, compiler_params=_cparams(("arbitrary",)), name="combine",
    )(pos, x1, wgt, eye, g2, b2, ys)


def _rope_tables(T, dim):
    half = dim // 2
    inv_freq = ROPE_THETA ** (-jnp.arange(half, dtype=F32) * 2.0 / dim)
    return inv_freq


def _tables(T):
    pos = jnp.arange(T).astype(F32)
    f64 = _rope_tables(T, NSA_HD)
    ang = pos[:, None] * f64[None, :]
    cos, sin = jnp.cos(ang), jnp.sin(ang)
    c64 = jnp.tile(cos, (1, 4))
    s64 = jnp.tile(jnp.concatenate([-sin, sin], axis=1), (1, 2))
    f128 = _rope_tables(T, RET_DK)
    ang2 = pos[:, None] * f128[None, :]
    cos2, sin2 = jnp.cos(ang2), jnp.sin(ang2)
    c128 = jnp.tile(cos2, (1, 2))
    s128 = jnp.concatenate([-sin2, sin2], axis=1)
    n_rows = T // CMP_STRIDE
    cmp_end = (jnp.arange(n_rows) * CMP_STRIDE + CMP_BLOCK - 1).astype(F32)
    angc = cmp_end[:, None] * f64[None, :]
    cc = jnp.tile(jnp.cos(angc), (1, 2))
    sc = jnp.tile(jnp.sin(angc), (1, 2))
    return (c64, s64, c128, s128, cos.T, sin.T, cos2.T, sin2.T), cc, sc


def _overlap_matrix(T):
    n_rows = T // CMP_STRIDE
    n_sel = T // SLC_BLOCK
    cmp_start = np.arange(n_rows) * CMP_STRIDE
    sel_start = np.arange(n_sel) * SLC_BLOCK
    ov = np.clip(np.minimum(cmp_start[None, :] + CMP_BLOCK, sel_start[:, None] + SLC_BLOCK)
                 - np.maximum(cmp_start[None, :], sel_start[:, None]), 0, None)
    return jnp.asarray(ov.astype(np.float32) / CMP_STRIDE, dtype=BF16)


def _retention_tables():
    C = RET_CHUNK
    gamma = 1.0 - 2.0 ** (-5.0 - jnp.arange(RET_HEADS, dtype=F32))
    log_g = jnp.log(gamma)
    i = jnp.arange(C, dtype=F32)
    diff = i[:, None] - i[None, :]
    dec = jnp.where(diff >= 0, jnp.exp(jnp.maximum(diff, 0.0) * log_g[:, None, None]), 0.0)
    xi = jnp.exp((i + 1.0) * log_g[:, None])
    zeta = jnp.exp((C - 1.0 - i) * log_g[:, None])
    cd = jnp.exp(C * log_g)
    xi_b = jnp.broadcast_to(xi[:, :, None], (RET_HEADS, C, RET_DV))
    cd_b = jnp.broadcast_to(cd[:, None, None], (RET_HEADS, 1, RET_DV))
    return dec, xi_b, zeta[:, None, :], cd_b


def _rot_half_cols(w):
    half = w.shape[-1] // 2
    return jnp.concatenate([-w[..., half:], w[..., :half]], axis=-1)


def kernel(x, w_in, cmp_pos_k, cmp_k_w1, cmp_k_b1, cmp_k_w2, cmp_k_b2, cmp_pos_v, cmp_v_w1, cmp_v_b1, cmp_v_w2, cmp_v_b2, ret_gn_g, ret_gn_b, w_up_attn, w_up_ret, w_out, ln1_g, ln1_b, router_group_w, router_group_b, router_inner_w, router_inner_b, expert_w_gate, expert_w_up, expert_w_down, ln2_g, ln2_b):
    B, T, D = x.shape
    M = B * T
    depth = w_in.shape[0]
    alpha = (2.0 * depth) ** 0.25
    tabs, cc, sc = _tables(T)
    ov = _overlap_matrix(T)
    dec, xi_b, zeta, cd_b = _retention_tables()
    tri = jnp.asarray(np.triu(np.ones((RANK_TB, RANK_TB), np.float32)), dtype=BF16)
    eye = jnp.asarray(np.eye(COMBINE_TM, dtype=np.float32), dtype=BF16)
    n_pad = 2 * M + N_EXPERTS * EXPERT_TM
    n_tiles = n_pad // EXPERT_TM

    for l in range(depth):
        w = w_in[l]
        col = lambda n: w[:, _OFF[n][0]:_OFF[n][1]]
        wn = jnp.concatenate([col("cmp_k"), col("cmp_v"), col("slc_k"), col("win_k"), col("ret_q"), col("ret_v")],
                             axis=1).astype(BF16)
        wt = jnp.concatenate([col("nsa_q"), col("slc_v"), col("win_v"), col("ret_k"), col("nsa_gate"),
                              jnp.zeros((D, 8), F32)], axis=1).T.astype(BF16)
        qT, ckv, sk, wk, svT, wvT, gT, rq, rkT, rv = _proj(x, wn, wt, tabs)

        kc, vcT = _compress(
            ckv, cmp_pos_k[l].reshape(2, -1), cmp_pos_v[l].reshape(2, -1),
            cmp_k_w1[l], cmp_k_b1[l][None, :], cmp_k_w2[l], _rot_half_cols(cmp_k_w2[l]),
            cmp_k_b2[l][None, :], _rot_half_cols(cmp_k_b2[l])[None, :], cc, sc,
            cmp_v_w1[l], cmp_v_b1[l][None, :], cmp_v_w2[l].T, cmp_v_b2[l][:, None])
        o_attn = _nsa(qT, kc, vcT, ov, sk, svT, wk, wvT, gT)
        o_ret = _retention(rq, rkT, rv, dec, xi_b, zeta, cd_b, ret_gn_g[l][None, :], ret_gn_b[l][None, :])

        wr = jnp.concatenate([router_inner_w[l].transpose(0, 2, 1).reshape(N_EXPERTS, D),
                              router_group_w[l].T, jnp.zeros((4, D), F32)], axis=0)
        wrh = wr.astype(BF16)
        wrl = (wr - wrh.astype(F32)).astype(BF16)
        rb = jnp.concatenate([router_inner_b[l].reshape(-1), router_group_b[l], jnp.zeros((4,), F32)])[:, None]
        x1, eid, wgt = _merge(
            alpha, x.reshape(M, D), o_attn.reshape(M, -1), o_ret.reshape(M, -1),
            col("ret_gate").astype(BF16), col("merge_gate").astype(BF16), w_up_attn[l].astype(BF16),
            w_up_ret[l].astype(BF16), w_out[l].astype(BF16), ln1_g[l][None, :], ln1_b[l][None, :], wrh, wrl, rb)

        eid_flat = eid[:2].reshape(1, 2 * M)
        rank, cnt = _rank(eid_flat, tri)
        counts = cnt[:, 0].astype(I32)
        tiles_per = (counts + EXPERT_TM - 1) // EXPERT_TM
        tile_end = jnp.cumsum(tiles_per)
        row_start = (tile_end - tiles_per) * EXPERT_TM
        pos = (row_start[eid_flat[0]] + rank[0]).astype(I32)
        tile_ids = jnp.arange(n_tiles, dtype=I32)
        tile_expert = jnp.minimum(jnp.sum((tile_end[None, :] <= tile_ids[:, None]).astype(I32), axis=1),
                                  N_EXPERTS - 1).astype(I32)
        n_used = tile_end[-1:].astype(I32)

        xs = _dispatch(pos, x1, jnp.zeros((n_pad, D), F32))
        ys = _experts(tile_expert, n_used, xs, expert_w_gate[l], expert_w_up[l], expert_w_down[l])
        x = _combine(alpha, pos, x1, wgt, eye, ln2_g[l][None, :], ln2_b[l][None, :], ys).reshape(B, T, D)
    return x
```

```python
import functools

import numpy as np
import jax
import jax.numpy as jnp
from jax import lax
from jax.experimental import pallas as pl
from jax.experimental.pallas import tpu as pltpu

F32 = jnp.float32
BF16 = jnp.bfloat16
I32 = jnp.int32

D_MODEL = 1024
NSA_HEADS = 8
NSA_HD = 64
NSA_GROUPS = 2
NSA_HPG = NSA_HEADS // NSA_GROUPS
CMP_BLOCK = 32
CMP_STRIDE = 16
CMP_HIDDEN = 256
SLC_BLOCK = 64
SLC_TOPK = 16
WINDOW = 512
RET_HEADS = 4
RET_DK = 128
RET_DV = 256
RET_CHUNK = 128
MOE_GROUPS = 4
EXPERTS_PER_GROUP = 8
N_EXPERTS = MOE_GROUPS * EXPERTS_PER_GROUP
D_FF = 512
ROPE_THETA = 10000.0
LN_EPS = 1e-5
GN_EPS = 1e-5
NEG_INF = -1e30
LOG2_E = 1.4426950408889634

VMEM_LIMIT_V7X = 56 * 1024 * 1024

_OFF = {}
_o = 0
for _n, _w in (("nsa_q", 512), ("cmp_k", 128), ("cmp_v", 128), ("slc_k", 128), ("slc_v", 128),
               ("win_k", 128), ("win_v", 128), ("nsa_gate", 24), ("ret_q", 512), ("ret_k", 512),
               ("ret_v", 1024), ("ret_gate", 1024), ("merge_gate", 2048)):
    _OFF[_n] = (_o, _o + _w)
    _o += _w

PROJ_TM = 512
NSA_TQ = 256
NSA_KC = 256
MERGE_TM = 512
RANK_TB = 512
DISPATCH_TM = 512
EXPERT_TM = 256
COMBINE_TM = 256


def _cparams(sem, **kw):
    return pltpu.CompilerParams(dimension_semantics=sem, vmem_limit_bytes=VMEM_LIMIT_V7X, **kw)


def _nt(a, b):
    return lax.dot_general(a, b, (((1,), (1,)), ((), ())), preferred_element_type=F32)


def _nn(a, b):
    return jnp.dot(a, b, preferred_element_type=F32)


def _proj_body(x_ref, wn_ref, wt_ref, c64_ref, s64_ref, c128_ref, s128_ref,
               ct64_ref, st64_ref, ct128_ref, st128_ref,
               qT_ref, ckv_ref, sk_ref, wk_ref, svT_ref, wvT_ref, gT_ref, rq_ref, rkT_ref, rv_ref):
    tm = PROJ_TM
    xb = x_ref[0].astype(BF16)

    def nn(a, b):
        return _nn(xb, wn_ref[:, a:b])

    def nt(a, b):
        return _nt(wt_ref[a:b, :], xb)

    ckv = nn(0, 256)
    for j in range(4):
        ckv_ref[0, j] = ckv[:, 64 * j:64 * (j + 1)]

    lane = lax.broadcasted_iota(I32, (tm, 128), 1)
    first = (lane % 64) < 32
    c64 = c64_ref[...]
    s64 = s64_ref[...]

    def rope64(k):
        rot = jnp.where(first, pltpu.roll(k, 96, 1), pltpu.roll(k, 32, 1))
        return k * c64 + rot * s64

    sk = rope64(nn(256, 384)).astype(BF16)
    sk_ref[0, 0] = sk[:, :64]
    sk_ref[0, 1] = sk[:, 64:]
    wk = rope64(nn(384, 512)).astype(BF16)
    wk_ref[0, 0] = wk[:, :64]
    wk_ref[0, 1] = wk[:, 64:]

    c128 = c128_ref[...]
    s128 = s128_ref[...]
    rq = nn(512, 1024)
    for h in range(RET_HEADS):
        ch = rq[:, 128 * h:128 * (h + 1)]
        rq_ref[0, :, 128 * h:128 * (h + 1)] = (ch * c128 + pltpu.roll(ch, 64, 1) * s128).astype(BF16)
    rv_ref[0] = nn(1024, 2048).astype(BF16)

    ct = ct64_ref[...]
    st = st64_ref[...]
    qT = nt(0, 512)
    scale_q = NSA_HD ** -0.5 * LOG2_E
    tq = NSA_TQ
    for hh in range(NSA_HEADS):
        g, h = divmod(hh, NSA_HPG)
        x1 = qT[64 * hh:64 * hh + 32]
        x2 = qT[64 * hh + 32:64 * hh + 64]
        o1 = ((x1 * ct - x2 * st) * scale_q).astype(BF16)
        o2 = ((x1 * st + x2 * ct) * scale_q).astype(BF16)
        for it in range(tm // tq):
            qT_ref[0, g, it, 0:32, h * tq:(h + 1) * tq] = o1[:, it * tq:(it + 1) * tq]
            qT_ref[0, g, it, 32:64, h * tq:(h + 1) * tq] = o2[:, it * tq:(it + 1) * tq]

    svT = nt(512, 640).astype(BF16)
    wvT = nt(640, 768).astype(BF16)
    for c in range(tm // NSA_KC):
        svT_ref[0, c] = svT[:, NSA_KC * c:NSA_KC * (c + 1)]
        wvT_ref[0, c] = wvT[:, NSA_KC * c:NSA_KC * (c + 1)]

    ct2 = ct128_ref[...]
    st2 = st128_ref[...]
    rkT = nt(768, 1280)
    scale_k = RET_DK ** -0.5
    for h in range(RET_HEADS):
        x1 = rkT[128 * h:128 * h + 64]
        x2 = rkT[128 * h + 64:128 * h + 128]
        o1 = (x1 * ct2 - x2 * st2) * scale_k
        o2 = (x1 * st2 + x2 * ct2) * scale_k
        for c in range(tm // RET_CHUNK):
            rkT_ref[0, c, 128 * h:128 * h + 64, :] = o1[:, 128 * c:128 * (c + 1)]
            rkT_ref[0, c, 128 * h + 64:128 * h + 128, :] = o2[:, 128 * c:128 * (c + 1)]

    gT_ref[0] = jax.nn.sigmoid(nt(1280, 1312))


def _proj(x, wn, wt, tabs):
    B, T, D = x.shape
    tm = PROJ_TM
    c64, s64, c128, s128, ct64, st64, ct128, st128 = tabs
    const = lambda b, i: (0, 0)
    in_specs = [
        pl.BlockSpec((1, tm, D), lambda b, i: (b, i, 0)),
        pl.BlockSpec(wn.shape, const),
        pl.BlockSpec(wt.shape, const),
        pl.BlockSpec((tm, 128), lambda b, i: (i, 0)),
        pl.BlockSpec((tm, 128), lambda b, i: (i, 0)),
        pl.BlockSpec((tm, 128), lambda b, i: (i, 0)),
        pl.BlockSpec((tm, 128), lambda b, i: (i, 0)),
        pl.BlockSpec((32, tm), lambda b, i: (0, i)),
        pl.BlockSpec((32, tm), lambda b, i: (0, i)),
        pl.BlockSpec((64, tm), lambda b, i: (0, i)),
        pl.BlockSpec((64, tm), lambda b, i: (0, i)),
    ]
    out_shape = [
        jax.ShapeDtypeStruct((B, NSA_GROUPS, T // NSA_TQ, NSA_HD, NSA_HPG * NSA_TQ), BF16),
        jax.ShapeDtypeStruct((B, 4, T, 64), F32),
        jax.ShapeDtypeStruct((B, 2, T, 64), BF16),
        jax.ShapeDtypeStruct((B, 2, T, 64), BF16),
        jax.ShapeDtypeStruct((B, T // NSA_KC, 128, NSA_KC), BF16),
        jax.ShapeDtypeStruct((B, T // NSA_KC, 128, NSA_KC), BF16),
        jax.ShapeDtypeStruct((B, 32, T), F32),
        jax.ShapeDtypeStruct((B, T, 512), BF16),
        jax.ShapeDtypeStruct((B, T // RET_CHUNK, 512, RET_CHUNK), F32),
        jax.ShapeDtypeStruct((B, T, 1024), BF16),
    ]
    out_specs = [
        pl.BlockSpec((1, NSA_GROUPS, tm // NSA_TQ, NSA_HD, NSA_HPG * NSA_TQ), lambda b, i: (b, 0, i, 0, 0)),
        pl.BlockSpec((1, 4, tm, 64), lambda b, i: (b, 0, i, 0)),
        pl.BlockSpec((1, 2, tm, 64), lambda b, i: (b, 0, i, 0)),
        pl.BlockSpec((1, 2, tm, 64), lambda b, i: (b, 0, i, 0)),
        pl.BlockSpec((1, tm // NSA_KC, 128, NSA_KC), lambda b, i: (b, i, 0, 0)),
        pl.BlockSpec((1, tm // NSA_KC, 128, NSA_KC), lambda b, i: (b, i, 0, 0)),
        pl.BlockSpec((1, 32, tm), lambda b, i: (b, 0, i)),
        pl.BlockSpec((1, tm, 512), lambda b, i: (b, i, 0)),
        pl.BlockSpec((1, tm // RET_CHUNK, 512, RET_CHUNK), lambda b, i: (b, i, 0, 0)),
        pl.BlockSpec((1, tm, 1024), lambda b, i: (b, i, 0)),
    ]
    return pl.pallas_call(
        _proj_body, out_shape=out_shape, grid=(B, T // tm), in_specs=in_specs, out_specs=out_specs,
        compiler_params=_cparams(("parallel", "parallel")), name="proj",
    )(x, wn, wt, c64, s64, c128, s128, ct64, st64, ct128, st128)


def _compress_body(uk_ref, uv_ref, posk_ref, posv_ref, w1k_ref, b1k_ref, w2k_ref, w2kr_ref, b2k_ref, b2kr_ref,
                   cc_ref, sc_ref, w1v_ref, b1v_ref, w2vT_ref, b2vT_ref, kc_ref, vcT_ref):
    half = CMP_STRIDE * NSA_HD

    def hidden(u_ref, pos_ref, w1_ref, b1_ref):
        u = u_ref[0, 0]
        top = (u + pos_ref[0:1, :]).astype(BF16)
        bot = (u + pos_ref[1:2, :]).astype(BF16)
        a = _nn(top, w1_ref[0:half, :].astype(BF16))
        bm = _nn(bot, w1_ref[half:2 * half, :].astype(BF16))
        n = bm.shape[0]
        return jax.nn.gelu(a + pltpu.roll(bm, n - 1, 0) + b1_ref[...])

    hk = hidden(uk_ref, posk_ref, w1k_ref, b1k_ref).astype(BF16)
    k = _nn(hk, w2k_ref[...].astype(BF16)) + b2k_ref[...]
    kr = _nn(hk, w2kr_ref[...].astype(BF16)) + b2kr_ref[...]
    kc_ref[0, 0] = (k * cc_ref[...] + kr * sc_ref[...]).astype(BF16)

    hv = hidden(uv_ref, posv_ref, w1v_ref, b1v_ref).astype(BF16)
    vcT_ref[0, 0] = (_nt(w2vT_ref[...].astype(BF16), hv) + b2vT_ref[...]).astype(BF16)


def _compress(ckv, posk, posv, w1k, b1k, w2k, w2kr, b2k, b2kr, cc, sc, w1v, b1v, w2vT, b2vT):
    B = ckv.shape[0]
    T = ckv.shape[2]
    nr = T // CMP_STRIDE
    u = ckv.reshape(B, 4, nr, CMP_STRIDE * NSA_HD)
    full = lambda a: pl.BlockSpec(a.shape, lambda b, g: (0,) * a.ndim)
    in_specs = [
        pl.BlockSpec((1, 1, nr, 1024), lambda b, g: (b, g, 0, 0)),
        pl.BlockSpec((1, 1, nr, 1024), lambda b, g: (b, g + 2, 0, 0)),
    ] + [full(a) for a in (posk, posv, w1k, b1k, w2k, w2kr, b2k, b2kr, cc, sc, w1v, b1v, w2vT, b2vT)]
    out_shape = [jax.ShapeDtypeStruct((B, NSA_GROUPS, nr, NSA_HD), BF16),
                 jax.ShapeDtypeStruct((B, NSA_GROUPS, NSA_HD, nr), BF16)]
    out_specs = [pl.BlockSpec((1, 1, nr, NSA_HD), lambda b, g: (b, g, 0, 0)),
                 pl.BlockSpec((1, 1, NSA_HD, nr), lambda b, g: (b, g, 0, 0))]
    return pl.pallas_call(
        _compress_body, out_shape=out_shape, grid=(B, NSA_GROUPS), in_specs=in_specs, out_specs=out_specs,
        compiler_params=_cparams(("parallel", "parallel")), name="compress",
    )(u, u, posk, posv, w1k, b1k, w2k, w2kr, b2k, b2kr, cc, sc, w1v, b1v, w2vT, b2vT)


def _nsa_body(q_ref, kc_ref, vcT_ref, ov_ref, sk_ref, svT_ref, wk_ref, wvT_ref, gT_ref, lo_ref, hi_ref, o_ref,
              score_ref, bias_ref, acc_ref, tot_ref, s_ref):
    tq = NSA_TQ
    kc_n = NSA_KC
    hq = NSA_HPG * tq
    g = pl.program_id(1)
    i = pl.program_id(2)
    t0 = i * tq
    t_row = t0 + lax.broadcasted_iota(I32, (1, tq), 1)
    t_row4 = t0 + lax.broadcasted_iota(I32, (1, hq), 1) % tq
    n_cmp = kc_ref.shape[2]
    n_sel = ov_ref.shape[0]
    q = q_ref[0, 0, 0]

    def gate4(br):
        return jnp.concatenate([gT_ref[0, pl.ds((g * NSA_HPG + h) * 3 + br, 1), :] for h in range(NSA_HPG)],
                               axis=1)

    cmp_end = lax.broadcasted_iota(I32, (n_cmp, 1), 0) * CMP_STRIDE + (CMP_BLOCK - 1)
    mask_c = cmp_end <= t_row4
    s = jnp.where(mask_c, _nn(kc_ref[0, 0], q), NEG_INF)
    e = jnp.exp2(s - jnp.max(s, axis=0, keepdims=True))
    p = jnp.where(mask_c, e / jnp.sum(e, axis=0, keepdims=True), 0.0)
    tot_ref[...] = gate4(0) * _nn(vcT_ref[0, 0], p.astype(BF16))
    psum = p[:, 0:tq]
    for h in range(1, NSA_HPG):
        psum = psum + p[:, h * tq:(h + 1) * tq]

    p_hi = psum.astype(BF16)
    p_lo = (psum - p_hi.astype(F32)).astype(BF16)
    ov = ov_ref[...]
    imp = _nn(ov, p_hi) + _nn(ov, p_lo)
    j_col = lax.broadcasted_iota(I32, (n_sel, 1), 0)
    blk_t = t_row // SLC_BLOCK
    forced = (j_col == 0) | (j_col == blk_t) | (j_col == blk_t - 1)
    score = jnp.where(forced, 1e9, jnp.where(j_col <= blk_t, imp, -1e9)).astype(F32)
    score_ref[...] = score
    sub = lax.broadcasted_iota(I32, (8, 1), 0)
    for v in range(n_sel // 8):
        sc_v = score[8 * v:8 * (v + 1)]
        cnt = jnp.zeros((8, tq), F32)
        for r in range(n_sel):
            row = score_ref[r:r + 1, :]
            if r < 8 * v:
                cnt = cnt + jnp.where(row >= sc_v, 1.0, 0.0)
            elif r >= 8 * (v + 1):
                cnt = cnt + jnp.where(row > sc_v, 1.0, 0.0)
            else:
                cnt = cnt + jnp.where(sub + 8 * v > r, jnp.where(row >= sc_v, 1.0, 0.0),
                                      jnp.where(row > sc_v, 1.0, 0.0))
        bias_ref[8 * v:8 * (v + 1), :] = jnp.where(
            cnt < float(SLC_TOPK), jnp.where(sub + 8 * v <= blk_t, 0.0, NEG_INF), NEG_INF)

    def qk(k_ref, c):
        k_c = k_ref[0, 0, pl.ds(pl.multiple_of(c * kc_n, kc_n), kc_n), :]
        return [_nn(k_c, q[:, h * tq:(h + 1) * tq]) for h in range(NSA_HPG)]

    def stage_a(scores, add_bias, m):
        biased = [add_bias(scores[h]) for h in range(NSA_HPG)]
        m_new = tuple(jnp.maximum(m[h], jnp.max(biased[h], axis=0, keepdims=True)) for h in range(NSA_HPG))
        return biased, m_new

    def stage_b(biased, vT_c, m_old, m_new, l):
        l_out = []
        for h in range(NSA_HPG):
            hs = slice(h * tq, (h + 1) * tq)
            alpha = jnp.exp2(m_old[h] - m_new[h])
            p = jnp.exp2(biased[h] - m_new[h])
            acc_ref[:, hs] = alpha * acc_ref[:, hs] + _nn(vT_c, p.astype(BF16))
            l_out.append(alpha * l[h] + jnp.sum(p, axis=0, keepdims=True))
        return tuple(l_out)

    nb = kc_n // SLC_BLOCK

    def block_bias(c, s):
        return jnp.concatenate([s[SLC_BLOCK * b:SLC_BLOCK * (b + 1)] + bias_ref[pl.ds(c * nb + b, 1), :]
                                for b in range(nb)], axis=0)

    def finish(br, l):
        tot_ref[...] = tot_ref[...] + gate4(br) * (acc_ref[...] / jnp.concatenate(l, axis=1))

    m0 = tuple(jnp.full((1, tq), NEG_INF, F32) for _ in range(NSA_HPG))
    l0 = tuple(jnp.zeros((1, tq), F32) for _ in range(NSA_HPG))

    acc_ref[...] = jnp.zeros(acc_ref.shape, F32)
    biased, m_new = stage_a(qk(sk_ref, i), lambda s: block_bias(i, s) + lo_ref[...], m0)
    for h in range(NSA_HPG):
        s_ref[h] = biased[h]

    def slc_chunk(c, carry):
        prev, m_old, m_new, l = carry
        scores = qk(sk_ref, c)
        l = stage_b([s_ref[h] for h in range(NSA_HPG)], svT_ref[0, prev], m_old, m_new, l)
        biased, m_next = stage_a(scores, functools.partial(block_bias, c), m_new)
        for h in range(NSA_HPG):
            s_ref[h] = biased[h]
        return c, m_new, m_next, l

    prev, m_old, m_new, l = lax.fori_loop(0, i, slc_chunk, (i, m0, m_new, l0))

    c_far = jnp.maximum(i - 2, 0)
    c_near = jnp.maximum(i - 1, 0)
    pen_far = jnp.where(i >= 2, 0.0, NEG_INF).astype(F32)
    pen_near = jnp.where(i >= 1, 0.0, NEG_INF).astype(F32)

    scores = qk(wk_ref, c_far)
    l = stage_b([s_ref[h] for h in range(NSA_HPG)], svT_ref[0, prev], m_old, m_new, l)
    b_far, m_far = stage_a(scores, lambda s: s + (hi_ref[...] + pen_far), m0)
    finish(1, l)

    acc_ref[...] = jnp.zeros(acc_ref.shape, F32)
    scores = qk(wk_ref, c_near)
    l = stage_b(b_far, wvT_ref[0, c_far], m0, m_far, l0)
    b_near, m_near = stage_a(scores, lambda s: s + pen_near, m_far)
    scores = qk(wk_ref, i)
    l = stage_b(b_near, wvT_ref[0, c_near], m_far, m_near, l)
    b_diag, m_diag = stage_a(scores, lambda s: s + lo_ref[...], m_near)
    l = stage_b(b_diag, wvT_ref[0, i], m_near, m_diag, l)
    finish(2, l)

    tot = tot_ref[...]
    o_ref[0] = jnp.concatenate([tot[:, h * tq:(h + 1) * tq] for h in range(NSA_HPG)],
                               axis=0).T.astype(BF16)


def _nsa(qT, kc, vcT, ov, sk, svT, wk, wvT, gT):
    B, _, T = gT.shape
    tq = NSA_TQ
    assert NSA_KC == tq and WINDOW == 2 * NSA_KC
    nr = kc.shape[2]
    nch = T // NSA_KC
    kk = np.arange(NSA_KC)[:, None]
    tt = np.arange(tq)[None, :]
    lo = jnp.asarray(np.where(kk <= tt, 0.0, NEG_INF), dtype=F32)
    hi = jnp.asarray(np.where(kk > tt, 0.0, NEG_INF), dtype=F32)
    in_specs = [
        pl.BlockSpec((1, 1, 1, NSA_HD, NSA_HPG * tq), lambda b, g, i: (b, g, i, 0, 0)),
        pl.BlockSpec((1, 1, nr, NSA_HD), lambda b, g, i: (b, g, 0, 0)),
        pl.BlockSpec((1, 1, NSA_HD, nr), lambda b, g, i: (b, g, 0, 0)),
        pl.BlockSpec(ov.shape, lambda b, g, i: (0, 0)),
        pl.BlockSpec((1, 1, T, NSA_HD), lambda b, g, i: (b, g, 0, 0)),
        pl.BlockSpec((1, nch, NSA_HD, NSA_KC), lambda b, g, i: (b, 0, g, 0)),
        pl.BlockSpec((1, 1, T, NSA_HD), lambda b, g, i: (b, g, 0, 0)),
        pl.BlockSpec((1, nch, NSA_HD, NSA_KC), lambda b, g, i: (b, 0, g, 0)),
        pl.BlockSpec((1, 32, tq), lambda b, g, i: (b, 0, i)),
        pl.BlockSpec(lo.shape, lambda b, g, i: (0, 0)),
        pl.BlockSpec(hi.shape, lambda b, g, i: (0, 0)),
    ]
    n_sel = T // SLC_BLOCK
    hq = NSA_HPG * tq
    return pl.pallas_call(
        _nsa_body, out_shape=jax.ShapeDtypeStruct((B, T, 512), BF16),
        grid=(B, NSA_GROUPS, T // tq), in_specs=in_specs,
        out_specs=pl.BlockSpec((1, tq, 256), lambda b, g, i: (b, i, g)),
        scratch_shapes=[pltpu.VMEM((n_sel, tq), F32), pltpu.VMEM((n_sel, tq), F32),
                        pltpu.VMEM((NSA_HD, hq), F32), pltpu.VMEM((NSA_HD, hq), F32),
                        pltpu.VMEM((NSA_HPG, NSA_KC, tq), F32)],
        compiler_params=_cparams(("parallel", "parallel", "parallel")), name="nsa",
    )(qT, kc, vcT, ov, sk, svT, wk, wvT, gT, lo, hi)


def _ret_body(q_ref, kT_ref, v_ref, dec_ref, xi_ref, zeta_ref, cd_ref, gg_ref, gb_ref, o_ref, r_ref):
    C = RET_CHUNK
    n_chunks = q_ref.shape[1] // C
    r_ref[...] = jnp.zeros(r_ref.shape, F32)
    dec = dec_ref[0]
    xi = xi_ref[0]
    zeta = zeta_ref[0]
    cd = cd_ref[0]
    gg = gg_ref[...]
    gb = gb_ref[...]

    def chunk(n, carry):
        rows = pl.ds(pl.multiple_of(n * C, C), C)
        qc = q_ref[0, rows, :]
        kT = kT_ref[0, n]
        vc = v_ref[0, rows, :]
        r_old = r_ref[...]
        s = _nn(qc, kT.astype(BF16)) * dec
        o = _nn(s.astype(BF16), vc) + _nn(qc, r_old.astype(BF16)) * xi
        r_ref[...] = r_old * cd + _nn((kT * zeta).astype(BF16), vc)
        mu = jnp.mean(o, axis=-1, keepdims=True)
        var = jnp.mean(jnp.square(o - mu), axis=-1, keepdims=True)
        o_ref[0, rows, :] = (o - mu) * lax.rsqrt(var + GN_EPS) * gg + gb
        return carry

    lax.fori_loop(0, n_chunks, chunk, 0)


def _retention(rq, rkT, rv, dec, xi, zeta, cd, gn_g, gn_b):
    B, T, _ = rq.shape
    nch = T // RET_CHUNK
    in_specs = [
        pl.BlockSpec((1, T, RET_DK), lambda b, h: (b, 0, h)),
        pl.BlockSpec((1, nch, RET_DK, RET_CHUNK), lambda b, h: (b, 0, h, 0)),
        pl.BlockSpec((1, T, RET_DV), lambda b, h: (b, 0, h)),
        pl.BlockSpec((1, RET_CHUNK, RET_CHUNK), lambda b, h: (h, 0, 0)),
        pl.BlockSpec((1, RET_CHUNK, RET_DV), lambda b, h: (h, 0, 0)),
        pl.BlockSpec((1, 1, RET_CHUNK), lambda b, h: (h, 0, 0)),
        pl.BlockSpec((1, 1, RET_DV), lambda b, h: (h, 0, 0)),
        pl.BlockSpec((1, RET_DV), lambda b, h: (0, h)),
        pl.BlockSpec((1, RET_DV), lambda b, h: (0, h)),
    ]
    return pl.pallas_call(
        _ret_body, out_shape=jax.ShapeDtypeStruct((B, T, RET_HEADS * RET_DV), F32),
        grid=(B, RET_HEADS), in_specs=in_specs,
        out_specs=pl.BlockSpec((1, T, RET_DV), lambda b, h: (b, 0, h)),
        scratch_shapes=[pltpu.VMEM((RET_DK, RET_DV), F32)],
        compiler_params=_cparams(("parallel", "parallel")), name="retention",
    )(rq, rkT, rv, dec, xi, zeta, cd, gn_g, gn_b)


def _layer_norm(y, g, b):
    mu = jnp.mean(y, axis=-1, keepdims=True)
    var = jnp.mean(jnp.square(y - mu), axis=-1, keepdims=True)
    return (y - mu) * lax.rsqrt(var + LN_EPS) * g + b


def _merge_body(alpha, x_ref, oa_ref, or_ref, wrg_ref, wmg_ref, wua_ref, wur_ref, wo_ref, g1_ref, b1_ref,
                wrh_ref, wrl_ref, rb_ref, x1_ref, eid_ref, wgt_ref):
    x = x_ref[...]
    xb = x.astype(BF16)
    rgate = jax.nn.silu(_nn(xb, wrg_ref[...]))
    o_ret = (or_ref[...] * rgate).astype(BF16)
    a = _nn(oa_ref[...], wua_ref[...])
    r = _nn(o_ret, wur_ref[...])
    mg = jax.nn.sigmoid(_nn(xb, wmg_ref[...]))
    merged = mg[:, :D_MODEL] * a + mg[:, D_MODEL:] * r
    mix = _nn(merged.astype(BF16), wo_ref[...])
    x1 = _layer_norm(alpha * x + mix, g1_ref[...], b1_ref[...])
    x1_ref[...] = x1

    xh = x1.astype(BF16)
    xl = (x1 - xh.astype(F32)).astype(BF16)
    wh = wrh_ref[...]
    lg = _nt(wh, xh) + _nt(wh, xl) + _nt(wrl_ref[...], xh) + rb_ref[...]
    ne = N_EXPERTS
    gl = lg[ne:ne + MOE_GROUPS]
    ge = jnp.exp(gl - jnp.max(gl, axis=0, keepdims=True))
    pg = ge / jnp.sum(ge, axis=0, keepdims=True)
    g_prob = jnp.max(pg, axis=0, keepdims=True)
    gi = lax.broadcasted_iota(I32, pg.shape, 0)
    g_idx = jnp.min(jnp.where(pg == g_prob, gi, MOE_GROUPS), axis=0, keepdims=True)
    inner = jnp.zeros((EXPERTS_PER_GROUP, lg.shape[1]), F32)
    for gq in range(MOE_GROUPS):
        inner = inner + jnp.where(g_idx == gq, lg[8 * gq:8 * (gq + 1)], 0.0)
    ei = lax.broadcasted_iota(I32, inner.shape, 0)
    m1 = jnp.max(inner, axis=0, keepdims=True)
    i1 = jnp.min(jnp.where(inner == m1, ei, EXPERTS_PER_GROUP), axis=0, keepdims=True)
    rest = jnp.where(ei == i1, -jnp.inf, inner)
    m2 = jnp.max(rest, axis=0, keepdims=True)
    i2 = jnp.min(jnp.where(rest == m2, ei, EXPERTS_PER_GROUP), axis=0, keepdims=True)
    e2 = jnp.exp(m2 - m1)
    den = 1.0 + e2
    w1 = (1.0 / den) * g_prob
    w2 = (e2 / den) * g_prob
    zi = jnp.zeros((6, lg.shape[1]), I32)
    eid_ref[...] = jnp.concatenate([g_idx * EXPERTS_PER_GROUP + i1, g_idx * EXPERTS_PER_GROUP + i2, zi], axis=0)
    wgt_ref[...] = jnp.concatenate([w1, w2, jnp.zeros((6, lg.shape[1]), F32)], axis=0)


def _merge(alpha, x2, oa2, or2, wrg, wmg, wua, wur, wo, g1, b1, wrh, wrl, rb):
    M, D = x2.shape
    tm = MERGE_TM
    full = lambda a: pl.BlockSpec(a.shape, lambda i: (0,) * a.ndim)
    in_specs = [pl.BlockSpec((tm, D), lambda i: (i, 0)),
                pl.BlockSpec((tm, oa2.shape[1]), lambda i: (i, 0)),
                pl.BlockSpec((tm, or2.shape[1]), lambda i: (i, 0))] + [
        full(a) for a in (wrg, wmg, wua, wur, wo, g1, b1, wrh, wrl, rb)]
    out_shape = [jax.ShapeDtypeStruct((M, D), F32),
                 jax.ShapeDtypeStruct((8, M), I32),
                 jax.ShapeDtypeStruct((8, M), F32)]
    out_specs = [pl.BlockSpec((tm, D), lambda i: (i, 0)),
                 pl.BlockSpec((8, tm), lambda i: (0, i)),
                 pl.BlockSpec((8, tm), lambda i: (0, i))]
    return pl.pallas_call(
        functools.partial(_merge_body, alpha), out_shape=out_shape, grid=(M // tm,),
        in_specs=in_specs, out_specs=out_specs,
        compiler_params=_cparams(("parallel",)), name="merge",
    )(x2, oa2, or2, wrg, wmg, wua, wur, wo, g1, b1, wrh, wrl, rb)


def _rank_body(eid_ref, tri_ref, rank_ref, cnt_ref, carry_ref):
    j = pl.program_id(0)

    @pl.when(j == 0)
    def _():
        carry_ref[...] = jnp.zeros(carry_ref.shape, F32)

    e = eid_ref[...]
    rows = lax.broadcasted_iota(I32, (N_EXPERTS, e.shape[1]), 0)
    hit = rows == e
    onehot = jnp.where(hit, 1.0, 0.0).astype(BF16)
    incl = _nn(onehot, tri_ref[...])
    carry = carry_ref[:, 0:1]
    rank = jnp.sum(jnp.where(hit, incl - 1.0 + carry, 0.0), axis=0, keepdims=True)
    rank_ref[...] = rank.astype(I32)
    carry_ref[...] = carry_ref[...] + jnp.sum(jnp.where(hit, 1.0, 0.0), axis=1, keepdims=True)
    cnt_ref[...] = carry_ref[...]


def _rank(eid_flat, tri):
    n = eid_flat.shape[1]
    tb = RANK_TB
    return pl.pallas_call(
        _rank_body,
        out_shape=[jax.ShapeDtypeStruct((1, n), I32), jax.ShapeDtypeStruct((N_EXPERTS, 128), F32)],
        grid=(n // tb,),
        in_specs=[pl.BlockSpec((1, tb), lambda j: (0, j)), pl.BlockSpec((tb, tb), lambda j: (0, 0))],
        out_specs=[pl.BlockSpec((1, tb), lambda j: (0, j)), pl.BlockSpec((N_EXPERTS, 128), lambda j: (0, 0))],
        scratch_shapes=[pltpu.VMEM((N_EXPERTS, 128), F32)],
        compiler_params=_cparams(("arbitrary",)), name="rank",
    )(eid_flat, tri)


def _dispatch_body(pos_ref, x_ref, xs_in_ref, xs_ref, sem):
    del xs_in_ref
    tm = DISPATCH_TM
    m_tok = pos_ref.shape[0] // 2
    base = pl.program_id(0) * tm

    def row_copy(r, p):
        return pltpu.make_async_copy(x_ref.at[pl.ds(r, 1), :], xs_ref.at[pl.ds(p, 1), :], sem)

    def issue(r, carry):
        row_copy(r, pos_ref[base + r]).start()
        row_copy(r, pos_ref[m_tok + base + r]).start()
        return carry

    lax.fori_loop(0, tm, issue, 0)

    def drain(r, carry):
        row_copy(0, 0).wait()
        row_copy(0, 0).wait()
        return carry

    lax.fori_loop(0, tm, drain, 0)


def _dispatch(pos, x1, xs_init):
    M, D = x1.shape
    tm = DISPATCH_TM
    grid_spec = pltpu.PrefetchScalarGridSpec(
        num_scalar_prefetch=1, grid=(M // tm,),
        in_specs=[pl.BlockSpec((tm, D), lambda i, pos: (i, 0)),
                  pl.BlockSpec(memory_space=pl.ANY)],
        out_specs=pl.BlockSpec(memory_space=pl.ANY),
        scratch_shapes=[pltpu.SemaphoreType.DMA],
    )
    return pl.pallas_call(
        _dispatch_body, out_shape=jax.ShapeDtypeStruct(xs_init.shape, F32), grid_spec=grid_spec,
        input_output_aliases={2: 0},
        compiler_params=_cparams(("arbitrary",), has_side_effects=True), name="dispatch",
    )(pos, x1, xs_init)


def _experts_body(te_ref, nu_ref, xs_ref, wg_ref, wu_ref, wd_ref, ys_ref):
    i = pl.program_id(0)

    @pl.when(i < nu_ref[0])
    def _():
        xb = xs_ref[...].astype(BF16)
        hg = _nn(xb, wg_ref[0].astype(BF16))
        hu = _nn(xb, wu_ref[0].astype(BF16))
        h = (jax.nn.silu(hg) * hu).astype(BF16)
        ys_ref[...] = _nn(h, wd_ref[0].astype(BF16))

    @pl.when(i >= nu_ref[0])
    def _():
        ys_ref[...] = jnp.zeros(ys_ref.shape, F32)


def _experts(tile_expert, n_used, xs, wg, wu, wd):
    npad, D = xs.shape
    tm = EXPERT_TM
    grid_spec = pltpu.PrefetchScalarGridSpec(
        num_scalar_prefetch=2, grid=(npad // tm,),
        in_specs=[pl.BlockSpec((tm, D), lambda i, te, nu: (i, 0)),
                  pl.BlockSpec((1, D, D_FF), lambda i, te, nu: (te[i], 0, 0)),
                  pl.BlockSpec((1, D, D_FF), lambda i, te, nu: (te[i], 0, 0)),
                  pl.BlockSpec((1, D_FF, D), lambda i, te, nu: (te[i], 0, 0))],
        out_specs=pl.BlockSpec((tm, D), lambda i, te, nu: (i, 0)),
    )
    return pl.pallas_call(
        _experts_body, out_shape=jax.ShapeDtypeStruct((npad, D), F32), grid_spec=grid_spec,
        compiler_params=_cparams(("arbitrary",)), name="experts",
    )(tile_expert, n_used, xs, wg, wu, wd)


def _combine_body(alpha, pos_ref, x1_ref, w_ref, eye_ref, g2_ref, b2_ref, ys_ref, o_ref, buf_ref, sem):
    tm = COMBINE_TM
    m_tok = pos_ref.shape[0] // 2
    base = pl.program_id(0) * tm

    def row_copy(slot, r, p):
        return pltpu.make_async_copy(ys_ref.at[pl.ds(p, 1), :], buf_ref.at[slot, pl.ds(r, 1), :], sem)

    def issue(r, carry):
        row_copy(0, r, pos_ref[base + r]).start()
        row_copy(1, r, pos_ref[m_tok + base + r]).start()
        return carry

    lax.fori_loop(0, tm, issue, 0)

    w = w_ref[...]
    eye = eye_ref[...]
    w_a = w.astype(BF16)
    w_b = (w - w_a.astype(F32)).astype(BF16)
    w_c = (w - w_a.astype(F32) - w_b.astype(F32)).astype(BF16)
    wcol = _nt(eye, w_a) + _nt(eye, w_b) + _nt(eye, w_c)

    def drain(r, carry):
        row_copy(0, 0, 0).wait()
        row_copy(1, 0, 0).wait()
        return carry

    lax.fori_loop(0, tm, drain, 0)
    moe = buf_ref[0] * wcol[:, 0:1] + buf_ref[1] * wcol[:, 1:2]
    o_ref[...] = _layer_norm(alpha * x1_ref[...] + moe, g2_ref[...], b2_ref[...])


def _combine(alpha, pos, x1, wgt, eye, g2, b2, ys):
    M, D = x1.shape
    tm = COMBINE_TM
    grid_spec = pltpu.PrefetchScalarGridSpec(
        num_scalar_prefetch=1, grid=(M // tm,),
        in_specs=[pl.BlockSpec((tm, D), lambda i, pos: (i, 0)),
                  pl.BlockSpec((8, tm), lambda i, pos: (0, i)),
                  pl.BlockSpec((tm, tm), lambda i, pos: (0, 0)),
                  pl.BlockSpec((1, D), lambda i, pos: (0, 0)),
                  pl.BlockSpec((1, D), lambda i, pos: (0, 0)),
                  pl.BlockSpec(memory_space=pl.ANY)],
        out_specs=pl.BlockSpec((tm, D), lambda i, pos: (i, 0)),
        scratch_shapes=[pltpu.VMEM((2, tm, D), F32), pltpu.SemaphoreType.DMA],
    )
    return pl.pallas_call(
        functools.partial(_combine_body, alpha), out_shape=jax.ShapeDtypeStruct((M, D), F32),
        grid_spec=grid_spec, compiler_params=_cparams(("arbitrary",)), name="combine",
    )(pos, x1, wgt, eye, g2, b2, ys)


def _rope_tables(T, dim):
    half = dim // 2
    inv_freq = ROPE_THETA ** (-jnp.arange(half, dtype=F32) * 2.0 / dim)
    return inv_freq


def _tables(T):
    pos = jnp.arange(T).astype(F32)
    f64 = _rope_tables(T, NSA_HD)
    ang = pos[:, None] * f64[None, :]
    cos, sin = jnp.cos(ang), jnp.sin(ang)
    c64 = jnp.tile(cos, (1, 4))
    s64 = jnp.tile(jnp.concatenate([-sin, sin], axis=1), (1, 2))
    f128 = _rope_tables(T, RET_DK)
    ang2 = pos[:, None] * f128[None, :]
    cos2, sin2 = jnp.cos(ang2), jnp.sin(ang2)
    c128 = jnp.tile(cos2, (1, 2))
    s128 = jnp.concatenate([-sin2, sin2], axis=1)
    n_rows = T // CMP_STRIDE
    cmp_end = (jnp.arange(n_rows) * CMP_STRIDE + CMP_BLOCK - 1).astype(F32)
    angc = cmp_end[:, None] * f64[None, :]
    cc = jnp.tile(jnp.cos(angc), (1, 2))
    sc = jnp.tile(jnp.sin(angc), (1, 2))
    return (c64, s64, c128, s128, cos.T, sin.T, cos2.T, sin2.T), cc, sc


def _overlap_matrix(T):
    n_rows = T // CMP_STRIDE
    n_sel = T // SLC_BLOCK
    cmp_start = np.arange(n_rows) * CMP_STRIDE
    sel_start = np.arange(n_sel) * SLC_BLOCK
    ov = np.clip(np.minimum(cmp_start[None, :] + CMP_BLOCK, sel_start[:, None] + SLC_BLOCK)
                 - np.maximum(cmp_start[None, :], sel_start[:, None]), 0, None)
    return jnp.asarray(ov.astype(np.float32) / CMP_STRIDE, dtype=BF16)


def _retention_tables():
    C = RET_CHUNK
    gamma = 1.0 - 2.0 ** (-5.0 - jnp.arange(RET_HEADS, dtype=F32))
    log_g = jnp.log(gamma)
    i = jnp.arange(C, dtype=F32)
    diff = i[:, None] - i[None, :]
    dec = jnp.where(diff >= 0, jnp.exp(jnp.maximum(diff, 0.0) * log_g[:, None, None]), 0.0)
    xi = jnp.exp((i + 1.0) * log_g[:, None])
    zeta = jnp.exp((C - 1.0 - i) * log_g[:, None])
    cd = jnp.exp(C * log_g)
    xi_b = jnp.broadcast_to(xi[:, :, None], (RET_HEADS, C, RET_DV))
    cd_b = jnp.broadcast_to(cd[:, None, None], (RET_HEADS, 1, RET_DV))
    return dec, xi_b, zeta[:, None, :], cd_b


def _rot_half_cols(w):
    half = w.shape[-1] // 2
    return jnp.concatenate([-w[..., half:], w[..., :half]], axis=-1)


def kernel(x, w_in, cmp_pos_k, cmp_k_w1, cmp_k_b1, cmp_k_w2, cmp_k_b2, cmp_pos_v, cmp_v_w1, cmp_v_b1, cmp_v_w2, cmp_v_b2, ret_gn_g, ret_gn_b, w_up_attn, w_up_ret, w_out, ln1_g, ln1_b, router_group_w, router_group_b, router_inner_w, router_inner_b, expert_w_gate, expert_w_up, expert_w_down, ln2_g, ln2_b):
    B, T, D = x.shape
    M = B * T
    depth = w_in.shape[0]
    alpha = (2.0 * depth) ** 0.25
    tabs, cc, sc = _tables(T)
    ov = _overlap_matrix(T)
    dec, xi_b, zeta, cd_b = _retention_tables()
    tri = jnp.asarray(np.triu(np.ones((RANK_TB, RANK_TB), np.float32)), dtype=BF16)
    eye = jnp.asarray(np.eye(COMBINE_TM, dtype=np.float32), dtype=BF16)
    n_pad = 2 * M + N_EXPERTS * EXPERT_TM
    n_tiles = n_pad // EXPERT_TM

    for l in range(depth):
        w = w_in[l]
        col = lambda n: w[:, _OFF[n][0]:_OFF[n][1]]
        wn = jnp.concatenate([col("cmp_k"), col("cmp_v"), col("slc_k"), col("win_k"), col("ret_q"), col("ret_v")],
                             axis=1).astype(BF16)
        wt = jnp.concatenate([col("nsa_q"), col("slc_v"), col("win_v"), col("ret_k"), col("nsa_gate"),
                              jnp.zeros((D, 8), F32)], axis=1).T.astype(BF16)
        qT, ckv, sk, wk, svT, wvT, gT, rq, rkT, rv = _proj(x, wn, wt, tabs)

        kc, vcT = _compress(
            ckv, cmp_pos_k[l].reshape(2, -1), cmp_pos_v[l].reshape(2, -1),
            cmp_k_w1[l], cmp_k_b1[l][None, :], cmp_k_w2[l], _rot_half_cols(cmp_k_w2[l]),
            cmp_k_b2[l][None, :], _rot_half_cols(cmp_k_b2[l])[None, :], cc, sc,
            cmp_v_w1[l], cmp_v_b1[l][None, :], cmp_v_w2[l].T, cmp_v_b2[l][:, None])
        o_attn = _nsa(qT, kc, vcT, ov, sk, svT, wk, wvT, gT)
        o_ret = _retention(rq, rkT, rv, dec, xi_b, zeta, cd_b, ret_gn_g[l][None, :], ret_gn_b[l][None, :])

        wr = jnp.concatenate([router_inner_w[l].transpose(0, 2, 1).reshape(N_EXPERTS, D),
                              router_group_w[l].T, jnp.zeros((4, D), F32)], axis=0)
        wrh = wr.astype(BF16)
        wrl = (wr - wrh.astype(F32)).astype(BF16)
        rb = jnp.concatenate([router_inner_b[l].reshape(-1), router_group_b[l], jnp.zeros((4,), F32)])[:, None]
        x1, eid, wgt = _merge(
            alpha, x.reshape(M, D), o_attn.reshape(M, -1), o_ret.reshape(M, -1),
            col("ret_gate").astype(BF16), col("merge_gate").astype(BF16), w_up_attn[l].astype(BF16),
            w_up_ret[l].astype(BF16), w_out[l].astype(BF16), ln1_g[l][None, :], ln1_b[l][None, :], wrh, wrl, rb)

        eid_flat = eid[:2].reshape(1, 2 * M)
        rank, cnt = _rank(eid_flat, tri)
        counts = cnt[:, 0].astype(I32)
        tiles_per = (counts + EXPERT_TM - 1) // EXPERT_TM
        tile_end = jnp.cumsum(tiles_per)
        row_start = (tile_end - tiles_per) * EXPERT_TM
        pos = (row_start[eid_flat[0]] + rank[0]).astype(I32)
        tile_ids = jnp.arange(n_tiles, dtype=I32)
        tile_expert = jnp.minimum(jnp.sum((tile_end[None, :] <= tile_ids[:, None]).astype(I32), axis=1),
                                  N_EXPERTS - 1).astype(I32)
        n_used = tile_end[-1:].astype(I32)

        xs = _dispatch(pos, x1, jnp.zeros((n_pad, D), F32))
        ys = _experts(tile_expert, n_used, xs, expert_w_gate[l], expert_w_up[l], expert_w_down[l])
        x = _combine(alpha, pos, x1, wgt, eye, ln2_g[l][None, :], ln2_b[l][None, :], ys).reshape(B, T, D)
    return x
```

```python
import functools

import numpy as np
import jax
import jax.numpy as jnp
from jax import lax
from jax.experimental import pallas as pl
from jax.experimental.pallas import tpu as pltpu

F32 = jnp.float32
BF16 = jnp.bfloat16
I32 = jnp.int32

D_MODEL = 1024
NSA_HEADS = 8
NSA_HD = 64
NSA_GROUPS = 2
NSA_HPG = NSA_HEADS // NSA_GROUPS
CMP_BLOCK = 32
CMP_STRIDE = 16
CMP_HIDDEN = 256
SLC_BLOCK = 64
SLC_TOPK = 16
WINDOW = 512
RET_HEADS = 4
RET_DK = 128
RET_DV = 256
RET_CHUNK = 128
MOE_GROUPS = 4
EXPERTS_PER_GROUP = 8
N_EXPERTS = MOE_GROUPS * EXPERTS_PER_GROUP
D_FF = 512
ROPE_THETA = 10000.0
LN_EPS = 1e-5
GN_EPS = 1e-5
NEG_INF = -1e30
LOG2_E = 1.4426950408889634

VMEM_LIMIT_V7X = 56 * 1024 * 1024

_OFF = {}
_o = 0
for _n, _w in (("nsa_q", 512), ("cmp_k", 128), ("cmp_v", 128), ("slc_k", 128), ("slc_v", 128),
               ("win_k", 128), ("win_v", 128), ("nsa_gate", 24), ("ret_q", 512), ("ret_k", 512),
               ("ret_v", 1024), ("ret_gate", 1024), ("merge_gate", 2048)):
    _OFF[_n] = (_o, _o + _w)
    _o += _w

PROJ_TM = 512
NSA_TQ = 256
NSA_KC = 256
RET_STEP_CHUNKS = 4
MERGE_TM = 512
RANK_TB = 512
DISPATCH_TM = 512
EXPERT_TM = 256
COMBINE_TM = 256


def _cparams(sem, **kw):
    return pltpu.CompilerParams(dimension_semantics=sem, vmem_limit_bytes=VMEM_LIMIT_V7X, **kw)


def _nt(a, b):
    return lax.dot_general(a, b, (((1,), (1,)), ((), ())), preferred_element_type=F32)


def _nn(a, b):
    return jnp.dot(a, b, preferred_element_type=F32)


def _proj_body(x_ref, wn_ref, wt_ref, c64_ref, s64_ref, c128_ref, s128_ref,
               ct64_ref, st64_ref, ct128_ref, st128_ref,
               qT_ref, ckv_ref, sk_ref, wk_ref, svT_ref, wvT_ref, gT_ref, rq_ref, rkT_ref, rv_ref):
    tm = PROJ_TM
    xb = x_ref[0].astype(BF16)

    def nn(a, b):
        return _nn(xb, wn_ref[:, a:b])

    def nt(a, b):
        return _nt(wt_ref[a:b, :], xb)

    ckv = nn(0, 256)
    for j in range(4):
        ckv_ref[0, j] = ckv[:, 64 * j:64 * (j + 1)]

    lane = lax.broadcasted_iota(I32, (tm, 128), 1)
    first = (lane % 64) < 32
    c64 = c64_ref[...]
    s64 = s64_ref[...]

    def rope64(k):
        rot = jnp.where(first, pltpu.roll(k, 96, 1), pltpu.roll(k, 32, 1))
        return k * c64 + rot * s64

    sk = rope64(nn(256, 384)).astype(BF16)
    sk_ref[0, 0] = sk[:, :64]
    sk_ref[0, 1] = sk[:, 64:]
    wk = rope64(nn(384, 512)).astype(BF16)
    wk_ref[0, 0] = wk[:, :64]
    wk_ref[0, 1] = wk[:, 64:]

    c128 = c128_ref[...]
    s128 = s128_ref[...]
    rq = nn(512, 1024)
    for h in range(RET_HEADS):
        ch = rq[:, 128 * h:128 * (h + 1)]
        rq_ref[0, :, 128 * h:128 * (h + 1)] = (ch * c128 + pltpu.roll(ch, 64, 1) * s128).astype(BF16)
    rv_ref[0] = nn(1024, 2048).astype(BF16)

    ct = ct64_ref[...]
    st = st64_ref[...]
    qT = nt(0, 512)
    scale_q = NSA_HD ** -0.5 * LOG2_E
    tq = NSA_TQ
    for hh in range(NSA_HEADS):
        g, h = divmod(hh, NSA_HPG)
        x1 = qT[64 * hh:64 * hh + 32]
        x2 = qT[64 * hh + 32:64 * hh + 64]
        o1 = ((x1 * ct - x2 * st) * scale_q).astype(BF16)
        o2 = ((x1 * st + x2 * ct) * scale_q).astype(BF16)
        for it in range(tm // tq):
            qT_ref[0, g, it, 0:32, h * tq:(h + 1) * tq] = o1[:, it * tq:(it + 1) * tq]
            qT_ref[0, g, it, 32:64, h * tq:(h + 1) * tq] = o2[:, it * tq:(it + 1) * tq]

    svT = nt(512, 640).astype(BF16)
    wvT = nt(640, 768).astype(BF16)
    for c in range(tm // NSA_KC):
        svT_ref[0, c] = svT[:, NSA_KC * c:NSA_KC * (c + 1)]
        wvT_ref[0, c] = wvT[:, NSA_KC * c:NSA_KC * (c + 1)]

    ct2 = ct128_ref[...]
    st2 = st128_ref[...]
    rkT = nt(768, 1280)
    scale_k = RET_DK ** -0.5
    for h in range(RET_HEADS):
        x1 = rkT[128 * h:128 * h + 64]
        x2 = rkT[128 * h + 64:128 * h + 128]
        o1 = (x1 * ct2 - x2 * st2) * scale_k
        o2 = (x1 * st2 + x2 * ct2) * scale_k
        for c in range(tm // RET_CHUNK):
            rkT_ref[0, c, 128 * h:128 * h + 64, :] = o1[:, 128 * c:128 * (c + 1)]
            rkT_ref[0, c, 128 * h + 64:128 * h + 128, :] = o2[:, 128 * c:128 * (c + 1)]

    gT_ref[0] = jax.nn.sigmoid(nt(1280, 1312))


def _proj(x, wn, wt, tabs):
    B, T, D = x.shape
    tm = PROJ_TM
    c64, s64, c128, s128, ct64, st64, ct128, st128 = tabs
    const = lambda b, i: (0, 0)
    in_specs = [
        pl.BlockSpec((1, tm, D), lambda b, i: (b, i, 0)),
        pl.BlockSpec(wn.shape, const),
        pl.BlockSpec(wt.shape, const),
        pl.BlockSpec((tm, 128), lambda b, i: (i, 0)),
        pl.BlockSpec((tm, 128), lambda b, i: (i, 0)),
        pl.BlockSpec((tm, 128), lambda b, i: (i, 0)),
        pl.BlockSpec((tm, 128), lambda b, i: (i, 0)),
        pl.BlockSpec((32, tm), lambda b, i: (0, i)),
        pl.BlockSpec((32, tm), lambda b, i: (0, i)),
        pl.BlockSpec((64, tm), lambda b, i: (0, i)),
        pl.BlockSpec((64, tm), lambda b, i: (0, i)),
    ]
    out_shape = [
        jax.ShapeDtypeStruct((B, NSA_GROUPS, T // NSA_TQ, NSA_HD, NSA_HPG * NSA_TQ), BF16),
        jax.ShapeDtypeStruct((B, 4, T, 64), F32),
        jax.ShapeDtypeStruct((B, 2, T, 64), BF16),
        jax.ShapeDtypeStruct((B, 2, T, 64), BF16),
        jax.ShapeDtypeStruct((B, T // NSA_KC, 128, NSA_KC), BF16),
        jax.ShapeDtypeStruct((B, T // NSA_KC, 128, NSA_KC), BF16),
        jax.ShapeDtypeStruct((B, 32, T), F32),
        jax.ShapeDtypeStruct((B, T, 512), BF16),
        jax.ShapeDtypeStruct((B, T // RET_CHUNK, 512, RET_CHUNK), F32),
        jax.ShapeDtypeStruct((B, T, 1024), BF16),
    ]
    out_specs = [
        pl.BlockSpec((1, NSA_GROUPS, tm // NSA_TQ, NSA_HD, NSA_HPG * NSA_TQ), lambda b, i: (b, 0, i, 0, 0)),
        pl.BlockSpec((1, 4, tm, 64), lambda b, i: (b, 0, i, 0)),
        pl.BlockSpec((1, 2, tm, 64), lambda b, i: (b, 0, i, 0)),
        pl.BlockSpec((1, 2, tm, 64), lambda b, i: (b, 0, i, 0)),
        pl.BlockSpec((1, tm // NSA_KC, 128, NSA_KC), lambda b, i: (b, i, 0, 0)),
        pl.BlockSpec((1, tm // NSA_KC, 128, NSA_KC), lambda b, i: (b, i, 0, 0)),
        pl.BlockSpec((1, 32, tm), lambda b, i: (b, 0, i)),
        pl.BlockSpec((1, tm, 512), lambda b, i: (b, i, 0)),
        pl.BlockSpec((1, tm // RET_CHUNK, 512, RET_CHUNK), lambda b, i: (b, i, 0, 0)),
        pl.BlockSpec((1, tm, 1024), lambda b, i: (b, i, 0)),
    ]
    return pl.pallas_call(
        _proj_body, out_shape=out_shape, grid=(B, T // tm), in_specs=in_specs, out_specs=out_specs,
        compiler_params=_cparams(("parallel", "parallel")), name="proj",
    )(x, wn, wt, c64, s64, c128, s128, ct64, st64, ct128, st128)


def _compress_body(uk_ref, uv_ref, posk_ref, posv_ref, w1k_ref, b1k_ref, w2k_ref, w2kr_ref, b2k_ref, b2kr_ref,
                   cc_ref, sc_ref, w1v_ref, b1v_ref, w2vT_ref, b2vT_ref, kc_ref, vcT_ref):
    half = CMP_STRIDE * NSA_HD

    def hidden(u_ref, pos_ref, w1_ref, b1_ref):
        u = u_ref[0, 0]
        top = (u + pos_ref[0:1, :]).astype(BF16)
        bot = (u + pos_ref[1:2, :]).astype(BF16)
        a = _nn(top, w1_ref[0:half, :].astype(BF16))
        bm = _nn(bot, w1_ref[half:2 * half, :].astype(BF16))
        n = bm.shape[0]
        return jax.nn.gelu(a + pltpu.roll(bm, n - 1, 0) + b1_ref[...])

    hk = hidden(uk_ref, posk_ref, w1k_ref, b1k_ref).astype(BF16)
    k = _nn(hk, w2k_ref[...].astype(BF16)) + b2k_ref[...]
    kr = _nn(hk, w2kr_ref[...].astype(BF16)) + b2kr_ref[...]
    kc_ref[0, 0] = (k * cc_ref[...] + kr * sc_ref[...]).astype(BF16)

    hv = hidden(uv_ref, posv_ref, w1v_ref, b1v_ref).astype(BF16)
    vcT_ref[0, 0] = (_nt(w2vT_ref[...].astype(BF16), hv) + b2vT_ref[...]).astype(BF16)


def _compress(ckv, posk, posv, w1k, b1k, w2k, w2kr, b2k, b2kr, cc, sc, w1v, b1v, w2vT, b2vT):
    B = ckv.shape[0]
    T = ckv.shape[2]
    nr = T // CMP_STRIDE
    u = ckv.reshape(B, 4, nr, CMP_STRIDE * NSA_HD)
    full = lambda a: pl.BlockSpec(a.shape, lambda b, g: (0,) * a.ndim)
    in_specs = [
        pl.BlockSpec((1, 1, nr, 1024), lambda b, g: (b, g, 0, 0)),
        pl.BlockSpec((1, 1, nr, 1024), lambda b, g: (b, g + 2, 0, 0)),
    ] + [full(a) for a in (posk, posv, w1k, b1k, w2k, w2kr, b2k, b2kr, cc, sc, w1v, b1v, w2vT, b2vT)]
    out_shape = [jax.ShapeDtypeStruct((B, NSA_GROUPS, nr, NSA_HD), BF16),
                 jax.ShapeDtypeStruct((B, NSA_GROUPS, NSA_HD, nr), BF16)]
    out_specs = [pl.BlockSpec((1, 1, nr, NSA_HD), lambda b, g: (b, g, 0, 0)),
                 pl.BlockSpec((1, 1, NSA_HD, nr), lambda b, g: (b, g, 0, 0))]
    return pl.pallas_call(
        _compress_body, out_shape=out_shape, grid=(B, NSA_GROUPS), in_specs=in_specs, out_specs=out_specs,
        compiler_params=_cparams(("parallel", "parallel")), name="compress",
    )(u, u, posk, posv, w1k, b1k, w2k, w2kr, b2k, b2kr, cc, sc, w1v, b1v, w2vT, b2vT)


def _nsa_body(q_ref, kc_ref, vcT_ref, ov_ref, sk_ref, svT_ref, wk_ref, wvT_ref, gT_ref, lo_ref, hi_ref, o_ref,
              score_ref, bias_ref, acc_ref, tot_ref, s_ref):
    tq = NSA_TQ
    kc_n = NSA_KC
    hq = NSA_HPG * tq
    g = pl.program_id(1)
    i = pl.program_id(2)
    t0 = i * tq
    t_row = t0 + lax.broadcasted_iota(I32, (1, tq), 1)
    t_row4 = t0 + lax.broadcasted_iota(I32, (1, hq), 1) % tq
    n_cmp = kc_ref.shape[2]
    n_sel = ov_ref.shape[0]
    q = q_ref[0, 0, 0]

    def gate4(br):
        return jnp.concatenate([gT_ref[0, pl.ds((g * NSA_HPG + h) * 3 + br, 1), :] for h in range(NSA_HPG)],
                               axis=1)

    cmp_end = lax.broadcasted_iota(I32, (n_cmp, 1), 0) * CMP_STRIDE + (CMP_BLOCK - 1)
    cbias = jnp.where(cmp_end <= t_row4, 0.0, NEG_INF)
    any_valid = jnp.where(t_row4 >= CMP_BLOCK - 1, 1.0, 0.0)
    s = _nn(kc_ref[0, 0], q) + cbias
    e = jnp.exp2(s - jnp.max(s, axis=0, keepdims=True))
    p = e * (any_valid / jnp.sum(e, axis=0, keepdims=True))
    tot_ref[...] = gate4(0) * _nn(vcT_ref[0, 0], p.astype(BF16))
    psum = p[:, 0:tq]
    for h in range(1, NSA_HPG):
        psum = psum + p[:, h * tq:(h + 1) * tq]

    p_hi = psum.astype(BF16)
    p_lo = (psum - p_hi.astype(F32)).astype(BF16)
    ov = ov_ref[...]
    imp = _nn(ov, p_hi) + _nn(ov, p_lo)
    j_col = lax.broadcasted_iota(I32, (n_sel, 1), 0)
    blk_t = t_row // SLC_BLOCK
    forced = (j_col == 0) | (j_col == blk_t) | (j_col == blk_t - 1)
    score = jnp.where(forced, 1e9, jnp.where(j_col <= blk_t, imp, -1e9)).astype(F32)
    score_ref[...] = score
    sub = lax.broadcasted_iota(I32, (8, 1), 0)
    n_slab = n_sel // 8
    slabs = [score[8 * v:8 * (v + 1)] for v in range(n_slab)]
    cnt = [jnp.zeros((8, tq), F32) for _ in range(n_slab)]
    for r in range(n_sel):
        row = jnp.broadcast_to(score_ref[r:r + 1, :], (8, tq))
        for v in range(n_slab):
            if r < 8 * v:
                ahead = jnp.where(row >= slabs[v], 1.0, 0.0)
            elif r >= 8 * (v + 1):
                ahead = jnp.where(row > slabs[v], 1.0, 0.0)
            else:
                ahead = jnp.where(sub + 8 * v > r, jnp.where(row >= slabs[v], 1.0, 0.0),
                                  jnp.where(row > slabs[v], 1.0, 0.0))
            cnt[v] = cnt[v] + ahead
    for v in range(n_slab):
        bias_ref[8 * v:8 * (v + 1), :] = jnp.where(
            cnt[v] < float(SLC_TOPK), jnp.where(sub + 8 * v <= blk_t, 0.0, NEG_INF), NEG_INF)

    def qk(k_ref, c):
        k_c = k_ref[0, 0, pl.ds(pl.multiple_of(c * kc_n, kc_n), kc_n), :]
        return [_nn(k_c, q[:, h * tq:(h + 1) * tq]) for h in range(NSA_HPG)]

    def stage_a(scores, add_bias, m):
        biased = [add_bias(scores[h]) for h in range(NSA_HPG)]
        m_new = tuple(jnp.maximum(m[h], jnp.max(biased[h], axis=0, keepdims=True)) for h in range(NSA_HPG))
        return biased, m_new

    def stage_b(biased, vT_c, m_old, m_new, l):
        l_out = []
        for h in range(NSA_HPG):
            hs = slice(h * tq, (h + 1) * tq)
            alpha = jnp.exp2(m_old[h] - m_new[h])
            p = jnp.exp2(biased[h] - m_new[h])
            acc_ref[:, hs] = alpha * acc_ref[:, hs] + _nn(vT_c, p.astype(BF16))
            l_out.append(alpha * l[h] + jnp.sum(p, axis=0, keepdims=True))
        return tuple(l_out)

    nb = kc_n // SLC_BLOCK

    def block_bias(c, s):
        return jnp.concatenate([s[SLC_BLOCK * b:SLC_BLOCK * (b + 1)] + bias_ref[pl.ds(c * nb + b, 1), :]
                                for b in range(nb)], axis=0)

    def finish(br, l):
        tot_ref[...] = tot_ref[...] + gate4(br) * (acc_ref[...] / jnp.concatenate(l, axis=1))

    m0 = tuple(jnp.full((1, tq), NEG_INF, F32) for _ in range(NSA_HPG))
    l0 = tuple(jnp.zeros((1, tq), F32) for _ in range(NSA_HPG))

    acc_ref[...] = jnp.zeros(acc_ref.shape, F32)
    biased, m_new = stage_a(qk(sk_ref, i), lambda s: block_bias(i, s) + lo_ref[...], m0)
    for h in range(NSA_HPG):
        s_ref[h] = biased[h]

    def slc_chunk(c, carry):
        prev, m_old, m_new, l = carry
        scores = qk(sk_ref, c)
        l = stage_b([s_ref[h] for h in range(NSA_HPG)], svT_ref[0, prev], m_old, m_new, l)
        biased, m_next = stage_a(scores, functools.partial(block_bias, c), m_new)
        for h in range(NSA_HPG):
            s_ref[h] = biased[h]
        return c, m_new, m_next, l

    prev, m_old, m_new, l = lax.fori_loop(0, i, slc_chunk, (i, m0, m_new, l0))

    c_far = jnp.maximum(i - 2, 0)
    c_near = jnp.maximum(i - 1, 0)
    pen_far = jnp.where(i >= 2, 0.0, NEG_INF).astype(F32)
    pen_near = jnp.where(i >= 1, 0.0, NEG_INF).astype(F32)

    scores = qk(wk_ref, c_far)
    l = stage_b([s_ref[h] for h in range(NSA_HPG)], svT_ref[0, prev], m_old, m_new, l)
    b_far, m_far = stage_a(scores, lambda s: s + (hi_ref[...] + pen_far), m0)
    finish(1, l)

    acc_ref[...] = jnp.zeros(acc_ref.shape, F32)
    scores = qk(wk_ref, c_near)
    l = stage_b(b_far, wvT_ref[0, c_far], m0, m_far, l0)
    b_near, m_near = stage_a(scores, lambda s: s + pen_near, m_far)
    scores = qk(wk_ref, i)
    l = stage_b(b_near, wvT_ref[0, c_near], m_far, m_near, l)
    b_diag, m_diag = stage_a(scores, lambda s: s + lo_ref[...], m_near)
    l = stage_b(b_diag, wvT_ref[0, i], m_near, m_diag, l)
    finish(2, l)

    tot = tot_ref[...]
    o_ref[0] = jnp.concatenate([tot[:, h * tq:(h + 1) * tq] for h in range(NSA_HPG)],
                               axis=0).T.astype(BF16)


def _nsa(qT, kc, vcT, ov, sk, svT, wk, wvT, gT):
    B, _, T = gT.shape
    tq = NSA_TQ
    assert NSA_KC == tq and WINDOW == 2 * NSA_KC
    nr = kc.shape[2]
    nch = T // NSA_KC
    kk = np.arange(NSA_KC)[:, None]
    tt = np.arange(tq)[None, :]
    lo = jnp.asarray(np.where(kk <= tt, 0.0, NEG_INF), dtype=F32)
    hi = jnp.asarray(np.where(kk > tt, 0.0, NEG_INF), dtype=F32)
    in_specs = [
        pl.BlockSpec((1, 1, 1, NSA_HD, NSA_HPG * tq), lambda b, g, i: (b, g, i, 0, 0)),
        pl.BlockSpec((1, 1, nr, NSA_HD), lambda b, g, i: (b, g, 0, 0)),
        pl.BlockSpec((1, 1, NSA_HD, nr), lambda b, g, i: (b, g, 0, 0)),
        pl.BlockSpec(ov.shape, lambda b, g, i: (0, 0)),
        pl.BlockSpec((1, 1, T, NSA_HD), lambda b, g, i: (b, g, 0, 0)),
        pl.BlockSpec((1, nch, NSA_HD, NSA_KC), lambda b, g, i: (b, 0, g, 0)),
        pl.BlockSpec((1, 1, T, NSA_HD), lambda b, g, i: (b, g, 0, 0)),
        pl.BlockSpec((1, nch, NSA_HD, NSA_KC), lambda b, g, i: (b, 0, g, 0)),
        pl.BlockSpec((1, 32, tq), lambda b, g, i: (b, 0, i)),
        pl.BlockSpec(lo.shape, lambda b, g, i: (0, 0)),
        pl.BlockSpec(hi.shape, lambda b, g, i: (0, 0)),
    ]
    n_sel = T // SLC_BLOCK
    hq = NSA_HPG * tq
    return pl.pallas_call(
        _nsa_body, out_shape=jax.ShapeDtypeStruct((B, T, 512), BF16),
        grid=(B, NSA_GROUPS, T // tq), in_specs=in_specs,
        out_specs=pl.BlockSpec((1, tq, 256), lambda b, g, i: (b, i, g)),
        scratch_shapes=[pltpu.VMEM((n_sel, tq), F32), pltpu.VMEM((n_sel, tq), F32),
                        pltpu.VMEM((NSA_HD, hq), F32), pltpu.VMEM((NSA_HD, hq), F32),
                        pltpu.VMEM((NSA_HPG, NSA_KC, tq), F32)],
        compiler_params=_cparams(("parallel", "parallel", "parallel")), name="nsa",
    )(qT, kc, vcT, ov, sk, svT, wk, wvT, gT, lo, hi)


def _ret_body(q_ref, kT_ref, v_ref, dec_ref, xi_ref, zeta_ref, cd_ref, gg_ref, gb_ref, o_ref, r_ref):
    C = RET_CHUNK

    @pl.when(pl.program_id(1) == 0)
    def _():
        r_ref[...] = jnp.zeros(r_ref.shape, F32)

    for h in range(RET_HEADS):
        dk = slice(h * RET_DK, (h + 1) * RET_DK)
        dv = slice(h * RET_DV, (h + 1) * RET_DV)
        dec = dec_ref[h]
        xi = xi_ref[h]
        zeta = zeta_ref[h]
        cd = cd_ref[h]
        gg = gg_ref[:, dv]
        gb = gb_ref[:, dv]
        r = r_ref[h]
        for n in range(RET_STEP_CHUNKS):
            rows = slice(n * C, (n + 1) * C)
            qc = q_ref[0, rows, dk]
            kT = kT_ref[0, n, dk, :]
            vc = v_ref[0, rows, dv]
            s = _nn(qc, kT.astype(BF16)) * dec
            o = _nn(s.astype(BF16), vc) + _nn(qc, r.astype(BF16)) * xi
            r = r * cd + _nn((kT * zeta).astype(BF16), vc)
            mu = jnp.mean(o, axis=-1, keepdims=True)
            var = jnp.mean(jnp.square(o - mu), axis=-1, keepdims=True)
            o_ref[0, rows, dv] = (o - mu) * lax.rsqrt(var + GN_EPS) * gg + gb
        r_ref[h] = r


def _retention(rq, rkT, rv, dec, xi, zeta, cd, gn_g, gn_b):
    B, T, _ = rq.shape
    ts = RET_STEP_CHUNKS * RET_CHUNK
    full = lambda a: pl.BlockSpec(a.shape, lambda b, j: (0,) * a.ndim)
    in_specs = [
        pl.BlockSpec((1, ts, RET_HEADS * RET_DK), lambda b, j: (b, j, 0)),
        pl.BlockSpec((1, RET_STEP_CHUNKS, RET_HEADS * RET_DK, RET_CHUNK), lambda b, j: (b, j, 0, 0)),
        pl.BlockSpec((1, ts, RET_HEADS * RET_DV), lambda b, j: (b, j, 0)),
    ] + [full(a) for a in (dec, xi, zeta, cd, gn_g, gn_b)]
    return pl.pallas_call(
        _ret_body, out_shape=jax.ShapeDtypeStruct((B, T, RET_HEADS * RET_DV), F32),
        grid=(B, T // ts), in_specs=in_specs,
        out_specs=pl.BlockSpec((1, ts, RET_HEADS * RET_DV), lambda b, j: (b, j, 0)),
        scratch_shapes=[pltpu.VMEM((RET_HEADS, RET_DK, RET_DV), F32)],
        compiler_params=_cparams(("parallel", "arbitrary")), name="retention",
    )(rq, rkT, rv, dec, xi, zeta, cd, gn_g, gn_b)


def _layer_norm(y, g, b):
    mu = jnp.mean(y, axis=-1, keepdims=True)
    var = jnp.mean(jnp.square(y - mu), axis=-1, keepdims=True)
    return (y - mu) * lax.rsqrt(var + LN_EPS) * g + b


def _merge_body(alpha, x_ref, oa_ref, or_ref, wrg_ref, wmg_ref, wua_ref, wur_ref, wo_ref, g1_ref, b1_ref,
                wrh_ref, wrl_ref, rb_ref, x1_ref, eid_ref, wgt_ref):
    x = x_ref[...]
    xb = x.astype(BF16)
    rgate = jax.nn.silu(_nn(xb, wrg_ref[...]))
    o_ret = (or_ref[...] * rgate).astype(BF16)
    a = _nn(oa_ref[...], wua_ref[...])
    r = _nn(o_ret, wur_ref[...])
    mg = jax.nn.sigmoid(_nn(xb, wmg_ref[...]))
    merged = mg[:, :D_MODEL] * a + mg[:, D_MODEL:] * r
    mix = _nn(merged.astype(BF16), wo_ref[...])
    x1 = _layer_norm(alpha * x + mix, g1_ref[...], b1_ref[...])
    x1_ref[...] = x1

    xh = x1.astype(BF16)
    xl = (x1 - xh.astype(F32)).astype(BF16)
    wh = wrh_ref[...]
    lg = _nt(wh, xh) + _nt(wh, xl) + _nt(wrl_ref[...], xh) + rb_ref[...]
    ne = N_EXPERTS
    gl = lg[ne:ne + MOE_GROUPS]
    ge = jnp.exp(gl - jnp.max(gl, axis=0, keepdims=True))
    pg = ge / jnp.sum(ge, axis=0, keepdims=True)
    g_prob = jnp.max(pg, axis=0, keepdims=True)
    gi = lax.broadcasted_iota(I32, pg.shape, 0)
    g_idx = jnp.min(jnp.where(pg == g_prob, gi, MOE_GROUPS), axis=0, keepdims=True)
    inner = jnp.zeros((EXPERTS_PER_GROUP, lg.shape[1]), F32)
    for gq in range(MOE_GROUPS):
        inner = inner + jnp.where(g_idx == gq, lg[8 * gq:8 * (gq + 1)], 0.0)
    ei = lax.broadcasted_iota(I32, inner.shape, 0)
    m1 = jnp.max(inner, axis=0, keepdims=True)
    i1 = jnp.min(jnp.where(inner == m1, ei, EXPERTS_PER_GROUP), axis=0, keepdims=True)
    rest = jnp.where(ei == i1, -jnp.inf, inner)
    m2 = jnp.max(rest, axis=0, keepdims=True)
    i2 = jnp.min(jnp.where(rest == m2, ei, EXPERTS_PER_GROUP), axis=0, keepdims=True)
    e2 = jnp.exp(m2 - m1)
    den = 1.0 + e2
    w1 = (1.0 / den) * g_prob
    w2 = (e2 / den) * g_prob
    zi = jnp.zeros((6, lg.shape[1]), I32)
    eid_ref[...] = jnp.concatenate([g_idx * EXPERTS_PER_GROUP + i1, g_idx * EXPERTS_PER_GROUP + i2, zi], axis=0)
    wgt_ref[...] = jnp.concatenate([w1, w2, jnp.zeros((6, lg.shape[1]), F32)], axis=0)


def _merge(alpha, x2, oa2, or2, wrg, wmg, wua, wur, wo, g1, b1, wrh, wrl, rb):
    M, D = x2.shape
    tm = MERGE_TM
    full = lambda a: pl.BlockSpec(a.shape, lambda i: (0,) * a.ndim)
    in_specs = [pl.BlockSpec((tm, D), lambda i: (i, 0)),
                pl.BlockSpec((tm, oa2.shape[1]), lambda i: (i, 0)),
                pl.BlockSpec((tm, or2.shape[1]), lambda i: (i, 0))] + [
        full(a) for a in (wrg, wmg, wua, wur, wo, g1, b1, wrh, wrl, rb)]
    out_shape = [jax.ShapeDtypeStruct((M, D), F32),
                 jax.ShapeDtypeStruct((8, M), I32),
                 jax.ShapeDtypeStruct((8, M), F32)]
    out_specs = [pl.BlockSpec((tm, D), lambda i: (i, 0)),
                 pl.BlockSpec((8, tm), lambda i: (0, i)),
                 pl.BlockSpec((8, tm), lambda i: (0, i))]
    return pl.pallas_call(
        functools.partial(_merge_body, alpha), out_shape=out_shape, grid=(M // tm,),
        in_specs=in_specs, out_specs=out_specs,
        compiler_params=_cparams(("parallel",)), name="merge",
    )(x2, oa2, or2, wrg, wmg, wua, wur, wo, g1, b1, wrh, wrl, rb)


def _rank_body(eid_ref, tri_ref, rank_ref, cnt_ref, carry_ref):
    j = pl.program_id(0)

    @pl.when(j == 0)
    def _():
        carry_ref[...] = jnp.zeros(carry_ref.shape, F32)

    e = eid_ref[...]
    rows = lax.broadcasted_iota(I32, (N_EXPERTS, e.shape[1]), 0)
    hit = rows == e
    onehot = jnp.where(hit, 1.0, 0.0).astype(BF16)
    incl = _nn(onehot, tri_ref[...])
    carry = carry_ref[:, 0:1]
    rank = jnp.sum(jnp.where(hit, incl - 1.0 + carry, 0.0), axis=0, keepdims=True)
    rank_ref[...] = rank.astype(I32)
    carry_ref[...] = carry_ref[...] + jnp.sum(jnp.where(hit, 1.0, 0.0), axis=1, keepdims=True)
    cnt_ref[...] = carry_ref[...]


def _rank(eid_flat, tri):
    n = eid_flat.shape[1]
    tb = RANK_TB
    return pl.pallas_call(
        _rank_body,
        out_shape=[jax.ShapeDtypeStruct((1, n), I32), jax.ShapeDtypeStruct((N_EXPERTS, 128), F32)],
        grid=(n // tb,),
        in_specs=[pl.BlockSpec((1, tb), lambda j: (0, j)), pl.BlockSpec((tb, tb), lambda j: (0, 0))],
        out_specs=[pl.BlockSpec((1, tb), lambda j: (0, j)), pl.BlockSpec((N_EXPERTS, 128), lambda j: (0, 0))],
        scratch_shapes=[pltpu.VMEM((N_EXPERTS, 128), F32)],
        compiler_params=_cparams(("arbitrary",)), name="rank",
    )(eid_flat, tri)


def _dispatch_body(pos_ref, ztile_ref, x_ref, xs_ref, zbuf_ref, sem, zsem):
    tm = DISPATCH_TM
    m_tok = pos_ref.shape[0] // 2
    base = pl.program_id(0) * tm

    @pl.when(pl.program_id(0) == 0)
    def _():
        zbuf_ref[...] = jnp.zeros(zbuf_ref.shape, F32)

        def zero_copy(e):
            z = pl.multiple_of(jnp.maximum(ztile_ref[e], 0), EXPERT_TM)
            return pltpu.make_async_copy(zbuf_ref, xs_ref.at[pl.ds(z, EXPERT_TM), :], zsem)

        for e in range(N_EXPERTS):
            @pl.when(ztile_ref[e] >= 0)
            def _():
                zero_copy(e).start()
        for e in range(N_EXPERTS):
            @pl.when(ztile_ref[e] >= 0)
            def _():
                zero_copy(e).wait()

        def tail_copy(t):
            return pltpu.make_async_copy(
                zbuf_ref, xs_ref.at[pl.ds(pl.multiple_of(t * EXPERT_TM, EXPERT_TM), EXPERT_TM), :], zsem)

        n_tiles = xs_ref.shape[0] // EXPERT_TM
        lax.fori_loop(ztile_ref[N_EXPERTS], n_tiles, lambda t, c: (tail_copy(t).start(), c)[1], 0)
        lax.fori_loop(ztile_ref[N_EXPERTS], n_tiles, lambda t, c: (tail_copy(t).wait(), c)[1], 0)

    def row_copy(r, p):
        return pltpu.make_async_copy(x_ref.at[pl.ds(r, 1), :], xs_ref.at[pl.ds(p, 1), :], sem)

    def issue(r, carry):
        row_copy(r, pos_ref[base + r]).start()
        row_copy(r, pos_ref[m_tok + base + r]).start()
        return carry

    lax.fori_loop(0, tm, issue, 0, unroll=4)

    def drain(r, carry):
        row_copy(0, 0).wait()
        row_copy(0, 0).wait()
        return carry

    lax.fori_loop(0, tm, drain, 0, unroll=16)


def _dispatch(pos, ztile, x1, n_pad):
    M, D = x1.shape
    tm = DISPATCH_TM
    grid_spec = pltpu.PrefetchScalarGridSpec(
        num_scalar_prefetch=2, grid=(M // tm,),
        in_specs=[pl.BlockSpec((tm, D), lambda i, pos, zt: (i, 0))],
        out_specs=pl.BlockSpec(memory_space=pl.ANY),
        scratch_shapes=[pltpu.VMEM((EXPERT_TM, D), F32), pltpu.SemaphoreType.DMA, pltpu.SemaphoreType.DMA],
    )
    return pl.pallas_call(
        _dispatch_body, out_shape=jax.ShapeDtypeStruct((n_pad, D), F32), grid_spec=grid_spec,
        compiler_params=_cparams(("arbitrary",), has_side_effects=True, disable_bounds_checks=True),
        name="dispatch",
    )(pos, ztile, x1)


def _experts_body(te_ref, nu_ref, xs_ref, wg_ref, wu_ref, wd_ref, ys_ref, wgb_ref, wub_ref, wdb_ref):
    i = pl.program_id(0)
    live = i < nu_ref[0]

    @pl.when(live & ((i == 0) | (te_ref[i] != te_ref[jnp.maximum(i - 1, 0)])))
    def _():
        wgb_ref[...] = wg_ref[0].astype(BF16)
        wub_ref[...] = wu_ref[0].astype(BF16)
        wdb_ref[...] = wd_ref[0].astype(BF16)

    @pl.when(live)
    def _():
        xb = xs_ref[...].astype(BF16)
        hg = _nn(xb, wgb_ref[...])
        hu = _nn(xb, wub_ref[...])
        h = (jax.nn.silu(hg) * hu).astype(BF16)
        ys_ref[...] = _nn(h, wdb_ref[...])

    @pl.when(i >= nu_ref[0])
    def _():
        ys_ref[...] = jnp.zeros(ys_ref.shape, F32)


def _experts(tile_expert, n_used, xs, wg, wu, wd):
    npad, D = xs.shape
    tm = EXPERT_TM
    grid_spec = pltpu.PrefetchScalarGridSpec(
        num_scalar_prefetch=2, grid=(npad // tm,),
        in_specs=[pl.BlockSpec((tm, D), lambda i, te, nu: (jnp.minimum(i, nu[0] - 1), 0)),
                  pl.BlockSpec((1, D, D_FF), lambda i, te, nu: (te[i], 0, 0)),
                  pl.BlockSpec((1, D, D_FF), lambda i, te, nu: (te[i], 0, 0)),
                  pl.BlockSpec((1, D_FF, D), lambda i, te, nu: (te[i], 0, 0))],
        out_specs=pl.BlockSpec((tm, D), lambda i, te, nu: (i, 0)),
        scratch_shapes=[pltpu.VMEM((D, D_FF), BF16), pltpu.VMEM((D, D_FF), BF16), pltpu.VMEM((D_FF, D), BF16)],
    )
    return pl.pallas_call(
        _experts_body, out_shape=jax.ShapeDtypeStruct((npad, D), F32), grid_spec=grid_spec,
        compiler_params=_cparams(("arbitrary",)), name="experts",
    )(tile_expert, n_used, xs, wg, wu, wd)


def _combine_body(alpha, pos_ref, x1_ref, w_ref, eye_ref, g2_ref, b2_ref, ys_ref, o_ref, buf_ref, sem):
    tm = COMBINE_TM
    m_tok = pos_ref.shape[0] // 2
    base = pl.program_id(0) * tm

    def row_copy(slot, r, p):
        return pltpu.make_async_copy(ys_ref.at[pl.ds(p, 1), :], buf_ref.at[slot, pl.ds(r, 1), :], sem)

    def issue(r, carry):
        row_copy(0, r, pos_ref[base + r]).start()
        row_copy(1, r, pos_ref[m_tok + base + r]).start()
        return carry

    lax.fori_loop(0, tm, issue, 0, unroll=4)

    w = w_ref[...]
    eye = eye_ref[...]
    w_a = w.astype(BF16)
    w_b = (w - w_a.astype(F32)).astype(BF16)
    w_c = (w - w_a.astype(F32) - w_b.astype(F32)).astype(BF16)
    wcol = _nt(eye, w_a) + _nt(eye, w_b) + _nt(eye, w_c)

    def drain(r, carry):
        row_copy(0, 0, 0).wait()
        row_copy(1, 0, 0).wait()
        return carry

    lax.fori_loop(0, tm, drain, 0, unroll=16)
    moe = buf_ref[0] * wcol[:, 0:1] + buf_ref[1] * wcol[:, 1:2]
    o_ref[...] = _layer_norm(alpha * x1_ref[...] + moe, g2_ref[...], b2_ref[...])


def _combine(alpha, pos, x1, wgt, eye, g2, b2, ys):
    M, D = x1.shape
    tm = COMBINE_TM
    grid_spec = pltpu.PrefetchScalarGridSpec(
        num_scalar_prefetch=1, grid=(M // tm,),
        in_specs=[pl.BlockSpec((tm, D), lambda i, pos: (i, 0)),
                  pl.BlockSpec((8, tm), lambda i, pos: (0, i)),
                  pl.BlockSpec((tm, tm), lambda i, pos: (0, 0)),
                  pl.BlockSpec((1, D), lambda i, pos: (0, 0)),
                  pl.BlockSpec((1, D), lambda i, pos: (0, 0)),
                  pl.BlockSpec(memory_space=pl.ANY)],
        out_specs=pl.BlockSpec((tm, D), lambda i, pos: (i, 0)),
        scratch_shapes=[pltpu.VMEM((2, tm, D), F32), pltpu.SemaphoreType.DMA],
    )
    return pl.pallas_call(
        functools.partial(_combine_body, alpha), out_shape=jax.ShapeDtypeStruct((M, D), F32),
        grid_spec=grid_spec, compiler_params=_cparams(("arbitrary",), disable_bounds_checks=True),
        name="combine",
    )(pos, x1, wgt, eye, g2, b2, ys)


def _rope_tables(T, dim):
    half = dim // 2
    inv_freq = ROPE_THETA ** (-jnp.arange(half, dtype=F32) * 2.0 / dim)
    return inv_freq


def _tables(T):
    pos = jnp.arange(T).astype(F32)
    f64 = _rope_tables(T, NSA_HD)
    ang = pos[:, None] * f64[None, :]
    cos, sin = jnp.cos(ang), jnp.sin(ang)
    c64 = jnp.tile(cos, (1, 4))
    s64 = jnp.tile(jnp.concatenate([-sin, sin], axis=1), (1, 2))
    f128 = _rope_tables(T, RET_DK)
    ang2 = pos[:, None] * f128[None, :]
    cos2, sin2 = jnp.cos(ang2), jnp.sin(ang2)
    c128 = jnp.tile(cos2, (1, 2))
    s128 = jnp.concatenate([-sin2, sin2], axis=1)
    n_rows = T // CMP_STRIDE
    cmp_end = (jnp.arange(n_rows) * CMP_STRIDE + CMP_BLOCK - 1).astype(F32)
    angc = cmp_end[:, None] * f64[None, :]
    cc = jnp.tile(jnp.cos(angc), (1, 2))
    sc = jnp.tile(jnp.sin(angc), (1, 2))
    return (c64, s64, c128, s128, cos.T, sin.T, cos2.T, sin2.T), cc, sc


def _overlap_matrix(T):
    n_rows = T // CMP_STRIDE
    n_sel = T // SLC_BLOCK
    cmp_start = np.arange(n_rows) * CMP_STRIDE
    sel_start = np.arange(n_sel) * SLC_BLOCK
    ov = np.clip(np.minimum(cmp_start[None, :] + CMP_BLOCK, sel_start[:, None] + SLC_BLOCK)
                 - np.maximum(cmp_start[None, :], sel_start[:, None]), 0, None)
    return jnp.asarray(ov.astype(np.float32) / CMP_STRIDE, dtype=BF16)


def _retention_tables():
    C = RET_CHUNK
    gamma = 1.0 - 2.0 ** (-5.0 - jnp.arange(RET_HEADS, dtype=F32))
    log_g = jnp.log(gamma)
    i = jnp.arange(C, dtype=F32)
    diff = i[:, None] - i[None, :]
    dec = jnp.where(diff >= 0, jnp.exp(jnp.maximum(diff, 0.0) * log_g[:, None, None]), 0.0)
    xi = jnp.exp((i + 1.0) * log_g[:, None])
    zeta = jnp.exp((C - 1.0 - i) * log_g[:, None])
    cd = jnp.exp(C * log_g)
    xi_b = jnp.broadcast_to(xi[:, :, None], (RET_HEADS, C, RET_DV))
    cd_b = jnp.broadcast_to(cd[:, None, None], (RET_HEADS, 1, RET_DV))
    return dec, xi_b, zeta[:, None, :], cd_b


def _rot_half_cols(w):
    half = w.shape[-1] // 2
    return jnp.concatenate([-w[..., half:], w[..., :half]], axis=-1)


def kernel(x, w_in, cmp_pos_k, cmp_k_w1, cmp_k_b1, cmp_k_w2, cmp_k_b2, cmp_pos_v, cmp_v_w1, cmp_v_b1, cmp_v_w2, cmp_v_b2, ret_gn_g, ret_gn_b, w_up_attn, w_up_ret, w_out, ln1_g, ln1_b, router_group_w, router_group_b, router_inner_w, router_inner_b, expert_w_gate, expert_w_up, expert_w_down, ln2_g, ln2_b):
    B, T, D = x.shape
    M = B * T
    depth = w_in.shape[0]
    alpha = (2.0 * depth) ** 0.25
    tabs, cc, sc = _tables(T)
    ov = _overlap_matrix(T)
    dec, xi_b, zeta, cd_b = _retention_tables()
    tri = jnp.asarray(np.triu(np.ones((RANK_TB, RANK_TB), np.float32)), dtype=BF16)
    eye = jnp.asarray(np.eye(COMBINE_TM, dtype=np.float32), dtype=BF16)
    n_pad = 2 * M + N_EXPERTS * EXPERT_TM
    n_tiles = n_pad // EXPERT_TM

    for l in range(depth):
        w = w_in[l]
        col = lambda n: w[:, _OFF[n][0]:_OFF[n][1]]
        wn = jnp.concatenate([col("cmp_k"), col("cmp_v"), col("slc_k"), col("win_k"), col("ret_q"), col("ret_v")],
                             axis=1).astype(BF16)
        wt = jnp.concatenate([col("nsa_q"), col("slc_v"), col("win_v"), col("ret_k"), col("nsa_gate"),
                              jnp.zeros((D, 8), F32)], axis=1).T.astype(BF16)
        qT, ckv, sk, wk, svT, wvT, gT, rq, rkT, rv = _proj(x, wn, wt, tabs)

        kc, vcT = _compress(
            ckv, cmp_pos_k[l].reshape(2, -1), cmp_pos_v[l].reshape(2, -1),
            cmp_k_w1[l], cmp_k_b1[l][None, :], cmp_k_w2[l], _rot_half_cols(cmp_k_w2[l]),
            cmp_k_b2[l][None, :], _rot_half_cols(cmp_k_b2[l])[None, :], cc, sc,
            cmp_v_w1[l], cmp_v_b1[l][None, :], cmp_v_w2[l].T, cmp_v_b2[l][:, None])
        o_attn = _nsa(qT, kc, vcT, ov, sk, svT, wk, wvT, gT)
        o_ret = _retention(rq, rkT, rv, dec, xi_b, zeta, cd_b, ret_gn_g[l][None, :], ret_gn_b[l][None, :])

        wr = jnp.concatenate([router_inner_w[l].transpose(0, 2, 1).reshape(N_EXPERTS, D),
                              router_group_w[l].T, jnp.zeros((4, D), F32)], axis=0)
        wrh = wr.astype(BF16)
        wrl = (wr - wrh.astype(F32)).astype(BF16)
        rb = jnp.concatenate([router_inner_b[l].reshape(-1), router_group_b[l], jnp.zeros((4,), F32)])[:, None]
        x1, eid, wgt = _merge(
            alpha, x.reshape(M, D), o_attn.reshape(M, -1), o_ret.reshape(M, -1),
            col("ret_gate").astype(BF16), col("merge_gate").astype(BF16), w_up_attn[l].astype(BF16),
            w_up_ret[l].astype(BF16), w_out[l].astype(BF16), ln1_g[l][None, :], ln1_b[l][None, :], wrh, wrl, rb)

        eid_flat = eid[:2].reshape(1, 2 * M)
        rank, cnt = _rank(eid_flat, tri)
        counts = cnt[:, 0].astype(I32)
        tiles_per = (counts + EXPERT_TM - 1) // EXPERT_TM
        tile_end = jnp.cumsum(tiles_per)
        row_start = (tile_end - tiles_per) * EXPERT_TM
        pos = (row_start[eid_flat[0]] + rank[0]).astype(I32)
        tile_ids = jnp.arange(n_tiles, dtype=I32)
        tile_expert = jnp.minimum(jnp.sum((tile_end[None, :] <= tile_ids[:, None]).astype(I32), axis=1),
                                  N_EXPERTS - 1).astype(I32)
        n_used = tile_end[-1:].astype(I32)

        ztile = jnp.concatenate([jnp.where(tiles_per > 0, (tile_end - 1) * EXPERT_TM, -1), tile_end[-1:]]).astype(I32)
        xs = _dispatch(pos, ztile, x1, n_pad)
        ys = _experts(tile_expert, n_used, xs, expert_w_gate[l], expert_w_up[l], expert_w_down[l])
        x = _combine(alpha, pos, x1, wgt, eye, ln2_g[l][None, :], ln2_b[l][None, :], ys).reshape(B, T, D)
    return x
```

```python
import functools

import numpy as np
import jax
import jax.numpy as jnp
from jax import lax
from jax.experimental import pallas as pl
from jax.experimental.pallas import tpu as pltpu

F32 = jnp.float32
BF16 = jnp.bfloat16
I32 = jnp.int32

D_MODEL = 1024
NSA_HEADS = 8
NSA_HD = 64
NSA_GROUPS = 2
NSA_HPG = NSA_HEADS // NSA_GROUPS
CMP_BLOCK = 32
CMP_STRIDE = 16
CMP_HIDDEN = 256
SLC_BLOCK = 64
SLC_TOPK = 16
WINDOW = 512
RET_HEADS = 4
RET_DK = 128
RET_DV = 256
RET_CHUNK = 128
MOE_GROUPS = 4
EXPERTS_PER_GROUP = 8
N_EXPERTS = MOE_GROUPS * EXPERTS_PER_GROUP
D_FF = 512
ROPE_THETA = 10000.0
LN_EPS = 1e-5
GN_EPS = 1e-5
NEG_INF = -1e30
LOG2_E = 1.4426950408889634

VMEM_LIMIT_V7X = 56 * 1024 * 1024

_OFF = {}
_o = 0
for _n, _w in (("nsa_q", 512), ("cmp_k", 128), ("cmp_v", 128), ("slc_k", 128), ("slc_v", 128),
               ("win_k", 128), ("win_v", 128), ("nsa_gate", 24), ("ret_q", 512), ("ret_k", 512),
               ("ret_v", 1024), ("ret_gate", 1024), ("merge_gate", 2048)):
    _OFF[_n] = (_o, _o + _w)
    _o += _w

PROJ_TM = 512
NSA_TQ = 256
NSA_KC = 256
NSA_VROWS = 80
RET_STEP_CHUNKS = 4
MERGE_TM = 512
RANK_TB = 512
DISPATCH_TM = 512
EXPERT_TM = 256
COMBINE_TM = 256


def _cparams(sem, **kw):
    return pltpu.CompilerParams(dimension_semantics=sem, vmem_limit_bytes=VMEM_LIMIT_V7X, **kw)


def _nt(a, b):
    return lax.dot_general(a, b, (((1,), (1,)), ((), ())), preferred_element_type=F32)


def _nn(a, b):
    return jnp.dot(a, b, preferred_element_type=F32)


def _proj_body(x_ref, wn_ref, wt_ref, c64_ref, s64_ref, c128_ref, s128_ref,
               ct64_ref, st64_ref, ct128_ref, st128_ref,
               qT_ref, ckv_ref, sk_ref, wk_ref, svT_ref, wvT_ref, gT_ref, rq_ref, rkT_ref, rv_ref):
    tm = PROJ_TM
    xb = x_ref[0].astype(BF16)

    def nn(a, b):
        return _nn(xb, wn_ref[:, a:b])

    def nt(a, b):
        return _nt(wt_ref[a:b, :], xb)

    ckv = nn(0, 256)
    for j in range(4):
        ckv_ref[0, j] = ckv[:, 64 * j:64 * (j + 1)]

    lane = lax.broadcasted_iota(I32, (tm, 128), 1)
    first = (lane % 64) < 32
    c64 = c64_ref[...]
    s64 = s64_ref[...]

    def rope64(k):
        rot = jnp.where(first, pltpu.roll(k, 96, 1), pltpu.roll(k, 32, 1))
        return k * c64 + rot * s64

    sk = rope64(nn(256, 384)).astype(BF16)
    sk_ref[0, 0] = sk[:, :64]
    sk_ref[0, 1] = sk[:, 64:]
    wk = rope64(nn(384, 512)).astype(BF16)
    wk_ref[0, 0] = wk[:, :64]
    wk_ref[0, 1] = wk[:, 64:]

    c128 = c128_ref[...]
    s128 = s128_ref[...]
    rq = nn(512, 1024)
    for h in range(RET_HEADS):
        ch = rq[:, 128 * h:128 * (h + 1)]
        rq_ref[0, :, 128 * h:128 * (h + 1)] = (ch * c128 + pltpu.roll(ch, 64, 1) * s128).astype(BF16)
    rv_ref[0] = nn(1024, 2048).astype(BF16)

    ct = ct64_ref[...]
    st = st64_ref[...]
    qT = nt(0, 512)
    scale_q = NSA_HD ** -0.5 * LOG2_E
    tq = NSA_TQ
    for hh in range(NSA_HEADS):
        g, h = divmod(hh, NSA_HPG)
        x1 = qT[64 * hh:64 * hh + 32]
        x2 = qT[64 * hh + 32:64 * hh + 64]
        o1 = ((x1 * ct - x2 * st) * scale_q).astype(BF16)
        o2 = ((x1 * st + x2 * ct) * scale_q).astype(BF16)
        for it in range(tm // tq):
            qT_ref[0, g, it, 0:32, h * tq:(h + 1) * tq] = o1[:, it * tq:(it + 1) * tq]
            qT_ref[0, g, it, 32:64, h * tq:(h + 1) * tq] = o2[:, it * tq:(it + 1) * tq]

    svT = nt(512, 640).astype(BF16)
    wvT = nt(640, 768).astype(BF16)
    row16 = lax.broadcasted_iota(I32, (NSA_VROWS - NSA_HD, NSA_KC), 0)
    ones_blk = jnp.where(row16 == 0, 1.0, 0.0).astype(BF16)
    for c in range(tm // NSA_KC):
        for g in range(NSA_GROUPS):
            for vT, ref in ((svT, svT_ref), (wvT, wvT_ref)):
                ref[0, c, g, 0:NSA_HD, :] = vT[64 * g:64 * (g + 1), NSA_KC * c:NSA_KC * (c + 1)]
                ref[0, c, g, NSA_HD:NSA_VROWS, :] = ones_blk

    ct2 = ct128_ref[...]
    st2 = st128_ref[...]
    rkT = nt(768, 1280)
    scale_k = RET_DK ** -0.5
    for h in range(RET_HEADS):
        x1 = rkT[128 * h:128 * h + 64]
        x2 = rkT[128 * h + 64:128 * h + 128]
        o1 = (x1 * ct2 - x2 * st2) * scale_k
        o2 = (x1 * st2 + x2 * ct2) * scale_k
        for c in range(tm // RET_CHUNK):
            rkT_ref[0, c, 128 * h:128 * h + 64, :] = o1[:, 128 * c:128 * (c + 1)]
            rkT_ref[0, c, 128 * h + 64:128 * h + 128, :] = o2[:, 128 * c:128 * (c + 1)]

    gT_ref[0] = jax.nn.sigmoid(nt(1280, 1312))


def _proj(x, wn, wt, tabs):
    B, T, D = x.shape
    tm = PROJ_TM
    c64, s64, c128, s128, ct64, st64, ct128, st128 = tabs
    const = lambda b, i: (0, 0)
    in_specs = [
        pl.BlockSpec((1, tm, D), lambda b, i: (b, i, 0)),
        pl.BlockSpec(wn.shape, const),
        pl.BlockSpec(wt.shape, const),
        pl.BlockSpec((tm, 128), lambda b, i: (i, 0)),
        pl.BlockSpec((tm, 128), lambda b, i: (i, 0)),
        pl.BlockSpec((tm, 128), lambda b, i: (i, 0)),
        pl.BlockSpec((tm, 128), lambda b, i: (i, 0)),
        pl.BlockSpec((32, tm), lambda b, i: (0, i)),
        pl.BlockSpec((32, tm), lambda b, i: (0, i)),
        pl.BlockSpec((64, tm), lambda b, i: (0, i)),
        pl.BlockSpec((64, tm), lambda b, i: (0, i)),
    ]
    out_shape = [
        jax.ShapeDtypeStruct((B, NSA_GROUPS, T // NSA_TQ, NSA_HD, NSA_HPG * NSA_TQ), BF16),
        jax.ShapeDtypeStruct((B, 4, T, 64), F32),
        jax.ShapeDtypeStruct((B, 2, T, 64), BF16),
        jax.ShapeDtypeStruct((B, 2, T, 64), BF16),
        jax.ShapeDtypeStruct((B, T // NSA_KC, NSA_GROUPS, NSA_VROWS, NSA_KC), BF16),
        jax.ShapeDtypeStruct((B, T // NSA_KC, NSA_GROUPS, NSA_VROWS, NSA_KC), BF16),
        jax.ShapeDtypeStruct((B, 32, T), F32),
        jax.ShapeDtypeStruct((B, T, 512), BF16),
        jax.ShapeDtypeStruct((B, T // RET_CHUNK, 512, RET_CHUNK), F32),
        jax.ShapeDtypeStruct((B, T, 1024), BF16),
    ]
    out_specs = [
        pl.BlockSpec((1, NSA_GROUPS, tm // NSA_TQ, NSA_HD, NSA_HPG * NSA_TQ), lambda b, i: (b, 0, i, 0, 0)),
        pl.BlockSpec((1, 4, tm, 64), lambda b, i: (b, 0, i, 0)),
        pl.BlockSpec((1, 2, tm, 64), lambda b, i: (b, 0, i, 0)),
        pl.BlockSpec((1, 2, tm, 64), lambda b, i: (b, 0, i, 0)),
        pl.BlockSpec((1, tm // NSA_KC, NSA_GROUPS, NSA_VROWS, NSA_KC), lambda b, i: (b, i, 0, 0, 0)),
        pl.BlockSpec((1, tm // NSA_KC, NSA_GROUPS, NSA_VROWS, NSA_KC), lambda b, i: (b, i, 0, 0, 0)),
        pl.BlockSpec((1, 32, tm), lambda b, i: (b, 0, i)),
        pl.BlockSpec((1, tm, 512), lambda b, i: (b, i, 0)),
        pl.BlockSpec((1, tm // RET_CHUNK, 512, RET_CHUNK), lambda b, i: (b, i, 0, 0)),
        pl.BlockSpec((1, tm, 1024), lambda b, i: (b, i, 0)),
    ]
    return pl.pallas_call(
        _proj_body, out_shape=out_shape, grid=(B, T // tm), in_specs=in_specs, out_specs=out_specs,
        compiler_params=_cparams(("parallel", "parallel")), name="proj",
    )(x, wn, wt, c64, s64, c128, s128, ct64, st64, ct128, st128)


def _compress_body(uk_ref, uv_ref, posk_ref, posv_ref, w1k_ref, b1k_ref, w2k_ref, w2kr_ref, b2k_ref, b2kr_ref,
                   cc_ref, sc_ref, w1v_ref, b1v_ref, w2vT_ref, b2vT_ref, kc_ref, vcT_ref):
    half = CMP_STRIDE * NSA_HD

    def hidden(u_ref, pos_ref, w1_ref, b1_ref):
        u = u_ref[0, 0]
        top = (u + pos_ref[0:1, :]).astype(BF16)
        bot = (u + pos_ref[1:2, :]).astype(BF16)
        a = _nn(top, w1_ref[0:half, :].astype(BF16))
        bm = _nn(bot, w1_ref[half:2 * half, :].astype(BF16))
        n = bm.shape[0]
        return jax.nn.gelu(a + pltpu.roll(bm, n - 1, 0) + b1_ref[...])

    hk = hidden(uk_ref, posk_ref, w1k_ref, b1k_ref).astype(BF16)
    k = _nn(hk, w2k_ref[...].astype(BF16)) + b2k_ref[...]
    kr = _nn(hk, w2kr_ref[...].astype(BF16)) + b2kr_ref[...]
    kc_ref[0, 0] = (k * cc_ref[...] + kr * sc_ref[...]).astype(BF16)

    hv = hidden(uv_ref, posv_ref, w1v_ref, b1v_ref).astype(BF16)
    vcT_ref[0, 0] = (_nt(w2vT_ref[...].astype(BF16), hv) + b2vT_ref[...]).astype(BF16)


def _compress(ckv, posk, posv, w1k, b1k, w2k, w2kr, b2k, b2kr, cc, sc, w1v, b1v, w2vT, b2vT):
    B = ckv.shape[0]
    T = ckv.shape[2]
    nr = T // CMP_STRIDE
    u = ckv.reshape(B, 4, nr, CMP_STRIDE * NSA_HD)
    full = lambda a: pl.BlockSpec(a.shape, lambda b, g: (0,) * a.ndim)
    in_specs = [
        pl.BlockSpec((1, 1, nr, 1024), lambda b, g: (b, g, 0, 0)),
        pl.BlockSpec((1, 1, nr, 1024), lambda b, g: (b, g + 2, 0, 0)),
    ] + [full(a) for a in (posk, posv, w1k, b1k, w2k, w2kr, b2k, b2kr, cc, sc, w1v, b1v, w2vT, b2vT)]
    out_shape = [jax.ShapeDtypeStruct((B, NSA_GROUPS, nr, NSA_HD), BF16),
                 jax.ShapeDtypeStruct((B, NSA_GROUPS, NSA_HD, nr), BF16)]
    out_specs = [pl.BlockSpec((1, 1, nr, NSA_HD), lambda b, g: (b, g, 0, 0)),
                 pl.BlockSpec((1, 1, NSA_HD, nr), lambda b, g: (b, g, 0, 0))]
    return pl.pallas_call(
        _compress_body, out_shape=out_shape, grid=(B, NSA_GROUPS), in_specs=in_specs, out_specs=out_specs,
        compiler_params=_cparams(("parallel", "parallel")), name="compress",
    )(u, u, posk, posv, w1k, b1k, w2k, w2kr, b2k, b2kr, cc, sc, w1v, b1v, w2vT, b2vT)


def _nsa_body(q_ref, kc_ref, vcT_ref, ov_ref, sk_ref, svT_ref, wk_ref, wvT_ref, gT_ref, lo_ref, hi_ref, o_ref,
              score_ref, bias_ref, acc_ref, tot_ref, s_ref):
    tq = NSA_TQ
    kc_n = NSA_KC
    hq = NSA_HPG * tq
    groups = range(NSA_GROUPS)
    i = pl.program_id(1)
    t0 = i * tq
    t_row = t0 + lax.broadcasted_iota(I32, (1, tq), 1)
    t_row4 = t0 + lax.broadcasted_iota(I32, (1, hq), 1) % tq
    n_cmp = kc_ref.shape[2]
    n_sel = ov_ref.shape[0]
    q = [q_ref[0, g, 0] for g in groups]

    def gate4(g, br):
        return jnp.concatenate([gT_ref[0, (g * NSA_HPG + h) * 3 + br:(g * NSA_HPG + h) * 3 + br + 1, :]
                                for h in range(NSA_HPG)], axis=1)

    cmp_end = lax.broadcasted_iota(I32, (n_cmp, 1), 0) * CMP_STRIDE + (CMP_BLOCK - 1)
    cbias = jnp.where(cmp_end <= t_row4, 0.0, NEG_INF)
    any_valid = jnp.where(t_row4 >= CMP_BLOCK - 1, 1.0, 0.0)
    j_col = lax.broadcasted_iota(I32, (n_sel, 1), 0)
    blk_t = t_row // SLC_BLOCK
    forced = (j_col == 0) | (j_col == blk_t) | (j_col == blk_t - 1)
    sub = lax.broadcasted_iota(I32, (8, 1), 0)
    n_slab = n_sel // 8

    def select_blocks(g):
        s = _nn(kc_ref[0, g], q[g]) + cbias
        e = jnp.exp2(s - jnp.max(s, axis=0, keepdims=True))
        p = e * (any_valid / jnp.sum(e, axis=0, keepdims=True))
        tot_ref[g] = gate4(g, 0) * _nn(vcT_ref[0, g], p.astype(BF16))
        psum = p[:, 0:tq]
        for h in range(1, NSA_HPG):
            psum = psum + p[:, h * tq:(h + 1) * tq]
        p_hi = psum.astype(BF16)
        p_lo = (psum - p_hi.astype(F32)).astype(BF16)
        ov = ov_ref[...]
        imp = _nn(ov, p_hi) + _nn(ov, p_lo)
        score = jnp.where(forced, 1e9, jnp.where(j_col <= blk_t, imp, -1e9)).astype(F32)
        score_ref[g] = score
        slabs = [score[8 * v:8 * (v + 1)] for v in range(n_slab)]
        cnt = [jnp.zeros((8, tq), F32) for _ in range(n_slab)]
        for r in range(n_sel):
            row = jnp.broadcast_to(score_ref[g, r:r + 1, :], (8, tq))
            for v in range(n_slab):
                if r < 8 * v:
                    ahead = jnp.where(row >= slabs[v], 1.0, 0.0)
                elif r >= 8 * (v + 1):
                    ahead = jnp.where(row > slabs[v], 1.0, 0.0)
                else:
                    ahead = jnp.where(sub + 8 * v > r, jnp.where(row >= slabs[v], 1.0, 0.0),
                                      jnp.where(row > slabs[v], 1.0, 0.0))
                cnt[v] = cnt[v] + ahead
        for v in range(n_slab):
            bias_ref[g, 8 * v:8 * (v + 1), :] = jnp.where(
                cnt[v] < float(SLC_TOPK), jnp.where(sub + 8 * v <= blk_t, 0.0, NEG_INF), NEG_INF)

    for g in groups:
        select_blocks(g)

    def qk(k_ref, g, c):
        k_c = k_ref[0, g, pl.ds(pl.multiple_of(c * kc_n, kc_n), kc_n), :]
        return [_nn(k_c, q[g][:, h * tq:(h + 1) * tq]) for h in range(NSA_HPG)]

    def stage_a(scores, add_bias, m):
        biased = [add_bias(scores[h]) for h in range(NSA_HPG)]
        m_new = tuple(jnp.maximum(m[h], jnp.max(biased[h], axis=0, keepdims=True)) for h in range(NSA_HPG))
        return biased, m_new

    def stage_b(g, biased, vT_c, m_old, m_new):
        for h in range(NSA_HPG):
            hs = slice(h * tq, (h + 1) * tq)
            alpha = jnp.exp2(m_old[h] - m_new[h])
            p = jnp.exp2((biased[h] - m_new[h]).astype(BF16))
            acc_ref[g, :, hs] = alpha * acc_ref[g, :, hs] + _nn(vT_c, p)

    nb = kc_n // SLC_BLOCK

    def block_bias(g, c, s):
        return jnp.concatenate([s[SLC_BLOCK * b:SLC_BLOCK * (b + 1)] + bias_ref[g, pl.ds(c * nb + b, 1), :]
                                for b in range(nb)], axis=0)

    def finish(g, br):
        acc = acc_ref[g]
        tot_ref[g] = tot_ref[g] + gate4(g, br) * (acc[0:NSA_HD] / acc[NSA_HD:NSA_HD + 1])
        acc_ref[g] = jnp.zeros(acc.shape, F32)

    def park(g, biased):
        for h in range(NSA_HPG):
            s_ref[g, h] = biased[h]

    def parked(g):
        return [s_ref[g, h] for h in range(NSA_HPG)]

    m0 = tuple(jnp.full((1, tq), NEG_INF, F32) for _ in range(NSA_HPG))

    acc_ref[...] = jnp.zeros(acc_ref.shape, F32)
    m_new = []
    for g in groups:
        biased, m_g = stage_a(qk(sk_ref, g, i), lambda s, g=g: block_bias(g, i, s) + lo_ref[...], m0)
        park(g, biased)
        m_new.append(m_g)

    def slc_chunk(c, carry):
        prev, m_old, m_new = carry
        scores = [qk(sk_ref, g, c) for g in groups]
        for g in groups:
            stage_b(g, parked(g), svT_ref[0, prev, g], m_old[g], m_new[g])
        m_next = []
        for g in groups:
            biased, m_g = stage_a(scores[g], functools.partial(block_bias, g, c), m_new[g])
            park(g, biased)
            m_next.append(m_g)
        return c, m_new, tuple(m_next)

    prev, m_old, m_new = lax.fori_loop(0, i, slc_chunk, (i, (m0,) * NSA_GROUPS, tuple(m_new)))

    c_far = jnp.maximum(i - 2, 0)
    c_near = jnp.maximum(i - 1, 0)
    pen_far = jnp.where(i >= 2, 0.0, NEG_INF).astype(F32)
    pen_near = jnp.where(i >= 1, 0.0, NEG_INF).astype(F32)

    scores = [qk(wk_ref, g, c_far) for g in groups]
    far = []
    for g in groups:
        stage_b(g, parked(g), svT_ref[0, prev, g], m_old[g], m_new[g])
        far.append(stage_a(scores[g], lambda s: s + (hi_ref[...] + pen_far), m0))
        finish(g, 1)

    scores = [qk(wk_ref, g, c_near) for g in groups]
    near = []
    for g in groups:
        stage_b(g, far[g][0], wvT_ref[0, c_far, g], m0, far[g][1])
        near.append(stage_a(scores[g], lambda s: s + pen_near, far[g][1]))
    scores = [qk(wk_ref, g, i) for g in groups]
    for g in groups:
        stage_b(g, near[g][0], wvT_ref[0, c_near, g], far[g][1], near[g][1])
        b_diag, m_diag = stage_a(scores[g], lambda s: s + lo_ref[...], near[g][1])
        stage_b(g, b_diag, wvT_ref[0, i, g], near[g][1], m_diag)
        finish(g, 2)

    o_ref[0] = jnp.concatenate([tot_ref[g, :, h * tq:(h + 1) * tq] for g in groups for h in range(NSA_HPG)],
                               axis=0).T.astype(BF16)


def _nsa(qT, kc, vcT, ov, sk, svT, wk, wvT, gT):
    B, _, T = gT.shape
    tq = NSA_TQ
    assert NSA_KC == tq and WINDOW == 2 * NSA_KC
    nr = kc.shape[2]
    nch = T // NSA_KC
    kk = np.arange(NSA_KC)[:, None]
    tt = np.arange(tq)[None, :]
    lo = jnp.asarray(np.where(kk <= tt, 0.0, NEG_INF), dtype=F32)
    hi = jnp.asarray(np.where(kk > tt, 0.0, NEG_INF), dtype=F32)
    G = NSA_GROUPS
    in_specs = [
        pl.BlockSpec((1, G, 1, NSA_HD, NSA_HPG * tq), lambda b, i: (b, 0, i, 0, 0)),
        pl.BlockSpec((1, G, nr, NSA_HD), lambda b, i: (b, 0, 0, 0)),
        pl.BlockSpec((1, G, NSA_HD, nr), lambda b, i: (b, 0, 0, 0)),
        pl.BlockSpec(ov.shape, lambda b, i: (0, 0)),
        pl.BlockSpec((1, G, T, NSA_HD), lambda b, i: (b, 0, 0, 0)),
        pl.BlockSpec((1, nch, G, NSA_VROWS, NSA_KC), lambda b, i: (b, 0, 0, 0, 0)),
        pl.BlockSpec((1, G, T, NSA_HD), lambda b, i: (b, 0, 0, 0)),
        pl.BlockSpec((1, nch, G, NSA_VROWS, NSA_KC), lambda b, i: (b, 0, 0, 0, 0)),
        pl.BlockSpec((1, 32, tq), lambda b, i: (b, 0, i)),
        pl.BlockSpec(lo.shape, lambda b, i: (0, 0)),
        pl.BlockSpec(hi.shape, lambda b, i: (0, 0)),
    ]
    n_sel = T // SLC_BLOCK
    hq = NSA_HPG * tq
    return pl.pallas_call(
        _nsa_body, out_shape=jax.ShapeDtypeStruct((B, T, NSA_HEADS * NSA_HD), BF16),
        grid=(B, T // tq), in_specs=in_specs,
        out_specs=pl.BlockSpec((1, tq, NSA_HEADS * NSA_HD), lambda b, i: (b, i, 0)),
        scratch_shapes=[pltpu.VMEM((G, n_sel, tq), F32), pltpu.VMEM((G, n_sel, tq), F32),
                        pltpu.VMEM((G, NSA_VROWS, hq), F32), pltpu.VMEM((G, NSA_HD, hq), F32),
                        pltpu.VMEM((G, NSA_HPG, NSA_KC, tq), F32)],
        compiler_params=_cparams(("parallel", "parallel")), name="nsa",
    )(qT, kc, vcT, ov, sk, svT, wk, wvT, gT, lo, hi)


def _ret_body(q_ref, kT_ref, v_ref, dec_ref, xi_ref, zeta_ref, cd_ref, gg_ref, gb_ref, o_ref, r_ref):
    C = RET_CHUNK

    @pl.when(pl.program_id(1) == 0)
    def _():
        r_ref[...] = jnp.zeros(r_ref.shape, F32)

    for h in range(RET_HEADS):
        dk = slice(h * RET_DK, (h + 1) * RET_DK)
        dv = slice(h * RET_DV, (h + 1) * RET_DV)
        dec = dec_ref[h]
        xi = xi_ref[h]
        zeta = zeta_ref[h]
        cd = cd_ref[h]
        gg = gg_ref[:, dv]
        gb = gb_ref[:, dv]
        r = r_ref[h]
        for n in range(RET_STEP_CHUNKS):
            rows = slice(n * C, (n + 1) * C)
            qc = q_ref[0, rows, dk]
            kT = kT_ref[0, n, dk, :]
            vc = v_ref[0, rows, dv]
            s = _nn(qc, kT.astype(BF16)) * dec
            o = _nn(s.astype(BF16), vc) + _nn(qc, r.astype(BF16)) * xi
            r = r * cd + _nn((kT * zeta).astype(BF16), vc)
            mu = jnp.mean(o, axis=-1, keepdims=True)
            var = jnp.mean(jnp.square(o - mu), axis=-1, keepdims=True)
            o_ref[0, rows, dv] = (o - mu) * lax.rsqrt(var + GN_EPS) * gg + gb
        r_ref[h] = r


def _retention(rq, rkT, rv, dec, xi, zeta, cd, gn_g, gn_b):
    B, T, _ = rq.shape
    ts = RET_STEP_CHUNKS * RET_CHUNK
    full = lambda a: pl.BlockSpec(a.shape, lambda b, j: (0,) * a.ndim)
    in_specs = [
        pl.BlockSpec((1, ts, RET_HEADS * RET_DK), lambda b, j: (b, j, 0)),
        pl.BlockSpec((1, RET_STEP_CHUNKS, RET_HEADS * RET_DK, RET_CHUNK), lambda b, j: (b, j, 0, 0)),
        pl.BlockSpec((1, ts, RET_HEADS * RET_DV), lambda b, j: (b, j, 0)),
    ] + [full(a) for a in (dec, xi, zeta, cd, gn_g, gn_b)]
    return pl.pallas_call(
        _ret_body, out_shape=jax.ShapeDtypeStruct((B, T, RET_HEADS * RET_DV), F32),
        grid=(B, T // ts), in_specs=in_specs,
        out_specs=pl.BlockSpec((1, ts, RET_HEADS * RET_DV), lambda b, j: (b, j, 0)),
        scratch_shapes=[pltpu.VMEM((RET_HEADS, RET_DK, RET_DV), F32)],
        compiler_params=_cparams(("parallel", "arbitrary")), name="retention",
    )(rq, rkT, rv, dec, xi, zeta, cd, gn_g, gn_b)


def _layer_norm(y, g, b):
    mu = jnp.mean(y, axis=-1, keepdims=True)
    var = jnp.mean(jnp.square(y - mu), axis=-1, keepdims=True)
    return (y - mu) * lax.rsqrt(var + LN_EPS) * g + b


def _merge_body(alpha, x_ref, oa_ref, or_ref, wrg_ref, wmg_ref, wua_ref, wur_ref, wo_ref, g1_ref, b1_ref,
                wrh_ref, wrl_ref, rb_ref, x1_ref, eid_ref, wgt_ref):
    x = x_ref[...]
    xb = x.astype(BF16)
    rgate = jax.nn.silu(_nn(xb, wrg_ref[...]))
    o_ret = (or_ref[...] * rgate).astype(BF16)
    a = _nn(oa_ref[...], wua_ref[...])
    r = _nn(o_ret, wur_ref[...])
    mg = jax.nn.sigmoid(_nn(xb, wmg_ref[...]))
    merged = mg[:, :D_MODEL] * a + mg[:, D_MODEL:] * r
    mix = _nn(merged.astype(BF16), wo_ref[...])
    x1 = _layer_norm(alpha * x + mix, g1_ref[...], b1_ref[...])
    x1_ref[...] = x1

    xh = x1.astype(BF16)
    xl = (x1 - xh.astype(F32)).astype(BF16)
    wh = wrh_ref[...]
    lg = _nt(wh, xh) + _nt(wh, xl) + _nt(wrl_ref[...], xh) + rb_ref[...]
    ne = N_EXPERTS
    gl = lg[ne:ne + MOE_GROUPS]
    ge = jnp.exp(gl - jnp.max(gl, axis=0, keepdims=True))
    pg = ge / jnp.sum(ge, axis=0, keepdims=True)
    g_prob = jnp.max(pg, axis=0, keepdims=True)
    gi = lax.broadcasted_iota(I32, pg.shape, 0)
    g_idx = jnp.min(jnp.where(pg == g_prob, gi, MOE_GROUPS), axis=0, keepdims=True)
    inner = jnp.zeros((EXPERTS_PER_GROUP, lg.shape[1]), F32)
    for gq in range(MOE_GROUPS):
        inner = inner + jnp.where(g_idx == gq, lg[8 * gq:8 * (gq + 1)], 0.0)
    ei = lax.broadcasted_iota(I32, inner.shape, 0)
    m1 = jnp.max(inner, axis=0, keepdims=True)
    i1 = jnp.min(jnp.where(inner == m1, ei, EXPERTS_PER_GROUP), axis=0, keepdims=True)
    rest = jnp.where(ei == i1, -jnp.inf, inner)
    m2 = jnp.max(rest, axis=0, keepdims=True)
    i2 = jnp.min(jnp.where(rest == m2, ei, EXPERTS_PER_GROUP), axis=0, keepdims=True)
    e2 = jnp.exp(m2 - m1)
    den = 1.0 + e2
    w1 = (1.0 / den) * g_prob
    w2 = (e2 / den) * g_prob
    zi = jnp.zeros((6, lg.shape[1]), I32)
    eid_ref[...] = jnp.concatenate([g_idx * EXPERTS_PER_GROUP + i1, g_idx * EXPERTS_PER_GROUP + i2, zi], axis=0)
    wgt_ref[...] = jnp.concatenate([w1, w2, jnp.zeros((6, lg.shape[1]), F32)], axis=0)


def _merge(alpha, x2, oa2, or2, wrg, wmg, wua, wur, wo, g1, b1, wrh, wrl, rb):
    M, D = x2.shape
    tm = MERGE_TM
    full = lambda a: pl.BlockSpec(a.shape, lambda i: (0,) * a.ndim)
    in_specs = [pl.BlockSpec((tm, D), lambda i: (i, 0)),
                pl.BlockSpec((tm, oa2.shape[1]), lambda i: (i, 0)),
                pl.BlockSpec((tm, or2.shape[1]), lambda i: (i, 0))] + [
        full(a) for a in (wrg, wmg, wua, wur, wo, g1, b1, wrh, wrl, rb)]
    out_shape = [jax.ShapeDtypeStruct((M, D), F32),
                 jax.ShapeDtypeStruct((8, M), I32),
                 jax.ShapeDtypeStruct((8, M), F32)]
    out_specs = [pl.BlockSpec((tm, D), lambda i: (i, 0)),
                 pl.BlockSpec((8, tm), lambda i: (0, i)),
                 pl.BlockSpec((8, tm), lambda i: (0, i))]
    return pl.pallas_call(
        functools.partial(_merge_body, alpha), out_shape=out_shape, grid=(M // tm,),
        in_specs=in_specs, out_specs=out_specs,
        compiler_params=_cparams(("parallel",)), name="merge",
    )(x2, oa2, or2, wrg, wmg, wua, wur, wo, g1, b1, wrh, wrl, rb)


def _rank_body(eid_ref, tri_ref, rank_ref, cnt_ref, carry_ref):
    j = pl.program_id(0)

    @pl.when(j == 0)
    def _():
        carry_ref[...] = jnp.zeros(carry_ref.shape, F32)

    e = eid_ref[...]
    rows = lax.broadcasted_iota(I32, (N_EXPERTS, e.shape[1]), 0)
    hit = rows == e
    onehot = jnp.where(hit, 1.0, 0.0).astype(BF16)
    incl = _nn(onehot, tri_ref[...])
    carry = carry_ref[:, 0:1]
    rank = jnp.sum(jnp.where(hit, incl - 1.0 + carry, 0.0), axis=0, keepdims=True)
    rank_ref[...] = rank.astype(I32)
    carry_ref[...] = carry_ref[...] + jnp.sum(jnp.where(hit, 1.0, 0.0), axis=1, keepdims=True)
    cnt_ref[...] = carry_ref[...]


def _rank(eid_flat, tri):
    n = eid_flat.shape[1]
    tb = RANK_TB
    return pl.pallas_call(
        _rank_body,
        out_shape=[jax.ShapeDtypeStruct((1, n), I32), jax.ShapeDtypeStruct((N_EXPERTS, 128), F32)],
        grid=(n // tb,),
        in_specs=[pl.BlockSpec((1, tb), lambda j: (0, j)), pl.BlockSpec((tb, tb), lambda j: (0, 0))],
        out_specs=[pl.BlockSpec((1, tb), lambda j: (0, j)), pl.BlockSpec((N_EXPERTS, 128), lambda j: (0, 0))],
        scratch_shapes=[pltpu.VMEM((N_EXPERTS, 128), F32)],
        compiler_params=_cparams(("arbitrary",)), name="rank",
    )(eid_flat, tri)


def _dispatch_body(pos_ref, ztile_ref, x_ref, xs_ref, zbuf_ref, sem, zsem):
    tm = DISPATCH_TM
    m_tok = pos_ref.shape[0] // 2
    base = pl.program_id(0) * tm

    @pl.when(pl.program_id(0) == 0)
    def _():
        zbuf_ref[...] = jnp.zeros(zbuf_ref.shape, F32)

        def zero_copy(e):
            z = pl.multiple_of(jnp.maximum(ztile_ref[e], 0), EXPERT_TM)
            return pltpu.make_async_copy(zbuf_ref, xs_ref.at[pl.ds(z, EXPERT_TM), :], zsem)

        for e in range(N_EXPERTS):
            @pl.when(ztile_ref[e] >= 0)
            def _():
                zero_copy(e).start()
        for e in range(N_EXPERTS):
            @pl.when(ztile_ref[e] >= 0)
            def _():
                zero_copy(e).wait()

        def tail_copy(t):
            return pltpu.make_async_copy(
                zbuf_ref, xs_ref.at[pl.ds(pl.multiple_of(t * EXPERT_TM, EXPERT_TM), EXPERT_TM), :], zsem)

        n_tiles = xs_ref.shape[0] // EXPERT_TM
        lax.fori_loop(ztile_ref[N_EXPERTS], n_tiles, lambda t, c: (tail_copy(t).start(), c)[1], 0)
        lax.fori_loop(ztile_ref[N_EXPERTS], n_tiles, lambda t, c: (tail_copy(t).wait(), c)[1], 0)

    def row_copy(r, p):
        return pltpu.make_async_copy(x_ref.at[pl.ds(r, 1), :], xs_ref.at[pl.ds(p, 1), :], sem)

    def issue(r, carry):
        row_copy(r, pos_ref[base + r]).start()
        row_copy(r, pos_ref[m_tok + base + r]).start()
        return carry

    lax.fori_loop(0, tm, issue, 0, unroll=4)

    def drain(r, carry):
        row_copy(0, 0).wait()
        row_copy(0, 0).wait()
        return carry

    lax.fori_loop(0, tm, drain, 0, unroll=16)


def _dispatch(pos, ztile, x1, n_pad):
    M, D = x1.shape
    tm = DISPATCH_TM
    grid_spec = pltpu.PrefetchScalarGridSpec(
        num_scalar_prefetch=2, grid=(M // tm,),
        in_specs=[pl.BlockSpec((tm, D), lambda i, pos, zt: (i, 0))],
        out_specs=pl.BlockSpec(memory_space=pl.ANY),
        scratch_shapes=[pltpu.VMEM((EXPERT_TM, D), F32), pltpu.SemaphoreType.DMA, pltpu.SemaphoreType.DMA],
    )
    return pl.pallas_call(
        _dispatch_body, out_shape=jax.ShapeDtypeStruct((n_pad, D), F32), grid_spec=grid_spec,
        compiler_params=_cparams(("arbitrary",), has_side_effects=True, disable_bounds_checks=True),
        name="dispatch",
    )(pos, ztile, x1)


def _experts_body(te_ref, nu_ref, xs_ref, wg_ref, wu_ref, wd_ref, ys_ref, wgb_ref, wub_ref, wdb_ref):
    i = pl.program_id(0)
    live = i < nu_ref[0]

    @pl.when(live & ((i == 0) | (te_ref[i] != te_ref[jnp.maximum(i - 1, 0)])))
    def _():
        wgb_ref[...] = wg_ref[0].astype(BF16)
        wub_ref[...] = wu_ref[0].astype(BF16)
        wdb_ref[...] = wd_ref[0].astype(BF16)

    @pl.when(live)
    def _():
        xb = xs_ref[...].astype(BF16)
        hg = _nn(xb, wgb_ref[...])
        hu = _nn(xb, wub_ref[...])
        h = (jax.nn.silu(hg) * hu).astype(BF16)
        ys_ref[...] = _nn(h, wdb_ref[...])

    @pl.when(i >= nu_ref[0])
    def _():
        ys_ref[...] = jnp.zeros(ys_ref.shape, F32)


def _experts(tile_expert, n_used, xs, wg, wu, wd):
    npad, D = xs.shape
    tm = EXPERT_TM
    grid_spec = pltpu.PrefetchScalarGridSpec(
        num_scalar_prefetch=2, grid=(npad // tm,),
        in_specs=[pl.BlockSpec((tm, D), lambda i, te, nu: (jnp.minimum(i, nu[0] - 1), 0)),
                  pl.BlockSpec((1, D, D_FF), lambda i, te, nu: (te[i], 0, 0)),
                  pl.BlockSpec((1, D, D_FF), lambda i, te, nu: (te[i], 0, 0)),
                  pl.BlockSpec((1, D_FF, D), lambda i, te, nu: (te[i], 0, 0))],
        out_specs=pl.BlockSpec((tm, D), lambda i, te, nu: (i, 0)),
        scratch_shapes=[pltpu.VMEM((D, D_FF), BF16), pltpu.VMEM((D, D_FF), BF16), pltpu.VMEM((D_FF, D), BF16)],
    )
    return pl.pallas_call(
        _experts_body, out_shape=jax.ShapeDtypeStruct((npad, D), F32), grid_spec=grid_spec,
        compiler_params=_cparams(("arbitrary",)), name="experts",
    )(tile_expert, n_used, xs, wg, wu, wd)


def _combine_body(alpha, pos_ref, x1_ref, w_ref, eye_ref, g2_ref, b2_ref, ys_ref, o_ref, buf_ref, sem):
    tm = COMBINE_TM
    m_tok = pos_ref.shape[0] // 2
    base = pl.program_id(0) * tm

    def row_copy(slot, r, p):
        return pltpu.make_async_copy(ys_ref.at[pl.ds(p, 1), :], buf_ref.at[slot, pl.ds(r, 1), :], sem)

    def issue(r, carry):
        row_copy(0, r, pos_ref[base + r]).start()
        row_copy(1, r, pos_ref[m_tok + base + r]).start()
        return carry

    lax.fori_loop(0, tm, issue, 0, unroll=4)

    w = w_ref[...]
    eye = eye_ref[...]
    w_a = w.astype(BF16)
    w_b = (w - w_a.astype(F32)).astype(BF16)
    w_c = (w - w_a.astype(F32) - w_b.astype(F32)).astype(BF16)
    wcol = _nt(eye, w_a) + _nt(eye, w_b) + _nt(eye, w_c)

    def drain(r, carry):
        row_copy(0, 0, 0).wait()
        row_copy(1, 0, 0).wait()
        return carry

    lax.fori_loop(0, tm, drain, 0, unroll=16)
    moe = buf_ref[0] * wcol[:, 0:1] + buf_ref[1] * wcol[:, 1:2]
    o_ref[...] = _layer_norm(alpha * x1_ref[...] + moe, g2_ref[...], b2_ref[...])


def _combine(alpha, pos, x1, wgt, eye, g2, b2, ys):
    M, D = x1.shape
    tm = COMBINE_TM
    grid_spec = pltpu.PrefetchScalarGridSpec(
        num_scalar_prefetch=1, grid=(M // tm,),
        in_specs=[pl.BlockSpec((tm, D), lambda i, pos: (i, 0)),
                  pl.BlockSpec((8, tm), lambda i, pos: (0, i)),
                  pl.BlockSpec((tm, tm), lambda i, pos: (0, 0)),
                  pl.BlockSpec((1, D), lambda i, pos: (0, 0)),
                  pl.BlockSpec((1, D), lambda i, pos: (0, 0)),
                  pl.BlockSpec(memory_space=pl.ANY)],
        out_specs=pl.BlockSpec((tm, D), lambda i, pos: (i, 0)),
        scratch_shapes=[pltpu.VMEM((2, tm, D), F32), pltpu.SemaphoreType.DMA],
    )
    return pl.pallas_call(
        functools.partial(_combine_body, alpha), out_shape=jax.ShapeDtypeStruct((M, D), F32),
        grid_spec=grid_spec, compiler_params=_cparams(("arbitrary",), disable_bounds_checks=True),
        name="combine",
    )(pos, x1, wgt, eye, g2, b2, ys)


def _rope_tables(T, dim):
    half = dim // 2
    inv_freq = ROPE_THETA ** (-jnp.arange(half, dtype=F32) * 2.0 / dim)
    return inv_freq


def _tables(T):
    pos = jnp.arange(T).astype(F32)
    f64 = _rope_tables(T, NSA_HD)
    ang = pos[:, None] * f64[None, :]
    cos, sin = jnp.cos(ang), jnp.sin(ang)
    c64 = jnp.tile(cos, (1, 4))
    s64 = jnp.tile(jnp.concatenate([-sin, sin], axis=1), (1, 2))
    f128 = _rope_tables(T, RET_DK)
    ang2 = pos[:, None] * f128[None, :]
    cos2, sin2 = jnp.cos(ang2), jnp.sin(ang2)
    c128 = jnp.tile(cos2, (1, 2))
    s128 = jnp.concatenate([-sin2, sin2], axis=1)
    n_rows = T // CMP_STRIDE
    cmp_end = (jnp.arange(n_rows) * CMP_STRIDE + CMP_BLOCK - 1).astype(F32)
    angc = cmp_end[:, None] * f64[None, :]
    cc = jnp.tile(jnp.cos(angc), (1, 2))
    sc = jnp.tile(jnp.sin(angc), (1, 2))
    return (c64, s64, c128, s128, cos.T, sin.T, cos2.T, sin2.T), cc, sc


def _overlap_matrix(T):
    n_rows = T // CMP_STRIDE
    n_sel = T // SLC_BLOCK
    cmp_start = np.arange(n_rows) * CMP_STRIDE
    sel_start = np.arange(n_sel) * SLC_BLOCK
    ov = np.clip(np.minimum(cmp_start[None, :] + CMP_BLOCK, sel_start[:, None] + SLC_BLOCK)
                 - np.maximum(cmp_start[None, :], sel_start[:, None]), 0, None)
    return jnp.asarray(ov.astype(np.float32) / CMP_STRIDE, dtype=BF16)


def _retention_tables():
    C = RET_CHUNK
    gamma = 1.0 - 2.0 ** (-5.0 - jnp.arange(RET_HEADS, dtype=F32))
    log_g = jnp.log(gamma)
    i = jnp.arange(C, dtype=F32)
    diff = i[:, None] - i[None, :]
    dec = jnp.where(diff >= 0, jnp.exp(jnp.maximum(diff, 0.0) * log_g[:, None, None]), 0.0)
    xi = jnp.exp((i + 1.0) * log_g[:, None])
    zeta = jnp.exp((C - 1.0 - i) * log_g[:, None])
    cd = jnp.exp(C * log_g)
    xi_b = jnp.broadcast_to(xi[:, :, None], (RET_HEADS, C, RET_DV))
    cd_b = jnp.broadcast_to(cd[:, None, None], (RET_HEADS, 1, RET_DV))
    return dec, xi_b, zeta[:, None, :], cd_b


def _rot_half_cols(w):
    half = w.shape[-1] // 2
    return jnp.concatenate([-w[..., half:], w[..., :half]], axis=-1)


def kernel(x, w_in, cmp_pos_k, cmp_k_w1, cmp_k_b1, cmp_k_w2, cmp_k_b2, cmp_pos_v, cmp_v_w1, cmp_v_b1, cmp_v_w2, cmp_v_b2, ret_gn_g, ret_gn_b, w_up_attn, w_up_ret, w_out, ln1_g, ln1_b, router_group_w, router_group_b, router_inner_w, router_inner_b, expert_w_gate, expert_w_up, expert_w_down, ln2_g, ln2_b):
    B, T, D = x.shape
    M = B * T
    depth = w_in.shape[0]
    alpha = (2.0 * depth) ** 0.25
    tabs, cc, sc = _tables(T)
    ov = _overlap_matrix(T)
    dec, xi_b, zeta, cd_b = _retention_tables()
    tri = jnp.asarray(np.triu(np.ones((RANK_TB, RANK_TB), np.float32)), dtype=BF16)
    eye = jnp.asarray(np.eye(COMBINE_TM, dtype=np.float32), dtype=BF16)
    n_pad = 2 * M + N_EXPERTS * EXPERT_TM
    n_tiles = n_pad // EXPERT_TM

    for l in range(depth):
        w = w_in[l]
        col = lambda n: w[:, _OFF[n][0]:_OFF[n][1]]
        wn = jnp.concatenate([col("cmp_k"), col("cmp_v"), col("slc_k"), col("win_k"), col("ret_q"), col("ret_v")],
                             axis=1).astype(BF16)
        wt = jnp.concatenate([col("nsa_q"), col("slc_v"), col("win_v"), col("ret_k"), col("nsa_gate"),
                              jnp.zeros((D, 8), F32)], axis=1).T.astype(BF16)
        qT, ckv, sk, wk, svT, wvT, gT, rq, rkT, rv = _proj(x, wn, wt, tabs)

        kc, vcT = _compress(
            ckv, cmp_pos_k[l].reshape(2, -1), cmp_pos_v[l].reshape(2, -1),
            cmp_k_w1[l], cmp_k_b1[l][None, :], cmp_k_w2[l], _rot_half_cols(cmp_k_w2[l]),
            cmp_k_b2[l][None, :], _rot_half_cols(cmp_k_b2[l])[None, :], cc, sc,
            cmp_v_w1[l], cmp_v_b1[l][None, :], cmp_v_w2[l].T, cmp_v_b2[l][:, None])
        o_attn = _nsa(qT, kc, vcT, ov, sk, svT, wk, wvT, gT)
        o_ret = _retention(rq, rkT, rv, dec, xi_b, zeta, cd_b, ret_gn_g[l][None, :], ret_gn_b[l][None, :])

        wr = jnp.concatenate([router_inner_w[l].transpose(0, 2, 1).reshape(N_EXPERTS, D),
                              router_group_w[l].T, jnp.zeros((4, D), F32)], axis=0)
        wrh = wr.astype(BF16)
        wrl = (wr - wrh.astype(F32)).astype(BF16)
        rb = jnp.concatenate([router_inner_b[l].reshape(-1), router_group_b[l], jnp.zeros((4,), F32)])[:, None]
        x1, eid, wgt = _merge(
            alpha, x.reshape(M, D), o_attn.reshape(M, -1), o_ret.reshape(M, -1),
            col("ret_gate").astype(BF16), col("merge_gate").astype(BF16), w_up_attn[l].astype(BF16),
            w_up_ret[l].astype(BF16), w_out[l].astype(BF16), ln1_g[l][None, :], ln1_b[l][None, :], wrh, wrl, rb)

        eid_flat = eid[:2].reshape(1, 2 * M)
        rank, cnt = _rank(eid_flat, tri)
        counts = cnt[:, 0].astype(I32)
        tiles_per = (counts + EXPERT_TM - 1) // EXPERT_TM
        tile_end = jnp.cumsum(tiles_per)
        row_start = (tile_end - tiles_per) * EXPERT_TM
        pos = (row_start[eid_flat[0]] + rank[0]).astype(I32)
        tile_ids = jnp.arange(n_tiles, dtype=I32)
        tile_expert = jnp.minimum(jnp.sum((tile_end[None, :] <= tile_ids[:, None]).astype(I32), axis=1),
                                  N_EXPERTS - 1).astype(I32)
        n_used = tile_end[-1:].astype(I32)

        ztile = jnp.concatenate([jnp.where(tiles_per > 0, (tile_end - 1) * EXPERT_TM, -1), tile_end[-1:]]).astype(I32)
        xs = _dispatch(pos, ztile, x1, n_pad)
        ys = _experts(tile_expert, n_used, xs, expert_w_gate[l], expert_w_up[l], expert_w_down[l])
        x = _combine(alpha, pos, x1, wgt, eye, ln2_g[l][None, :], ln2_b[l][None, :], ys).reshape(B, T, D)
    return x
```

```python
import functools

import numpy as np
import jax
import jax.numpy as jnp
from jax import lax
from jax.experimental import pallas as pl
from jax.experimental.pallas import tpu as pltpu

F32 = jnp.float32
BF16 = jnp.bfloat16
I32 = jnp.int32

D_MODEL = 1024
NSA_HEADS = 8
NSA_HD = 64
NSA_GROUPS = 2
NSA_HPG = NSA_HEADS // NSA_GROUPS
CMP_BLOCK = 32
CMP_STRIDE = 16
CMP_HIDDEN = 256
SLC_BLOCK = 64
SLC_TOPK = 16
WINDOW = 512
RET_HEADS = 4
RET_DK = 128
RET_DV = 256
RET_CHUNK = 128
MOE_GROUPS = 4
EXPERTS_PER_GROUP = 8
N_EXPERTS = MOE_GROUPS * EXPERTS_PER_GROUP
D_FF = 512
ROPE_THETA = 10000.0
LN_EPS = 1e-5
GN_EPS = 1e-5
NEG_INF = -1e30
LOG2_E = 1.4426950408889634

VMEM_LIMIT_V7X = 56 * 1024 * 1024

_OFF = {}
_o = 0
for _n, _w in (("nsa_q", 512), ("cmp_k", 128), ("cmp_v", 128), ("slc_k", 128), ("slc_v", 128),
               ("win_k", 128), ("win_v", 128), ("nsa_gate", 24), ("ret_q", 512), ("ret_k", 512),
               ("ret_v", 1024), ("ret_gate", 1024), ("merge_gate", 2048)):
    _OFF[_n] = (_o, _o + _w)
    _o += _w

PROJ_TM = 512
NSA_TQ = 256
NSA_KC = 256
NSA_VROWS = 80
RET_STEP_CHUNKS = 4
MERGE_TM = 512
RANK_TB = 1024
DISPATCH_TM = 512
EXPERT_TM = 256
COMBINE_TM = 256


def _cparams(sem, **kw):
    return pltpu.CompilerParams(dimension_semantics=sem, vmem_limit_bytes=VMEM_LIMIT_V7X, **kw)


def _nt(a, b):
    return lax.dot_general(a, b, (((1,), (1,)), ((), ())), preferred_element_type=F32)


def _nn(a, b):
    return jnp.dot(a, b, preferred_element_type=F32)


def _proj_body(x_ref, wn_ref, wt_ref, c64_ref, s64_ref, c128_ref, s128_ref,
               ct64_ref, st64_ref, ct128_ref, st128_ref,
               qT_ref, ckv_ref, sk_ref, wk_ref, svT_ref, wvT_ref, gT_ref, rq_ref, rkT_ref, rv_ref):
    tm = PROJ_TM
    xb = x_ref[0].astype(BF16)

    def nn(a, b):
        return _nn(xb, wn_ref[:, a:b])

    def nt(a, b):
        return _nt(wt_ref[a:b, :], xb)

    ckv = nn(0, 256)
    for j in range(4):
        ckv_ref[0, j] = ckv[:, 64 * j:64 * (j + 1)]

    lane = lax.broadcasted_iota(I32, (tm, 128), 1)
    first = (lane % 64) < 32
    c64 = c64_ref[...]
    s64 = s64_ref[...]

    def rope64(k):
        rot = jnp.where(first, pltpu.roll(k, 96, 1), pltpu.roll(k, 32, 1))
        return k * c64 + rot * s64

    sk = rope64(nn(256, 384)).astype(BF16)
    sk_ref[0, 0] = sk[:, :64]
    sk_ref[0, 1] = sk[:, 64:]
    wk = rope64(nn(384, 512)).astype(BF16)
    wk_ref[0, 0] = wk[:, :64]
    wk_ref[0, 1] = wk[:, 64:]

    c128 = c128_ref[...]
    s128 = s128_ref[...]
    rq = nn(512, 1024)
    for h in range(RET_HEADS):
        ch = rq[:, 128 * h:128 * (h + 1)]
        rq_ref[0, :, 128 * h:128 * (h + 1)] = (ch * c128 + pltpu.roll(ch, 64, 1) * s128).astype(BF16)
    rv_ref[0] = nn(1024, 2048).astype(BF16)

    ct = ct64_ref[...]
    st = st64_ref[...]
    qT = nt(0, 512)
    scale_q = NSA_HD ** -0.5 * LOG2_E
    tq = NSA_TQ
    for hh in range(NSA_HEADS):
        g, h = divmod(hh, NSA_HPG)
        x1 = qT[64 * hh:64 * hh + 32]
        x2 = qT[64 * hh + 32:64 * hh + 64]
        o1 = ((x1 * ct - x2 * st) * scale_q).astype(BF16)
        o2 = ((x1 * st + x2 * ct) * scale_q).astype(BF16)
        for it in range(tm // tq):
            qT_ref[0, g, it, 0:32, h * tq:(h + 1) * tq] = o1[:, it * tq:(it + 1) * tq]
            qT_ref[0, g, it, 32:64, h * tq:(h + 1) * tq] = o2[:, it * tq:(it + 1) * tq]

    svT = nt(512, 640).astype(BF16)
    wvT = nt(640, 768).astype(BF16)
    row16 = lax.broadcasted_iota(I32, (NSA_VROWS - NSA_HD, NSA_KC), 0)
    ones_blk = jnp.where(row16 == 0, 1.0, 0.0).astype(BF16)
    for c in range(tm // NSA_KC):
        for g in range(NSA_GROUPS):
            for vT, ref in ((svT, svT_ref), (wvT, wvT_ref)):
                ref[0, c, g, 0:NSA_HD, :] = vT[64 * g:64 * (g + 1), NSA_KC * c:NSA_KC * (c + 1)]
                ref[0, c, g, NSA_HD:NSA_VROWS, :] = ones_blk

    ct2 = ct128_ref[...]
    st2 = st128_ref[...]
    rkT = nt(768, 1280)
    scale_k = RET_DK ** -0.5
    for h in range(RET_HEADS):
        x1 = rkT[128 * h:128 * h + 64]
        x2 = rkT[128 * h + 64:128 * h + 128]
        o1 = (x1 * ct2 - x2 * st2) * scale_k
        o2 = (x1 * st2 + x2 * ct2) * scale_k
        for c in range(tm // RET_CHUNK):
            rkT_ref[0, c, 128 * h:128 * h + 64, :] = o1[:, 128 * c:128 * (c + 1)]
            rkT_ref[0, c, 128 * h + 64:128 * h + 128, :] = o2[:, 128 * c:128 * (c + 1)]

    gT_ref[0] = jax.nn.sigmoid(nt(1280, 1312))


def _proj(x, wn, wt, tabs):
    B, T, D = x.shape
    tm = PROJ_TM
    c64, s64, c128, s128, ct64, st64, ct128, st128 = tabs
    const = lambda b, i: (0, 0)
    in_specs = [
        pl.BlockSpec((1, tm, D), lambda b, i: (b, i, 0)),
        pl.BlockSpec(wn.shape, const),
        pl.BlockSpec(wt.shape, const),
        pl.BlockSpec((tm, 128), lambda b, i: (i, 0)),
        pl.BlockSpec((tm, 128), lambda b, i: (i, 0)),
        pl.BlockSpec((tm, 128), lambda b, i: (i, 0)),
        pl.BlockSpec((tm, 128), lambda b, i: (i, 0)),
        pl.BlockSpec((32, tm), lambda b, i: (0, i)),
        pl.BlockSpec((32, tm), lambda b, i: (0, i)),
        pl.BlockSpec((64, tm), lambda b, i: (0, i)),
        pl.BlockSpec((64, tm), lambda b, i: (0, i)),
    ]
    out_shape = [
        jax.ShapeDtypeStruct((B, NSA_GROUPS, T // NSA_TQ, NSA_HD, NSA_HPG * NSA_TQ), BF16),
        jax.ShapeDtypeStruct((B, 4, T, 64), F32),
        jax.ShapeDtypeStruct((B, 2, T, 64), BF16),
        jax.ShapeDtypeStruct((B, 2, T, 64), BF16),
        jax.ShapeDtypeStruct((B, T // NSA_KC, NSA_GROUPS, NSA_VROWS, NSA_KC), BF16),
        jax.ShapeDtypeStruct((B, T // NSA_KC, NSA_GROUPS, NSA_VROWS, NSA_KC), BF16),
        jax.ShapeDtypeStruct((B, 32, T), F32),
        jax.ShapeDtypeStruct((B, T, 512), BF16),
        jax.ShapeDtypeStruct((B, T // RET_CHUNK, 512, RET_CHUNK), F32),
        jax.ShapeDtypeStruct((B, T, 1024), BF16),
    ]
    out_specs = [
        pl.BlockSpec((1, NSA_GROUPS, tm // NSA_TQ, NSA_HD, NSA_HPG * NSA_TQ), lambda b, i: (b, 0, i, 0, 0)),
        pl.BlockSpec((1, 4, tm, 64), lambda b, i: (b, 0, i, 0)),
        pl.BlockSpec((1, 2, tm, 64), lambda b, i: (b, 0, i, 0)),
        pl.BlockSpec((1, 2, tm, 64), lambda b, i: (b, 0, i, 0)),
        pl.BlockSpec((1, tm // NSA_KC, NSA_GROUPS, NSA_VROWS, NSA_KC), lambda b, i: (b, i, 0, 0, 0)),
        pl.BlockSpec((1, tm // NSA_KC, NSA_GROUPS, NSA_VROWS, NSA_KC), lambda b, i: (b, i, 0, 0, 0)),
        pl.BlockSpec((1, 32, tm), lambda b, i: (b, 0, i)),
        pl.BlockSpec((1, tm, 512), lambda b, i: (b, i, 0)),
        pl.BlockSpec((1, tm // RET_CHUNK, 512, RET_CHUNK), lambda b, i: (b, i, 0, 0)),
        pl.BlockSpec((1, tm, 1024), lambda b, i: (b, i, 0)),
    ]
    return pl.pallas_call(
        _proj_body, out_shape=out_shape, grid=(B, T // tm), in_specs=in_specs, out_specs=out_specs,
        compiler_params=_cparams(("parallel", "parallel")), name="proj",
    )(x, wn, wt, c64, s64, c128, s128, ct64, st64, ct128, st128)


def _compress_body(uk_ref, uv_ref, posk_ref, posv_ref, w1k_ref, b1k_ref, w2k_ref, w2kr_ref, b2k_ref, b2kr_ref,
                   cc_ref, sc_ref, w1v_ref, b1v_ref, w2vT_ref, b2vT_ref, kc_ref, vcT_ref):
    half = CMP_STRIDE * NSA_HD

    def hidden(u_ref, pos_ref, w1_ref, b1_ref):
        u = u_ref[0, 0]
        top = (u + pos_ref[0:1, :]).astype(BF16)
        bot = (u + pos_ref[1:2, :]).astype(BF16)
        a = _nn(top, w1_ref[0:half, :].astype(BF16))
        bm = _nn(bot, w1_ref[half:2 * half, :].astype(BF16))
        n = bm.shape[0]
        return jax.nn.gelu(a + pltpu.roll(bm, n - 1, 0) + b1_ref[...])

    hk = hidden(uk_ref, posk_ref, w1k_ref, b1k_ref).astype(BF16)
    k = _nn(hk, w2k_ref[...].astype(BF16)) + b2k_ref[...]
    kr = _nn(hk, w2kr_ref[...].astype(BF16)) + b2kr_ref[...]
    kc_ref[0, 0] = (k * cc_ref[...] + kr * sc_ref[...]).astype(BF16)

    hv = hidden(uv_ref, posv_ref, w1v_ref, b1v_ref).astype(BF16)
    vcT_ref[0, 0] = (_nt(w2vT_ref[...].astype(BF16), hv) + b2vT_ref[...]).astype(BF16)


def _compress(ckv, posk, posv, w1k, b1k, w2k, w2kr, b2k, b2kr, cc, sc, w1v, b1v, w2vT, b2vT):
    B = ckv.shape[0]
    T = ckv.shape[2]
    nr = T // CMP_STRIDE
    u = ckv.reshape(B, 4, nr, CMP_STRIDE * NSA_HD)
    full = lambda a: pl.BlockSpec(a.shape, lambda b, g: (0,) * a.ndim)
    in_specs = [
        pl.BlockSpec((1, 1, nr, 1024), lambda b, g: (b, g, 0, 0)),
        pl.BlockSpec((1, 1, nr, 1024), lambda b, g: (b, g + 2, 0, 0)),
    ] + [full(a) for a in (posk, posv, w1k, b1k, w2k, w2kr, b2k, b2kr, cc, sc, w1v, b1v, w2vT, b2vT)]
    out_shape = [jax.ShapeDtypeStruct((B, NSA_GROUPS, nr, NSA_HD), BF16),
                 jax.ShapeDtypeStruct((B, NSA_GROUPS, NSA_HD, nr), BF16)]
    out_specs = [pl.BlockSpec((1, 1, nr, NSA_HD), lambda b, g: (b, g, 0, 0)),
                 pl.BlockSpec((1, 1, NSA_HD, nr), lambda b, g: (b, g, 0, 0))]
    return pl.pallas_call(
        _compress_body, out_shape=out_shape, grid=(B, NSA_GROUPS), in_specs=in_specs, out_specs=out_specs,
        compiler_params=_cparams(("parallel", "parallel")), name="compress",
    )(u, u, posk, posv, w1k, b1k, w2k, w2kr, b2k, b2kr, cc, sc, w1v, b1v, w2vT, b2vT)


def _nsa_body(q_ref, kc_ref, vcT_ref, ov_ref, sk_ref, svT_ref, wk_ref, wvT_ref, gT_ref, lo_ref, hi_ref, o_ref,
              score_ref, bias_ref, acc_ref, tot_ref, s_ref):
    tq = NSA_TQ
    kc_n = NSA_KC
    hq = NSA_HPG * tq
    groups = range(NSA_GROUPS)
    i = pl.program_id(1)
    t0 = i * tq
    t_row = t0 + lax.broadcasted_iota(I32, (1, tq), 1)
    t_row4 = t0 + lax.broadcasted_iota(I32, (1, hq), 1) % tq
    n_cmp = kc_ref.shape[2]
    n_sel = ov_ref.shape[0]
    q = [q_ref[0, g, 0] for g in groups]

    def gate4(g, br):
        return jnp.concatenate([gT_ref[0, (g * NSA_HPG + h) * 3 + br:(g * NSA_HPG + h) * 3 + br + 1, :]
                                for h in range(NSA_HPG)], axis=1)

    cmp_end = lax.broadcasted_iota(I32, (n_cmp, 1), 0) * CMP_STRIDE + (CMP_BLOCK - 1)
    cbias = jnp.where(cmp_end <= t_row4, 0.0, NEG_INF)
    any_valid = jnp.where(t_row4 >= CMP_BLOCK - 1, 1.0, 0.0)
    j_col = lax.broadcasted_iota(I32, (n_sel, 1), 0)
    blk_t = t_row // SLC_BLOCK
    forced = (j_col == 0) | (j_col == blk_t) | (j_col == blk_t - 1)
    sub = lax.broadcasted_iota(I32, (8, 1), 0)
    n_slab = n_sel // 8

    def select_blocks(g):
        s = _nn(kc_ref[0, g], q[g]) + cbias
        e = jnp.exp2(s - jnp.max(s, axis=0, keepdims=True))
        p = e * (any_valid / jnp.sum(e, axis=0, keepdims=True))
        tot_ref[g] = gate4(g, 0) * _nn(vcT_ref[0, g], p.astype(BF16))
        psum = p[:, 0:tq]
        for h in range(1, NSA_HPG):
            psum = psum + p[:, h * tq:(h + 1) * tq]
        p_hi = psum.astype(BF16)
        p_lo = (psum - p_hi.astype(F32)).astype(BF16)
        ov = ov_ref[...]
        imp = _nn(ov, p_hi) + _nn(ov, p_lo)
        score = jnp.where(forced, 1e9, jnp.where(j_col <= blk_t, imp, -1e9)).astype(F32)
        score_ref[g] = score
        slabs = [score[8 * v:8 * (v + 1)] for v in range(n_slab)]
        cnt = [jnp.zeros((8, tq), F32) for _ in range(n_slab)]
        for r in range(n_sel):
            row = jnp.broadcast_to(score_ref[g, r:r + 1, :], (8, tq))
            for v in range(n_slab):
                if r < 8 * v:
                    ahead = jnp.where(row >= slabs[v], 1.0, 0.0)
                elif r >= 8 * (v + 1):
                    ahead = jnp.where(row > slabs[v], 1.0, 0.0)
                else:
                    ahead = jnp.where(sub + 8 * v > r, jnp.where(row >= slabs[v], 1.0, 0.0),
                                      jnp.where(row > slabs[v], 1.0, 0.0))
                cnt[v] = cnt[v] + ahead
        for v in range(n_slab):
            bias_ref[g, 8 * v:8 * (v + 1), :] = jnp.where(
                cnt[v] < float(SLC_TOPK), jnp.where(sub + 8 * v <= blk_t, 0.0, NEG_INF), NEG_INF)

    for g in groups:
        select_blocks(g)

    def qk(k_ref, g, c):
        k_c = k_ref[0, g, pl.ds(pl.multiple_of(c * kc_n, kc_n), kc_n), :]
        return [_nn(k_c, q[g][:, h * tq:(h + 1) * tq]) for h in range(NSA_HPG)]

    def stage_a(scores, add_bias, m):
        biased = [add_bias(scores[h]) for h in range(NSA_HPG)]
        m_new = tuple(jnp.maximum(m[h], jnp.max(biased[h], axis=0, keepdims=True)) for h in range(NSA_HPG))
        return biased, m_new

    def stage_b(g, biased, vT_c, m_old, m_new):
        for h in range(NSA_HPG):
            hs = slice(h * tq, (h + 1) * tq)
            alpha = jnp.exp2(m_old[h] - m_new[h])
            p = jnp.exp2((biased[h] - m_new[h]).astype(BF16))
            acc_ref[g, :, hs] = alpha * acc_ref[g, :, hs] + _nn(vT_c, p)

    nb = kc_n // SLC_BLOCK

    def block_bias(g, c, s):
        return jnp.concatenate([s[SLC_BLOCK * b:SLC_BLOCK * (b + 1)] + bias_ref[g, pl.ds(c * nb + b, 1), :]
                                for b in range(nb)], axis=0)

    def finish(g, br):
        acc = acc_ref[g]
        tot_ref[g] = tot_ref[g] + gate4(g, br) * (acc[0:NSA_HD] / acc[NSA_HD:NSA_HD + 1])
        acc_ref[g] = jnp.zeros(acc.shape, F32)

    def park(g, biased):
        for h in range(NSA_HPG):
            s_ref[g, h] = biased[h]

    def parked(g):
        return [s_ref[g, h] for h in range(NSA_HPG)]

    m0 = tuple(jnp.full((1, tq), NEG_INF, F32) for _ in range(NSA_HPG))

    acc_ref[...] = jnp.zeros(acc_ref.shape, F32)
    m_new = []
    for g in groups:
        biased, m_g = stage_a(qk(sk_ref, g, i), lambda s, g=g: block_bias(g, i, s) + lo_ref[...], m0)
        park(g, biased)
        m_new.append(m_g)

    def slc_chunk(c, carry):
        prev, m_old, m_new = carry
        scores = [qk(sk_ref, g, c) for g in groups]
        for g in groups:
            stage_b(g, parked(g), svT_ref[0, prev, g], m_old[g], m_new[g])
        m_next = []
        for g in groups:
            biased, m_g = stage_a(scores[g], functools.partial(block_bias, g, c), m_new[g])
            park(g, biased)
            m_next.append(m_g)
        return c, m_new, tuple(m_next)

    prev, m_old, m_new = lax.fori_loop(0, i, slc_chunk, (i, (m0,) * NSA_GROUPS, tuple(m_new)))

    c_far = jnp.maximum(i - 2, 0)
    c_near = jnp.maximum(i - 1, 0)
    pen_far = jnp.where(i >= 2, 0.0, NEG_INF).astype(F32)
    pen_near = jnp.where(i >= 1, 0.0, NEG_INF).astype(F32)

    scores = [qk(wk_ref, g, c_far) for g in groups]
    far = []
    for g in groups:
        stage_b(g, parked(g), svT_ref[0, prev, g], m_old[g], m_new[g])
        far.append(stage_a(scores[g], lambda s: s + (hi_ref[...] + pen_far), m0))
        finish(g, 1)

    scores = [qk(wk_ref, g, c_near) for g in groups]
    near = []
    for g in groups:
        stage_b(g, far[g][0], wvT_ref[0, c_far, g], m0, far[g][1])
        near.append(stage_a(scores[g], lambda s: s + pen_near, far[g][1]))
    scores = [qk(wk_ref, g, i) for g in groups]
    for g in groups:
        stage_b(g, near[g][0], wvT_ref[0, c_near, g], far[g][1], near[g][1])
        b_diag, m_diag = stage_a(scores[g], lambda s: s + lo_ref[...], near[g][1])
        stage_b(g, b_diag, wvT_ref[0, i, g], near[g][1], m_diag)
        finish(g, 2)

    o_ref[0] = jnp.concatenate([tot_ref[g, :, h * tq:(h + 1) * tq] for g in groups for h in range(NSA_HPG)],
                               axis=0).T.astype(BF16)


def _nsa(qT, kc, vcT, ov, sk, svT, wk, wvT, gT):
    B, _, T = gT.shape
    tq = NSA_TQ
    assert NSA_KC == tq and WINDOW == 2 * NSA_KC
    nr = kc.shape[2]
    nch = T // NSA_KC
    kk = np.arange(NSA_KC)[:, None]
    tt = np.arange(tq)[None, :]
    lo = jnp.asarray(np.where(kk <= tt, 0.0, NEG_INF), dtype=F32)
    hi = jnp.asarray(np.where(kk > tt, 0.0, NEG_INF), dtype=F32)
    G = NSA_GROUPS
    in_specs = [
        pl.BlockSpec((1, G, 1, NSA_HD, NSA_HPG * tq), lambda b, i: (b, 0, i, 0, 0)),
        pl.BlockSpec((1, G, nr, NSA_HD), lambda b, i: (b, 0, 0, 0)),
        pl.BlockSpec((1, G, NSA_HD, nr), lambda b, i: (b, 0, 0, 0)),
        pl.BlockSpec(ov.shape, lambda b, i: (0, 0)),
        pl.BlockSpec((1, G, T, NSA_HD), lambda b, i: (b, 0, 0, 0)),
        pl.BlockSpec((1, nch, G, NSA_VROWS, NSA_KC), lambda b, i: (b, 0, 0, 0, 0)),
        pl.BlockSpec((1, G, T, NSA_HD), lambda b, i: (b, 0, 0, 0)),
        pl.BlockSpec((1, nch, G, NSA_VROWS, NSA_KC), lambda b, i: (b, 0, 0, 0, 0)),
        pl.BlockSpec((1, 32, tq), lambda b, i: (b, 0, i)),
        pl.BlockSpec(lo.shape, lambda b, i: (0, 0)),
        pl.BlockSpec(hi.shape, lambda b, i: (0, 0)),
    ]
    n_sel = T // SLC_BLOCK
    hq = NSA_HPG * tq
    return pl.pallas_call(
        _nsa_body, out_shape=jax.ShapeDtypeStruct((B, T, NSA_HEADS * NSA_HD), BF16),
        grid=(B, T // tq), in_specs=in_specs,
        out_specs=pl.BlockSpec((1, tq, NSA_HEADS * NSA_HD), lambda b, i: (b, i, 0)),
        scratch_shapes=[pltpu.VMEM((G, n_sel, tq), F32), pltpu.VMEM((G, n_sel, tq), F32),
                        pltpu.VMEM((G, NSA_VROWS, hq), F32), pltpu.VMEM((G, NSA_HD, hq), F32),
                        pltpu.VMEM((G, NSA_HPG, NSA_KC, tq), F32)],
        compiler_params=_cparams(("parallel", "parallel")), name="nsa",
    )(qT, kc, vcT, ov, sk, svT, wk, wvT, gT, lo, hi)


def _ret_body(q_ref, kT_ref, v_ref, dec_ref, xi_ref, zeta_ref, cd_ref, gg_ref, gb_ref, o_ref, r_ref):
    C = RET_CHUNK

    @pl.when(pl.program_id(1) == 0)
    def _():
        r_ref[...] = jnp.zeros(r_ref.shape, F32)

    for h in range(RET_HEADS):
        dk = slice(h * RET_DK, (h + 1) * RET_DK)
        dv = slice(h * RET_DV, (h + 1) * RET_DV)
        dec = dec_ref[h]
        xi = xi_ref[h]
        zeta = zeta_ref[h]
        cd = cd_ref[h]
        gg = gg_ref[:, dv]
        gb = gb_ref[:, dv]
        r = r_ref[h]
        for n in range(RET_STEP_CHUNKS):
            rows = slice(n * C, (n + 1) * C)
            qc = q_ref[0, rows, dk]
            kT = kT_ref[0, n, dk, :]
            vc = v_ref[0, rows, dv]
            s = _nn(qc, kT.astype(BF16)) * dec
            o = _nn(s.astype(BF16), vc) + _nn(qc, r.astype(BF16)) * xi
            r = r * cd + _nn((kT * zeta).astype(BF16), vc)
            mu = jnp.mean(o, axis=-1, keepdims=True)
            var = jnp.mean(jnp.square(o - mu), axis=-1, keepdims=True)
            o_ref[0, rows, dv] = (o - mu) * lax.rsqrt(var + GN_EPS) * gg + gb
        r_ref[h] = r


def _retention(rq, rkT, rv, dec, xi, zeta, cd, gn_g, gn_b):
    B, T, _ = rq.shape
    ts = RET_STEP_CHUNKS * RET_CHUNK
    full = lambda a: pl.BlockSpec(a.shape, lambda b, j: (0,) * a.ndim)
    in_specs = [
        pl.BlockSpec((1, ts, RET_HEADS * RET_DK), lambda b, j: (b, j, 0)),
        pl.BlockSpec((1, RET_STEP_CHUNKS, RET_HEADS * RET_DK, RET_CHUNK), lambda b, j: (b, j, 0, 0)),
        pl.BlockSpec((1, ts, RET_HEADS * RET_DV), lambda b, j: (b, j, 0)),
    ] + [full(a) for a in (dec, xi, zeta, cd, gn_g, gn_b)]
    return pl.pallas_call(
        _ret_body, out_shape=jax.ShapeDtypeStruct((B, T, RET_HEADS * RET_DV), F32),
        grid=(B, T // ts), in_specs=in_specs,
        out_specs=pl.BlockSpec((1, ts, RET_HEADS * RET_DV), lambda b, j: (b, j, 0)),
        scratch_shapes=[pltpu.VMEM((RET_HEADS, RET_DK, RET_DV), F32)],
        compiler_params=_cparams(("parallel", "arbitrary")), name="retention",
    )(rq, rkT, rv, dec, xi, zeta, cd, gn_g, gn_b)


def _layer_norm(y, g, b):
    mu = jnp.mean(y, axis=-1, keepdims=True)
    var = jnp.mean(jnp.square(y - mu), axis=-1, keepdims=True)
    return (y - mu) * lax.rsqrt(var + LN_EPS) * g + b


def _merge_body(alpha, x_ref, oa_ref, or_ref, wrg_ref, wmg_ref, wua_ref, wur_ref, wo_ref, g1_ref, b1_ref,
                wrh_ref, wrl_ref, rb_ref, x1_ref, eid_ref, wgt_ref):
    x = x_ref[...]
    xb = x.astype(BF16)
    rgate = jax.nn.silu(_nn(xb, wrg_ref[...]))
    o_ret = (or_ref[...] * rgate).astype(BF16)
    a = _nn(oa_ref[...], wua_ref[...])
    r = _nn(o_ret, wur_ref[...])
    mg = jax.nn.sigmoid(_nn(xb, wmg_ref[...]))
    merged = mg[:, :D_MODEL] * a + mg[:, D_MODEL:] * r
    mix = _nn(merged.astype(BF16), wo_ref[...])
    x1 = _layer_norm(alpha * x + mix, g1_ref[...], b1_ref[...])
    x1_ref[...] = x1

    xh = x1.astype(BF16)
    xl = (x1 - xh.astype(F32)).astype(BF16)
    wh = wrh_ref[...]
    lg = _nt(wh, xh) + _nt(wh, xl) + _nt(wrl_ref[...], xh) + rb_ref[...]
    ne = N_EXPERTS
    gl = lg[ne:ne + MOE_GROUPS]
    ge = jnp.exp(gl - jnp.max(gl, axis=0, keepdims=True))
    pg = ge / jnp.sum(ge, axis=0, keepdims=True)
    g_prob = jnp.max(pg, axis=0, keepdims=True)
    gi = lax.broadcasted_iota(I32, pg.shape, 0)
    g_idx = jnp.min(jnp.where(pg == g_prob, gi, MOE_GROUPS), axis=0, keepdims=True)
    inner = jnp.zeros((EXPERTS_PER_GROUP, lg.shape[1]), F32)
    for gq in range(MOE_GROUPS):
        inner = inner + jnp.where(g_idx == gq, lg[8 * gq:8 * (gq + 1)], 0.0)
    ei = lax.broadcasted_iota(I32, inner.shape, 0)
    m1 = jnp.max(inner, axis=0, keepdims=True)
    i1 = jnp.min(jnp.where(inner == m1, ei, EXPERTS_PER_GROUP), axis=0, keepdims=True)
    rest = jnp.where(ei == i1, -jnp.inf, inner)
    m2 = jnp.max(rest, axis=0, keepdims=True)
    i2 = jnp.min(jnp.where(rest == m2, ei, EXPERTS_PER_GROUP), axis=0, keepdims=True)
    e2 = jnp.exp(m2 - m1)
    den = 1.0 + e2
    w1 = (1.0 / den) * g_prob
    w2 = (e2 / den) * g_prob
    zi = jnp.zeros((6, lg.shape[1]), I32)
    eid_ref[...] = jnp.concatenate([g_idx * EXPERTS_PER_GROUP + i1, g_idx * EXPERTS_PER_GROUP + i2, zi], axis=0)
    wgt_ref[...] = jnp.concatenate([w1, w2, jnp.zeros((6, lg.shape[1]), F32)], axis=0)


def _merge(alpha, x2, oa2, or2, wrg, wmg, wua, wur, wo, g1, b1, wrh, wrl, rb):
    M, D = x2.shape
    tm = MERGE_TM
    full = lambda a: pl.BlockSpec(a.shape, lambda i: (0,) * a.ndim)
    in_specs = [pl.BlockSpec((tm, D), lambda i: (i, 0)),
                pl.BlockSpec((tm, oa2.shape[1]), lambda i: (i, 0)),
                pl.BlockSpec((tm, or2.shape[1]), lambda i: (i, 0))] + [
        full(a) for a in (wrg, wmg, wua, wur, wo, g1, b1, wrh, wrl, rb)]
    out_shape = [jax.ShapeDtypeStruct((M, D), F32),
                 jax.ShapeDtypeStruct((8, M), I32),
                 jax.ShapeDtypeStruct((8, M), F32)]
    out_specs = [pl.BlockSpec((tm, D), lambda i: (i, 0)),
                 pl.BlockSpec((8, tm), lambda i: (0, i)),
                 pl.BlockSpec((8, tm), lambda i: (0, i))]
    return pl.pallas_call(
        functools.partial(_merge_body, alpha), out_shape=out_shape, grid=(M // tm,),
        in_specs=in_specs, out_specs=out_specs,
        compiler_params=_cparams(("parallel",)), name="merge",
    )(x2, oa2, or2, wrg, wmg, wua, wur, wo, g1, b1, wrh, wrl, rb)


def _rank_body(eid_ref, tri_ref, low_ref, pos_ref, cnt_ref, carry_ref, off_ref):
    p = pl.program_id(0)
    j = pl.program_id(1)
    e = eid_ref[...]
    rows = lax.broadcasted_iota(I32, (N_EXPERTS, e.shape[1]), 0)
    hit = rows == e
    per_expert = jnp.sum(jnp.where(hit, 1.0, 0.0), axis=1, keepdims=True)

    @pl.when((p == 0) & (j == 0))
    def _():
        carry_ref[...] = jnp.zeros(carry_ref.shape, F32)

    @pl.when(p == 0)
    def _():
        pos_ref[...] = jnp.zeros(pos_ref.shape, I32)
        carry_ref[...] = carry_ref[...] + per_expert
        cnt_ref[...] = carry_ref[...]

    @pl.when((p == 1) & (j == 0))
    def _():
        tiles = jnp.floor((carry_ref[...] + (EXPERT_TM - 1)) * (1.0 / EXPERT_TM))
        off_ref[...] = _nn(low_ref[...], tiles.astype(BF16)) * EXPERT_TM
        carry_ref[...] = jnp.zeros(carry_ref.shape, F32)

    @pl.when(p == 1)
    def _():
        onehot = jnp.where(hit, 1.0, 0.0).astype(BF16)
        incl = _nn(onehot, tri_ref[...])
        base = carry_ref[:, 0:1] + off_ref[:, 0:1] - 1.0
        pos_ref[...] = jnp.sum(jnp.where(hit, incl + base, 0.0), axis=0, keepdims=True).astype(I32)
        carry_ref[...] = carry_ref[...] + per_expert


def _rank(eid_flat, tri, low):
    n = eid_flat.shape[1]
    tb = RANK_TB
    return pl.pallas_call(
        _rank_body,
        out_shape=[jax.ShapeDtypeStruct((1, n), I32), jax.ShapeDtypeStruct((N_EXPERTS, 128), F32)],
        grid=(2, n // tb),
        in_specs=[pl.BlockSpec((1, tb), lambda p, j: (0, j)), pl.BlockSpec((tb, tb), lambda p, j: (0, 0)),
                  pl.BlockSpec(low.shape, lambda p, j: (0, 0))],
        out_specs=[pl.BlockSpec((1, tb), lambda p, j: (0, j * p)),
                   pl.BlockSpec((N_EXPERTS, 128), lambda p, j: (0, 0))],
        scratch_shapes=[pltpu.VMEM((N_EXPERTS, 128), F32), pltpu.VMEM((N_EXPERTS, 128), F32)],
        compiler_params=_cparams(("arbitrary", "arbitrary")), name="rank",
    )(eid_flat, tri, low)


def _dispatch_body(pos_ref, ztile_ref, x_ref, xs_ref, zbuf_ref, sem, zsem):
    tm = DISPATCH_TM
    m_tok = pos_ref.shape[0] // 2
    base = pl.program_id(0) * tm

    @pl.when(pl.program_id(0) == 0)
    def _():
        zbuf_ref[...] = jnp.zeros(zbuf_ref.shape, F32)

        def zero_copy(e):
            z = pl.multiple_of(jnp.maximum(ztile_ref[e], 0), EXPERT_TM)
            return pltpu.make_async_copy(zbuf_ref, xs_ref.at[pl.ds(z, EXPERT_TM), :], zsem)

        for e in range(N_EXPERTS):
            @pl.when(ztile_ref[e] >= 0)
            def _():
                zero_copy(e).start()
        for e in range(N_EXPERTS):
            @pl.when(ztile_ref[e] >= 0)
            def _():
                zero_copy(e).wait()

        def tail_copy(t):
            return pltpu.make_async_copy(
                zbuf_ref, xs_ref.at[pl.ds(pl.multiple_of(t * EXPERT_TM, EXPERT_TM), EXPERT_TM), :], zsem)

        n_tiles = xs_ref.shape[0] // EXPERT_TM
        lax.fori_loop(ztile_ref[N_EXPERTS], n_tiles, lambda t, c: (tail_copy(t).start(), c)[1], 0)
        lax.fori_loop(ztile_ref[N_EXPERTS], n_tiles, lambda t, c: (tail_copy(t).wait(), c)[1], 0)

    def row_copy(r, p):
        return pltpu.make_async_copy(x_ref.at[pl.ds(r, 1), :], xs_ref.at[pl.ds(p, 1), :], sem)

    def issue(r, carry):
        row_copy(r, pos_ref[base + r]).start()
        row_copy(r, pos_ref[m_tok + base + r]).start()
        return carry

    lax.fori_loop(0, tm, issue, 0, unroll=16)

    def drain(r, carry):
        row_copy(0, 0).wait()
        row_copy(0, 0).wait()
        return carry

    lax.fori_loop(0, tm, drain, 0, unroll=16)


def _dispatch(pos, ztile, x1, n_pad):
    M, D = x1.shape
    tm = DISPATCH_TM
    grid_spec = pltpu.PrefetchScalarGridSpec(
        num_scalar_prefetch=2, grid=(M // tm,),
        in_specs=[pl.BlockSpec((tm, D), lambda i, pos, zt: (i, 0))],
        out_specs=pl.BlockSpec(memory_space=pl.ANY),
        scratch_shapes=[pltpu.VMEM((EXPERT_TM, D), F32), pltpu.SemaphoreType.DMA, pltpu.SemaphoreType.DMA],
    )
    return pl.pallas_call(
        _dispatch_body, out_shape=jax.ShapeDtypeStruct((n_pad, D), F32), grid_spec=grid_spec,
        compiler_params=_cparams(("arbitrary",), has_side_effects=True, disable_bounds_checks=True),
        name="dispatch",
    )(pos, ztile, x1)


def _experts_body(te_ref, nu_ref, seg_ref, nxt_ref, xs_ref, wg_hbm, wu_hbm, wd_hbm, ys_ref,
                  wgb_ref, wub_ref, wdb_ref, sg_ref, su_ref, sd_ref, sem):
    i = pl.program_id(0)
    live = i < nu_ref[0]

    def fetch(e, slot):
        return (pltpu.make_async_copy(wg_hbm.at[e], sg_ref.at[slot], sem.at[0, slot]),
                pltpu.make_async_copy(wu_hbm.at[e], su_ref.at[slot], sem.at[1, slot]),
                pltpu.make_async_copy(wd_hbm.at[e], sd_ref.at[slot], sem.at[2, slot]))

    @pl.when(live & (i == 0))
    def _():
        for c in fetch(te_ref[0], 0):
            c.start()

    @pl.when(live & (seg_ref[i] >= 0))
    def _():
        slot = seg_ref[i]
        for c in fetch(te_ref[i], slot):
            c.wait()
        wgb_ref[...] = sg_ref[slot].astype(BF16)
        wub_ref[...] = su_ref[slot].astype(BF16)
        wdb_ref[...] = sd_ref[slot].astype(BF16)

        @pl.when(nxt_ref[i] >= 0)
        def _():
            for c in fetch(nxt_ref[i], 1 - slot):
                c.start()

    @pl.when(live)
    def _():
        xb = xs_ref[...].astype(BF16)
        hg = _nn(xb, wgb_ref[...])
        hu = _nn(xb, wub_ref[...])
        h = (jax.nn.silu(hg) * hu).astype(BF16)
        ys_ref[...] = _nn(h, wdb_ref[...])

    @pl.when(i >= nu_ref[0])
    def _():
        ys_ref[...] = jnp.zeros(ys_ref.shape, F32)


def _experts(tile_expert, n_used, seg_slot, next_expert, xs, wg, wu, wd):
    npad, D = xs.shape
    tm = EXPERT_TM
    grid_spec = pltpu.PrefetchScalarGridSpec(
        num_scalar_prefetch=4, grid=(npad // tm,),
        in_specs=[pl.BlockSpec((tm, D), lambda i, te, nu, sg, nx: (jnp.minimum(i, nu[0] - 1), 0)),
                  pl.BlockSpec(memory_space=pl.ANY), pl.BlockSpec(memory_space=pl.ANY),
                  pl.BlockSpec(memory_space=pl.ANY)],
        out_specs=pl.BlockSpec((tm, D), lambda i, te, nu, sg, nx: (i, 0)),
        scratch_shapes=[pltpu.VMEM((D, D_FF), BF16), pltpu.VMEM((D, D_FF), BF16), pltpu.VMEM((D_FF, D), BF16),
                        pltpu.VMEM((2, D, D_FF), F32), pltpu.VMEM((2, D, D_FF), F32),
                        pltpu.VMEM((2, D_FF, D), F32), pltpu.SemaphoreType.DMA((3, 2))],
    )
    return pl.pallas_call(
        _experts_body, out_shape=jax.ShapeDtypeStruct((npad, D), F32), grid_spec=grid_spec,
        compiler_params=_cparams(("arbitrary",)), name="experts",
    )(tile_expert, n_used, seg_slot, next_expert, xs, wg, wu, wd)


def _combine_body(alpha, pos_ref, x1_ref, w_ref, eye_ref, g2_ref, b2_ref, ys_ref, o_ref, buf_ref, sem):
    tm = COMBINE_TM
    m_tok = pos_ref.shape[0] // 2
    base = pl.program_id(0) * tm

    def row_copy(slot, r, p):
        return pltpu.make_async_copy(ys_ref.at[pl.ds(p, 1), :], buf_ref.at[slot, pl.ds(r, 1), :], sem)

    def issue(r, carry):
        row_copy(0, r, pos_ref[base + r]).start()
        row_copy(1, r, pos_ref[m_tok + base + r]).start()
        return carry

    lax.fori_loop(0, tm, issue, 0, unroll=16)

    w = w_ref[...]
    eye = eye_ref[...]
    w_a = w.astype(BF16)
    w_b = (w - w_a.astype(F32)).astype(BF16)
    w_c = (w - w_a.astype(F32) - w_b.astype(F32)).astype(BF16)
    wcol = _nt(eye, w_a) + _nt(eye, w_b) + _nt(eye, w_c)

    def drain(r, carry):
        row_copy(0, 0, 0).wait()
        row_copy(1, 0, 0).wait()
        return carry

    lax.fori_loop(0, tm, drain, 0, unroll=16)
    moe = buf_ref[0] * wcol[:, 0:1] + buf_ref[1] * wcol[:, 1:2]
    o_ref[...] = _layer_norm(alpha * x1_ref[...] + moe, g2_ref[...], b2_ref[...])


def _combine(alpha, pos, x1, wgt, eye, g2, b2, ys):
    M, D = x1.shape
    tm = COMBINE_TM
    grid_spec = pltpu.PrefetchScalarGridSpec(
        num_scalar_prefetch=1, grid=(M // tm,),
        in_specs=[pl.BlockSpec((tm, D), lambda i, pos: (i, 0)),
                  pl.BlockSpec((8, tm), lambda i, pos: (0, i)),
                  pl.BlockSpec((tm, tm), lambda i, pos: (0, 0)),
                  pl.BlockSpec((1, D), lambda i, pos: (0, 0)),
                  pl.BlockSpec((1, D), lambda i, pos: (0, 0)),
                  pl.BlockSpec(memory_space=pl.ANY)],
        out_specs=pl.BlockSpec((tm, D), lambda i, pos: (i, 0)),
        scratch_shapes=[pltpu.VMEM((2, tm, D), F32), pltpu.SemaphoreType.DMA],
    )
    return pl.pallas_call(
        functools.partial(_combine_body, alpha), out_shape=jax.ShapeDtypeStruct((M, D), F32),
        grid_spec=grid_spec, compiler_params=_cparams(("arbitrary",), disable_bounds_checks=True),
        name="combine",
    )(pos, x1, wgt, eye, g2, b2, ys)


def _rope_angles(pos, dim):
    half = dim // 2
    inv_freq = ROPE_THETA ** (-np.arange(half, dtype=np.float64) * 2.0 / dim)
    return pos.astype(np.float64)[:, None] * inv_freq[None, :]


def _tables(T):
    f32 = lambda a: jnp.asarray(np.ascontiguousarray(a), dtype=F32)
    ang = _rope_angles(np.arange(T), NSA_HD)
    cos, sin = np.cos(ang), np.sin(ang)
    c64 = np.tile(cos, (1, 4))
    s64 = np.tile(np.concatenate([-sin, sin], axis=1), (1, 2))
    ang2 = _rope_angles(np.arange(T), RET_DK)
    cos2, sin2 = np.cos(ang2), np.sin(ang2)
    c128 = np.tile(cos2, (1, 2))
    s128 = np.concatenate([-sin2, sin2], axis=1)
    n_rows = T // CMP_STRIDE
    angc = _rope_angles(np.arange(n_rows) * CMP_STRIDE + CMP_BLOCK - 1, NSA_HD)
    cc = np.tile(np.cos(angc), (1, 2))
    sc = np.tile(np.sin(angc), (1, 2))
    return tuple(f32(a) for a in (c64, s64, c128, s128, cos.T, sin.T, cos2.T, sin2.T)), f32(cc), f32(sc)


def _overlap_matrix(T):
    n_rows = T // CMP_STRIDE
    n_sel = T // SLC_BLOCK
    cmp_start = np.arange(n_rows) * CMP_STRIDE
    sel_start = np.arange(n_sel) * SLC_BLOCK
    ov = np.clip(np.minimum(cmp_start[None, :] + CMP_BLOCK, sel_start[:, None] + SLC_BLOCK)
                 - np.maximum(cmp_start[None, :], sel_start[:, None]), 0, None)
    return jnp.asarray(ov.astype(np.float32) / CMP_STRIDE, dtype=BF16)


def _retention_tables():
    C = RET_CHUNK
    f32 = lambda a: jnp.asarray(np.ascontiguousarray(a), dtype=F32)
    gamma = 1.0 - 2.0 ** (-5.0 - np.arange(RET_HEADS, dtype=np.float64))
    log_g = np.log(gamma)
    i = np.arange(C, dtype=np.float64)
    diff = i[:, None] - i[None, :]
    dec = np.where(diff >= 0, np.exp(np.maximum(diff, 0.0) * log_g[:, None, None]), 0.0)
    xi = np.exp((i + 1.0) * log_g[:, None])
    zeta = np.exp((C - 1.0 - i) * log_g[:, None])
    cd = np.exp(C * log_g)
    xi_b = np.broadcast_to(xi[:, :, None], (RET_HEADS, C, RET_DV))
    cd_b = np.broadcast_to(cd[:, None, None], (RET_HEADS, 1, RET_DV))
    return f32(dec), f32(xi_b), f32(zeta[:, None, :]), f32(cd_b)


def _rot_half_cols(w):
    half = w.shape[-1] // 2
    return jnp.concatenate([-w[..., half:], w[..., :half]], axis=-1)


def kernel(x, w_in, cmp_pos_k, cmp_k_w1, cmp_k_b1, cmp_k_w2, cmp_k_b2, cmp_pos_v, cmp_v_w1, cmp_v_b1, cmp_v_w2, cmp_v_b2, ret_gn_g, ret_gn_b, w_up_attn, w_up_ret, w_out, ln1_g, ln1_b, router_group_w, router_group_b, router_inner_w, router_inner_b, expert_w_gate, expert_w_up, expert_w_down, ln2_g, ln2_b):
    B, T, D = x.shape
    M = B * T
    depth = w_in.shape[0]
    alpha = (2.0 * depth) ** 0.25
    tabs, cc, sc = _tables(T)
    ov = _overlap_matrix(T)
    dec, xi_b, zeta, cd_b = _retention_tables()
    tri = jnp.asarray(np.triu(np.ones((RANK_TB, RANK_TB), np.float32)), dtype=BF16)
    low = jnp.asarray(np.tril(np.ones((N_EXPERTS, N_EXPERTS), np.float32), k=-1), dtype=BF16)
    eye = jnp.asarray(np.eye(COMBINE_TM, dtype=np.float32), dtype=BF16)
    n_pad = 2 * M + N_EXPERTS * EXPERT_TM
    n_tiles = n_pad // EXPERT_TM

    for l in range(depth):
        w = w_in[l]
        col = lambda n: w[:, _OFF[n][0]:_OFF[n][1]]
        wn = jnp.concatenate([col("cmp_k"), col("cmp_v"), col("slc_k"), col("win_k"), col("ret_q"), col("ret_v")],
                             axis=1).astype(BF16)
        wt = jnp.concatenate([col("nsa_q"), col("slc_v"), col("win_v"), col("ret_k"), col("nsa_gate"),
                              jnp.zeros((D, 8), F32)], axis=1).T.astype(BF16)
        qT, ckv, sk, wk, svT, wvT, gT, rq, rkT, rv = _proj(x, wn, wt, tabs)

        kc, vcT = _compress(
            ckv, cmp_pos_k[l].reshape(2, -1), cmp_pos_v[l].reshape(2, -1),
            cmp_k_w1[l], cmp_k_b1[l][None, :], cmp_k_w2[l], _rot_half_cols(cmp_k_w2[l]),
            cmp_k_b2[l][None, :], _rot_half_cols(cmp_k_b2[l])[None, :], cc, sc,
            cmp_v_w1[l], cmp_v_b1[l][None, :], cmp_v_w2[l].T, cmp_v_b2[l][:, None])
        o_attn = _nsa(qT, kc, vcT, ov, sk, svT, wk, wvT, gT)
        o_ret = _retention(rq, rkT, rv, dec, xi_b, zeta, cd_b, ret_gn_g[l][None, :], ret_gn_b[l][None, :])

        wr = jnp.concatenate([router_inner_w[l].transpose(0, 2, 1).reshape(N_EXPERTS, D),
                              router_group_w[l].T, jnp.zeros((4, D), F32)], axis=0)
        wrh = wr.astype(BF16)
        wrl = (wr - wrh.astype(F32)).astype(BF16)
        rb = jnp.concatenate([router_inner_b[l].reshape(-1), router_group_b[l], jnp.zeros((4,), F32)])[:, None]
        x1, eid, wgt = _merge(
            alpha, x.reshape(M, D), o_attn.reshape(M, -1), o_ret.reshape(M, -1),
            col("ret_gate").astype(BF16), col("merge_gate").astype(BF16), w_up_attn[l].astype(BF16),
            w_up_ret[l].astype(BF16), w_out[l].astype(BF16), ln1_g[l][None, :], ln1_b[l][None, :], wrh, wrl, rb)

        eid_flat = eid[:2].reshape(1, 2 * M)
        pos, cnt = _rank(eid_flat, tri, low)
        pos = pos[0]
        counts = cnt[:, 0].astype(I32)
        tiles_per = (counts + EXPERT_TM - 1) // EXPERT_TM
        tile_end = jnp.cumsum(tiles_per)
        tile_ids = jnp.arange(n_tiles, dtype=I32)
        tile_expert = jnp.minimum(jnp.sum((tile_end[None, :] <= tile_ids[:, None]).astype(I32), axis=1),
                                  N_EXPERTS - 1).astype(I32)
        n_used = tile_end[-1:].astype(I32)

        ztile = jnp.concatenate([jnp.where(tiles_per > 0, (tile_end - 1) * EXPERT_TM, -1), tile_end[-1:]]).astype(I32)
        xs = _dispatch(pos, ztile, x1, n_pad)
        prev_expert = jnp.concatenate([jnp.full((1,), -1, I32), tile_expert[:-1]])
        is_first = (tile_ids < n_used[0]) & (tile_expert != prev_expert)
        seg_slot = jnp.where(is_first, (jnp.cumsum(is_first.astype(I32)) - 1) % 2, -1).astype(I32)
        seg_end = tile_end[tile_expert]
        next_expert = jnp.where(is_first & (seg_end < n_used[0]),
                                tile_expert[jnp.minimum(seg_end, n_tiles - 1)], -1).astype(I32)
        ys = _experts(tile_expert, n_used, seg_slot, next_expert, xs,
                      expert_w_gate[l], expert_w_up[l], expert_w_down[l])
        x = _combine(alpha, pos, x1, wgt, eye, ln2_g[l][None, :], ln2_b[l][None, :], ys).reshape(B, T, D)
    return x
```

```python
import functools

import numpy as np
import jax
import jax.numpy as jnp
from jax import lax
from jax.experimental import pallas as pl
from jax.experimental.pallas import tpu as pltpu

F32 = jnp.float32
BF16 = jnp.bfloat16
I32 = jnp.int32

D_MODEL = 1024
NSA_HEADS = 8
NSA_HD = 64
NSA_GROUPS = 2
NSA_HPG = NSA_HEADS // NSA_GROUPS
CMP_BLOCK = 32
CMP_STRIDE = 16
CMP_HIDDEN = 256
SLC_BLOCK = 64
SLC_TOPK = 16
WINDOW = 512
RET_HEADS = 4
RET_DK = 128
RET_DV = 256
RET_CHUNK = 128
MOE_GROUPS = 4
EXPERTS_PER_GROUP = 8
N_EXPERTS = MOE_GROUPS * EXPERTS_PER_GROUP
D_FF = 512
ROPE_THETA = 10000.0
LN_EPS = 1e-5
GN_EPS = 1e-5
NEG_INF = -1e30
LOG2_E = 1.4426950408889634

VMEM_LIMIT_V7X = 56 * 1024 * 1024

_OFF = {}
_o = 0
for _n, _w in (("nsa_q", 512), ("cmp_k", 128), ("cmp_v", 128), ("slc_k", 128), ("slc_v", 128),
               ("win_k", 128), ("win_v", 128), ("nsa_gate", 24), ("ret_q", 512), ("ret_k", 512),
               ("ret_v", 1024), ("ret_gate", 1024), ("merge_gate", 2048)):
    _OFF[_n] = (_o, _o + _w)
    _o += _w

PROJ_TM = 512
NSA_TQ = 256
NSA_KC = 256
NSA_VROWS = 80
RET_STEP_CHUNKS = 4
MERGE_TM = 512
RANK_TB = 1024
DISPATCH_TM = 512
EXPERT_TM = 256
COMBINE_TM = 256


def _cparams(sem, **kw):
    return pltpu.CompilerParams(dimension_semantics=sem, vmem_limit_bytes=VMEM_LIMIT_V7X, **kw)


def _nt(a, b):
    return lax.dot_general(a, b, (((1,), (1,)), ((), ())), preferred_element_type=F32)


def _nn(a, b):
    return jnp.dot(a, b, preferred_element_type=F32)


def _proj_body(x_ref, wn_ref, wt_ref, c64_ref, s64_ref, c128_ref, s128_ref,
               ct64_ref, st64_ref, ct128_ref, st128_ref,
               qT_ref, ckv_ref, sk_ref, wk_ref, svT_ref, wvT_ref, gT_ref, rq_ref, rkT_ref, rv_ref):
    tm = PROJ_TM
    xb = x_ref[0].astype(BF16)

    def nn(a, b):
        return _nn(xb, wn_ref[:, a:b])

    def nt(a, b):
        return _nt(wt_ref[a:b, :], xb)

    ckv = nn(0, 256)
    for j in range(4):
        ckv_ref[0, j] = ckv[:, 64 * j:64 * (j + 1)]

    lane = lax.broadcasted_iota(I32, (tm, 128), 1)
    first = (lane % 64) < 32
    c64 = c64_ref[...]
    s64 = s64_ref[...]

    def rope64(k):
        rot = jnp.where(first, pltpu.roll(k, 96, 1), pltpu.roll(k, 32, 1))
        return k * c64 + rot * s64

    sk = rope64(nn(256, 384)).astype(BF16)
    sk_ref[0, 0] = sk[:, :64]
    sk_ref[0, 1] = sk[:, 64:]
    wk = rope64(nn(384, 512)).astype(BF16)
    wk_ref[0, 0] = wk[:, :64]
    wk_ref[0, 1] = wk[:, 64:]

    c128 = c128_ref[...]
    s128 = s128_ref[...]
    rq = nn(512, 1024)
    for h in range(RET_HEADS):
        ch = rq[:, 128 * h:128 * (h + 1)]
        rq_ref[0, :, 128 * h:128 * (h + 1)] = (ch * c128 + pltpu.roll(ch, 64, 1) * s128).astype(BF16)
    rv_ref[0] = nn(1024, 2048).astype(BF16)

    ct = ct64_ref[...]
    st = st64_ref[...]
    qT = nt(0, 512)
    scale_q = NSA_HD ** -0.5 * LOG2_E
    tq = NSA_TQ
    for hh in range(NSA_HEADS):
        g, h = divmod(hh, NSA_HPG)
        x1 = qT[64 * hh:64 * hh + 32]
        x2 = qT[64 * hh + 32:64 * hh + 64]
        o1 = ((x1 * ct - x2 * st) * scale_q).astype(BF16)
        o2 = ((x1 * st + x2 * ct) * scale_q).astype(BF16)
        for it in range(tm // tq):
            qT_ref[0, g, it, 0:32, h * tq:(h + 1) * tq] = o1[:, it * tq:(it + 1) * tq]
            qT_ref[0, g, it, 32:64, h * tq:(h + 1) * tq] = o2[:, it * tq:(it + 1) * tq]

    svT = nt(512, 640).astype(BF16)
    wvT = nt(640, 768).astype(BF16)
    row16 = lax.broadcasted_iota(I32, (NSA_VROWS - NSA_HD, NSA_KC), 0)
    ones_blk = jnp.where(row16 == 0, 1.0, 0.0).astype(BF16)
    for c in range(tm // NSA_KC):
        for g in range(NSA_GROUPS):
            for vT, ref in ((svT, svT_ref), (wvT, wvT_ref)):
                ref[0, c, g, 0:NSA_HD, :] = vT[64 * g:64 * (g + 1), NSA_KC * c:NSA_KC * (c + 1)]
                ref[0, c, g, NSA_HD:NSA_VROWS, :] = ones_blk

    ct2 = ct128_ref[...]
    st2 = st128_ref[...]
    rkT = nt(768, 1280)
    scale_k = RET_DK ** -0.5
    for h in range(RET_HEADS):
        x1 = rkT[128 * h:128 * h + 64]
        x2 = rkT[128 * h + 64:128 * h + 128]
        o1 = (x1 * ct2 - x2 * st2) * scale_k
        o2 = (x1 * st2 + x2 * ct2) * scale_k
        for c in range(tm // RET_CHUNK):
            rkT_ref[0, c, 128 * h:128 * h + 64, :] = o1[:, 128 * c:128 * (c + 1)]
            rkT_ref[0, c, 128 * h + 64:128 * h + 128, :] = o2[:, 128 * c:128 * (c + 1)]

    gT_ref[0] = jax.nn.sigmoid(nt(1280, 1312))


def _proj(x, wn, wt, tabs):
    B, T, D = x.shape
    tm = PROJ_TM
    c64, s64, c128, s128, ct64, st64, ct128, st128 = tabs
    const = lambda b, i: (0, 0)
    in_specs = [
        pl.BlockSpec((1, tm, D), lambda b, i: (b, i, 0)),
        pl.BlockSpec(wn.shape, const),
        pl.BlockSpec(wt.shape, const),
        pl.BlockSpec((tm, 128), lambda b, i: (i, 0)),
        pl.BlockSpec((tm, 128), lambda b, i: (i, 0)),
        pl.BlockSpec((tm, 128), lambda b, i: (i, 0)),
        pl.BlockSpec((tm, 128), lambda b, i: (i, 0)),
        pl.BlockSpec((32, tm), lambda b, i: (0, i)),
        pl.BlockSpec((32, tm), lambda b, i: (0, i)),
        pl.BlockSpec((64, tm), lambda b, i: (0, i)),
        pl.BlockSpec((64, tm), lambda b, i: (0, i)),
    ]
    out_shape = [
        jax.ShapeDtypeStruct((B, NSA_GROUPS, T // NSA_TQ, NSA_HD, NSA_HPG * NSA_TQ), BF16),
        jax.ShapeDtypeStruct((B, 4, T, 64), F32),
        jax.ShapeDtypeStruct((B, 2, T, 64), BF16),
        jax.ShapeDtypeStruct((B, 2, T, 64), BF16),
        jax.ShapeDtypeStruct((B, T // NSA_KC, NSA_GROUPS, NSA_VROWS, NSA_KC), BF16),
        jax.ShapeDtypeStruct((B, T // NSA_KC, NSA_GROUPS, NSA_VROWS, NSA_KC), BF16),
        jax.ShapeDtypeStruct((B, 32, T), F32),
        jax.ShapeDtypeStruct((B, T, 512), BF16),
        jax.ShapeDtypeStruct((B, T // RET_CHUNK, 512, RET_CHUNK), F32),
        jax.ShapeDtypeStruct((B, T, 1024), BF16),
    ]
    out_specs = [
        pl.BlockSpec((1, NSA_GROUPS, tm // NSA_TQ, NSA_HD, NSA_HPG * NSA_TQ), lambda b, i: (b, 0, i, 0, 0)),
        pl.BlockSpec((1, 4, tm, 64), lambda b, i: (b, 0, i, 0)),
        pl.BlockSpec((1, 2, tm, 64), lambda b, i: (b, 0, i, 0)),
        pl.BlockSpec((1, 2, tm, 64), lambda b, i: (b, 0, i, 0)),
        pl.BlockSpec((1, tm // NSA_KC, NSA_GROUPS, NSA_VROWS, NSA_KC), lambda b, i: (b, i, 0, 0, 0)),
        pl.BlockSpec((1, tm // NSA_KC, NSA_GROUPS, NSA_VROWS, NSA_KC), lambda b, i: (b, i, 0, 0, 0)),
        pl.BlockSpec((1, 32, tm), lambda b, i: (b, 0, i)),
        pl.BlockSpec((1, tm, 512), lambda b, i: (b, i, 0)),
        pl.BlockSpec((1, tm // RET_CHUNK, 512, RET_CHUNK), lambda b, i: (b, i, 0, 0)),
        pl.BlockSpec((1, tm, 1024), lambda b, i: (b, i, 0)),
    ]
    return pl.pallas_call(
        _proj_body, out_shape=out_shape, grid=(B, T // tm), in_specs=in_specs, out_specs=out_specs,
        compiler_params=_cparams(("parallel", "parallel")), name="proj",
    )(x, wn, wt, c64, s64, c128, s128, ct64, st64, ct128, st128)


def _compress_body(uk_ref, uv_ref, posk_ref, posv_ref, w1k_ref, b1k_ref, w2k_ref, w2kr_ref, b2k_ref, b2kr_ref,
                   cc_ref, sc_ref, w1v_ref, b1v_ref, w2vT_ref, b2vT_ref, kc_ref, vcT_ref):
    half = CMP_STRIDE * NSA_HD

    def hidden(u_ref, pos_ref, w1_ref, b1_ref):
        u = u_ref[0, 0]
        top = (u + pos_ref[0:1, :]).astype(BF16)
        bot = (u + pos_ref[1:2, :]).astype(BF16)
        a = _nn(top, w1_ref[0:half, :].astype(BF16))
        bm = _nn(bot, w1_ref[half:2 * half, :].astype(BF16))
        n = bm.shape[0]
        return jax.nn.gelu(a + pltpu.roll(bm, n - 1, 0) + b1_ref[...])

    hk = hidden(uk_ref, posk_ref, w1k_ref, b1k_ref).astype(BF16)
    k = _nn(hk, w2k_ref[...].astype(BF16)) + b2k_ref[...]
    kr = _nn(hk, w2kr_ref[...].astype(BF16)) + b2kr_ref[...]
    kc_ref[0, 0] = (k * cc_ref[...] + kr * sc_ref[...]).astype(BF16)

    hv = hidden(uv_ref, posv_ref, w1v_ref, b1v_ref).astype(BF16)
    vcT_ref[0, 0] = (_nt(w2vT_ref[...].astype(BF16), hv) + b2vT_ref[...]).astype(BF16)


def _compress(ckv, posk, posv, w1k, b1k, w2k, w2kr, b2k, b2kr, cc, sc, w1v, b1v, w2vT, b2vT):
    B = ckv.shape[0]
    T = ckv.shape[2]
    nr = T // CMP_STRIDE
    u = ckv.reshape(B, 4, nr, CMP_STRIDE * NSA_HD)
    full = lambda a: pl.BlockSpec(a.shape, lambda b, g: (0,) * a.ndim)
    in_specs = [
        pl.BlockSpec((1, 1, nr, 1024), lambda b, g: (b, g, 0, 0)),
        pl.BlockSpec((1, 1, nr, 1024), lambda b, g: (b, g + 2, 0, 0)),
    ] + [full(a) for a in (posk, posv, w1k, b1k, w2k, w2kr, b2k, b2kr, cc, sc, w1v, b1v, w2vT, b2vT)]
    out_shape = [jax.ShapeDtypeStruct((B, NSA_GROUPS, nr, NSA_HD), BF16),
                 jax.ShapeDtypeStruct((B, NSA_GROUPS, NSA_HD, nr), BF16)]
    out_specs = [pl.BlockSpec((1, 1, nr, NSA_HD), lambda b, g: (b, g, 0, 0)),
                 pl.BlockSpec((1, 1, NSA_HD, nr), lambda b, g: (b, g, 0, 0))]
    return pl.pallas_call(
        _compress_body, out_shape=out_shape, grid=(B, NSA_GROUPS), in_specs=in_specs, out_specs=out_specs,
        compiler_params=_cparams(("parallel", "parallel")), name="compress",
    )(u, u, posk, posv, w1k, b1k, w2k, w2kr, b2k, b2kr, cc, sc, w1v, b1v, w2vT, b2vT)


def _nsa_body(q_ref, kc_ref, vcT_ref, ov_ref, sk_ref, svT_ref, wk_ref, wvT_ref, gT_ref, lo_ref, hi_ref, o_ref,
              score_ref, bias_ref, acc_ref, tot_ref, s_ref):
    tq = NSA_TQ
    kc_n = NSA_KC
    hq = NSA_HPG * tq
    groups = range(NSA_GROUPS)
    i = pl.program_id(1)
    t0 = i * tq
    t_row = t0 + lax.broadcasted_iota(I32, (1, tq), 1)
    t_row4 = t0 + lax.broadcasted_iota(I32, (1, hq), 1) % tq
    n_cmp = kc_ref.shape[2]
    n_sel = ov_ref.shape[0]
    q = [q_ref[0, g, 0] for g in groups]

    def gate4(g, br):
        return jnp.concatenate([gT_ref[0, (g * NSA_HPG + h) * 3 + br:(g * NSA_HPG + h) * 3 + br + 1, :]
                                for h in range(NSA_HPG)], axis=1)

    cmp_end = lax.broadcasted_iota(I32, (n_cmp, 1), 0) * CMP_STRIDE + (CMP_BLOCK - 1)
    cbias = jnp.where(cmp_end <= t_row4, 0.0, NEG_INF)
    any_valid = jnp.where(t_row4 >= CMP_BLOCK - 1, 1.0, 0.0)
    j_col = lax.broadcasted_iota(I32, (n_sel, 1), 0)
    blk_t = t_row // SLC_BLOCK
    forced = (j_col == 0) | (j_col == blk_t) | (j_col == blk_t - 1)
    sub = lax.broadcasted_iota(I32, (8, 1), 0)
    n_slab = n_sel // 8

    def select_blocks(g):
        s = _nn(kc_ref[0, g], q[g]) + cbias
        e = jnp.exp2(s - jnp.max(s, axis=0, keepdims=True))
        p = e * (any_valid / jnp.sum(e, axis=0, keepdims=True))
        tot_ref[g] = gate4(g, 0) * _nn(vcT_ref[0, g], p.astype(BF16))
        psum = p[:, 0:tq]
        for h in range(1, NSA_HPG):
            psum = psum + p[:, h * tq:(h + 1) * tq]
        p_hi = psum.astype(BF16)
        p_lo = (psum - p_hi.astype(F32)).astype(BF16)
        ov = ov_ref[...]
        imp = _nn(ov, p_hi) + _nn(ov, p_lo)
        score = jnp.where(forced, 1e9, jnp.where(j_col <= blk_t, imp, -1e9)).astype(F32)
        score_ref[g] = score
        slabs = [score[8 * v:8 * (v + 1)] for v in range(n_slab)]
        cnt = [jnp.zeros((8, tq), F32) for _ in range(n_slab)]
        for r in range(n_sel):
            row = jnp.broadcast_to(score_ref[g, r:r + 1, :], (8, tq))
            for v in range(n_slab):
                if r < 8 * v:
                    ahead = jnp.where(row >= slabs[v], 1.0, 0.0)
                elif r >= 8 * (v + 1):
                    ahead = jnp.where(row > slabs[v], 1.0, 0.0)
                else:
                    ahead = jnp.where(sub + 8 * v > r, jnp.where(row >= slabs[v], 1.0, 0.0),
                                      jnp.where(row > slabs[v], 1.0, 0.0))
                cnt[v] = cnt[v] + ahead
        for v in range(n_slab):
            bias_ref[g, 8 * v:8 * (v + 1), :] = jnp.where(
                cnt[v] < float(SLC_TOPK), jnp.where(sub + 8 * v <= blk_t, 0.0, NEG_INF), NEG_INF)

    for g in groups:
        select_blocks(g)

    def qk(k_ref, g, c):
        k_c = k_ref[0, g, pl.ds(pl.multiple_of(c * kc_n, kc_n), kc_n), :]
        return [_nn(k_c, q[g][:, h * tq:(h + 1) * tq]) for h in range(NSA_HPG)]

    def stage_a(scores, add_bias, m):
        biased = [add_bias(scores[h]) for h in range(NSA_HPG)]
        m_new = tuple(jnp.maximum(m[h], jnp.max(biased[h], axis=0, keepdims=True)) for h in range(NSA_HPG))
        return biased, m_new

    def stage_b(g, biased, vT_c, m_old, m_new):
        for h in range(NSA_HPG):
            hs = slice(h * tq, (h + 1) * tq)
            alpha = jnp.exp2(m_old[h] - m_new[h])
            p = jnp.exp2((biased[h] - m_new[h]).astype(BF16))
            acc_ref[g, :, hs] = alpha * acc_ref[g, :, hs] + _nn(vT_c, p)

    nb = kc_n // SLC_BLOCK

    def block_bias(g, c, s):
        return jnp.concatenate([s[SLC_BLOCK * b:SLC_BLOCK * (b + 1)] + bias_ref[g, pl.ds(c * nb + b, 1), :]
                                for b in range(nb)], axis=0)

    def finish(g, br):
        acc = acc_ref[g]
        tot_ref[g] = tot_ref[g] + gate4(g, br) * (acc[0:NSA_HD] / acc[NSA_HD:NSA_HD + 1])
        acc_ref[g] = jnp.zeros(acc.shape, F32)

    def park(g, biased):
        for h in range(NSA_HPG):
            s_ref[g, h] = biased[h]

    def parked(g):
        return [s_ref[g, h] for h in range(NSA_HPG)]

    m0 = tuple(jnp.full((1, tq), NEG_INF, F32) for _ in range(NSA_HPG))

    acc_ref[...] = jnp.zeros(acc_ref.shape, F32)
    m_new = []
    for g in groups:
        biased, m_g = stage_a(qk(sk_ref, g, i), lambda s, g=g: block_bias(g, i, s) + lo_ref[...], m0)
        park(g, biased)
        m_new.append(m_g)

    def slc_chunk(c, carry):
        prev, m_old, m_new = carry
        scores = [qk(sk_ref, g, c) for g in groups]
        for g in groups:
            stage_b(g, parked(g), svT_ref[0, prev, g], m_old[g], m_new[g])
        m_next = []
        for g in groups:
            biased, m_g = stage_a(scores[g], functools.partial(block_bias, g, c), m_new[g])
            park(g, biased)
            m_next.append(m_g)
        return c, m_new, tuple(m_next)

    prev, m_old, m_new = lax.fori_loop(0, i, slc_chunk, (i, (m0,) * NSA_GROUPS, tuple(m_new)))

    c_far = jnp.maximum(i - 2, 0)
    c_near = jnp.maximum(i - 1, 0)
    pen_far = jnp.where(i >= 2, 0.0, NEG_INF).astype(F32)
    pen_near = jnp.where(i >= 1, 0.0, NEG_INF).astype(F32)

    scores = [qk(wk_ref, g, c_far) for g in groups]
    far = []
    for g in groups:
        stage_b(g, parked(g), svT_ref[0, prev, g], m_old[g], m_new[g])
        far.append(stage_a(scores[g], lambda s: s + (hi_ref[...] + pen_far), m0))
        finish(g, 1)

    scores = [qk(wk_ref, g, c_near) for g in groups]
    near = []
    for g in groups:
        stage_b(g, far[g][0], wvT_ref[0, c_far, g], m0, far[g][1])
        near.append(stage_a(scores[g], lambda s: s + pen_near, far[g][1]))
    scores = [qk(wk_ref, g, i) for g in groups]
    for g in groups:
        stage_b(g, near[g][0], wvT_ref[0, c_near, g], far[g][1], near[g][1])
        b_diag, m_diag = stage_a(scores[g], lambda s: s + lo_ref[...], near[g][1])
        stage_b(g, b_diag, wvT_ref[0, i, g], near[g][1], m_diag)
        finish(g, 2)

    o_ref[0] = jnp.concatenate([tot_ref[g, :, h * tq:(h + 1) * tq] for g in groups for h in range(NSA_HPG)],
                               axis=0).T.astype(BF16)


def _nsa(qT, kc, vcT, ov, sk, svT, wk, wvT, gT):
    B, _, T = gT.shape
    tq = NSA_TQ
    assert NSA_KC == tq and WINDOW == 2 * NSA_KC
    nr = kc.shape[2]
    nch = T // NSA_KC
    kk = np.arange(NSA_KC)[:, None]
    tt = np.arange(tq)[None, :]
    lo = jnp.asarray(np.where(kk <= tt, 0.0, NEG_INF), dtype=F32)
    hi = jnp.asarray(np.where(kk > tt, 0.0, NEG_INF), dtype=F32)
    G = NSA_GROUPS
    in_specs = [
        pl.BlockSpec((1, G, 1, NSA_HD, NSA_HPG * tq), lambda b, i: (b, 0, i, 0, 0)),
        pl.BlockSpec((1, G, nr, NSA_HD), lambda b, i: (b, 0, 0, 0)),
        pl.BlockSpec((1, G, NSA_HD, nr), lambda b, i: (b, 0, 0, 0)),
        pl.BlockSpec(ov.shape, lambda b, i: (0, 0)),
        pl.BlockSpec((1, G, T, NSA_HD), lambda b, i: (b, 0, 0, 0)),
        pl.BlockSpec((1, nch, G, NSA_VROWS, NSA_KC), lambda b, i: (b, 0, 0, 0, 0)),
        pl.BlockSpec((1, G, T, NSA_HD), lambda b, i: (b, 0, 0, 0)),
        pl.BlockSpec((1, nch, G, NSA_VROWS, NSA_KC), lambda b, i: (b, 0, 0, 0, 0)),
        pl.BlockSpec((1, 32, tq), lambda b, i: (b, 0, i)),
        pl.BlockSpec(lo.shape, lambda b, i: (0, 0)),
        pl.BlockSpec(hi.shape, lambda b, i: (0, 0)),
    ]
    n_sel = T // SLC_BLOCK
    hq = NSA_HPG * tq
    return pl.pallas_call(
        _nsa_body, out_shape=jax.ShapeDtypeStruct((B, T, NSA_HEADS * NSA_HD), BF16),
        grid=(B, T // tq), in_specs=in_specs,
        out_specs=pl.BlockSpec((1, tq, NSA_HEADS * NSA_HD), lambda b, i: (b, i, 0)),
        scratch_shapes=[pltpu.VMEM((G, n_sel, tq), F32), pltpu.VMEM((G, n_sel, tq), F32),
                        pltpu.VMEM((G, NSA_VROWS, hq), F32), pltpu.VMEM((G, NSA_HD, hq), F32),
                        pltpu.VMEM((G, NSA_HPG, NSA_KC, tq), F32)],
        compiler_params=_cparams(("parallel", "parallel")), name="nsa",
    )(qT, kc, vcT, ov, sk, svT, wk, wvT, gT, lo, hi)


def _ret_body(q_ref, kT_ref, v_ref, dec_ref, xi_ref, zeta_ref, cd_ref, gg_ref, gb_ref, o_ref, r_ref):
    C = RET_CHUNK

    @pl.when(pl.program_id(1) == 0)
    def _():
        r_ref[...] = jnp.zeros(r_ref.shape, F32)

    for h in range(RET_HEADS):
        dk = slice(h * RET_DK, (h + 1) * RET_DK)
        dv = slice(h * RET_DV, (h + 1) * RET_DV)
        dec = dec_ref[h]
        xi = xi_ref[h]
        zeta = zeta_ref[h]
        cd = cd_ref[h]
        gg = gg_ref[:, dv]
        gb = gb_ref[:, dv]
        r = r_ref[h]
        for n in range(RET_STEP_CHUNKS):
            rows = slice(n * C, (n + 1) * C)
            qc = q_ref[0, rows, dk]
            kT = kT_ref[0, n, dk, :]
            vc = v_ref[0, rows, dv]
            s = _nn(qc, kT.astype(BF16)) * dec
            o = _nn(s.astype(BF16), vc) + _nn(qc, r.astype(BF16)) * xi
            r = r * cd + _nn((kT * zeta).astype(BF16), vc)
            mu = jnp.mean(o, axis=-1, keepdims=True)
            var = jnp.mean(jnp.square(o - mu), axis=-1, keepdims=True)
            o_ref[0, rows, dv] = (o - mu) * lax.rsqrt(var + GN_EPS) * gg + gb
        r_ref[h] = r


def _retention(rq, rkT, rv, dec, xi, zeta, cd, gn_g, gn_b):
    B, T, _ = rq.shape
    ts = RET_STEP_CHUNKS * RET_CHUNK
    full = lambda a: pl.BlockSpec(a.shape, lambda b, j: (0,) * a.ndim)
    in_specs = [
        pl.BlockSpec((1, ts, RET_HEADS * RET_DK), lambda b, j: (b, j, 0)),
        pl.BlockSpec((1, RET_STEP_CHUNKS, RET_HEADS * RET_DK, RET_CHUNK), lambda b, j: (b, j, 0, 0)),
        pl.BlockSpec((1, ts, RET_HEADS * RET_DV), lambda b, j: (b, j, 0)),
    ] + [full(a) for a in (dec, xi, zeta, cd, gn_g, gn_b)]
    return pl.pallas_call(
        _ret_body, out_shape=jax.ShapeDtypeStruct((B, T, RET_HEADS * RET_DV), F32),
        grid=(B, T // ts), in_specs=in_specs,
        out_specs=pl.BlockSpec((1, ts, RET_HEADS * RET_DV), lambda b, j: (b, j, 0)),
        scratch_shapes=[pltpu.VMEM((RET_HEADS, RET_DK, RET_DV), F32)],
        compiler_params=_cparams(("parallel", "arbitrary")), name="retention",
    )(rq, rkT, rv, dec, xi, zeta, cd, gn_g, gn_b)


def _layer_norm(y, g, b):
    mu = jnp.mean(y, axis=-1, keepdims=True)
    var = jnp.mean(jnp.square(y - mu), axis=-1, keepdims=True)
    return (y - mu) * lax.rsqrt(var + LN_EPS) * g + b


def _merge_body(alpha, x_ref, oa_ref, or_ref, wrg_ref, wmg_ref, wua_ref, wur_ref, wo_ref, g1_ref, b1_ref,
                wrh_ref, wrl_ref, rb_ref, x1_ref, eid_ref, wgt_ref):
    x = x_ref[...]
    xb = x.astype(BF16)
    rgate = jax.nn.silu(_nn(xb, wrg_ref[...]))
    o_ret = (or_ref[...] * rgate).astype(BF16)
    a = _nn(oa_ref[...], wua_ref[...])
    r = _nn(o_ret, wur_ref[...])
    mg = jax.nn.sigmoid(_nn(xb, wmg_ref[...]))
    merged = mg[:, :D_MODEL] * a + mg[:, D_MODEL:] * r
    mix = _nn(merged.astype(BF16), wo_ref[...])
    x1 = _layer_norm(alpha * x + mix, g1_ref[...], b1_ref[...])
    x1_ref[...] = x1

    xh = x1.astype(BF16)
    xl = (x1 - xh.astype(F32)).astype(BF16)
    wh = wrh_ref[...]
    lg = _nt(wh, xh) + _nt(wh, xl) + _nt(wrl_ref[...], xh) + rb_ref[...]
    ne = N_EXPERTS
    gl = lg[ne:ne + MOE_GROUPS]
    ge = jnp.exp(gl - jnp.max(gl, axis=0, keepdims=True))
    pg = ge / jnp.sum(ge, axis=0, keepdims=True)
    g_prob = jnp.max(pg, axis=0, keepdims=True)
    gi = lax.broadcasted_iota(I32, pg.shape, 0)
    g_idx = jnp.min(jnp.where(pg == g_prob, gi, MOE_GROUPS), axis=0, keepdims=True)
    inner = jnp.zeros((EXPERTS_PER_GROUP, lg.shape[1]), F32)
    for gq in range(MOE_GROUPS):
        inner = inner + jnp.where(g_idx == gq, lg[8 * gq:8 * (gq + 1)], 0.0)
    ei = lax.broadcasted_iota(I32, inner.shape, 0)
    m1 = jnp.max(inner, axis=0, keepdims=True)
    i1 = jnp.min(jnp.where(inner == m1, ei, EXPERTS_PER_GROUP), axis=0, keepdims=True)
    rest = jnp.where(ei == i1, -jnp.inf, inner)
    m2 = jnp.max(rest, axis=0, keepdims=True)
    i2 = jnp.min(jnp.where(rest == m2, ei, EXPERTS_PER_GROUP), axis=0, keepdims=True)
    e2 = jnp.exp(m2 - m1)
    den = 1.0 + e2
    w1 = (1.0 / den) * g_prob
    w2 = (e2 / den) * g_prob
    zi = jnp.zeros((6, lg.shape[1]), I32)
    eid_ref[...] = jnp.concatenate([g_idx * EXPERTS_PER_GROUP + i1, g_idx * EXPERTS_PER_GROUP + i2, zi], axis=0)
    wgt_ref[...] = jnp.concatenate([w1, w2, jnp.zeros((6, lg.shape[1]), F32)], axis=0)


def _merge(alpha, x2, oa2, or2, wrg, wmg, wua, wur, wo, g1, b1, wrh, wrl, rb):
    M, D = x2.shape
    tm = MERGE_TM
    full = lambda a: pl.BlockSpec(a.shape, lambda i: (0,) * a.ndim)
    in_specs = [pl.BlockSpec((tm, D), lambda i: (i, 0)),
                pl.BlockSpec((tm, oa2.shape[1]), lambda i: (i, 0)),
                pl.BlockSpec((tm, or2.shape[1]), lambda i: (i, 0))] + [
        full(a) for a in (wrg, wmg, wua, wur, wo, g1, b1, wrh, wrl, rb)]
    out_shape = [jax.ShapeDtypeStruct((M, D), F32),
                 jax.ShapeDtypeStruct((8, M), I32),
                 jax.ShapeDtypeStruct((8, M), F32)]
    out_specs = [pl.BlockSpec((tm, D), lambda i: (i, 0)),
                 pl.BlockSpec((8, tm), lambda i: (0, i)),
                 pl.BlockSpec((8, tm), lambda i: (0, i))]
    return pl.pallas_call(
        functools.partial(_merge_body, alpha), out_shape=out_shape, grid=(M // tm,),
        in_specs=in_specs, out_specs=out_specs,
        compiler_params=_cparams(("parallel",)), name="merge",
    )(x2, oa2, or2, wrg, wmg, wua, wur, wo, g1, b1, wrh, wrl, rb)


def _rank_body(eid_ref, tri_ref, low_ref, pos_ref, cnt_ref, carry_ref, off_ref):
    p = pl.program_id(0)
    j = pl.program_id(1)
    e = eid_ref[...]
    rows = lax.broadcasted_iota(I32, (N_EXPERTS, e.shape[1]), 0)
    hit = rows == e
    per_expert = jnp.sum(jnp.where(hit, 1.0, 0.0), axis=1, keepdims=True)

    @pl.when((p == 0) & (j == 0))
    def _():
        carry_ref[...] = jnp.zeros(carry_ref.shape, F32)

    @pl.when(p == 0)
    def _():
        pos_ref[...] = jnp.zeros(pos_ref.shape, I32)
        carry_ref[...] = carry_ref[...] + per_expert
        cnt_ref[...] = carry_ref[...]

    @pl.when((p == 1) & (j == 0))
    def _():
        tiles = jnp.floor((carry_ref[...] + (EXPERT_TM - 1)) * (1.0 / EXPERT_TM))
        off_ref[...] = _nn(low_ref[...], tiles.astype(BF16)) * EXPERT_TM
        carry_ref[...] = jnp.zeros(carry_ref.shape, F32)

    @pl.when(p == 1)
    def _():
        onehot = jnp.where(hit, 1.0, 0.0).astype(BF16)
        incl = _nn(onehot, tri_ref[...])
        base = carry_ref[:, 0:1] + off_ref[:, 0:1] - 1.0
        pos_ref[...] = jnp.sum(jnp.where(hit, incl + base, 0.0), axis=0, keepdims=True).astype(I32)
        carry_ref[...] = carry_ref[...] + per_expert


def _rank(eid_flat, tri, low):
    n = eid_flat.shape[1]
    tb = RANK_TB
    return pl.pallas_call(
        _rank_body,
        out_shape=[jax.ShapeDtypeStruct((1, n), I32), jax.ShapeDtypeStruct((N_EXPERTS, 128), F32)],
        grid=(2, n // tb),
        in_specs=[pl.BlockSpec((1, tb), lambda p, j: (0, j)), pl.BlockSpec((tb, tb), lambda p, j: (0, 0)),
                  pl.BlockSpec(low.shape, lambda p, j: (0, 0))],
        out_specs=[pl.BlockSpec((1, tb), lambda p, j: (0, j * p)),
                   pl.BlockSpec((N_EXPERTS, 128), lambda p, j: (0, 0))],
        scratch_shapes=[pltpu.VMEM((N_EXPERTS, 128), F32), pltpu.VMEM((N_EXPERTS, 128), F32)],
        compiler_params=_cparams(("arbitrary", "arbitrary")), name="rank",
    )(eid_flat, tri, low)


def _dispatch_body(pos_ref, ztile_ref, x_ref, xs_ref, zbuf_ref, sem, zsem):
    tm = DISPATCH_TM
    m_tok = pos_ref.shape[0] // 2
    base = pl.program_id(0) * tm

    @pl.when(pl.program_id(0) == 0)
    def _():
        zbuf_ref[...] = jnp.zeros(zbuf_ref.shape, F32)

        def zero_copy(e):
            z = pl.multiple_of(jnp.maximum(ztile_ref[e], 0), EXPERT_TM)
            return pltpu.make_async_copy(zbuf_ref, xs_ref.at[pl.ds(z, EXPERT_TM), :], zsem)

        for e in range(N_EXPERTS):
            @pl.when(ztile_ref[e] >= 0)
            def _():
                zero_copy(e).start()
        for e in range(N_EXPERTS):
            @pl.when(ztile_ref[e] >= 0)
            def _():
                zero_copy(e).wait()

        def tail_copy(t):
            return pltpu.make_async_copy(
                zbuf_ref, xs_ref.at[pl.ds(pl.multiple_of(t * EXPERT_TM, EXPERT_TM), EXPERT_TM), :], zsem)

        n_tiles = xs_ref.shape[0] // EXPERT_TM
        lax.fori_loop(ztile_ref[N_EXPERTS], n_tiles, lambda t, c: (tail_copy(t).start(), c)[1], 0)
        lax.fori_loop(ztile_ref[N_EXPERTS], n_tiles, lambda t, c: (tail_copy(t).wait(), c)[1], 0)

    def row_copy(r, p):
        return pltpu.make_async_copy(x_ref.at[pl.ds(r, 1), :], xs_ref.at[pl.ds(p, 1), :], sem)

    def issue(r, carry):
        row_copy(r, pos_ref[base + r]).start()
        row_copy(r, pos_ref[m_tok + base + r]).start()
        return carry

    lax.fori_loop(0, tm, issue, 0, unroll=16)

    def drain(r, carry):
        row_copy(0, 0).wait()
        row_copy(0, 0).wait()
        return carry

    lax.fori_loop(0, tm, drain, 0, unroll=16)


def _dispatch(pos, ztile, x1, n_pad):
    M, D = x1.shape
    tm = DISPATCH_TM
    grid_spec = pltpu.PrefetchScalarGridSpec(
        num_scalar_prefetch=2, grid=(M // tm,),
        in_specs=[pl.BlockSpec((tm, D), lambda i, pos, zt: (i, 0))],
        out_specs=pl.BlockSpec(memory_space=pl.ANY),
        scratch_shapes=[pltpu.VMEM((EXPERT_TM, D), F32), pltpu.SemaphoreType.DMA, pltpu.SemaphoreType.DMA],
    )
    return pl.pallas_call(
        _dispatch_body, out_shape=jax.ShapeDtypeStruct((n_pad, D), F32), grid_spec=grid_spec,
        compiler_params=_cparams(("arbitrary",), has_side_effects=True, disable_bounds_checks=True),
        name="dispatch",
    )(pos, ztile, x1)


def _experts_body(te_ref, nu_ref, seg_ref, nxt_ref, xs_ref, wg_hbm, wu_hbm, wd_hbm, ys_ref,
                  wgb_ref, wub_ref, wdb_ref, sg_ref, su_ref, sd_ref, sem):
    i = pl.program_id(0)
    live = i < nu_ref[0]

    def fetch(e, slot):
        return (pltpu.make_async_copy(wg_hbm.at[e], sg_ref.at[slot], sem.at[0, slot]),
                pltpu.make_async_copy(wu_hbm.at[e], su_ref.at[slot], sem.at[1, slot]),
                pltpu.make_async_copy(wd_hbm.at[e], sd_ref.at[slot], sem.at[2, slot]))

    @pl.when(live & (i == 0))
    def _():
        for c in fetch(te_ref[0], 0):
            c.start()

    @pl.when(live & (seg_ref[i] >= 0))
    def _():
        slot = seg_ref[i]
        for c in fetch(te_ref[i], slot):
            c.wait()
        wgb_ref[...] = sg_ref[slot].astype(BF16)
        wub_ref[...] = su_ref[slot].astype(BF16)
        wdb_ref[...] = sd_ref[slot].astype(BF16)

        @pl.when(nxt_ref[i] >= 0)
        def _():
            for c in fetch(nxt_ref[i], 1 - slot):
                c.start()

    @pl.when(live)
    def _():
        xb = xs_ref[...].astype(BF16)
        hg = _nn(xb, wgb_ref[...])
        hu = _nn(xb, wub_ref[...])
        h = (jax.nn.silu(hg) * hu).astype(BF16)
        ys_ref[...] = _nn(h, wdb_ref[...])

    @pl.when(i >= nu_ref[0])
    def _():
        ys_ref[...] = jnp.zeros(ys_ref.shape, F32)


def _experts(tile_expert, n_used, seg_slot, next_expert, xs, wg, wu, wd):
    npad, D = xs.shape
    tm = EXPERT_TM
    grid_spec = pltpu.PrefetchScalarGridSpec(
        num_scalar_prefetch=4, grid=(npad // tm,),
        in_specs=[pl.BlockSpec((tm, D), lambda i, te, nu, sg, nx: (jnp.minimum(i, nu[0] - 1), 0)),
                  pl.BlockSpec(memory_space=pl.ANY), pl.BlockSpec(memory_space=pl.ANY),
                  pl.BlockSpec(memory_space=pl.ANY)],
        out_specs=pl.BlockSpec((tm, D), lambda i, te, nu, sg, nx: (i, 0)),
        scratch_shapes=[pltpu.VMEM((D, D_FF), BF16), pltpu.VMEM((D, D_FF), BF16), pltpu.VMEM((D_FF, D), BF16),
                        pltpu.VMEM((2, D, D_FF), F32), pltpu.VMEM((2, D, D_FF), F32),
                        pltpu.VMEM((2, D_FF, D), F32), pltpu.SemaphoreType.DMA((3, 2))],
    )
    return pl.pallas_call(
        _experts_body, out_shape=jax.ShapeDtypeStruct((npad, D), F32), grid_spec=grid_spec,
        compiler_params=_cparams(("arbitrary",)), name="experts",
    )(tile_expert, n_used, seg_slot, next_expert, xs, wg, wu, wd)


def _combine_body(alpha, pos_ref, x1_ref, w_ref, eye_ref, g2_ref, b2_ref, ys_ref, o_ref,
                  buf_a, buf_b, sem_a, sem_b):
    tm = COMBINE_TM
    m_tok = pos_ref.shape[0] // 2
    k = pl.program_id(0)
    n = pl.num_programs(0)

    def row_copies(buf, sem, r, base):
        return (pltpu.make_async_copy(ys_ref.at[pl.ds(pos_ref[base + r], 1), :], buf.at[0, pl.ds(r, 1), :], sem),
                pltpu.make_async_copy(ys_ref.at[pl.ds(pos_ref[m_tok + base + r], 1), :],
                                      buf.at[1, pl.ds(r, 1), :], sem))

    def issue_inline(buf, sem, tile):
        for r in range(tm):
            for c in row_copies(buf, sem, r, tile * tm):
                c.start()

    def drain(buf, sem):
        def body(r, carry):
            for c in row_copies(buf, sem, 0, 0):
                c.wait()
            return carry
        lax.fori_loop(0, tm, body, 0, unroll=16)

    def compute(buf, s):
        rows = slice(s * tm, (s + 1) * tm)
        w = w_ref[:, rows]
        eye = eye_ref[...]
        w_a = w.astype(BF16)
        w_b = (w - w_a.astype(F32)).astype(BF16)
        w_c = (w - w_a.astype(F32) - w_b.astype(F32)).astype(BF16)
        wcol = _nt(eye, w_a) + _nt(eye, w_b) + _nt(eye, w_c)
        moe = buf[0] * wcol[:, 0:1] + buf[1] * wcol[:, 1:2]
        o_ref[rows, :] = _layer_norm(alpha * x1_ref[rows, :] + moe, g2_ref[...], b2_ref[...])

    @pl.when(k == 0)
    def _():
        def body(r, carry):
            for c in row_copies(buf_a, sem_a, r, 0):
                c.start()
            return carry
        lax.fori_loop(0, tm, body, 0, unroll=16)

    drain(buf_a, sem_a)
    issue_inline(buf_b, sem_b, 2 * k + 1)
    compute(buf_a, 0)
    drain(buf_b, sem_b)
    issue_inline(buf_a, sem_a, jnp.minimum(2 * k + 2, 2 * n - 2))
    compute(buf_b, 1)

    @pl.when(k == n - 1)
    def _():
        drain(buf_a, sem_a)


def _combine(alpha, pos, x1, wgt, eye, g2, b2, ys):
    M, D = x1.shape
    tm = COMBINE_TM
    grid_spec = pltpu.PrefetchScalarGridSpec(
        num_scalar_prefetch=1, grid=(M // (2 * tm),),
        in_specs=[pl.BlockSpec((2 * tm, D), lambda i, pos: (i, 0)),
                  pl.BlockSpec((8, 2 * tm), lambda i, pos: (0, i)),
                  pl.BlockSpec((tm, tm), lambda i, pos: (0, 0)),
                  pl.BlockSpec((1, D), lambda i, pos: (0, 0)),
                  pl.BlockSpec((1, D), lambda i, pos: (0, 0)),
                  pl.BlockSpec(memory_space=pl.ANY)],
        out_specs=pl.BlockSpec((2 * tm, D), lambda i, pos: (i, 0)),
        scratch_shapes=[pltpu.VMEM((2, tm, D), F32), pltpu.VMEM((2, tm, D), F32),
                        pltpu.SemaphoreType.DMA, pltpu.SemaphoreType.DMA],
    )
    return pl.pallas_call(
        functools.partial(_combine_body, alpha), out_shape=jax.ShapeDtypeStruct((M, D), F32),
        grid_spec=grid_spec, compiler_params=_cparams(("arbitrary",), disable_bounds_checks=True),
        name="combine",
    )(pos, x1, wgt, eye, g2, b2, ys)


def _rope_angles(pos, dim):
    half = dim // 2
    inv_freq = ROPE_THETA ** (-np.arange(half, dtype=np.float64) * 2.0 / dim)
    return pos.astype(np.float64)[:, None] * inv_freq[None, :]


def _tables(T):
    f32 = lambda a: jnp.asarray(np.ascontiguousarray(a), dtype=F32)
    ang = _rope_angles(np.arange(T), NSA_HD)
    cos, sin = np.cos(ang), np.sin(ang)
    c64 = np.tile(cos, (1, 4))
    s64 = np.tile(np.concatenate([-sin, sin], axis=1), (1, 2))
    ang2 = _rope_angles(np.arange(T), RET_DK)
    cos2, sin2 = np.cos(ang2), np.sin(ang2)
    c128 = np.tile(cos2, (1, 2))
    s128 = np.concatenate([-sin2, sin2], axis=1)
    n_rows = T // CMP_STRIDE
    angc = _rope_angles(np.arange(n_rows) * CMP_STRIDE + CMP_BLOCK - 1, NSA_HD)
    cc = np.tile(np.cos(angc), (1, 2))
    sc = np.tile(np.sin(angc), (1, 2))
    return tuple(f32(a) for a in (c64, s64, c128, s128, cos.T, sin.T, cos2.T, sin2.T)), f32(cc), f32(sc)


def _overlap_matrix(T):
    n_rows = T // CMP_STRIDE
    n_sel = T // SLC_BLOCK
    cmp_start = np.arange(n_rows) * CMP_STRIDE
    sel_start = np.arange(n_sel) * SLC_BLOCK
    ov = np.clip(np.minimum(cmp_start[None, :] + CMP_BLOCK, sel_start[:, None] + SLC_BLOCK)
                 - np.maximum(cmp_start[None, :], sel_start[:, None]), 0, None)
    return jnp.asarray(ov.astype(np.float32) / CMP_STRIDE, dtype=BF16)


def _retention_tables():
    C = RET_CHUNK
    f32 = lambda a: jnp.asarray(np.ascontiguousarray(a), dtype=F32)
    gamma = 1.0 - 2.0 ** (-5.0 - np.arange(RET_HEADS, dtype=np.float64))
    log_g = np.log(gamma)
    i = np.arange(C, dtype=np.float64)
    diff = i[:, None] - i[None, :]
    dec = np.where(diff >= 0, np.exp(np.maximum(diff, 0.0) * log_g[:, None, None]), 0.0)
    xi = np.exp((i + 1.0) * log_g[:, None])
    zeta = np.exp((C - 1.0 - i) * log_g[:, None])
    cd = np.exp(C * log_g)
    xi_b = np.broadcast_to(xi[:, :, None], (RET_HEADS, C, RET_DV))
    cd_b = np.broadcast_to(cd[:, None, None], (RET_HEADS, 1, RET_DV))
    return f32(dec), f32(xi_b), f32(zeta[:, None, :]), f32(cd_b)


def _rot_half_cols(w):
    half = w.shape[-1] // 2
    return jnp.concatenate([-w[..., half:], w[..., :half]], axis=-1)


def kernel(x, w_in, cmp_pos_k, cmp_k_w1, cmp_k_b1, cmp_k_w2, cmp_k_b2, cmp_pos_v, cmp_v_w1, cmp_v_b1, cmp_v_w2, cmp_v_b2, ret_gn_g, ret_gn_b, w_up_attn, w_up_ret, w_out, ln1_g, ln1_b, router_group_w, router_group_b, router_inner_w, router_inner_b, expert_w_gate, expert_w_up, expert_w_down, ln2_g, ln2_b):
    B, T, D = x.shape
    M = B * T
    depth = w_in.shape[0]
    alpha = (2.0 * depth) ** 0.25
    tabs, cc, sc = _tables(T)
    ov = _overlap_matrix(T)
    dec, xi_b, zeta, cd_b = _retention_tables()
    tri = jnp.asarray(np.triu(np.ones((RANK_TB, RANK_TB), np.float32)), dtype=BF16)
    low = jnp.asarray(np.tril(np.ones((N_EXPERTS, N_EXPERTS), np.float32), k=-1), dtype=BF16)
    eye = jnp.asarray(np.eye(COMBINE_TM, dtype=np.float32), dtype=BF16)
    n_pad = 2 * M + N_EXPERTS * EXPERT_TM
    n_tiles = n_pad // EXPERT_TM

    for l in range(depth):
        w = w_in[l]
        col = lambda n: w[:, _OFF[n][0]:_OFF[n][1]]
        wn = jnp.concatenate([col("cmp_k"), col("cmp_v"), col("slc_k"), col("win_k"), col("ret_q"), col("ret_v")],
                             axis=1).astype(BF16)
        wt = jnp.concatenate([col("nsa_q"), col("slc_v"), col("win_v"), col("ret_k"), col("nsa_gate"),
                              jnp.zeros((D, 8), F32)], axis=1).T.astype(BF16)
        qT, ckv, sk, wk, svT, wvT, gT, rq, rkT, rv = _proj(x, wn, wt, tabs)

        kc, vcT = _compress(
            ckv, cmp_pos_k[l].reshape(2, -1), cmp_pos_v[l].reshape(2, -1),
            cmp_k_w1[l], cmp_k_b1[l][None, :], cmp_k_w2[l], _rot_half_cols(cmp_k_w2[l]),
            cmp_k_b2[l][None, :], _rot_half_cols(cmp_k_b2[l])[None, :], cc, sc,
            cmp_v_w1[l], cmp_v_b1[l][None, :], cmp_v_w2[l].T, cmp_v_b2[l][:, None])
        o_attn = _nsa(qT, kc, vcT, ov, sk, svT, wk, wvT, gT)
        o_ret = _retention(rq, rkT, rv, dec, xi_b, zeta, cd_b, ret_gn_g[l][None, :], ret_gn_b[l][None, :])

        wr = jnp.concatenate([router_inner_w[l].transpose(0, 2, 1).reshape(N_EXPERTS, D),
                              router_group_w[l].T, jnp.zeros((4, D), F32)], axis=0)
        wrh = wr.astype(BF16)
        wrl = (wr - wrh.astype(F32)).astype(BF16)
        rb = jnp.concatenate([router_inner_b[l].reshape(-1), router_group_b[l], jnp.zeros((4,), F32)])[:, None]
        x1, eid, wgt = _merge(
            alpha, x.reshape(M, D), o_attn.reshape(M, -1), o_ret.reshape(M, -1),
            col("ret_gate").astype(BF16), col("merge_gate").astype(BF16), w_up_attn[l].astype(BF16),
            w_up_ret[l].astype(BF16), w_out[l].astype(BF16), ln1_g[l][None, :], ln1_b[l][None, :], wrh, wrl, rb)

        eid_flat = eid[:2].reshape(1, 2 * M)
        pos, cnt = _rank(eid_flat, tri, low)
        pos = pos[0]
        counts = cnt[:, 0].astype(I32)
        tiles_per = (counts + EXPERT_TM - 1) // EXPERT_TM
        tile_end = jnp.cumsum(tiles_per)
        tile_ids = jnp.arange(n_tiles, dtype=I32)
        tile_expert = jnp.minimum(jnp.sum((tile_end[None, :] <= tile_ids[:, None]).astype(I32), axis=1),
                                  N_EXPERTS - 1).astype(I32)
        n_used = tile_end[-1:].astype(I32)

        ztile = jnp.concatenate([jnp.where(tiles_per > 0, (tile_end - 1) * EXPERT_TM, -1), tile_end[-1:]]).astype(I32)
        xs = _dispatch(pos, ztile, x1, n_pad)
        prev_expert = jnp.concatenate([jnp.full((1,), -1, I32), tile_expert[:-1]])
        is_first = (tile_ids < n_used[0]) & (tile_expert != prev_expert)
        seg_slot = jnp.where(is_first, (jnp.cumsum(is_first.astype(I32)) - 1) % 2, -1).astype(I32)
        seg_end = tile_end[tile_expert]
        next_expert = jnp.where(is_first & (seg_end < n_used[0]),
                                tile_expert[jnp.minimum(seg_end, n_tiles - 1)], -1).astype(I32)
        ys = _experts(tile_expert, n_used, seg_slot, next_expert, xs,
                      expert_w_gate[l], expert_w_up[l], expert_w_down[l])
        x = _combine(alpha, pos, x1, wgt, eye, ln2_g[l][None, :], ln2_b[l][None, :], ys).reshape(B, T, D)
    return x
```

```python
import functools

import numpy as np
import jax
import jax.numpy as jnp
from jax import lax
from jax.experimental import pallas as pl
from jax.experimental.pallas import tpu as pltpu

F32 = jnp.float32
BF16 = jnp.bfloat16
I32 = jnp.int32

D_MODEL = 1024
NSA_HEADS = 8
NSA_HD = 64
NSA_GROUPS = 2
NSA_HPG = NSA_HEADS // NSA_GROUPS
CMP_BLOCK = 32
CMP_STRIDE = 16
CMP_HIDDEN = 256
SLC_BLOCK = 64
SLC_TOPK = 16
WINDOW = 512
RET_HEADS = 4
RET_DK = 128
RET_DV = 256
RET_CHUNK = 128
MOE_GROUPS = 4
EXPERTS_PER_GROUP = 8
N_EXPERTS = MOE_GROUPS * EXPERTS_PER_GROUP
D_FF = 512
ROPE_THETA = 10000.0
LN_EPS = 1e-5
GN_EPS = 1e-5
NEG_INF = -1e30
LOG2_E = 1.4426950408889634

VMEM_LIMIT_V7X = 56 * 1024 * 1024

_OFF = {}
_o = 0
for _n, _w in (("nsa_q", 512), ("cmp_k", 128), ("cmp_v", 128), ("slc_k", 128), ("slc_v", 128),
               ("win_k", 128), ("win_v", 128), ("nsa_gate", 24), ("ret_q", 512), ("ret_k", 512),
               ("ret_v", 1024), ("ret_gate", 1024), ("merge_gate", 2048)):
    _OFF[_n] = (_o, _o + _w)
    _o += _w

PROJ_TM = 512
NSA_TQ = 256
NSA_KC = 256
NSA_VROWS = 80
RET_STEP_CHUNKS = 4
MERGE_TM = 512
RANK_TB = 1024
DISPATCH_TM = 512
EXPERT_TM = 256
COMBINE_TM = 256


def _cparams(sem, **kw):
    return pltpu.CompilerParams(dimension_semantics=sem, vmem_limit_bytes=VMEM_LIMIT_V7X, **kw)


def _nt(a, b):
    return lax.dot_general(a, b, (((1,), (1,)), ((), ())), preferred_element_type=F32)


def _nn(a, b):
    return jnp.dot(a, b, preferred_element_type=F32)


def _proj_body(x_ref, wn_ref, wt_ref, c64_ref, s64_ref, c128_ref, s128_ref,
               ct64_ref, st64_ref, ct128_ref, st128_ref,
               qT_ref, ckv_ref, sk_ref, wk_ref, svT_ref, wvT_ref, gT_ref, rq_ref, rkT_ref, rv_ref):
    tm = PROJ_TM
    xb = x_ref[0].astype(BF16)

    def nn(a, b):
        return _nn(xb, wn_ref[:, a:b])

    def nt(a, b):
        return _nt(wt_ref[a:b, :], xb)

    ckv = nn(0, 256)
    for j in range(4):
        ckv_ref[0, j] = ckv[:, 64 * j:64 * (j + 1)]

    lane = lax.broadcasted_iota(I32, (tm, 128), 1)
    first = (lane % 64) < 32
    c64 = c64_ref[...]
    s64 = s64_ref[...]

    def rope64(k):
        rot = jnp.where(first, pltpu.roll(k, 96, 1), pltpu.roll(k, 32, 1))
        return k * c64 + rot * s64

    sk = rope64(nn(256, 384)).astype(BF16)
    sk_ref[0, 0] = sk[:, :64]
    sk_ref[0, 1] = sk[:, 64:]
    wk = rope64(nn(384, 512)).astype(BF16)
    wk_ref[0, 0] = wk[:, :64]
    wk_ref[0, 1] = wk[:, 64:]

    c128 = c128_ref[...]
    s128 = s128_ref[...]
    rq = nn(512, 1024)
    for h in range(RET_HEADS):
        ch = rq[:, 128 * h:128 * (h + 1)]
        rq_ref[0, :, 128 * h:128 * (h + 1)] = (ch * c128 + pltpu.roll(ch, 64, 1) * s128).astype(BF16)
    rv_ref[0] = nn(1024, 2048).astype(BF16)

    ct = ct64_ref[...]
    st = st64_ref[...]
    qT = nt(0, 512)
    scale_q = NSA_HD ** -0.5 * LOG2_E
    tq = NSA_TQ
    for hh in range(NSA_HEADS):
        g, h = divmod(hh, NSA_HPG)
        x1 = qT[64 * hh:64 * hh + 32]
        x2 = qT[64 * hh + 32:64 * hh + 64]
        o1 = ((x1 * ct - x2 * st) * scale_q).astype(BF16)
        o2 = ((x1 * st + x2 * ct) * scale_q).astype(BF16)
        for it in range(tm // tq):
            qT_ref[0, g, it, 0:32, h * tq:(h + 1) * tq] = o1[:, it * tq:(it + 1) * tq]
            qT_ref[0, g, it, 32:64, h * tq:(h + 1) * tq] = o2[:, it * tq:(it + 1) * tq]

    svT = nt(512, 640).astype(BF16)
    wvT = nt(640, 768).astype(BF16)
    row16 = lax.broadcasted_iota(I32, (NSA_VROWS - NSA_HD, NSA_KC), 0)
    ones_blk = jnp.where(row16 == 0, 1.0, 0.0).astype(BF16)
    for c in range(tm // NSA_KC):
        for g in range(NSA_GROUPS):
            for vT, ref in ((svT, svT_ref), (wvT, wvT_ref)):
                ref[0, c, g, 0:NSA_HD, :] = vT[64 * g:64 * (g + 1), NSA_KC * c:NSA_KC * (c + 1)]
                ref[0, c, g, NSA_HD:NSA_VROWS, :] = ones_blk

    ct2 = ct128_ref[...]
    st2 = st128_ref[...]
    rkT = nt(768, 1280)
    scale_k = RET_DK ** -0.5
    for h in range(RET_HEADS):
        x1 = rkT[128 * h:128 * h + 64]
        x2 = rkT[128 * h + 64:128 * h + 128]
        o1 = (x1 * ct2 - x2 * st2) * scale_k
        o2 = (x1 * st2 + x2 * ct2) * scale_k
        for c in range(tm // RET_CHUNK):
            rkT_ref[0, c, 128 * h:128 * h + 64, :] = o1[:, 128 * c:128 * (c + 1)]
            rkT_ref[0, c, 128 * h + 64:128 * h + 128, :] = o2[:, 128 * c:128 * (c + 1)]

    gT_ref[0] = jax.nn.sigmoid(nt(1280, 1312))


def _proj(x, wn, wt, tabs):
    B, T, D = x.shape
    tm = PROJ_TM
    c64, s64, c128, s128, ct64, st64, ct128, st128 = tabs
    const = lambda b, i: (0, 0)
    in_specs = [
        pl.BlockSpec((1, tm, D), lambda b, i: (b, i, 0)),
        pl.BlockSpec(wn.shape, const),
        pl.BlockSpec(wt.shape, const),
        pl.BlockSpec((tm, 128), lambda b, i: (i, 0)),
        pl.BlockSpec((tm, 128), lambda b, i: (i, 0)),
        pl.BlockSpec((tm, 128), lambda b, i: (i, 0)),
        pl.BlockSpec((tm, 128), lambda b, i: (i, 0)),
        pl.BlockSpec((32, tm), lambda b, i: (0, i)),
        pl.BlockSpec((32, tm), lambda b, i: (0, i)),
        pl.BlockSpec((64, tm), lambda b, i: (0, i)),
        pl.BlockSpec((64, tm), lambda b, i: (0, i)),
    ]
    out_shape = [
        jax.ShapeDtypeStruct((B, NSA_GROUPS, T // NSA_TQ, NSA_HD, NSA_HPG * NSA_TQ), BF16),
        jax.ShapeDtypeStruct((B, 4, T, 64), F32),
        jax.ShapeDtypeStruct((B, 2, T, 64), BF16),
        jax.ShapeDtypeStruct((B, 2, T, 64), BF16),
        jax.ShapeDtypeStruct((B, T // NSA_KC, NSA_GROUPS, NSA_VROWS, NSA_KC), BF16),
        jax.ShapeDtypeStruct((B, T // NSA_KC, NSA_GROUPS, NSA_VROWS, NSA_KC), BF16),
        jax.ShapeDtypeStruct((B, 32, T), F32),
        jax.ShapeDtypeStruct((B, T, 512), BF16),
        jax.ShapeDtypeStruct((B, T // RET_CHUNK, 512, RET_CHUNK), F32),
        jax.ShapeDtypeStruct((B, T, 1024), BF16),
    ]
    out_specs = [
        pl.BlockSpec((1, NSA_GROUPS, tm // NSA_TQ, NSA_HD, NSA_HPG * NSA_TQ), lambda b, i: (b, 0, i, 0, 0)),
        pl.BlockSpec((1, 4, tm, 64), lambda b, i: (b, 0, i, 0)),
        pl.BlockSpec((1, 2, tm, 64), lambda b, i: (b, 0, i, 0)),
        pl.BlockSpec((1, 2, tm, 64), lambda b, i: (b, 0, i, 0)),
        pl.BlockSpec((1, tm // NSA_KC, NSA_GROUPS, NSA_VROWS, NSA_KC), lambda b, i: (b, i, 0, 0, 0)),
        pl.BlockSpec((1, tm // NSA_KC, NSA_GROUPS, NSA_VROWS, NSA_KC), lambda b, i: (b, i, 0, 0, 0)),
        pl.BlockSpec((1, 32, tm), lambda b, i: (b, 0, i)),
        pl.BlockSpec((1, tm, 512), lambda b, i: (b, i, 0)),
        pl.BlockSpec((1, tm // RET_CHUNK, 512, RET_CHUNK), lambda b, i: (b, i, 0, 0)),
        pl.BlockSpec((1, tm, 1024), lambda b, i: (b, i, 0)),
    ]
    return pl.pallas_call(
        _proj_body, out_shape=out_shape, grid=(B, T // tm), in_specs=in_specs, out_specs=out_specs,
        compiler_params=_cparams(("parallel", "parallel")), name="proj",
    )(x, wn, wt, c64, s64, c128, s128, ct64, st64, ct128, st128)


def _compress_body(uk_ref, uv_ref, posk_ref, posv_ref, w1k_ref, b1k_ref, w2k_ref, w2kr_ref, b2k_ref, b2kr_ref,
                   cc_ref, sc_ref, w1v_ref, b1v_ref, w2vT_ref, b2vT_ref, kc_ref, vcT_ref):
    half = CMP_STRIDE * NSA_HD

    def hidden(u_ref, pos_ref, w1_ref, b1_ref):
        u = u_ref[0, 0]
        top = (u + pos_ref[0:1, :]).astype(BF16)
        bot = (u + pos_ref[1:2, :]).astype(BF16)
        a = _nn(top, w1_ref[0:half, :].astype(BF16))
        bm = _nn(bot, w1_ref[half:2 * half, :].astype(BF16))
        n = bm.shape[0]
        return jax.nn.gelu(a + pltpu.roll(bm, n - 1, 0) + b1_ref[...])

    hk = hidden(uk_ref, posk_ref, w1k_ref, b1k_ref).astype(BF16)
    k = _nn(hk, w2k_ref[...].astype(BF16)) + b2k_ref[...]
    kr = _nn(hk, w2kr_ref[...].astype(BF16)) + b2kr_ref[...]
    kc_ref[0, 0] = (k * cc_ref[...] + kr * sc_ref[...]).astype(BF16)

    hv = hidden(uv_ref, posv_ref, w1v_ref, b1v_ref).astype(BF16)
    vcT_ref[0, 0] = (_nt(w2vT_ref[...].astype(BF16), hv) + b2vT_ref[...]).astype(BF16)


def _compress(ckv, posk, posv, w1k, b1k, w2k, w2kr, b2k, b2kr, cc, sc, w1v, b1v, w2vT, b2vT):
    B = ckv.shape[0]
    T = ckv.shape[2]
    nr = T // CMP_STRIDE
    u = ckv.reshape(B, 4, nr, CMP_STRIDE * NSA_HD)
    full = lambda a: pl.BlockSpec(a.shape, lambda b, g: (0,) * a.ndim)
    in_specs = [
        pl.BlockSpec((1, 1, nr, 1024), lambda b, g: (b, g, 0, 0)),
        pl.BlockSpec((1, 1, nr, 1024), lambda b, g: (b, g + 2, 0, 0)),
    ] + [full(a) for a in (posk, posv, w1k, b1k, w2k, w2kr, b2k, b2kr, cc, sc, w1v, b1v, w2vT, b2vT)]
    out_shape = [jax.ShapeDtypeStruct((B, NSA_GROUPS, nr, NSA_HD), BF16),
                 jax.ShapeDtypeStruct((B, NSA_GROUPS, NSA_HD, nr), BF16)]
    out_specs = [pl.BlockSpec((1, 1, nr, NSA_HD), lambda b, g: (b, g, 0, 0)),
                 pl.BlockSpec((1, 1, NSA_HD, nr), lambda b, g: (b, g, 0, 0))]
    return pl.pallas_call(
        _compress_body, out_shape=out_shape, grid=(B, NSA_GROUPS), in_specs=in_specs, out_specs=out_specs,
        compiler_params=_cparams(("parallel", "parallel")), name="compress",
    )(u, u, posk, posv, w1k, b1k, w2k, w2kr, b2k, b2kr, cc, sc, w1v, b1v, w2vT, b2vT)


def _nsa_body(q_ref, kc_ref, vcT_ref, ov_ref, sk_ref, svT_ref, wk_ref, wvT_ref, gT_ref, lo_ref, hi_ref, o_ref,
              score_ref, bias_ref, acc_ref, tot_ref, s_ref):
    tq = NSA_TQ
    kc_n = NSA_KC
    hq = NSA_HPG * tq
    groups = range(NSA_GROUPS)
    i = pl.program_id(1)
    t0 = i * tq
    t_row = t0 + lax.broadcasted_iota(I32, (1, tq), 1)
    t_row4 = t0 + lax.broadcasted_iota(I32, (1, hq), 1) % tq
    n_cmp = kc_ref.shape[2]
    n_sel = ov_ref.shape[0]
    q = [q_ref[0, g, 0] for g in groups]

    def gate4(g, br):
        return jnp.concatenate([gT_ref[0, (g * NSA_HPG + h) * 3 + br:(g * NSA_HPG + h) * 3 + br + 1, :]
                                for h in range(NSA_HPG)], axis=1)

    cmp_end = lax.broadcasted_iota(I32, (n_cmp, 1), 0) * CMP_STRIDE + (CMP_BLOCK - 1)
    cbias = jnp.where(cmp_end <= t_row4, 0.0, NEG_INF)
    any_valid = jnp.where(t_row4 >= CMP_BLOCK - 1, 1.0, 0.0)
    j_col = lax.broadcasted_iota(I32, (n_sel, 1), 0)
    blk_t = t_row // SLC_BLOCK
    forced = (j_col == 0) | (j_col == blk_t) | (j_col == blk_t - 1)
    sub = lax.broadcasted_iota(I32, (8, 1), 0)
    n_slab = n_sel // 8

    def select_blocks(g):
        s = _nn(kc_ref[0, g], q[g]) + cbias
        e = jnp.exp2(s - jnp.max(s, axis=0, keepdims=True))
        p = e * (any_valid / jnp.sum(e, axis=0, keepdims=True))
        tot_ref[g] = gate4(g, 0) * _nn(vcT_ref[0, g], p.astype(BF16))
        psum = p[:, 0:tq]
        for h in range(1, NSA_HPG):
            psum = psum + p[:, h * tq:(h + 1) * tq]
        p_hi = psum.astype(BF16)
        p_lo = (psum - p_hi.astype(F32)).astype(BF16)
        ov = ov_ref[...]
        imp = _nn(ov, p_hi) + _nn(ov, p_lo)
        score = jnp.where(forced, 1e9, jnp.where(j_col <= blk_t, imp, -1e9)).astype(F32)
        score_ref[g] = score
        slabs = [score[8 * v:8 * (v + 1)] for v in range(n_slab)]
        cnt = [jnp.zeros((8, tq), F32) for _ in range(n_slab)]
        for r in range(n_sel):
            row = jnp.broadcast_to(score_ref[g, r:r + 1, :], (8, tq))
            for v in range(n_slab):
                if r < 8 * v:
                    ahead = jnp.where(row >= slabs[v], 1.0, 0.0)
                elif r >= 8 * (v + 1):
                    ahead = jnp.where(row > slabs[v], 1.0, 0.0)
                else:
                    ahead = jnp.where(sub + 8 * v > r, jnp.where(row >= slabs[v], 1.0, 0.0),
                                      jnp.where(row > slabs[v], 1.0, 0.0))
                cnt[v] = cnt[v] + ahead
        for v in range(n_slab):
            bias_ref[g, 8 * v:8 * (v + 1), :] = jnp.where(
                cnt[v] < float(SLC_TOPK), jnp.where(sub + 8 * v <= blk_t, 0.0, NEG_INF), NEG_INF)

    for g in groups:
        select_blocks(g)

    def qk(k_ref, g, c):
        k_c = k_ref[0, g, pl.ds(pl.multiple_of(c * kc_n, kc_n), kc_n), :]
        return [_nn(k_c, q[g][:, h * tq:(h + 1) * tq]) for h in range(NSA_HPG)]

    def stage_a(scores, add_bias, m):
        biased = [add_bias(scores[h]) for h in range(NSA_HPG)]
        m_new = tuple(jnp.maximum(m[h], jnp.max(biased[h], axis=0, keepdims=True)) for h in range(NSA_HPG))
        return biased, m_new

    def stage_b(g, biased, vT_c, m_old, m_new):
        for h in range(NSA_HPG):
            hs = slice(h * tq, (h + 1) * tq)
            alpha = jnp.exp2(m_old[h] - m_new[h])
            p = jnp.exp2((biased[h] - m_new[h]).astype(BF16))
            acc_ref[g, :, hs] = alpha * acc_ref[g, :, hs] + _nn(vT_c, p)

    nb = kc_n // SLC_BLOCK

    def block_bias(g, c, s):
        return jnp.concatenate([s[SLC_BLOCK * b:SLC_BLOCK * (b + 1)] + bias_ref[g, pl.ds(c * nb + b, 1), :]
                                for b in range(nb)], axis=0)

    def finish(g, br):
        acc = acc_ref[g]
        tot_ref[g] = tot_ref[g] + gate4(g, br) * (acc[0:NSA_HD] / acc[NSA_HD:NSA_HD + 1])
        acc_ref[g] = jnp.zeros(acc.shape, F32)

    def park(g, biased):
        for h in range(NSA_HPG):
            s_ref[g, h] = biased[h]

    def parked(g):
        return [s_ref[g, h] for h in range(NSA_HPG)]

    m0 = tuple(jnp.full((1, tq), NEG_INF, F32) for _ in range(NSA_HPG))

    acc_ref[...] = jnp.zeros(acc_ref.shape, F32)
    m_new = []
    for g in groups:
        biased, m_g = stage_a(qk(sk_ref, g, i), lambda s, g=g: block_bias(g, i, s) + lo_ref[...], m0)
        park(g, biased)
        m_new.append(m_g)

    def slc_chunk(c, carry):
        prev, m_old, m_new = carry
        scores = [qk(sk_ref, g, c) for g in groups]
        for g in groups:
            stage_b(g, parked(g), svT_ref[0, prev, g], m_old[g], m_new[g])
        m_next = []
        for g in groups:
            biased, m_g = stage_a(scores[g], functools.partial(block_bias, g, c), m_new[g])
            park(g, biased)
            m_next.append(m_g)
        return c, m_new, tuple(m_next)

    prev, m_old, m_new = lax.fori_loop(0, i, slc_chunk, (i, (m0,) * NSA_GROUPS, tuple(m_new)))

    c_far = jnp.maximum(i - 2, 0)
    c_near = jnp.maximum(i - 1, 0)
    pen_far = jnp.where(i >= 2, 0.0, NEG_INF).astype(F32)
    pen_near = jnp.where(i >= 1, 0.0, NEG_INF).astype(F32)

    scores = [qk(wk_ref, g, c_far) for g in groups]
    far = []
    for g in groups:
        stage_b(g, parked(g), svT_ref[0, prev, g], m_old[g], m_new[g])
        far.append(stage_a(scores[g], lambda s: s + (hi_ref[...] + pen_far), m0))
        finish(g, 1)

    scores = [qk(wk_ref, g, c_near) for g in groups]
    near = []
    for g in groups:
        stage_b(g, far[g][0], wvT_ref[0, c_far, g], m0, far[g][1])
        near.append(stage_a(scores[g], lambda s: s + pen_near, far[g][1]))
    scores = [qk(wk_ref, g, i) for g in groups]
    for g in groups:
        stage_b(g, near[g][0], wvT_ref[0, c_near, g], far[g][1], near[g][1])
        b_diag, m_diag = stage_a(scores[g], lambda s: s + lo_ref[...], near[g][1])
        stage_b(g, b_diag, wvT_ref[0, i, g], near[g][1], m_diag)
        finish(g, 2)

    o_ref[0] = jnp.concatenate([tot_ref[g, :, h * tq:(h + 1) * tq] for g in groups for h in range(NSA_HPG)],
                               axis=0).T.astype(BF16)


def _nsa(qT, kc, vcT, ov, sk, svT, wk, wvT, gT):
    B, _, T = gT.shape
    tq = NSA_TQ
    assert NSA_KC == tq and WINDOW == 2 * NSA_KC
    nr = kc.shape[2]
    nch = T // NSA_KC
    kk = np.arange(NSA_KC)[:, None]
    tt = np.arange(tq)[None, :]
    lo = jnp.asarray(np.where(kk <= tt, 0.0, NEG_INF), dtype=F32)
    hi = jnp.asarray(np.where(kk > tt, 0.0, NEG_INF), dtype=F32)
    G = NSA_GROUPS
    in_specs = [
        pl.BlockSpec((1, G, 1, NSA_HD, NSA_HPG * tq), lambda b, i: (b, 0, i, 0, 0)),
        pl.BlockSpec((1, G, nr, NSA_HD), lambda b, i: (b, 0, 0, 0)),
        pl.BlockSpec((1, G, NSA_HD, nr), lambda b, i: (b, 0, 0, 0)),
        pl.BlockSpec(ov.shape, lambda b, i: (0, 0)),
        pl.BlockSpec((1, G, T, NSA_HD), lambda b, i: (b, 0, 0, 0)),
        pl.BlockSpec((1, nch, G, NSA_VROWS, NSA_KC), lambda b, i: (b, 0, 0, 0, 0)),
        pl.BlockSpec((1, G, T, NSA_HD), lambda b, i: (b, 0, 0, 0)),
        pl.BlockSpec((1, nch, G, NSA_VROWS, NSA_KC), lambda b, i: (b, 0, 0, 0, 0)),
        pl.BlockSpec((1, 32, tq), lambda b, i: (b, 0, i)),
        pl.BlockSpec(lo.shape, lambda b, i: (0, 0)),
        pl.BlockSpec(hi.shape, lambda b, i: (0, 0)),
    ]
    n_sel = T // SLC_BLOCK
    hq = NSA_HPG * tq
    return pl.pallas_call(
        _nsa_body, out_shape=jax.ShapeDtypeStruct((B, T, NSA_HEADS * NSA_HD), BF16),
        grid=(B, T // tq), in_specs=in_specs,
        out_specs=pl.BlockSpec((1, tq, NSA_HEADS * NSA_HD), lambda b, i: (b, i, 0)),
        scratch_shapes=[pltpu.VMEM((G, n_sel, tq), F32), pltpu.VMEM((G, n_sel, tq), F32),
                        pltpu.VMEM((G, NSA_VROWS, hq), F32), pltpu.VMEM((G, NSA_HD, hq), F32),
                        pltpu.VMEM((G, NSA_HPG, NSA_KC, tq), F32)],
        compiler_params=_cparams(("parallel", "parallel")), name="nsa",
    )(qT, kc, vcT, ov, sk, svT, wk, wvT, gT, lo, hi)


def _ret_body(q_ref, kT_ref, v_ref, dec_ref, xi_ref, zeta_ref, cd_ref, gg_ref, gb_ref, o_ref, r_ref):
    C = RET_CHUNK

    @pl.when(pl.program_id(1) == 0)
    def _():
        r_ref[...] = jnp.zeros(r_ref.shape, F32)

    for h in range(RET_HEADS):
        dk = slice(h * RET_DK, (h + 1) * RET_DK)
        dv = slice(h * RET_DV, (h + 1) * RET_DV)
        dec = dec_ref[h]
        xi = xi_ref[h]
        zeta = zeta_ref[h]
        cd = cd_ref[h]
        gg = gg_ref[:, dv]
        gb = gb_ref[:, dv]
        r = r_ref[h]
        for n in range(RET_STEP_CHUNKS):
            rows = slice(n * C, (n + 1) * C)
            qc = q_ref[0, rows, dk]
            kT = kT_ref[0, n, dk, :]
            vc = v_ref[0, rows, dv]
            s = _nn(qc, kT.astype(BF16)) * dec
            o = _nn(s.astype(BF16), vc) + _nn(qc, r.astype(BF16)) * xi
            r = r * cd + _nn((kT * zeta).astype(BF16), vc)
            mu = jnp.mean(o, axis=-1, keepdims=True)
            var = jnp.mean(jnp.square(o - mu), axis=-1, keepdims=True)
            o_ref[0, rows, dv] = (o - mu) * lax.rsqrt(var + GN_EPS) * gg + gb
        r_ref[h] = r


def _retention(rq, rkT, rv, dec, xi, zeta, cd, gn_g, gn_b):
    B, T, _ = rq.shape
    ts = RET_STEP_CHUNKS * RET_CHUNK
    full = lambda a: pl.BlockSpec(a.shape, lambda b, j: (0,) * a.ndim)
    in_specs = [
        pl.BlockSpec((1, ts, RET_HEADS * RET_DK), lambda b, j: (b, j, 0)),
        pl.BlockSpec((1, RET_STEP_CHUNKS, RET_HEADS * RET_DK, RET_CHUNK), lambda b, j: (b, j, 0, 0)),
        pl.BlockSpec((1, ts, RET_HEADS * RET_DV), lambda b, j: (b, j, 0)),
    ] + [full(a) for a in (dec, xi, zeta, cd, gn_g, gn_b)]
    return pl.pallas_call(
        _ret_body, out_shape=jax.ShapeDtypeStruct((B, T, RET_HEADS * RET_DV), F32),
        grid=(B, T // ts), in_specs=in_specs,
        out_specs=pl.BlockSpec((1, ts, RET_HEADS * RET_DV), lambda b, j: (b, j, 0)),
        scratch_shapes=[pltpu.VMEM((RET_HEADS, RET_DK, RET_DV), F32)],
        compiler_params=_cparams(("parallel", "arbitrary")), name="retention",
    )(rq, rkT, rv, dec, xi, zeta, cd, gn_g, gn_b)


def _layer_norm(y, g, b):
    mu = jnp.mean(y, axis=-1, keepdims=True)
    var = jnp.mean(jnp.square(y - mu), axis=-1, keepdims=True)
    return (y - mu) * lax.rsqrt(var + LN_EPS) * g + b


def _merge_body(alpha, x_ref, oa_ref, or_ref, wrg_ref, wmg_ref, wua_ref, wur_ref, wo_ref, g1_ref, b1_ref,
                wrh_ref, wrl_ref, rb_ref, x1_ref, eid_ref, wgt_ref):
    x = x_ref[...]
    xb = x.astype(BF16)
    rgate = jax.nn.silu(_nn(xb, wrg_ref[...]))
    o_ret = (or_ref[...] * rgate).astype(BF16)
    a = _nn(oa_ref[...], wua_ref[...])
    r = _nn(o_ret, wur_ref[...])
    mg = jax.nn.sigmoid(_nn(xb, wmg_ref[...]))
    merged = mg[:, :D_MODEL] * a + mg[:, D_MODEL:] * r
    mix = _nn(merged.astype(BF16), wo_ref[...])
    x1 = _layer_norm(alpha * x + mix, g1_ref[...], b1_ref[...])
    x1_ref[...] = x1

    xh = x1.astype(BF16)
    xl = (x1 - xh.astype(F32)).astype(BF16)
    wh = wrh_ref[...]
    lg = _nt(wh, xh) + _nt(wh, xl) + _nt(wrl_ref[...], xh) + rb_ref[...]
    ne = N_EXPERTS
    gl = lg[ne:ne + MOE_GROUPS]
    ge = jnp.exp(gl - jnp.max(gl, axis=0, keepdims=True))
    pg = ge / jnp.sum(ge, axis=0, keepdims=True)
    g_prob = jnp.max(pg, axis=0, keepdims=True)
    gi = lax.broadcasted_iota(I32, pg.shape, 0)
    g_idx = jnp.min(jnp.where(pg == g_prob, gi, MOE_GROUPS), axis=0, keepdims=True)
    inner = jnp.zeros((EXPERTS_PER_GROUP, lg.shape[1]), F32)
    for gq in range(MOE_GROUPS):
        inner = inner + jnp.where(g_idx == gq, lg[8 * gq:8 * (gq + 1)], 0.0)
    ei = lax.broadcasted_iota(I32, inner.shape, 0)
    m1 = jnp.max(inner, axis=0, keepdims=True)
    i1 = jnp.min(jnp.where(inner == m1, ei, EXPERTS_PER_GROUP), axis=0, keepdims=True)
    rest = jnp.where(ei == i1, -jnp.inf, inner)
    m2 = jnp.max(rest, axis=0, keepdims=True)
    i2 = jnp.min(jnp.where(rest == m2, ei, EXPERTS_PER_GROUP), axis=0, keepdims=True)
    e2 = jnp.exp(m2 - m1)
    den = 1.0 + e2
    w1 = (1.0 / den) * g_prob
    w2 = (e2 / den) * g_prob
    zi = jnp.zeros((6, lg.shape[1]), I32)
    eid_ref[...] = jnp.concatenate([g_idx * EXPERTS_PER_GROUP + i1, g_idx * EXPERTS_PER_GROUP + i2, zi], axis=0)
    wgt_ref[...] = jnp.concatenate([w1, w2, jnp.zeros((6, lg.shape[1]), F32)], axis=0)


def _merge(alpha, x2, oa2, or2, wrg, wmg, wua, wur, wo, g1, b1, wrh, wrl, rb):
    M, D = x2.shape
    tm = MERGE_TM
    full = lambda a: pl.BlockSpec(a.shape, lambda i: (0,) * a.ndim)
    in_specs = [pl.BlockSpec((tm, D), lambda i: (i, 0)),
                pl.BlockSpec((tm, oa2.shape[1]), lambda i: (i, 0)),
                pl.BlockSpec((tm, or2.shape[1]), lambda i: (i, 0))] + [
        full(a) for a in (wrg, wmg, wua, wur, wo, g1, b1, wrh, wrl, rb)]
    out_shape = [jax.ShapeDtypeStruct((M, D), F32),
                 jax.ShapeDtypeStruct((8, M), I32),
                 jax.ShapeDtypeStruct((8, M), F32)]
    out_specs = [pl.BlockSpec((tm, D), lambda i: (i, 0)),
                 pl.BlockSpec((8, tm), lambda i: (0, i)),
                 pl.BlockSpec((8, tm), lambda i: (0, i))]
    return pl.pallas_call(
        functools.partial(_merge_body, alpha), out_shape=out_shape, grid=(M // tm,),
        in_specs=in_specs, out_specs=out_specs,
        compiler_params=_cparams(("parallel",)), name="merge",
    )(x2, oa2, or2, wrg, wmg, wua, wur, wo, g1, b1, wrh, wrl, rb)


def _rank_body(eid_ref, tri_ref, low_ref, pos_ref, cnt_ref, carry_ref, off_ref):
    p = pl.program_id(0)
    j = pl.program_id(1)
    e = eid_ref[...]
    rows = lax.broadcasted_iota(I32, (N_EXPERTS, e.shape[1]), 0)
    hit = rows == e
    per_expert = jnp.sum(jnp.where(hit, 1.0, 0.0), axis=1, keepdims=True)

    @pl.when((p == 0) & (j == 0))
    def _():
        carry_ref[...] = jnp.zeros(carry_ref.shape, F32)

    @pl.when(p == 0)
    def _():
        pos_ref[...] = jnp.zeros(pos_ref.shape, I32)
        carry_ref[...] = carry_ref[...] + per_expert
        cnt_ref[...] = carry_ref[...]

    @pl.when((p == 1) & (j == 0))
    def _():
        tiles = jnp.floor((carry_ref[...] + (EXPERT_TM - 1)) * (1.0 / EXPERT_TM))
        off_ref[...] = _nn(low_ref[...], tiles.astype(BF16)) * EXPERT_TM
        carry_ref[...] = jnp.zeros(carry_ref.shape, F32)

    @pl.when(p == 1)
    def _():
        onehot = jnp.where(hit, 1.0, 0.0).astype(BF16)
        incl = _nn(onehot, tri_ref[...])
        base = carry_ref[:, 0:1] + off_ref[:, 0:1] - 1.0
        pos_ref[...] = jnp.sum(jnp.where(hit, incl + base, 0.0), axis=0, keepdims=True).astype(I32)
        carry_ref[...] = carry_ref[...] + per_expert


def _rank(eid_flat, tri, low):
    n = eid_flat.shape[1]
    tb = RANK_TB
    return pl.pallas_call(
        _rank_body,
        out_shape=[jax.ShapeDtypeStruct((1, n), I32), jax.ShapeDtypeStruct((N_EXPERTS, 128), F32)],
        grid=(2, n // tb),
        in_specs=[pl.BlockSpec((1, tb), lambda p, j: (0, j)), pl.BlockSpec((tb, tb), lambda p, j: (0, 0)),
                  pl.BlockSpec(low.shape, lambda p, j: (0, 0))],
        out_specs=[pl.BlockSpec((1, tb), lambda p, j: (0, j * p)),
                   pl.BlockSpec((N_EXPERTS, 128), lambda p, j: (0, 0))],
        scratch_shapes=[pltpu.VMEM((N_EXPERTS, 128), F32), pltpu.VMEM((N_EXPERTS, 128), F32)],
        compiler_params=_cparams(("arbitrary", "arbitrary")), name="rank",
    )(eid_flat, tri, low)


def _invert_body(pos_ref, inv_ref):
    n_pad = inv_ref.shape[0]
    m_tok = pos_ref.shape[0] // 2

    def clear(s, carry):
        inv_ref[s] = 0
        return carry

    lax.fori_loop(0, n_pad, clear, 0, unroll=16)

    def place(t, carry):
        inv_ref[pos_ref[t]] = t
        inv_ref[pos_ref[m_tok + t]] = t
        return carry

    lax.fori_loop(0, m_tok, place, 0, unroll=8)


def _invert(pos, n_pad):
    grid_spec = pltpu.PrefetchScalarGridSpec(
        num_scalar_prefetch=1, grid=(1,), in_specs=[],
        out_specs=pl.BlockSpec(memory_space=pltpu.SMEM),
    )
    return pl.pallas_call(
        _invert_body, out_shape=jax.ShapeDtypeStruct((n_pad,), I32), grid_spec=grid_spec,
        compiler_params=_cparams(("arbitrary",)), name="invert",
    )(pos)


def _experts_body(te_ref, nu_ref, seg_ref, nxt_ref, inv_ref, x_hbm, wg_hbm, wu_hbm, wd_hbm, ys_ref,
                  wgb_ref, wub_ref, wdb_ref, sg_ref, su_ref, sd_ref, xa_ref, xb_ref, wsem, sem_a, sem_b):
    tm = EXPERT_TM
    k = pl.program_id(0)
    last = nu_ref[0] - 1

    def fetch(e, slot):
        return (pltpu.make_async_copy(wg_hbm.at[e], sg_ref.at[slot], wsem.at[0, slot]),
                pltpu.make_async_copy(wu_hbm.at[e], su_ref.at[slot], wsem.at[1, slot]),
                pltpu.make_async_copy(wd_hbm.at[e], sd_ref.at[slot], wsem.at[2, slot]))

    def row_copy(buf, sem, r, base):
        return pltpu.make_async_copy(x_hbm.at[pl.ds(inv_ref[base + r], 1), :], buf.at[pl.ds(r, 1), :], sem)

    def issue_inline(buf, sem, tile):
        for r in range(tm):
            row_copy(buf, sem, r, tile * tm).start()

    def drain(buf, sem):
        lax.fori_loop(0, tm, lambda r, c: (row_copy(buf, sem, 0, 0).wait(), c)[1], 0, unroll=16)

    @pl.when(k == 0)
    def _():
        lax.fori_loop(0, tm, lambda r, c: (row_copy(xa_ref, sem_a, r, 0).start(), c)[1], 0, unroll=16)
        for c in fetch(te_ref[0], 0):
            c.start()

    def sub_tile(s, t, buf, sem, other, other_sem):
        rows = slice(s * tm, (s + 1) * tm)

        @pl.when((t <= last) & (seg_ref[t] >= 0))
        def _():
            slot = seg_ref[t]
            for c in fetch(te_ref[t], slot):
                c.wait()
            wgb_ref[...] = sg_ref[slot].astype(BF16)
            wub_ref[...] = su_ref[slot].astype(BF16)
            wdb_ref[...] = sd_ref[slot].astype(BF16)

            @pl.when(nxt_ref[t] >= 0)
            def _():
                for c in fetch(nxt_ref[t], 1 - slot):
                    c.start()

        @pl.when(t <= last)
        def _():
            drain(buf, sem)
            issue_inline(other, other_sem, jnp.minimum(t + 1, last))
            xb = buf[...].astype(BF16)
            hg = _nn(xb, wgb_ref[...])
            hu = _nn(xb, wub_ref[...])
            h = (jax.nn.silu(hg) * hu).astype(BF16)
            ys_ref[rows, :] = _nn(h, wdb_ref[...])

        @pl.when(t == last)
        def _():
            drain(other, other_sem)

        @pl.when(t > last)
        def _():
            ys_ref[rows, :] = jnp.zeros((tm, ys_ref.shape[1]), F32)

    sub_tile(0, 2 * k, xa_ref, sem_a, xb_ref, sem_b)
    sub_tile(1, 2 * k + 1, xb_ref, sem_b, xa_ref, sem_a)


def _experts(tile_expert, n_used, seg_slot, next_expert, inv, x1, wg, wu, wd):
    npad = inv.shape[0]
    D = x1.shape[1]
    tm = EXPERT_TM
    any_spec = pl.BlockSpec(memory_space=pl.ANY)
    grid_spec = pltpu.PrefetchScalarGridSpec(
        num_scalar_prefetch=5, grid=(npad // (2 * tm),),
        in_specs=[any_spec, any_spec, any_spec, any_spec],
        out_specs=pl.BlockSpec((2 * tm, D), lambda i, *_: (i, 0)),
        scratch_shapes=[pltpu.VMEM((D, D_FF), BF16), pltpu.VMEM((D, D_FF), BF16), pltpu.VMEM((D_FF, D), BF16),
                        pltpu.VMEM((2, D, D_FF), F32), pltpu.VMEM((2, D, D_FF), F32),
                        pltpu.VMEM((2, D_FF, D), F32),
                        pltpu.VMEM((tm, D), F32), pltpu.VMEM((tm, D), F32),
                        pltpu.SemaphoreType.DMA((3, 2)), pltpu.SemaphoreType.DMA, pltpu.SemaphoreType.DMA],
    )
    return pl.pallas_call(
        _experts_body, out_shape=jax.ShapeDtypeStruct((npad, D), F32), grid_spec=grid_spec,
        compiler_params=_cparams(("arbitrary",), disable_bounds_checks=True), name="experts",
    )(tile_expert, n_used, seg_slot, next_expert, inv, x1, wg, wu, wd)


def _combine_body(alpha, pos_ref, x1_ref, w_ref, eye_ref, g2_ref, b2_ref, ys_ref, o_ref,
                  buf_a, buf_b, sem_a, sem_b):
    tm = COMBINE_TM
    m_tok = pos_ref.shape[0] // 2
    k = pl.program_id(0)
    n = pl.num_programs(0)

    def row_copies(buf, sem, r, base):
        return (pltpu.make_async_copy(ys_ref.at[pl.ds(pos_ref[base + r], 1), :], buf.at[0, pl.ds(r, 1), :], sem),
                pltpu.make_async_copy(ys_ref.at[pl.ds(pos_ref[m_tok + base + r], 1), :],
                                      buf.at[1, pl.ds(r, 1), :], sem))

    def issue_inline(buf, sem, tile):
        for r in range(tm):
            for c in row_copies(buf, sem, r, tile * tm):
                c.start()

    def drain(buf, sem):
        def body(r, carry):
            for c in row_copies(buf, sem, 0, 0):
                c.wait()
            return carry
        lax.fori_loop(0, tm, body, 0, unroll=16)

    def compute(buf, s):
        rows = slice(s * tm, (s + 1) * tm)
        w = w_ref[:, rows]
        eye = eye_ref[...]
        w_a = w.astype(BF16)
        w_b = (w - w_a.astype(F32)).astype(BF16)
        w_c = (w - w_a.astype(F32) - w_b.astype(F32)).astype(BF16)
        wcol = _nt(eye, w_a) + _nt(eye, w_b) + _nt(eye, w_c)
        moe = buf[0] * wcol[:, 0:1] + buf[1] * wcol[:, 1:2]
        o_ref[rows, :] = _layer_norm(alpha * x1_ref[rows, :] + moe, g2_ref[...], b2_ref[...])

    @pl.when(k == 0)
    def _():
        def body(r, carry):
            for c in row_copies(buf_a, sem_a, r, 0):
                c.start()
            return carry
        lax.fori_loop(0, tm, body, 0, unroll=16)

    drain(buf_a, sem_a)
    issue_inline(buf_b, sem_b, 2 * k + 1)
    compute(buf_a, 0)
    drain(buf_b, sem_b)
    issue_inline(buf_a, sem_a, jnp.minimum(2 * k + 2, 2 * n - 2))
    compute(buf_b, 1)

    @pl.when(k == n - 1)
    def _():
        drain(buf_a, sem_a)


def _combine(alpha, pos, x1, wgt, eye, g2, b2, ys):
    M, D = x1.shape
    tm = COMBINE_TM
    grid_spec = pltpu.PrefetchScalarGridSpec(
        num_scalar_prefetch=1, grid=(M // (2 * tm),),
        in_specs=[pl.BlockSpec((2 * tm, D), lambda i, pos: (i, 0)),
                  pl.BlockSpec((8, 2 * tm), lambda i, pos: (0, i)),
                  pl.BlockSpec((tm, tm), lambda i, pos: (0, 0)),
                  pl.BlockSpec((1, D), lambda i, pos: (0, 0)),
                  pl.BlockSpec((1, D), lambda i, pos: (0, 0)),
                  pl.BlockSpec(memory_space=pl.ANY)],
        out_specs=pl.BlockSpec((2 * tm, D), lambda i, pos: (i, 0)),
        scratch_shapes=[pltpu.VMEM((2, tm, D), F32), pltpu.VMEM((2, tm, D), F32),
                        pltpu.SemaphoreType.DMA, pltpu.SemaphoreType.DMA],
    )
    return pl.pallas_call(
        functools.partial(_combine_body, alpha), out_shape=jax.ShapeDtypeStruct((M, D), F32),
        grid_spec=grid_spec, compiler_params=_cparams(("arbitrary",), disable_bounds_checks=True),
        name="combine",
    )(pos, x1, wgt, eye, g2, b2, ys)


def _rope_angles(pos, dim):
    half = dim // 2
    inv_freq = ROPE_THETA ** (-np.arange(half, dtype=np.float64) * 2.0 / dim)
    return pos.astype(np.float64)[:, None] * inv_freq[None, :]


def _tables(T):
    f32 = lambda a: jnp.asarray(np.ascontiguousarray(a), dtype=F32)
    ang = _rope_angles(np.arange(T), NSA_HD)
    cos, sin = np.cos(ang), np.sin(ang)
    c64 = np.tile(cos, (1, 4))
    s64 = np.tile(np.concatenate([-sin, sin], axis=1), (1, 2))
    ang2 = _rope_angles(np.arange(T), RET_DK)
    cos2, sin2 = np.cos(ang2), np.sin(ang2)
    c128 = np.tile(cos2, (1, 2))
    s128 = np.concatenate([-sin2, sin2], axis=1)
    n_rows = T // CMP_STRIDE
    angc = _rope_angles(np.arange(n_rows) * CMP_STRIDE + CMP_BLOCK - 1, NSA_HD)
    cc = np.tile(np.cos(angc), (1, 2))
    sc = np.tile(np.sin(angc), (1, 2))
    return tuple(f32(a) for a in (c64, s64, c128, s128, cos.T, sin.T, cos2.T, sin2.T)), f32(cc), f32(sc)


def _overlap_matrix(T):
    n_rows = T // CMP_STRIDE
    n_sel = T // SLC_BLOCK
    cmp_start = np.arange(n_rows) * CMP_STRIDE
    sel_start = np.arange(n_sel) * SLC_BLOCK
    ov = np.clip(np.minimum(cmp_start[None, :] + CMP_BLOCK, sel_start[:, None] + SLC_BLOCK)
                 - np.maximum(cmp_start[None, :], sel_start[:, None]), 0, None)
    return jnp.asarray(ov.astype(np.float32) / CMP_STRIDE, dtype=BF16)


def _retention_tables():
    C = RET_CHUNK
    f32 = lambda a: jnp.asarray(np.ascontiguousarray(a), dtype=F32)
    gamma = 1.0 - 2.0 ** (-5.0 - np.arange(RET_HEADS, dtype=np.float64))
    log_g = np.log(gamma)
    i = np.arange(C, dtype=np.float64)
    diff = i[:, None] - i[None, :]
    dec = np.where(diff >= 0, np.exp(np.maximum(diff, 0.0) * log_g[:, None, None]), 0.0)
    xi = np.exp((i + 1.0) * log_g[:, None])
    zeta = np.exp((C - 1.0 - i) * log_g[:, None])
    cd = np.exp(C * log_g)
    xi_b = np.broadcast_to(xi[:, :, None], (RET_HEADS, C, RET_DV))
    cd_b = np.broadcast_to(cd[:, None, None], (RET_HEADS, 1, RET_DV))
    return f32(dec), f32(xi_b), f32(zeta[:, None, :]), f32(cd_b)


def _rot_half_cols(w):
    half = w.shape[-1] // 2
    return jnp.concatenate([-w[..., half:], w[..., :half]], axis=-1)


def kernel(x, w_in, cmp_pos_k, cmp_k_w1, cmp_k_b1, cmp_k_w2, cmp_k_b2, cmp_pos_v, cmp_v_w1, cmp_v_b1, cmp_v_w2, cmp_v_b2, ret_gn_g, ret_gn_b, w_up_attn, w_up_ret, w_out, ln1_g, ln1_b, router_group_w, router_group_b, router_inner_w, router_inner_b, expert_w_gate, expert_w_up, expert_w_down, ln2_g, ln2_b):
    B, T, D = x.shape
    M = B * T
    depth = w_in.shape[0]
    alpha = (2.0 * depth) ** 0.25
    tabs, cc, sc = _tables(T)
    ov = _overlap_matrix(T)
    dec, xi_b, zeta, cd_b = _retention_tables()
    tri = jnp.asarray(np.triu(np.ones((RANK_TB, RANK_TB), np.float32)), dtype=BF16)
    low = jnp.asarray(np.tril(np.ones((N_EXPERTS, N_EXPERTS), np.float32), k=-1), dtype=BF16)
    eye = jnp.asarray(np.eye(COMBINE_TM, dtype=np.float32), dtype=BF16)
    n_pad = 2 * M + N_EXPERTS * EXPERT_TM
    n_tiles = n_pad // EXPERT_TM

    for l in range(depth):
        w = w_in[l]
        col = lambda n: w[:, _OFF[n][0]:_OFF[n][1]]
        wn = jnp.concatenate([col("cmp_k"), col("cmp_v"), col("slc_k"), col("win_k"), col("ret_q"), col("ret_v")],
                             axis=1).astype(BF16)
        wt = jnp.concatenate([col("nsa_q"), col("slc_v"), col("win_v"), col("ret_k"), col("nsa_gate"),
                              jnp.zeros((D, 8), F32)], axis=1).T.astype(BF16)
        qT, ckv, sk, wk, svT, wvT, gT, rq, rkT, rv = _proj(x, wn, wt, tabs)

        kc, vcT = _compress(
            ckv, cmp_pos_k[l].reshape(2, -1), cmp_pos_v[l].reshape(2, -1),
            cmp_k_w1[l], cmp_k_b1[l][None, :], cmp_k_w2[l], _rot_half_cols(cmp_k_w2[l]),
            cmp_k_b2[l][None, :], _rot_half_cols(cmp_k_b2[l])[None, :], cc, sc,
            cmp_v_w1[l], cmp_v_b1[l][None, :], cmp_v_w2[l].T, cmp_v_b2[l][:, None])
        o_attn = _nsa(qT, kc, vcT, ov, sk, svT, wk, wvT, gT)
        o_ret = _retention(rq, rkT, rv, dec, xi_b, zeta, cd_b, ret_gn_g[l][None, :], ret_gn_b[l][None, :])

        wr = jnp.concatenate([router_inner_w[l].transpose(0, 2, 1).reshape(N_EXPERTS, D),
                              router_group_w[l].T, jnp.zeros((4, D), F32)], axis=0)
        wrh = wr.astype(BF16)
        wrl = (wr - wrh.astype(F32)).astype(BF16)
        rb = jnp.concatenate([router_inner_b[l].reshape(-1), router_group_b[l], jnp.zeros((4,), F32)])[:, None]
        x1, eid, wgt = _merge(
            alpha, x.reshape(M, D), o_attn.reshape(M, -1), o_ret.reshape(M, -1),
            col("ret_gate").astype(BF16), col("merge_gate").astype(BF16), w_up_attn[l].astype(BF16),
            w_up_ret[l].astype(BF16), w_out[l].astype(BF16), ln1_g[l][None, :], ln1_b[l][None, :], wrh, wrl, rb)

        eid_flat = eid[:2].reshape(1, 2 * M)
        pos, cnt = _rank(eid_flat, tri, low)
        pos = pos[0]
        counts = cnt[:, 0].astype(I32)
        tiles_per = (counts + EXPERT_TM - 1) // EXPERT_TM
        tile_end = jnp.cumsum(tiles_per)
        tile_ids = jnp.arange(n_tiles, dtype=I32)
        tile_expert = jnp.minimum(jnp.sum((tile_end[None, :] <= tile_ids[:, None]).astype(I32), axis=1),
                                  N_EXPERTS - 1).astype(I32)
        n_used = tile_end[-1:].astype(I32)

        inv = _invert(pos, n_pad)
        prev_expert = jnp.concatenate([jnp.full((1,), -1, I32), tile_expert[:-1]])
        is_first = (tile_ids < n_used[0]) & (tile_expert != prev_expert)
        seg_slot = jnp.where(is_first, (jnp.cumsum(is_first.astype(I32)) - 1) % 2, -1).astype(I32)
        own = tile_expert[:, None] == jnp.arange(N_EXPERTS, dtype=I32)[None, :]
        seg_end = jnp.sum(jnp.where(own, tile_end[None, :], 0), axis=1)
        expert_at_end = jnp.minimum(jnp.sum((tile_end[None, :] <= seg_end[:, None]).astype(I32), axis=1),
                                    N_EXPERTS - 1)
        next_expert = jnp.where(is_first & (seg_end < n_used[0]), expert_at_end, -1).astype(I32)
        ys = _experts(tile_expert, n_used, seg_slot, next_expert, inv, x1,
                      expert_w_gate[l], expert_w_up[l], expert_w_down[l])
        x = _combine(alpha, pos, x1, wgt, eye, ln2_g[l][None, :], ln2_b[l][None, :], ys).reshape(B, T, D)
    return x
```

```python
import functools

import numpy as np
import jax
import jax.numpy as jnp
from jax import lax
from jax.experimental import pallas as pl
from jax.experimental.pallas import tpu as pltpu

F32 = jnp.float32
BF16 = jnp.bfloat16
I32 = jnp.int32

D_MODEL = 1024
NSA_HEADS = 8
NSA_HD = 64
NSA_GROUPS = 2
NSA_HPG = NSA_HEADS // NSA_GROUPS
CMP_BLOCK = 32
CMP_STRIDE = 16
CMP_HIDDEN = 256
SLC_BLOCK = 64
SLC_TOPK = 16
WINDOW = 512
RET_HEADS = 4
RET_DK = 128
RET_DV = 256
RET_CHUNK = 128
MOE_GROUPS = 4
EXPERTS_PER_GROUP = 8
N_EXPERTS = MOE_GROUPS * EXPERTS_PER_GROUP
D_FF = 512
ROPE_THETA = 10000.0
LN_EPS = 1e-5
GN_EPS = 1e-5
NEG_INF = -1e30
LOG2_E = 1.4426950408889634

VMEM_LIMIT_V7X = 56 * 1024 * 1024

_OFF = {}
_o = 0
for _n, _w in (("nsa_q", 512), ("cmp_k", 128), ("cmp_v", 128), ("slc_k", 128), ("slc_v", 128),
               ("win_k", 128), ("win_v", 128), ("nsa_gate", 24), ("ret_q", 512), ("ret_k", 512),
               ("ret_v", 1024), ("ret_gate", 1024), ("merge_gate", 2048)):
    _OFF[_n] = (_o, _o + _w)
    _o += _w

PROJ_TM = 512
NSA_TQ = 256
NSA_KC = 256
NSA_VROWS = 80
RET_STEP_CHUNKS = 4
MERGE_TM = 512
RANK_TB = 1024
DISPATCH_TM = 512
EXPERT_TM = 256
COMBINE_TM = 256


def _cparams(sem, **kw):
    return pltpu.CompilerParams(dimension_semantics=sem, vmem_limit_bytes=VMEM_LIMIT_V7X, **kw)


def _nt(a, b):
    return lax.dot_general(a, b, (((1,), (1,)), ((), ())), preferred_element_type=F32)


def _nn(a, b):
    return jnp.dot(a, b, preferred_element_type=F32)


def _proj_body(x_ref, wn_ref, wt_ref, c64_ref, s64_ref, c128_ref, s128_ref,
               ct64_ref, st64_ref, ct128_ref, st128_ref,
               qT_ref, ckv_ref, sk_ref, wk_ref, svT_ref, wvT_ref, gT_ref, rq_ref, rkT_ref, rv_ref):
    tm = PROJ_TM
    xb = x_ref[0].astype(BF16)

    def nn(a, b):
        return _nn(xb, wn_ref[:, a:b])

    def nt(a, b):
        return _nt(wt_ref[a:b, :], xb)

    ckv = nn(0, 256)
    for j in range(4):
        ckv_ref[0, j] = ckv[:, 64 * j:64 * (j + 1)]

    lane = lax.broadcasted_iota(I32, (tm, 128), 1)
    first = (lane % 64) < 32
    c64 = c64_ref[...]
    s64 = s64_ref[...]

    def rope64(k):
        rot = jnp.where(first, pltpu.roll(k, 96, 1), pltpu.roll(k, 32, 1))
        return k * c64 + rot * s64

    sk = rope64(nn(256, 384)).astype(BF16)
    sk_ref[0, 0] = sk[:, :64]
    sk_ref[0, 1] = sk[:, 64:]
    wk = rope64(nn(384, 512)).astype(BF16)
    wk_ref[0, 0] = wk[:, :64]
    wk_ref[0, 1] = wk[:, 64:]

    c128 = c128_ref[...]
    s128 = s128_ref[...]
    rq = nn(512, 1024)
    for h in range(RET_HEADS):
        ch = rq[:, 128 * h:128 * (h + 1)]
        rq_ref[0, :, 128 * h:128 * (h + 1)] = (ch * c128 + pltpu.roll(ch, 64, 1) * s128).astype(BF16)
    rv_ref[0] = nn(1024, 2048).astype(BF16)

    ct = ct64_ref[...]
    st = st64_ref[...]
    qT = nt(0, 512)
    scale_q = NSA_HD ** -0.5 * LOG2_E
    tq = NSA_TQ
    for hh in range(NSA_HEADS):
        g, h = divmod(hh, NSA_HPG)
        x1 = qT[64 * hh:64 * hh + 32]
        x2 = qT[64 * hh + 32:64 * hh + 64]
        o1 = ((x1 * ct - x2 * st) * scale_q).astype(BF16)
        o2 = ((x1 * st + x2 * ct) * scale_q).astype(BF16)
        for it in range(tm // tq):
            qT_ref[0, g, it, 0:32, h * tq:(h + 1) * tq] = o1[:, it * tq:(it + 1) * tq]
            qT_ref[0, g, it, 32:64, h * tq:(h + 1) * tq] = o2[:, it * tq:(it + 1) * tq]

    svT = nt(512, 640).astype(BF16)
    wvT = nt(640, 768).astype(BF16)
    row16 = lax.broadcasted_iota(I32, (NSA_VROWS - NSA_HD, NSA_KC), 0)
    ones_blk = jnp.where(row16 == 0, 1.0, 0.0).astype(BF16)
    for c in range(tm // NSA_KC):
        for g in range(NSA_GROUPS):
            for vT, ref in ((svT, svT_ref), (wvT, wvT_ref)):
                ref[0, c, g, 0:NSA_HD, :] = vT[64 * g:64 * (g + 1), NSA_KC * c:NSA_KC * (c + 1)]
                ref[0, c, g, NSA_HD:NSA_VROWS, :] = ones_blk

    ct2 = ct128_ref[...]
    st2 = st128_ref[...]
    rkT = nt(768, 1280)
    scale_k = RET_DK ** -0.5
    for h in range(RET_HEADS):
        x1 = rkT[128 * h:128 * h + 64]
        x2 = rkT[128 * h + 64:128 * h + 128]
        o1 = (x1 * ct2 - x2 * st2) * scale_k
        o2 = (x1 * st2 + x2 * ct2) * scale_k
        for c in range(tm // RET_CHUNK):
            rkT_ref[0, c, 128 * h:128 * h + 64, :] = o1[:, 128 * c:128 * (c + 1)]
            rkT_ref[0, c, 128 * h + 64:128 * h + 128, :] = o2[:, 128 * c:128 * (c + 1)]

    gT_ref[0] = jax.nn.sigmoid(nt(1280, 1312))


def _proj(x, wn, wt, tabs):
    B, T, D = x.shape
    tm = PROJ_TM
    c64, s64, c128, s128, ct64, st64, ct128, st128 = tabs
    const = lambda b, i: (0, 0)
    in_specs = [
        pl.BlockSpec((1, tm, D), lambda b, i: (b, i, 0)),
        pl.BlockSpec(wn.shape, const),
        pl.BlockSpec(wt.shape, const),
        pl.BlockSpec((tm, 128), lambda b, i: (i, 0)),
        pl.BlockSpec((tm, 128), lambda b, i: (i, 0)),
        pl.BlockSpec((tm, 128), lambda b, i: (i, 0)),
        pl.BlockSpec((tm, 128), lambda b, i: (i, 0)),
        pl.BlockSpec((32, tm), lambda b, i: (0, i)),
        pl.BlockSpec((32, tm), lambda b, i: (0, i)),
        pl.BlockSpec((64, tm), lambda b, i: (0, i)),
        pl.BlockSpec((64, tm), lambda b, i: (0, i)),
    ]
    out_shape = [
        jax.ShapeDtypeStruct((B, NSA_GROUPS, T // NSA_TQ, NSA_HD, NSA_HPG * NSA_TQ), BF16),
        jax.ShapeDtypeStruct((B, 4, T, 64), F32),
        jax.ShapeDtypeStruct((B, 2, T, 64), BF16),
        jax.ShapeDtypeStruct((B, 2, T, 64), BF16),
        jax.ShapeDtypeStruct((B, T // NSA_KC, NSA_GROUPS, NSA_VROWS, NSA_KC), BF16),
        jax.ShapeDtypeStruct((B, T // NSA_KC, NSA_GROUPS, NSA_VROWS, NSA_KC), BF16),
        jax.ShapeDtypeStruct((B, 32, T), F32),
        jax.ShapeDtypeStruct((B, T, 512), BF16),
        jax.ShapeDtypeStruct((B, T // RET_CHUNK, 512, RET_CHUNK), F32),
        jax.ShapeDtypeStruct((B, T, 1024), BF16),
    ]
    out_specs = [
        pl.BlockSpec((1, NSA_GROUPS, tm // NSA_TQ, NSA_HD, NSA_HPG * NSA_TQ), lambda b, i: (b, 0, i, 0, 0)),
        pl.BlockSpec((1, 4, tm, 64), lambda b, i: (b, 0, i, 0)),
        pl.BlockSpec((1, 2, tm, 64), lambda b, i: (b, 0, i, 0)),
        pl.BlockSpec((1, 2, tm, 64), lambda b, i: (b, 0, i, 0)),
        pl.BlockSpec((1, tm // NSA_KC, NSA_GROUPS, NSA_VROWS, NSA_KC), lambda b, i: (b, i, 0, 0, 0)),
        pl.BlockSpec((1, tm // NSA_KC, NSA_GROUPS, NSA_VROWS, NSA_KC), lambda b, i: (b, i, 0, 0, 0)),
        pl.BlockSpec((1, 32, tm), lambda b, i: (b, 0, i)),
        pl.BlockSpec((1, tm, 512), lambda b, i: (b, i, 0)),
        pl.BlockSpec((1, tm // RET_CHUNK, 512, RET_CHUNK), lambda b, i: (b, i, 0, 0)),
        pl.BlockSpec((1, tm, 1024), lambda b, i: (b, i, 0)),
    ]
    return pl.pallas_call(
        _proj_body, out_shape=out_shape, grid=(B, T // tm), in_specs=in_specs, out_specs=out_specs,
        compiler_params=_cparams(("parallel", "parallel")), name="proj",
    )(x, wn, wt, c64, s64, c128, s128, ct64, st64, ct128, st128)


def _compress_body(uk_ref, uv_ref, posk_ref, posv_ref, w1k_ref, b1k_ref, w2k_ref, w2kr_ref, b2k_ref, b2kr_ref,
                   cc_ref, sc_ref, w1v_ref, b1v_ref, w2vT_ref, b2vT_ref, kc_ref, vcT_ref):
    half = CMP_STRIDE * NSA_HD

    def hidden(u_ref, pos_ref, w1_ref, b1_ref):
        u = u_ref[0, 0]
        top = (u + pos_ref[0:1, :]).astype(BF16)
        bot = (u + pos_ref[1:2, :]).astype(BF16)
        a = _nn(top, w1_ref[0:half, :].astype(BF16))
        bm = _nn(bot, w1_ref[half:2 * half, :].astype(BF16))
        n = bm.shape[0]
        return jax.nn.gelu(a + pltpu.roll(bm, n - 1, 0) + b1_ref[...])

    hk = hidden(uk_ref, posk_ref, w1k_ref, b1k_ref).astype(BF16)
    k = _nn(hk, w2k_ref[...].astype(BF16)) + b2k_ref[...]
    kr = _nn(hk, w2kr_ref[...].astype(BF16)) + b2kr_ref[...]
    kc_ref[0, 0] = (k * cc_ref[...] + kr * sc_ref[...]).astype(BF16)

    hv = hidden(uv_ref, posv_ref, w1v_ref, b1v_ref).astype(BF16)
    vcT_ref[0, 0] = (_nt(w2vT_ref[...].astype(BF16), hv) + b2vT_ref[...]).astype(BF16)


def _compress(ckv, posk, posv, w1k, b1k, w2k, w2kr, b2k, b2kr, cc, sc, w1v, b1v, w2vT, b2vT):
    B = ckv.shape[0]
    T = ckv.shape[2]
    nr = T // CMP_STRIDE
    u = ckv.reshape(B, 4, nr, CMP_STRIDE * NSA_HD)
    full = lambda a: pl.BlockSpec(a.shape, lambda b, g: (0,) * a.ndim)
    in_specs = [
        pl.BlockSpec((1, 1, nr, 1024), lambda b, g: (b, g, 0, 0)),
        pl.BlockSpec((1, 1, nr, 1024), lambda b, g: (b, g + 2, 0, 0)),
    ] + [full(a) for a in (posk, posv, w1k, b1k, w2k, w2kr, b2k, b2kr, cc, sc, w1v, b1v, w2vT, b2vT)]
    out_shape = [jax.ShapeDtypeStruct((B, NSA_GROUPS, nr, NSA_HD), BF16),
                 jax.ShapeDtypeStruct((B, NSA_GROUPS, NSA_HD, nr), BF16)]
    out_specs = [pl.BlockSpec((1, 1, nr, NSA_HD), lambda b, g: (b, g, 0, 0)),
                 pl.BlockSpec((1, 1, NSA_HD, nr), lambda b, g: (b, g, 0, 0))]
    return pl.pallas_call(
        _compress_body, out_shape=out_shape, grid=(B, NSA_GROUPS), in_specs=in_specs, out_specs=out_specs,
        compiler_params=_cparams(("parallel", "parallel")), name="compress",
    )(u, u, posk, posv, w1k, b1k, w2k, w2kr, b2k, b2kr, cc, sc, w1v, b1v, w2vT, b2vT)


def _nsa_body(q_ref, kc_ref, vcT_ref, ov_ref, sk_ref, svT_ref, wk_ref, wvT_ref, gT_ref, lo_ref, hi_ref, o_ref,
              score_ref, bias_ref, acc_ref, tot_ref, s_ref):
    tq = NSA_TQ
    kc_n = NSA_KC
    hq = NSA_HPG * tq
    groups = range(NSA_GROUPS)
    i = pl.program_id(1)
    t0 = i * tq
    t_row = t0 + lax.broadcasted_iota(I32, (1, tq), 1)
    t_row4 = t0 + lax.broadcasted_iota(I32, (1, hq), 1) % tq
    n_cmp = kc_ref.shape[2]
    n_sel = ov_ref.shape[0]
    q = [q_ref[0, g, 0] for g in groups]

    def gate4(g, br):
        return jnp.concatenate([gT_ref[0, (g * NSA_HPG + h) * 3 + br:(g * NSA_HPG + h) * 3 + br + 1, :]
                                for h in range(NSA_HPG)], axis=1)

    cmp_end = lax.broadcasted_iota(I32, (n_cmp, 1), 0) * CMP_STRIDE + (CMP_BLOCK - 1)
    cbias = jnp.where(cmp_end <= t_row4, 0.0, NEG_INF)
    any_valid = jnp.where(t_row4 >= CMP_BLOCK - 1, 1.0, 0.0)
    blk_t = t_row // SLC_BLOCK
    sub = lax.broadcasted_iota(I32, (8, 1), 0)

    def select_blocks(g, l_cmp, l_sel):
        s = _nn(kc_ref[0, g, 0:l_cmp, :], q[g]) + cbias[0:l_cmp]
        e = jnp.exp2(s - jnp.max(s, axis=0, keepdims=True))
        p = e * (any_valid / jnp.sum(e, axis=0, keepdims=True))
        tot_ref[g] = gate4(g, 0) * _nn(vcT_ref[0, g, :, 0:l_cmp], p.astype(BF16))
        psum = p[:, 0:tq]
        for h in range(1, NSA_HPG):
            psum = psum + p[:, h * tq:(h + 1) * tq]
        p_hi = psum.astype(BF16)
        p_lo = (psum - p_hi.astype(F32)).astype(BF16)
        ov = ov_ref[0:l_sel, 0:l_cmp]
        imp = _nn(ov, p_hi) + _nn(ov, p_lo)
        jc = lax.broadcasted_iota(I32, (l_sel, 1), 0)
        forced = (jc == 0) | (jc == blk_t) | (jc == blk_t - 1)
        score = jnp.where(forced, 1e9, jnp.where(jc <= blk_t, imp, -1e9)).astype(F32)
        score_ref[g, 0:l_sel, :] = score
        n_slab = l_sel // 8
        slabs = [score[8 * v:8 * (v + 1)] for v in range(n_slab)]
        cnt = [jnp.zeros((8, tq), F32) for _ in range(n_slab)]
        for r in range(l_sel):
            row = jnp.broadcast_to(score_ref[g, r:r + 1, :], (8, tq))
            for v in range(n_slab):
                if r < 8 * v:
                    ahead = jnp.where(row >= slabs[v], 1.0, 0.0)
                elif r >= 8 * (v + 1):
                    ahead = jnp.where(row > slabs[v], 1.0, 0.0)
                else:
                    ahead = jnp.where(sub + 8 * v > r, jnp.where(row >= slabs[v], 1.0, 0.0),
                                      jnp.where(row > slabs[v], 1.0, 0.0))
                cnt[v] = cnt[v] + ahead
        for v in range(n_slab):
            bias_ref[g, 8 * v:8 * (v + 1), :] = jnp.where(
                cnt[v] < float(SLC_TOPK), jnp.where(sub + 8 * v <= blk_t, 0.0, NEG_INF), NEG_INF)
        if l_sel < n_sel:
            bias_ref[g, l_sel:n_sel, :] = jnp.full((n_sel - l_sel, tq), NEG_INF, F32)

    n_var = 4
    tiles_per_var = (n_sel * SLC_BLOCK // tq) // n_var
    for var in range(n_var):
        l_sel = (var + 1) * tiles_per_var * (tq // SLC_BLOCK)
        l_cmp = min(n_cmp, -(-((var + 1) * tiles_per_var * (tq // CMP_STRIDE)) // 128) * 128)

        @pl.when(i // tiles_per_var == var)
        def _():
            for g in groups:
                select_blocks(g, l_cmp, l_sel)

    def qk(k_ref, g, c):
        k_c = k_ref[0, g, pl.ds(pl.multiple_of(c * kc_n, kc_n), kc_n), :]
        return [_nn(k_c, q[g][:, h * tq:(h + 1) * tq]) for h in range(NSA_HPG)]

    def stage_a(scores, add_bias, m):
        biased = [add_bias(scores[h]) for h in range(NSA_HPG)]
        m_new = tuple(jnp.maximum(m[h], jnp.max(biased[h], axis=0, keepdims=True)) for h in range(NSA_HPG))
        return biased, m_new

    def stage_b(g, biased, vT_c, m_old, m_new):
        for h in range(NSA_HPG):
            hs = slice(h * tq, (h + 1) * tq)
            alpha = jnp.exp2(m_old[h] - m_new[h])
            p = jnp.exp2((biased[h] - m_new[h]).astype(BF16))
            acc_ref[g, :, hs] = alpha * acc_ref[g, :, hs] + _nn(vT_c, p)

    nb = kc_n // SLC_BLOCK

    def block_bias(g, c, s):
        return jnp.concatenate([s[SLC_BLOCK * b:SLC_BLOCK * (b + 1)] + bias_ref[g, pl.ds(c * nb + b, 1), :]
                                for b in range(nb)], axis=0)

    def finish(g, br):
        acc = acc_ref[g]
        tot_ref[g] = tot_ref[g] + gate4(g, br) * (acc[0:NSA_HD] / acc[NSA_HD:NSA_HD + 1])
        acc_ref[g] = jnp.zeros(acc.shape, F32)

    def park(g, biased):
        for h in range(NSA_HPG):
            s_ref[g, h] = biased[h]

    def parked(g):
        return [s_ref[g, h] for h in range(NSA_HPG)]

    m0 = tuple(jnp.full((1, tq), NEG_INF, F32) for _ in range(NSA_HPG))

    acc_ref[...] = jnp.zeros(acc_ref.shape, F32)
    m_new = []
    for g in groups:
        biased, m_g = stage_a(qk(sk_ref, g, i), lambda s, g=g: block_bias(g, i, s) + lo_ref[...], m0)
        park(g, biased)
        m_new.append(m_g)

    def slc_chunk(c, carry):
        prev, m_old, m_new = carry
        scores = [qk(sk_ref, g, c) for g in groups]
        for g in groups:
            stage_b(g, parked(g), svT_ref[0, prev, g], m_old[g], m_new[g])
        m_next = []
        for g in groups:
            biased, m_g = stage_a(scores[g], functools.partial(block_bias, g, c), m_new[g])
            park(g, biased)
            m_next.append(m_g)
        return c, m_new, tuple(m_next)

    prev, m_old, m_new = lax.fori_loop(0, i, slc_chunk, (i, (m0,) * NSA_GROUPS, tuple(m_new)))

    c_far = jnp.maximum(i - 2, 0)
    c_near = jnp.maximum(i - 1, 0)
    pen_far = jnp.where(i >= 2, 0.0, NEG_INF).astype(F32)
    pen_near = jnp.where(i >= 1, 0.0, NEG_INF).astype(F32)

    scores = [qk(wk_ref, g, c_far) for g in groups]
    far = []
    for g in groups:
        stage_b(g, parked(g), svT_ref[0, prev, g], m_old[g], m_new[g])
        far.append(stage_a(scores[g], lambda s: s + (hi_ref[...] + pen_far), m0))
        finish(g, 1)

    scores = [qk(wk_ref, g, c_near) for g in groups]
    near = []
    for g in groups:
        stage_b(g, far[g][0], wvT_ref[0, c_far, g], m0, far[g][1])
        near.append(stage_a(scores[g], lambda s: s + pen_near, far[g][1]))
    scores = [qk(wk_ref, g, i) for g in groups]
    for g in groups:
        stage_b(g, near[g][0], wvT_ref[0, c_near, g], far[g][1], near[g][1])
        b_diag, m_diag = stage_a(scores[g], lambda s: s + lo_ref[...], near[g][1])
        stage_b(g, b_diag, wvT_ref[0, i, g], near[g][1], m_diag)
        finish(g, 2)

    o_ref[0] = jnp.concatenate([tot_ref[g, :, h * tq:(h + 1) * tq] for g in groups for h in range(NSA_HPG)],
                               axis=0).T.astype(BF16)


def _nsa(qT, kc, vcT, ov, sk, svT, wk, wvT, gT):
    B, _, T = gT.shape
    tq = NSA_TQ
    assert NSA_KC == tq and WINDOW == 2 * NSA_KC
    nr = kc.shape[2]
    nch = T // NSA_KC
    kk = np.arange(NSA_KC)[:, None]
    tt = np.arange(tq)[None, :]
    lo = jnp.asarray(np.where(kk <= tt, 0.0, NEG_INF), dtype=F32)
    hi = jnp.asarray(np.where(kk > tt, 0.0, NEG_INF), dtype=F32)
    G = NSA_GROUPS
    in_specs = [
        pl.BlockSpec((1, G, 1, NSA_HD, NSA_HPG * tq), lambda b, i: (b, 0, i, 0, 0)),
        pl.BlockSpec((1, G, nr, NSA_HD), lambda b, i: (b, 0, 0, 0)),
        pl.BlockSpec((1, G, NSA_HD, nr), lambda b, i: (b, 0, 0, 0)),
        pl.BlockSpec(ov.shape, lambda b, i: (0, 0)),
        pl.BlockSpec((1, G, T, NSA_HD), lambda b, i: (b, 0, 0, 0)),
        pl.BlockSpec((1, nch, G, NSA_VROWS, NSA_KC), lambda b, i: (b, 0, 0, 0, 0)),
        pl.BlockSpec((1, G, T, NSA_HD), lambda b, i: (b, 0, 0, 0)),
        pl.BlockSpec((1, nch, G, NSA_VROWS, NSA_KC), lambda b, i: (b, 0, 0, 0, 0)),
        pl.BlockSpec((1, 32, tq), lambda b, i: (b, 0, i)),
        pl.BlockSpec(lo.shape, lambda b, i: (0, 0)),
        pl.BlockSpec(hi.shape, lambda b, i: (0, 0)),
    ]
    n_sel = T // SLC_BLOCK
    hq = NSA_HPG * tq
    return pl.pallas_call(
        _nsa_body, out_shape=jax.ShapeDtypeStruct((B, T, NSA_HEADS * NSA_HD), BF16),
        grid=(B, T // tq), in_specs=in_specs,
        out_specs=pl.BlockSpec((1, tq, NSA_HEADS * NSA_HD), lambda b, i: (b, i, 0)),
        scratch_shapes=[pltpu.VMEM((G, n_sel, tq), F32), pltpu.VMEM((G, n_sel, tq), F32),
                        pltpu.VMEM((G, NSA_VROWS, hq), F32), pltpu.VMEM((G, NSA_HD, hq), F32),
                        pltpu.VMEM((G, NSA_HPG, NSA_KC, tq), F32)],
        compiler_params=_cparams(("parallel", "parallel")), name="nsa",
    )(qT, kc, vcT, ov, sk, svT, wk, wvT, gT, lo, hi)


def _ret_body(q_ref, kT_ref, v_ref, dec_ref, xi_ref, zeta_ref, cd_ref, gg_ref, gb_ref, o_ref, r_ref):
    C = RET_CHUNK

    @pl.when(pl.program_id(1) == 0)
    def _():
        r_ref[...] = jnp.zeros(r_ref.shape, F32)

    for h in range(RET_HEADS):
        dk = slice(h * RET_DK, (h + 1) * RET_DK)
        dv = slice(h * RET_DV, (h + 1) * RET_DV)
        dec = dec_ref[h]
        xi = xi_ref[h]
        zeta = zeta_ref[h]
        cd = cd_ref[h]
        gg = gg_ref[:, dv]
        gb = gb_ref[:, dv]
        r = r_ref[h]
        for n in range(RET_STEP_CHUNKS):
            rows = slice(n * C, (n + 1) * C)
            qc = q_ref[0, rows, dk]
            kT = kT_ref[0, n, dk, :]
            vc = v_ref[0, rows, dv]
            s = _nn(qc, kT.astype(BF16)) * dec
            o = _nn(s.astype(BF16), vc) + _nn(qc, r.astype(BF16)) * xi
            r = r * cd + _nn((kT * zeta).astype(BF16), vc)
            mu = jnp.mean(o, axis=-1, keepdims=True)
            var = jnp.mean(jnp.square(o - mu), axis=-1, keepdims=True)
            o_ref[0, rows, dv] = (o - mu) * lax.rsqrt(var + GN_EPS) * gg + gb
        r_ref[h] = r


def _retention(rq, rkT, rv, dec, xi, zeta, cd, gn_g, gn_b):
    B, T, _ = rq.shape
    ts = RET_STEP_CHUNKS * RET_CHUNK
    full = lambda a: pl.BlockSpec(a.shape, lambda b, j: (0,) * a.ndim)
    in_specs = [
        pl.BlockSpec((1, ts, RET_HEADS * RET_DK), lambda b, j: (b, j, 0)),
        pl.BlockSpec((1, RET_STEP_CHUNKS, RET_HEADS * RET_DK, RET_CHUNK), lambda b, j: (b, j, 0, 0)),
        pl.BlockSpec((1, ts, RET_HEADS * RET_DV), lambda b, j: (b, j, 0)),
    ] + [full(a) for a in (dec, xi, zeta, cd, gn_g, gn_b)]
    return pl.pallas_call(
        _ret_body, out_shape=jax.ShapeDtypeStruct((B, T, RET_HEADS * RET_DV), F32),
        grid=(B, T // ts), in_specs=in_specs,
        out_specs=pl.BlockSpec((1, ts, RET_HEADS * RET_DV), lambda b, j: (b, j, 0)),
        scratch_shapes=[pltpu.VMEM((RET_HEADS, RET_DK, RET_DV), F32)],
        compiler_params=_cparams(("parallel", "arbitrary")), name="retention",
    )(rq, rkT, rv, dec, xi, zeta, cd, gn_g, gn_b)


def _layer_norm(y, g, b):
    mu = jnp.mean(y, axis=-1, keepdims=True)
    var = jnp.mean(jnp.square(y - mu), axis=-1, keepdims=True)
    return (y - mu) * lax.rsqrt(var + LN_EPS) * g + b


def _merge_body(alpha, x_ref, oa_ref, or_ref, wrg_ref, wmg_ref, wua_ref, wur_ref, wo_ref, g1_ref, b1_ref,
                wrh_ref, wrl_ref, rb_ref, x1_ref, eid_ref, wgt_ref):
    x = x_ref[...]
    xb = x.astype(BF16)
    rgate = jax.nn.silu(_nn(xb, wrg_ref[...]))
    o_ret = (or_ref[...] * rgate).astype(BF16)
    a = _nn(oa_ref[...], wua_ref[...])
    r = _nn(o_ret, wur_ref[...])
    mg = jax.nn.sigmoid(_nn(xb, wmg_ref[...]))
    merged = mg[:, :D_MODEL] * a + mg[:, D_MODEL:] * r
    mix = _nn(merged.astype(BF16), wo_ref[...])
    x1 = _layer_norm(alpha * x + mix, g1_ref[...], b1_ref[...])
    x1_ref[...] = x1

    xh = x1.astype(BF16)
    xl = (x1 - xh.astype(F32)).astype(BF16)
    wh = wrh_ref[...]
    lg = _nt(wh, xh) + _nt(wh, xl) + _nt(wrl_ref[...], xh) + rb_ref[...]
    ne = N_EXPERTS
    gl = lg[ne:ne + MOE_GROUPS]
    ge = jnp.exp(gl - jnp.max(gl, axis=0, keepdims=True))
    pg = ge / jnp.sum(ge, axis=0, keepdims=True)
    g_prob = jnp.max(pg, axis=0, keepdims=True)
    gi = lax.broadcasted_iota(I32, pg.shape, 0)
    g_idx = jnp.min(jnp.where(pg == g_prob, gi, MOE_GROUPS), axis=0, keepdims=True)
    inner = jnp.zeros((EXPERTS_PER_GROUP, lg.shape[1]), F32)
    for gq in range(MOE_GROUPS):
        inner = inner + jnp.where(g_idx == gq, lg[8 * gq:8 * (gq + 1)], 0.0)
    ei = lax.broadcasted_iota(I32, inner.shape, 0)
    m1 = jnp.max(inner, axis=0, keepdims=True)
    i1 = jnp.min(jnp.where(inner == m1, ei, EXPERTS_PER_GROUP), axis=0, keepdims=True)
    rest = jnp.where(ei == i1, -jnp.inf, inner)
    m2 = jnp.max(rest, axis=0, keepdims=True)
    i2 = jnp.min(jnp.where(rest == m2, ei, EXPERTS_PER_GROUP), axis=0, keepdims=True)
    e2 = jnp.exp(m2 - m1)
    den = 1.0 + e2
    w1 = (1.0 / den) * g_prob
    w2 = (e2 / den) * g_prob
    zi = jnp.zeros((6, lg.shape[1]), I32)
    eid_ref[...] = jnp.concatenate([g_idx * EXPERTS_PER_GROUP + i1, g_idx * EXPERTS_PER_GROUP + i2, zi], axis=0)
    wgt_ref[...] = jnp.concatenate([w1, w2, jnp.zeros((6, lg.shape[1]), F32)], axis=0)


def _merge(alpha, x2, oa2, or2, wrg, wmg, wua, wur, wo, g1, b1, wrh, wrl, rb):
    M, D = x2.shape
    tm = MERGE_TM
    full = lambda a: pl.BlockSpec(a.shape, lambda i: (0,) * a.ndim)
    in_specs = [pl.BlockSpec((tm, D), lambda i: (i, 0)),
                pl.BlockSpec((tm, oa2.shape[1]), lambda i: (i, 0)),
                pl.BlockSpec((tm, or2.shape[1]), lambda i: (i, 0))] + [
        full(a) for a in (wrg, wmg, wua, wur, wo, g1, b1, wrh, wrl, rb)]
    out_shape = [jax.ShapeDtypeStruct((M, D), F32),
                 jax.ShapeDtypeStruct((8, M), I32),
                 jax.ShapeDtypeStruct((8, M), F32)]
    out_specs = [pl.BlockSpec((tm, D), lambda i: (i, 0)),
                 pl.BlockSpec((8, tm), lambda i: (0, i)),
                 pl.BlockSpec((8, tm), lambda i: (0, i))]
    return pl.pallas_call(
        functools.partial(_merge_body, alpha), out_shape=out_shape, grid=(M // tm,),
        in_specs=in_specs, out_specs=out_specs,
        compiler_params=_cparams(("parallel",)), name="merge",
    )(x2, oa2, or2, wrg, wmg, wua, wur, wo, g1, b1, wrh, wrl, rb)


def _rank_body(eid_ref, tri_ref, low_ref, pos_ref, cnt_ref, carry_ref, off_ref):
    p = pl.program_id(0)
    j = pl.program_id(1)
    e = eid_ref[...]
    rows = lax.broadcasted_iota(I32, (N_EXPERTS, e.shape[1]), 0)
    hit = rows == e
    per_expert = jnp.sum(jnp.where(hit, 1.0, 0.0), axis=1, keepdims=True)

    @pl.when((p == 0) & (j == 0))
    def _():
        carry_ref[...] = jnp.zeros(carry_ref.shape, F32)

    @pl.when(p == 0)
    def _():
        pos_ref[...] = jnp.zeros(pos_ref.shape, I32)
        carry_ref[...] = carry_ref[...] + per_expert
        cnt_ref[...] = carry_ref[...]

    @pl.when((p == 1) & (j == 0))
    def _():
        tiles = jnp.floor((carry_ref[...] + (EXPERT_TM - 1)) * (1.0 / EXPERT_TM))
        off_ref[...] = _nn(low_ref[...], tiles.astype(BF16)) * EXPERT_TM
        carry_ref[...] = jnp.zeros(carry_ref.shape, F32)

    @pl.when(p == 1)
    def _():
        onehot = jnp.where(hit, 1.0, 0.0).astype(BF16)
        incl = _nn(onehot, tri_ref[...])
        base = carry_ref[:, 0:1] + off_ref[:, 0:1] - 1.0
        pos_ref[...] = jnp.sum(jnp.where(hit, incl + base, 0.0), axis=0, keepdims=True).astype(I32)
        carry_ref[...] = carry_ref[...] + per_expert


def _rank(eid_flat, tri, low):
    n = eid_flat.shape[1]
    tb = RANK_TB
    return pl.pallas_call(
        _rank_body,
        out_shape=[jax.ShapeDtypeStruct((1, n), I32), jax.ShapeDtypeStruct((N_EXPERTS, 128), F32)],
        grid=(2, n // tb),
        in_specs=[pl.BlockSpec((1, tb), lambda p, j: (0, j)), pl.BlockSpec((tb, tb), lambda p, j: (0, 0)),
                  pl.BlockSpec(low.shape, lambda p, j: (0, 0))],
        out_specs=[pl.BlockSpec((1, tb), lambda p, j: (0, j * p)),
                   pl.BlockSpec((N_EXPERTS, 128), lambda p, j: (0, 0))],
        scratch_shapes=[pltpu.VMEM((N_EXPERTS, 128), F32), pltpu.VMEM((N_EXPERTS, 128), F32)],
        compiler_params=_cparams(("arbitrary", "arbitrary")), name="rank",
    )(eid_flat, tri, low)


def _dispatch_body(pos_ref, ztile_ref, x_ref, xs_ref, zbuf_ref, sem, zsem):
    tm = DISPATCH_TM
    m_tok = pos_ref.shape[0] // 2
    base = pl.program_id(0) * tm

    @pl.when(pl.program_id(0) == 0)
    def _():
        zbuf_ref[...] = jnp.zeros(zbuf_ref.shape, F32)

        def zero_copy(e):
            z = pl.multiple_of(jnp.maximum(ztile_ref[e], 0), EXPERT_TM)
            return pltpu.make_async_copy(zbuf_ref, xs_ref.at[pl.ds(z, EXPERT_TM), :], zsem)

        for e in range(N_EXPERTS):
            @pl.when(ztile_ref[e] >= 0)
            def _():
                zero_copy(e).start()
        for e in range(N_EXPERTS):
            @pl.when(ztile_ref[e] >= 0)
            def _():
                zero_copy(e).wait()

        def tail_copy(t):
            return pltpu.make_async_copy(
                zbuf_ref, xs_ref.at[pl.ds(pl.multiple_of(t * EXPERT_TM, EXPERT_TM), EXPERT_TM), :], zsem)

        n_tiles = xs_ref.shape[0] // EXPERT_TM
        lax.fori_loop(ztile_ref[N_EXPERTS], n_tiles, lambda t, c: (tail_copy(t).start(), c)[1], 0)
        lax.fori_loop(ztile_ref[N_EXPERTS], n_tiles, lambda t, c: (tail_copy(t).wait(), c)[1], 0)

    def row_copy(r, p):
        return pltpu.make_async_copy(x_ref.at[pl.ds(r, 1), :], xs_ref.at[pl.ds(p, 1), :], sem)

    def issue(r, carry):
        row_copy(r, pos_ref[base + r]).start()
        row_copy(r, pos_ref[m_tok + base + r]).start(priority=1)
        return carry

    lax.fori_loop(0, tm, issue, 0, unroll=16)

    def drain(r, carry):
        row_copy(0, 0).wait()
        row_copy(0, 0).wait()
        return carry

    lax.fori_loop(0, tm, drain, 0, unroll=16)


def _dispatch(pos, ztile, x1, n_pad):
    M, D = x1.shape
    tm = DISPATCH_TM
    grid_spec = pltpu.PrefetchScalarGridSpec(
        num_scalar_prefetch=2, grid=(M // tm,),
        in_specs=[pl.BlockSpec((tm, D), lambda i, pos, zt: (i, 0))],
        out_specs=pl.BlockSpec(memory_space=pl.ANY),
        scratch_shapes=[pltpu.VMEM((EXPERT_TM, D), F32), pltpu.SemaphoreType.DMA, pltpu.SemaphoreType.DMA],
    )
    return pl.pallas_call(
        _dispatch_body, out_shape=jax.ShapeDtypeStruct((n_pad, D), F32), grid_spec=grid_spec,
        compiler_params=_cparams(("arbitrary",), has_side_effects=True, disable_bounds_checks=True),
        name="dispatch",
    )(pos, ztile, x1)


def _experts_body(te_ref, nu_ref, seg_ref, nxt_ref, xs_ref, wg_hbm, wu_hbm, wd_hbm, ys_ref,
                  wgb_ref, wub_ref, wdb_ref, sg_ref, su_ref, sd_ref, sem):
    i = pl.program_id(0)
    live = i < nu_ref[0]

    def fetch(e, slot):
        return (pltpu.make_async_copy(wg_hbm.at[e], sg_ref.at[slot], sem.at[0, slot]),
                pltpu.make_async_copy(wu_hbm.at[e], su_ref.at[slot], sem.at[1, slot]),
                pltpu.make_async_copy(wd_hbm.at[e], sd_ref.at[slot], sem.at[2, slot]))

    @pl.when(live & (i == 0))
    def _():
        for c in fetch(te_ref[0], 0):
            c.start()

    @pl.when(live & (seg_ref[i] >= 0))
    def _():
        slot = seg_ref[i]
        for c in fetch(te_ref[i], slot):
            c.wait()
        wgb_ref[...] = sg_ref[slot].astype(BF16)
        wub_ref[...] = su_ref[slot].astype(BF16)
        wdb_ref[...] = sd_ref[slot].astype(BF16)

        @pl.when(nxt_ref[i] >= 0)
        def _():
            for c in fetch(nxt_ref[i], 1 - slot):
                c.start()

    @pl.when(live)
    def _():
        xb = xs_ref[...].astype(BF16)
        hg = _nn(xb, wgb_ref[...])
        hu = _nn(xb, wub_ref[...])
        h = (jax.nn.silu(hg) * hu).astype(BF16)
        ys_ref[...] = _nn(h, wdb_ref[...])

    @pl.when(i >= nu_ref[0])
    def _():
        ys_ref[...] = jnp.zeros(ys_ref.shape, F32)


def _experts(tile_expert, n_used, seg_slot, next_expert, xs, wg, wu, wd):
    npad, D = xs.shape
    tm = EXPERT_TM
    grid_spec = pltpu.PrefetchScalarGridSpec(
        num_scalar_prefetch=4, grid=(npad // tm,),
        in_specs=[pl.BlockSpec((tm, D), lambda i, te, nu, sg, nx: (jnp.minimum(i, nu[0] - 1), 0)),
                  pl.BlockSpec(memory_space=pl.ANY), pl.BlockSpec(memory_space=pl.ANY),
                  pl.BlockSpec(memory_space=pl.ANY)],
        out_specs=pl.BlockSpec((tm, D), lambda i, te, nu, sg, nx: (i, 0)),
        scratch_shapes=[pltpu.VMEM((D, D_FF), BF16), pltpu.VMEM((D, D_FF), BF16), pltpu.VMEM((D_FF, D), BF16),
                        pltpu.VMEM((2, D, D_FF), F32), pltpu.VMEM((2, D, D_FF), F32),
                        pltpu.VMEM((2, D_FF, D), F32), pltpu.SemaphoreType.DMA((3, 2))],
    )
    return pl.pallas_call(
        _experts_body, out_shape=jax.ShapeDtypeStruct((npad, D), F32), grid_spec=grid_spec,
        compiler_params=_cparams(("arbitrary",)), name="experts",
    )(tile_expert, n_used, seg_slot, next_expert, xs, wg, wu, wd)


def _combine_body(alpha, pos_ref, x1_ref, w_ref, eye_ref, g2_ref, b2_ref, ys_ref, o_ref,
                  buf_a, buf_b, sem_a, sem_b):
    tm = COMBINE_TM
    m_tok = pos_ref.shape[0] // 2
    k = pl.program_id(0)
    n = pl.num_programs(0)

    def row_copies(buf, sem, r, base):
        return (pltpu.make_async_copy(ys_ref.at[pl.ds(pos_ref[base + r], 1), :], buf.at[0, pl.ds(r, 1), :], sem),
                pltpu.make_async_copy(ys_ref.at[pl.ds(pos_ref[m_tok + base + r], 1), :],
                                      buf.at[1, pl.ds(r, 1), :], sem))

    def issue_inline(buf, sem, tile):
        for r in range(tm):
            for queue, c in enumerate(row_copies(buf, sem, r, tile * tm)):
                c.start(priority=queue)

    def drain(buf, sem):
        def body(r, carry):
            for c in row_copies(buf, sem, 0, 0):
                c.wait()
            return carry
        lax.fori_loop(0, tm, body, 0, unroll=16)

    def compute(buf, s):
        rows = slice(s * tm, (s + 1) * tm)
        w = w_ref[:, rows]
        eye = eye_ref[...]
        w_a = w.astype(BF16)
        w_b = (w - w_a.astype(F32)).astype(BF16)
        w_c = (w - w_a.astype(F32) - w_b.astype(F32)).astype(BF16)
        wcol = _nt(eye, w_a) + _nt(eye, w_b) + _nt(eye, w_c)
        moe = buf[0] * wcol[:, 0:1] + buf[1] * wcol[:, 1:2]
        o_ref[rows, :] = _layer_norm(alpha * x1_ref[rows, :] + moe, g2_ref[...], b2_ref[...])

    @pl.when(k == 0)
    def _():
        def body(r, carry):
            for queue, c in enumerate(row_copies(buf_a, sem_a, r, 0)):
                c.start(priority=queue)
            return carry
        lax.fori_loop(0, tm, body, 0, unroll=16)

    drain(buf_a, sem_a)
    issue_inline(buf_b, sem_b, 2 * k + 1)
    compute(buf_a, 0)
    drain(buf_b, sem_b)
    issue_inline(buf_a, sem_a, jnp.minimum(2 * k + 2, 2 * n - 2))
    compute(buf_b, 1)

    @pl.when(k == n - 1)
    def _():
        drain(buf_a, sem_a)


def _combine(alpha, pos, x1, wgt, eye, g2, b2, ys):
    M, D = x1.shape
    tm = COMBINE_TM
    grid_spec = pltpu.PrefetchScalarGridSpec(
        num_scalar_prefetch=1, grid=(M // (2 * tm),),
        in_specs=[pl.BlockSpec((2 * tm, D), lambda i, pos: (i, 0)),
                  pl.BlockSpec((8, 2 * tm), lambda i, pos: (0, i)),
                  pl.BlockSpec((tm, tm), lambda i, pos: (0, 0)),
                  pl.BlockSpec((1, D), lambda i, pos: (0, 0)),
                  pl.BlockSpec((1, D), lambda i, pos: (0, 0)),
                  pl.BlockSpec(memory_space=pl.ANY)],
        out_specs=pl.BlockSpec((2 * tm, D), lambda i, pos: (i, 0)),
        scratch_shapes=[pltpu.VMEM((2, tm, D), F32), pltpu.VMEM((2, tm, D), F32),
                        pltpu.SemaphoreType.DMA, pltpu.SemaphoreType.DMA],
    )
    return pl.pallas_call(
        functools.partial(_combine_body, alpha), out_shape=jax.ShapeDtypeStruct((M, D), F32),
        grid_spec=grid_spec, compiler_params=_cparams(("arbitrary",), disable_bounds_checks=True),
        name="combine",
    )(pos, x1, wgt, eye, g2, b2, ys)


def _rope_angles(pos, dim):
    half = dim // 2
    inv_freq = ROPE_THETA ** (-np.arange(half, dtype=np.float64) * 2.0 / dim)
    return pos.astype(np.float64)[:, None] * inv_freq[None, :]


def _tables(T):
    f32 = lambda a: jnp.asarray(np.ascontiguousarray(a), dtype=F32)
    ang = _rope_angles(np.arange(T), NSA_HD)
    cos, sin = np.cos(ang), np.sin(ang)
    c64 = np.tile(cos, (1, 4))
    s64 = np.tile(np.concatenate([-sin, sin], axis=1), (1, 2))
    ang2 = _rope_angles(np.arange(T), RET_DK)
    cos2, sin2 = np.cos(ang2), np.sin(ang2)
    c128 = np.tile(cos2, (1, 2))
    s128 = np.concatenate([-sin2, sin2], axis=1)
    n_rows = T // CMP_STRIDE
    angc = _rope_angles(np.arange(n_rows) * CMP_STRIDE + CMP_BLOCK - 1, NSA_HD)
    cc = np.tile(np.cos(angc), (1, 2))
    sc = np.tile(np.sin(angc), (1, 2))
    return tuple(f32(a) for a in (c64, s64, c128, s128, cos.T, sin.T, cos2.T, sin2.T)), f32(cc), f32(sc)


def _overlap_matrix(T):
    n_rows = T // CMP_STRIDE
    n_sel = T // SLC_BLOCK
    cmp_start = np.arange(n_rows) * CMP_STRIDE
    sel_start = np.arange(n_sel) * SLC_BLOCK
    ov = np.clip(np.minimum(cmp_start[None, :] + CMP_BLOCK, sel_start[:, None] + SLC_BLOCK)
                 - np.maximum(cmp_start[None, :], sel_start[:, None]), 0, None)
    return jnp.asarray(ov.astype(np.float32) / CMP_STRIDE, dtype=BF16)


def _retention_tables():
    C = RET_CHUNK
    f32 = lambda a: jnp.asarray(np.ascontiguousarray(a), dtype=F32)
    gamma = 1.0 - 2.0 ** (-5.0 - np.arange(RET_HEADS, dtype=np.float64))
    log_g = np.log(gamma)
    i = np.arange(C, dtype=np.float64)
    diff = i[:, None] - i[None, :]
    dec = np.where(diff >= 0, np.exp(np.maximum(diff, 0.0) * log_g[:, None, None]), 0.0)
    xi = np.exp((i + 1.0) * log_g[:, None])
    zeta = np.exp((C - 1.0 - i) * log_g[:, None])
    cd = np.exp(C * log_g)
    xi_b = np.broadcast_to(xi[:, :, None], (RET_HEADS, C, RET_DV))
    cd_b = np.broadcast_to(cd[:, None, None], (RET_HEADS, 1, RET_DV))
    return f32(dec), f32(xi_b), f32(zeta[:, None, :]), f32(cd_b)


def _rot_half_cols(w):
    half = w.shape[-1] // 2
    return jnp.concatenate([-w[..., half:], w[..., :half]], axis=-1)


def kernel(x, w_in, cmp_pos_k, cmp_k_w1, cmp_k_b1, cmp_k_w2, cmp_k_b2, cmp_pos_v, cmp_v_w1, cmp_v_b1, cmp_v_w2, cmp_v_b2, ret_gn_g, ret_gn_b, w_up_attn, w_up_ret, w_out, ln1_g, ln1_b, router_group_w, router_group_b, router_inner_w, router_inner_b, expert_w_gate, expert_w_up, expert_w_down, ln2_g, ln2_b):
    B, T, D = x.shape
    M = B * T
    depth = w_in.shape[0]
    alpha = (2.0 * depth) ** 0.25
    tabs, cc, sc = _tables(T)
    ov = _overlap_matrix(T)
    dec, xi_b, zeta, cd_b = _retention_tables()
    tri = jnp.asarray(np.triu(np.ones((RANK_TB, RANK_TB), np.float32)), dtype=BF16)
    low = jnp.asarray(np.tril(np.ones((N_EXPERTS, N_EXPERTS), np.float32), k=-1), dtype=BF16)
    eye = jnp.asarray(np.eye(COMBINE_TM, dtype=np.float32), dtype=BF16)
    n_pad = 2 * M + N_EXPERTS * EXPERT_TM
    n_tiles = n_pad // EXPERT_TM

    for l in range(depth):
        w = w_in[l]
        col = lambda n: w[:, _OFF[n][0]:_OFF[n][1]]
        wn = jnp.concatenate([col("cmp_k"), col("cmp_v"), col("slc_k"), col("win_k"), col("ret_q"), col("ret_v")],
                             axis=1).astype(BF16)
        wt = jnp.concatenate([col("nsa_q"), col("slc_v"), col("win_v"), col("ret_k"), col("nsa_gate"),
                              jnp.zeros((D, 8), F32)], axis=1).T.astype(BF16)
        qT, ckv, sk, wk, svT, wvT, gT, rq, rkT, rv = _proj(x, wn, wt, tabs)

        kc, vcT = _compress(
            ckv, cmp_pos_k[l].reshape(2, -1), cmp_pos_v[l].reshape(2, -1),
            cmp_k_w1[l], cmp_k_b1[l][None, :], cmp_k_w2[l], _rot_half_cols(cmp_k_w2[l]),
            cmp_k_b2[l][None, :], _rot_half_cols(cmp_k_b2[l])[None, :], cc, sc,
            cmp_v_w1[l], cmp_v_b1[l][None, :], cmp_v_w2[l].T, cmp_v_b2[l][:, None])
        o_attn = _nsa(qT, kc, vcT, ov, sk, svT, wk, wvT, gT)
        o_ret = _retention(rq, rkT, rv, dec, xi_b, zeta, cd_b, ret_gn_g[l][None, :], ret_gn_b[l][None, :])

        wr = jnp.concatenate([router_inner_w[l].transpose(0, 2, 1).reshape(N_EXPERTS, D),
                              router_group_w[l].T, jnp.zeros((4, D), F32)], axis=0)
        wrh = wr.astype(BF16)
        wrl = (wr - wrh.astype(F32)).astype(BF16)
        rb = jnp.concatenate([router_inner_b[l].reshape(-1), router_group_b[l], jnp.zeros((4,), F32)])[:, None]
        x1, eid, wgt = _merge(
            alpha, x.reshape(M, D), o_attn.reshape(M, -1), o_ret.reshape(M, -1),
            col("ret_gate").astype(BF16), col("merge_gate").astype(BF16), w_up_attn[l].astype(BF16),
            w_up_ret[l].astype(BF16), w_out[l].astype(BF16), ln1_g[l][None, :], ln1_b[l][None, :], wrh, wrl, rb)

        eid_flat = eid[:2].reshape(1, 2 * M)
        pos, cnt = _rank(eid_flat, tri, low)
        pos = pos[0]
        counts = cnt[:, 0].astype(I32)
        tiles_per = (counts + EXPERT_TM - 1) // EXPERT_TM
        tile_end = jnp.cumsum(tiles_per)
        tile_ids = jnp.arange(n_tiles, dtype=I32)
        tile_expert = jnp.minimum(jnp.sum((tile_end[None, :] <= tile_ids[:, None]).astype(I32), axis=1),
                                  N_EXPERTS - 1).astype(I32)
        n_used = tile_end[-1:].astype(I32)

        ztile = jnp.concatenate([jnp.where(tiles_per > 0, (tile_end - 1) * EXPERT_TM, -1), tile_end[-1:]]).astype(I32)
        xs = _dispatch(pos, ztile, x1, n_pad)
        prev_expert = jnp.concatenate([jnp.full((1,), -1, I32), tile_expert[:-1]])
        is_first = (tile_ids < n_used[0]) & (tile_expert != prev_expert)
        seg_slot = jnp.where(is_first, (jnp.cumsum(is_first.astype(I32)) - 1) % 2, -1).astype(I32)
        own = tile_expert[:, None] == jnp.arange(N_EXPERTS, dtype=I32)[None, :]
        seg_end = jnp.sum(jnp.where(own, tile_end[None, :], 0), axis=1)
        expert_at_end = jnp.minimum(jnp.sum((tile_end[None, :] <= seg_end[:, None]).astype(I32), axis=1),
                                    N_EXPERTS - 1)
        next_expert = jnp.where(is_first & (seg_end < n_used[0]), expert_at_end, -1).astype(I32)
        ys = _experts(tile_expert, n_used, seg_slot, next_expert, xs,
                      expert_w_gate[l], expert_w_up[l], expert_w_down[l])
        x = _combine(alpha, pos, x1, wgt, eye, ln2_g[l][None, :], ln2_b[l][None, :], ys).reshape(B, T, D)
    return x
```

```python
import functools

import numpy as np
import jax
import jax.numpy as jnp
from jax import lax
from jax.experimental import pallas as pl
from jax.experimental.pallas import tpu as pltpu

F32 = jnp.float32
BF16 = jnp.bfloat16
I32 = jnp.int32

D_MODEL = 1024
NSA_HEADS = 8
NSA_HD = 64
NSA_GROUPS = 2
NSA_HPG = NSA_HEADS // NSA_GROUPS
CMP_BLOCK = 32
CMP_STRIDE = 16
CMP_HIDDEN = 256
SLC_BLOCK = 64
SLC_TOPK = 16
WINDOW = 512
RET_HEADS = 4
RET_DK = 128
RET_DV = 256
RET_CHUNK = 128
MOE_GROUPS = 4
EXPERTS_PER_GROUP = 8
N_EXPERTS = MOE_GROUPS * EXPERTS_PER_GROUP
D_FF = 512
ROPE_THETA = 10000.0
LN_EPS = 1e-5
GN_EPS = 1e-5
NEG_INF = -1e30
LOG2_E = 1.4426950408889634

VMEM_LIMIT_V7X = 56 * 1024 * 1024

_OFF = {}
_o = 0
for _n, _w in (("nsa_q", 512), ("cmp_k", 128), ("cmp_v", 128), ("slc_k", 128), ("slc_v", 128),
               ("win_k", 128), ("win_v", 128), ("nsa_gate", 24), ("ret_q", 512), ("ret_k", 512),
               ("ret_v", 1024), ("ret_gate", 1024), ("merge_gate", 2048)):
    _OFF[_n] = (_o, _o + _w)
    _o += _w

PROJ_TM = 512
NSA_TQ = 256
NSA_KC = 256
NSA_VROWS = 80
RET_STEP_CHUNKS = 4
RET_STEP_BATCH = 2
MERGE_TM = 512
RANK_TB = 1024
DISPATCH_TM = 512
EXPERT_TM = 256
COMBINE_TM = 256


def _cparams(sem, **kw):
    return pltpu.CompilerParams(dimension_semantics=sem, vmem_limit_bytes=VMEM_LIMIT_V7X, **kw)


def _nt(a, b):
    return lax.dot_general(a, b, (((1,), (1,)), ((), ())), preferred_element_type=F32)


def _nn(a, b):
    return jnp.dot(a, b, preferred_element_type=F32)


def _proj_body(x_ref, wn_ref, wt_ref, c64_ref, s64_ref, c128_ref, s128_ref,
               ct64_ref, st64_ref, ct128_ref, st128_ref,
               qT_ref, ckv_ref, sk_ref, wk_ref, svT_ref, wvT_ref, gT_ref, rq_ref, rkT_ref, rv_ref, ckv_scr):
    tm = PROJ_TM
    xb = x_ref[0].astype(BF16)

    def nn(a, b):
        return _nn(xb, wn_ref[:, a:b])

    def nt(a, b):
        return _nt(wt_ref[a:b, :], xb)

    ckv = nn(0, 256)
    for c in range(2):
        ckv_scr[c] = ckv[:, 128 * c:128 * (c + 1)]
    for l in range(CMP_STRIDE):
        for c in range(2):
            rows = ckv_scr[c, pl.ds(l, tm // CMP_STRIDE, stride=CMP_STRIDE), :]
            for jj in range(2):
                ckv_ref[0, 2 * c + jj, :, 64 * l:64 * (l + 1)] = rows[:, 64 * jj:64 * (jj + 1)]

    lane = lax.broadcasted_iota(I32, (tm, 128), 1)
    first = (lane % 64) < 32
    c64 = c64_ref[...]
    s64 = s64_ref[...]

    def rope64(k):
        rot = jnp.where(first, pltpu.roll(k, 96, 1), pltpu.roll(k, 32, 1))
        return k * c64 + rot * s64

    sk = rope64(nn(256, 384)).astype(BF16)
    sk_ref[0, 0] = sk[:, :64]
    sk_ref[0, 1] = sk[:, 64:]
    wk = rope64(nn(384, 512)).astype(BF16)
    wk_ref[0, 0] = wk[:, :64]
    wk_ref[0, 1] = wk[:, 64:]

    c128 = c128_ref[...]
    s128 = s128_ref[...]
    rq = nn(512, 1024)
    for h in range(RET_HEADS):
        ch = rq[:, 128 * h:128 * (h + 1)]
        rq_ref[0, :, 128 * h:128 * (h + 1)] = (ch * c128 + pltpu.roll(ch, 64, 1) * s128).astype(BF16)
    rv_ref[0] = nn(1024, 2048).astype(BF16)

    ct = ct64_ref[...]
    st = st64_ref[...]
    qT = nt(0, 512)
    scale_q = NSA_HD ** -0.5 * LOG2_E
    tq = NSA_TQ
    for hh in range(NSA_HEADS):
        g, h = divmod(hh, NSA_HPG)
        x1 = qT[64 * hh:64 * hh + 32]
        x2 = qT[64 * hh + 32:64 * hh + 64]
        o1 = ((x1 * ct - x2 * st) * scale_q).astype(BF16)
        o2 = ((x1 * st + x2 * ct) * scale_q).astype(BF16)
        for it in range(tm // tq):
            qT_ref[0, g, it, 0:32, h * tq:(h + 1) * tq] = o1[:, it * tq:(it + 1) * tq]
            qT_ref[0, g, it, 32:64, h * tq:(h + 1) * tq] = o2[:, it * tq:(it + 1) * tq]

    svT = nt(512, 640).astype(BF16)
    wvT = nt(640, 768).astype(BF16)
    row16 = lax.broadcasted_iota(I32, (NSA_VROWS - NSA_HD, NSA_KC), 0)
    ones_blk = jnp.where(row16 == 0, 1.0, 0.0).astype(BF16)
    for c in range(tm // NSA_KC):
        for g in range(NSA_GROUPS):
            for vT, ref in ((svT, svT_ref), (wvT, wvT_ref)):
                ref[0, c, g, 0:NSA_HD, :] = vT[64 * g:64 * (g + 1), NSA_KC * c:NSA_KC * (c + 1)]
                ref[0, c, g, NSA_HD:NSA_VROWS, :] = ones_blk

    ct2 = ct128_ref[...]
    st2 = st128_ref[...]
    rkT = nt(768, 1280)
    scale_k = RET_DK ** -0.5
    for h in range(RET_HEADS):
        x1 = rkT[128 * h:128 * h + 64]
        x2 = rkT[128 * h + 64:128 * h + 128]
        o1 = (x1 * ct2 - x2 * st2) * scale_k
        o2 = (x1 * st2 + x2 * ct2) * scale_k
        for c in range(tm // RET_CHUNK):
            rkT_ref[0, c, 128 * h:128 * h + 64, :] = o1[:, 128 * c:128 * (c + 1)]
            rkT_ref[0, c, 128 * h + 64:128 * h + 128, :] = o2[:, 128 * c:128 * (c + 1)]

    gT_ref[0] = jax.nn.sigmoid(nt(1280, 1312))


def _proj(x, wn, wt, tabs):
    B, T, D = x.shape
    tm = PROJ_TM
    c64, s64, c128, s128, ct64, st64, ct128, st128 = tabs
    const = lambda b, i: (0, 0)
    in_specs = [
        pl.BlockSpec((1, tm, D), lambda b, i: (b, i, 0)),
        pl.BlockSpec(wn.shape, const),
        pl.BlockSpec(wt.shape, const),
        pl.BlockSpec((tm, 128), lambda b, i: (i, 0)),
        pl.BlockSpec((tm, 128), lambda b, i: (i, 0)),
        pl.BlockSpec((tm, 128), lambda b, i: (i, 0)),
        pl.BlockSpec((tm, 128), lambda b, i: (i, 0)),
        pl.BlockSpec((32, tm), lambda b, i: (0, i)),
        pl.BlockSpec((32, tm), lambda b, i: (0, i)),
        pl.BlockSpec((64, tm), lambda b, i: (0, i)),
        pl.BlockSpec((64, tm), lambda b, i: (0, i)),
    ]
    out_shape = [
        jax.ShapeDtypeStruct((B, NSA_GROUPS, T // NSA_TQ, NSA_HD, NSA_HPG * NSA_TQ), BF16),
        jax.ShapeDtypeStruct((B, 4, T // CMP_STRIDE, CMP_STRIDE * NSA_HD), F32),
        jax.ShapeDtypeStruct((B, 2, T, 64), BF16),
        jax.ShapeDtypeStruct((B, 2, T, 64), BF16),
        jax.ShapeDtypeStruct((B, T // NSA_KC, NSA_GROUPS, NSA_VROWS, NSA_KC), BF16),
        jax.ShapeDtypeStruct((B, T // NSA_KC, NSA_GROUPS, NSA_VROWS, NSA_KC), BF16),
        jax.ShapeDtypeStruct((B, 32, T), F32),
        jax.ShapeDtypeStruct((B, T, 512), BF16),
        jax.ShapeDtypeStruct((B, T // RET_CHUNK, 512, RET_CHUNK), F32),
        jax.ShapeDtypeStruct((B, T, 1024), BF16),
    ]
    out_specs = [
        pl.BlockSpec((1, NSA_GROUPS, tm // NSA_TQ, NSA_HD, NSA_HPG * NSA_TQ), lambda b, i: (b, 0, i, 0, 0)),
        pl.BlockSpec((1, 4, tm // CMP_STRIDE, CMP_STRIDE * NSA_HD), lambda b, i: (b, 0, i, 0)),
        pl.BlockSpec((1, 2, tm, 64), lambda b, i: (b, 0, i, 0)),
        pl.BlockSpec((1, 2, tm, 64), lambda b, i: (b, 0, i, 0)),
        pl.BlockSpec((1, tm // NSA_KC, NSA_GROUPS, NSA_VROWS, NSA_KC), lambda b, i: (b, i, 0, 0, 0)),
        pl.BlockSpec((1, tm // NSA_KC, NSA_GROUPS, NSA_VROWS, NSA_KC), lambda b, i: (b, i, 0, 0, 0)),
        pl.BlockSpec((1, 32, tm), lambda b, i: (b, 0, i)),
        pl.BlockSpec((1, tm, 512), lambda b, i: (b, i, 0)),
        pl.BlockSpec((1, tm // RET_CHUNK, 512, RET_CHUNK), lambda b, i: (b, i, 0, 0)),
        pl.BlockSpec((1, tm, 1024), lambda b, i: (b, i, 0)),
    ]
    return pl.pallas_call(
        _proj_body, out_shape=out_shape, grid=(B, T // tm), in_specs=in_specs, out_specs=out_specs,
        scratch_shapes=[pltpu.VMEM((2, tm, 128), F32)],
        compiler_params=_cparams(("parallel", "parallel")), name="proj",
    )(x, wn, wt, c64, s64, c128, s128, ct64, st64, ct128, st128)


def _compress_body(uk_ref, uv_ref, posk_ref, posv_ref, w1k_ref, b1k_ref, w2k_ref, w2kr_ref, b2k_ref, b2kr_ref,
                   cc_ref, sc_ref, w1v_ref, b1v_ref, w2vT_ref, b2vT_ref, kc_ref, vcT_ref):
    half = CMP_STRIDE * NSA_HD

    def hidden(u_ref, pos_ref, w1_ref, b1_ref):
        u = u_ref[0, 0]
        top = (u + pos_ref[0:1, :]).astype(BF16)
        bot = (u + pos_ref[1:2, :]).astype(BF16)
        a = _nn(top, w1_ref[0:half, :].astype(BF16))
        bm = _nn(bot, w1_ref[half:2 * half, :].astype(BF16))
        n = bm.shape[0]
        return jax.nn.gelu(a + pltpu.roll(bm, n - 1, 0) + b1_ref[...])

    hk = hidden(uk_ref, posk_ref, w1k_ref, b1k_ref).astype(BF16)
    k = _nn(hk, w2k_ref[...].astype(BF16)) + b2k_ref[...]
    kr = _nn(hk, w2kr_ref[...].astype(BF16)) + b2kr_ref[...]
    kc_ref[0, 0] = (k * cc_ref[...] + kr * sc_ref[...]).astype(BF16)

    hv = hidden(uv_ref, posv_ref, w1v_ref, b1v_ref).astype(BF16)
    vcT_ref[0, 0] = (_nt(w2vT_ref[...].astype(BF16), hv) + b2vT_ref[...]).astype(BF16)


def _compress(ckv, posk, posv, w1k, b1k, w2k, w2kr, b2k, b2kr, cc, sc, w1v, b1v, w2vT, b2vT):
    B, _, nr, _ = ckv.shape
    u = ckv
    full = lambda a: pl.BlockSpec(a.shape, lambda b, g: (0,) * a.ndim)
    in_specs = [
        pl.BlockSpec((1, 1, nr, 1024), lambda b, g: (b, g, 0, 0)),
        pl.BlockSpec((1, 1, nr, 1024), lambda b, g: (b, g + 2, 0, 0)),
    ] + [full(a) for a in (posk, posv, w1k, b1k, w2k, w2kr, b2k, b2kr, cc, sc, w1v, b1v, w2vT, b2vT)]
    out_shape = [jax.ShapeDtypeStruct((B, NSA_GROUPS, nr, NSA_HD), BF16),
                 jax.ShapeDtypeStruct((B, NSA_GROUPS, NSA_HD, nr), BF16)]
    out_specs = [pl.BlockSpec((1, 1, nr, NSA_HD), lambda b, g: (b, g, 0, 0)),
                 pl.BlockSpec((1, 1, NSA_HD, nr), lambda b, g: (b, g, 0, 0))]
    return pl.pallas_call(
        _compress_body, out_shape=out_shape, grid=(B, NSA_GROUPS), in_specs=in_specs, out_specs=out_specs,
        compiler_params=_cparams(("parallel", "parallel")), name="compress",
    )(u, u, posk, posv, w1k, b1k, w2k, w2kr, b2k, b2kr, cc, sc, w1v, b1v, w2vT, b2vT)


def _nsa_body(q_ref, kc_ref, vcT_ref, ov_ref, sk_ref, svT_ref, wk_ref, wvT_ref, gT_ref, lo_ref, hi_ref, o_ref,
              score_ref, bias_ref, acc_ref, tot_ref, s_ref):
    tq = NSA_TQ
    kc_n = NSA_KC
    hq = NSA_HPG * tq
    groups = range(NSA_GROUPS)
    i = pl.program_id(1)
    t0 = i * tq
    t_row = t0 + lax.broadcasted_iota(I32, (1, tq), 1)
    t_row4 = t0 + lax.broadcasted_iota(I32, (1, hq), 1) % tq
    n_cmp = kc_ref.shape[2]
    n_sel = ov_ref.shape[0]
    q = [q_ref[0, g, 0] for g in groups]

    def gate4(g, br):
        return jnp.concatenate([gT_ref[0, (g * NSA_HPG + h) * 3 + br:(g * NSA_HPG + h) * 3 + br + 1, :]
                                for h in range(NSA_HPG)], axis=1)

    cmp_end = lax.broadcasted_iota(I32, (n_cmp, 1), 0) * CMP_STRIDE + (CMP_BLOCK - 1)
    cbias = jnp.where(cmp_end <= t_row4, 0.0, NEG_INF)
    any_valid = jnp.where(t_row4 >= CMP_BLOCK - 1, 1.0, 0.0)
    blk_t = t_row // SLC_BLOCK
    sub = lax.broadcasted_iota(I32, (8, 1), 0)

    def select_blocks(g, l_cmp, l_sel):
        s = _nn(kc_ref[0, g, 0:l_cmp, :], q[g]) + cbias[0:l_cmp]
        e = jnp.exp2(s - jnp.max(s, axis=0, keepdims=True))
        p = e * (any_valid / jnp.sum(e, axis=0, keepdims=True))
        tot_ref[g] = gate4(g, 0) * _nn(vcT_ref[0, g, :, 0:l_cmp], p.astype(BF16))
        psum = p[:, 0:tq]
        for h in range(1, NSA_HPG):
            psum = psum + p[:, h * tq:(h + 1) * tq]
        p_hi = psum.astype(BF16)
        p_lo = (psum - p_hi.astype(F32)).astype(BF16)
        ov = ov_ref[0:l_sel, 0:l_cmp]
        imp = _nn(ov, p_hi) + _nn(ov, p_lo)
        jc = lax.broadcasted_iota(I32, (l_sel, 1), 0)
        forced = (jc == 0) | (jc == blk_t) | (jc == blk_t - 1)
        score = jnp.where(forced, 1e9, jnp.where(jc <= blk_t, imp, -1e9)).astype(F32)
        score_ref[g, 0:l_sel, :] = score
        n_slab = l_sel // 8
        slabs = [score[8 * v:8 * (v + 1)] for v in range(n_slab)]
        cnt = [jnp.zeros((8, tq), F32) for _ in range(n_slab)]
        for r in range(l_sel):
            row = jnp.broadcast_to(score_ref[g, r:r + 1, :], (8, tq))
            for v in range(n_slab):
                if r < 8 * v:
                    ahead = jnp.where(row >= slabs[v], 1.0, 0.0)
                elif r >= 8 * (v + 1):
                    ahead = jnp.where(row > slabs[v], 1.0, 0.0)
                else:
                    ahead = jnp.where(sub + 8 * v > r, jnp.where(row >= slabs[v], 1.0, 0.0),
                                      jnp.where(row > slabs[v], 1.0, 0.0))
                cnt[v] = cnt[v] + ahead
        for v in range(n_slab):
            bias_ref[g, 8 * v:8 * (v + 1), :] = jnp.where(
                cnt[v] < float(SLC_TOPK), jnp.where(sub + 8 * v <= blk_t, 0.0, NEG_INF), NEG_INF)
        if l_sel < n_sel:
            bias_ref[g, l_sel:n_sel, :] = jnp.full((n_sel - l_sel, tq), NEG_INF, F32)

    n_var = 4
    tiles_per_var = (n_sel * SLC_BLOCK // tq) // n_var
    for var in range(n_var):
        l_sel = (var + 1) * tiles_per_var * (tq // SLC_BLOCK)
        l_cmp = min(n_cmp, -(-((var + 1) * tiles_per_var * (tq // CMP_STRIDE)) // 128) * 128)

        @pl.when(i // tiles_per_var == var)
        def _():
            for g in groups:
                select_blocks(g, l_cmp, l_sel)

    def qk(k_ref, g, c):
        k_c = k_ref[0, g, pl.ds(pl.multiple_of(c * kc_n, kc_n), kc_n), :]
        return [_nn(k_c, q[g][:, h * tq:(h + 1) * tq]) for h in range(NSA_HPG)]

    def stage_a(scores, add_bias, m):
        biased = [add_bias(scores[h]) for h in range(NSA_HPG)]
        m_new = tuple(jnp.maximum(m[h], jnp.max(biased[h], axis=0, keepdims=True)) for h in range(NSA_HPG))
        return biased, m_new

    def stage_b(g, biased, vT_c, m_old, m_new):
        for h in range(NSA_HPG):
            hs = slice(h * tq, (h + 1) * tq)
            alpha = jnp.exp2(m_old[h] - m_new[h])
            p = jnp.exp2((biased[h] - m_new[h]).astype(BF16))
            acc_ref[g, :, hs] = alpha * acc_ref[g, :, hs] + _nn(vT_c, p)

    nb = kc_n // SLC_BLOCK

    def block_bias(g, c, s):
        return jnp.concatenate([s[SLC_BLOCK * b:SLC_BLOCK * (b + 1)] + bias_ref[g, pl.ds(c * nb + b, 1), :]
                                for b in range(nb)], axis=0)

    def finish(g, br):
        acc = acc_ref[g]
        tot_ref[g] = tot_ref[g] + gate4(g, br) * (acc[0:NSA_HD] / acc[NSA_HD:NSA_HD + 1])
        acc_ref[g] = jnp.zeros(acc.shape, F32)

    def park(g, biased):
        for h in range(NSA_HPG):
            s_ref[g, h] = biased[h]

    def parked(g):
        return [s_ref[g, h] for h in range(NSA_HPG)]

    m0 = tuple(jnp.full((1, tq), NEG_INF, F32) for _ in range(NSA_HPG))

    acc_ref[...] = jnp.zeros(acc_ref.shape, F32)
    m_new = []
    for g in groups:
        biased, m_g = stage_a(qk(sk_ref, g, i), lambda s, g=g: block_bias(g, i, s) + lo_ref[...], m0)
        park(g, biased)
        m_new.append(m_g)

    def slc_chunk(c, carry):
        prev, m_old, m_new = carry
        scores = [qk(sk_ref, g, c) for g in groups]
        for g in groups:
            stage_b(g, parked(g), svT_ref[0, prev, g], m_old[g], m_new[g])
        m_next = []
        for g in groups:
            biased, m_g = stage_a(scores[g], functools.partial(block_bias, g, c), m_new[g])
            park(g, biased)
            m_next.append(m_g)
        return c, m_new, tuple(m_next)

    prev, m_old, m_new = lax.fori_loop(0, i, slc_chunk, (i, (m0,) * NSA_GROUPS, tuple(m_new)))

    c_far = jnp.maximum(i - 2, 0)
    c_near = jnp.maximum(i - 1, 0)
    pen_far = jnp.where(i >= 2, 0.0, NEG_INF).astype(F32)
    pen_near = jnp.where(i >= 1, 0.0, NEG_INF).astype(F32)

    scores = [qk(wk_ref, g, c_far) for g in groups]
    far = []
    for g in groups:
        stage_b(g, parked(g), svT_ref[0, prev, g], m_old[g], m_new[g])
        far.append(stage_a(scores[g], lambda s: s + (hi_ref[...] + pen_far), m0))
        finish(g, 1)

    scores = [qk(wk_ref, g, c_near) for g in groups]
    near = []
    for g in groups:
        stage_b(g, far[g][0], wvT_ref[0, c_far, g], m0, far[g][1])
        near.append(stage_a(scores[g], lambda s: s + pen_near, far[g][1]))
    scores = [qk(wk_ref, g, i) for g in groups]
    for g in groups:
        stage_b(g, near[g][0], wvT_ref[0, c_near, g], far[g][1], near[g][1])
        b_diag, m_diag = stage_a(scores[g], lambda s: s + lo_ref[...], near[g][1])
        stage_b(g, b_diag, wvT_ref[0, i, g], near[g][1], m_diag)
        finish(g, 2)

    o_ref[0] = jnp.concatenate([tot_ref[g, :, h * tq:(h + 1) * tq] for g in groups for h in range(NSA_HPG)],
                               axis=0).T.astype(BF16)


def _nsa(qT, kc, vcT, ov, sk, svT, wk, wvT, gT):
    B, _, T = gT.shape
    tq = NSA_TQ
    assert NSA_KC == tq and WINDOW == 2 * NSA_KC
    nr = kc.shape[2]
    nch = T // NSA_KC
    kk = np.arange(NSA_KC)[:, None]
    tt = np.arange(tq)[None, :]
    lo = jnp.asarray(np.where(kk <= tt, 0.0, NEG_INF), dtype=F32)
    hi = jnp.asarray(np.where(kk > tt, 0.0, NEG_INF), dtype=F32)
    G = NSA_GROUPS
    in_specs = [
        pl.BlockSpec((1, G, 1, NSA_HD, NSA_HPG * tq), lambda b, i: (b, 0, i, 0, 0)),
        pl.BlockSpec((1, G, nr, NSA_HD), lambda b, i: (b, 0, 0, 0)),
        pl.BlockSpec((1, G, NSA_HD, nr), lambda b, i: (b, 0, 0, 0)),
        pl.BlockSpec(ov.shape, lambda b, i: (0, 0)),
        pl.BlockSpec((1, G, T, NSA_HD), lambda b, i: (b, 0, 0, 0)),
        pl.BlockSpec((1, nch, G, NSA_VROWS, NSA_KC), lambda b, i: (b, 0, 0, 0, 0)),
        pl.BlockSpec((1, G, T, NSA_HD), lambda b, i: (b, 0, 0, 0)),
        pl.BlockSpec((1, nch, G, NSA_VROWS, NSA_KC), lambda b, i: (b, 0, 0, 0, 0)),
        pl.BlockSpec((1, 32, tq), lambda b, i: (b, 0, i)),
        pl.BlockSpec(lo.shape, lambda b, i: (0, 0)),
        pl.BlockSpec(hi.shape, lambda b, i: (0, 0)),
    ]
    n_sel = T // SLC_BLOCK
    hq = NSA_HPG * tq
    return pl.pallas_call(
        _nsa_body, out_shape=jax.ShapeDtypeStruct((B, T, NSA_HEADS * NSA_HD), BF16),
        grid=(B, T // tq), in_specs=in_specs,
        out_specs=pl.BlockSpec((1, tq, NSA_HEADS * NSA_HD), lambda b, i: (b, i, 0)),
        scratch_shapes=[pltpu.VMEM((G, n_sel, tq), F32), pltpu.VMEM((G, n_sel, tq), F32),
                        pltpu.VMEM((G, NSA_VROWS, hq), F32), pltpu.VMEM((G, NSA_HD, hq), F32),
                        pltpu.VMEM((G, NSA_HPG, NSA_KC, tq), F32)],
        compiler_params=_cparams(("parallel", "parallel")), name="nsa",
    )(qT, kc, vcT, ov, sk, svT, wk, wvT, gT, lo, hi)


def _ret_body(q_ref, kT_ref, v_ref, dec_ref, xi_ref, zeta_ref, cd_ref, gg_ref, gb_ref, o_ref, r_ref):
    C = RET_CHUNK

    @pl.when(pl.program_id(1) == 0)
    def _():
        r_ref[...] = jnp.zeros(r_ref.shape, F32)

    for bb in range(q_ref.shape[0]):
        for h in range(RET_HEADS):
            dk = slice(h * RET_DK, (h + 1) * RET_DK)
            dv = slice(h * RET_DV, (h + 1) * RET_DV)
            dec = dec_ref[h]
            xi = xi_ref[h]
            zeta = zeta_ref[h]
            cd = cd_ref[h]
            gg = gg_ref[:, dv]
            gb = gb_ref[:, dv]
            r = r_ref[bb, h]
            for n in range(RET_STEP_CHUNKS):
                rows = slice(n * C, (n + 1) * C)
                qc = q_ref[bb, rows, dk]
                kT = kT_ref[bb, n, dk, :]
                vc = v_ref[bb, rows, dv]
                s = _nn(qc, kT.astype(BF16)) * dec
                o = _nn(s.astype(BF16), vc) + _nn(qc, r.astype(BF16)) * xi
                r = r * cd + _nn((kT * zeta).astype(BF16), vc)
                mu = jnp.mean(o, axis=-1, keepdims=True)
                var = jnp.mean(jnp.square(o - mu), axis=-1, keepdims=True)
                o_ref[bb, rows, dv] = (o - mu) * lax.rsqrt(var + GN_EPS) * gg + gb
            r_ref[bb, h] = r


def _retention(rq, rkT, rv, dec, xi, zeta, cd, gn_g, gn_b):
    B, T, _ = rq.shape
    ts = RET_STEP_CHUNKS * RET_CHUNK
    full = lambda a: pl.BlockSpec(a.shape, lambda b, j: (0,) * a.ndim)
    nb = RET_STEP_BATCH if B % RET_STEP_BATCH == 0 else 1
    in_specs = [
        pl.BlockSpec((nb, ts, RET_HEADS * RET_DK), lambda b, j: (b, j, 0)),
        pl.BlockSpec((nb, RET_STEP_CHUNKS, RET_HEADS * RET_DK, RET_CHUNK), lambda b, j: (b, j, 0, 0)),
        pl.BlockSpec((nb, ts, RET_HEADS * RET_DV), lambda b, j: (b, j, 0)),
    ] + [full(a) for a in (dec, xi, zeta, cd, gn_g, gn_b)]
    return pl.pallas_call(
        _ret_body, out_shape=jax.ShapeDtypeStruct((B, T, RET_HEADS * RET_DV), F32),
        grid=(B // nb, T // ts), in_specs=in_specs,
        out_specs=pl.BlockSpec((nb, ts, RET_HEADS * RET_DV), lambda b, j: (b, j, 0)),
        scratch_shapes=[pltpu.VMEM((nb, RET_HEADS, RET_DK, RET_DV), F32)],
        compiler_params=_cparams(("parallel", "arbitrary")), name="retention",
    )(rq, rkT, rv, dec, xi, zeta, cd, gn_g, gn_b)


def _layer_norm(y, g, b):
    mu = jnp.mean(y, axis=-1, keepdims=True)
    var = jnp.mean(jnp.square(y - mu), axis=-1, keepdims=True)
    return (y - mu) * lax.rsqrt(var + LN_EPS) * g + b


def _merge_body(alpha, x_ref, oa_ref, or_ref, wrg_ref, wmg_ref, wua_ref, wur_ref, wo_ref, g1_ref, b1_ref,
                wrh_ref, wrl_ref, rb_ref, x1_ref, eid_ref, wgt_ref):
    x = x_ref[...]
    xb = x.astype(BF16)
    rgate = jax.nn.silu(_nn(xb, wrg_ref[...]))
    o_ret = (or_ref[...] * rgate).astype(BF16)
    a = _nn(oa_ref[...], wua_ref[...])
    r = _nn(o_ret, wur_ref[...])
    mg = jax.nn.sigmoid(_nn(xb, wmg_ref[...]))
    merged = mg[:, :D_MODEL] * a + mg[:, D_MODEL:] * r
    mix = _nn(merged.astype(BF16), wo_ref[...])
    x1 = _layer_norm(alpha * x + mix, g1_ref[...], b1_ref[...])
    x1_ref[...] = x1

    xh = x1.astype(BF16)
    xl = (x1 - xh.astype(F32)).astype(BF16)
    wh = wrh_ref[...]
    lg = _nt(wh, xh) + _nt(wh, xl) + _nt(wrl_ref[...], xh) + rb_ref[...]
    ne = N_EXPERTS
    gl = lg[ne:ne + MOE_GROUPS]
    ge = jnp.exp(gl - jnp.max(gl, axis=0, keepdims=True))
    pg = ge / jnp.sum(ge, axis=0, keepdims=True)
    g_prob = jnp.max(pg, axis=0, keepdims=True)
    gi = lax.broadcasted_iota(I32, pg.shape, 0)
    g_idx = jnp.min(jnp.where(pg == g_prob, gi, MOE_GROUPS), axis=0, keepdims=True)
    inner = jnp.zeros((EXPERTS_PER_GROUP, lg.shape[1]), F32)
    for gq in range(MOE_GROUPS):
        inner = inner + jnp.where(g_idx == gq, lg[8 * gq:8 * (gq + 1)], 0.0)
    ei = lax.broadcasted_iota(I32, inner.shape, 0)
    m1 = jnp.max(inner, axis=0, keepdims=True)
    i1 = jnp.min(jnp.where(inner == m1, ei, EXPERTS_PER_GROUP), axis=0, keepdims=True)
    rest = jnp.where(ei == i1, -jnp.inf, inner)
    m2 = jnp.max(rest, axis=0, keepdims=True)
    i2 = jnp.min(jnp.where(rest == m2, ei, EXPERTS_PER_GROUP), axis=0, keepdims=True)
    e2 = jnp.exp(m2 - m1)
    den = 1.0 + e2
    w1 = (1.0 / den) * g_prob
    w2 = (e2 / den) * g_prob
    zi = jnp.zeros((6, lg.shape[1]), I32)
    eid_ref[...] = jnp.concatenate([g_idx * EXPERTS_PER_GROUP + i1, g_idx * EXPERTS_PER_GROUP + i2, zi], axis=0)
    wgt_ref[...] = jnp.concatenate([w1, w2, jnp.zeros((6, lg.shape[1]), F32)], axis=0)


def _merge(alpha, x2, oa2, or2, wrg, wmg, wua, wur, wo, g1, b1, wrh, wrl, rb):
    M, D = x2.shape
    tm = MERGE_TM
    full = lambda a: pl.BlockSpec(a.shape, lambda i: (0,) * a.ndim)
    in_specs = [pl.BlockSpec((tm, D), lambda i: (i, 0)),
                pl.BlockSpec((tm, oa2.shape[1]), lambda i: (i, 0)),
                pl.BlockSpec((tm, or2.shape[1]), lambda i: (i, 0))] + [
        full(a) for a in (wrg, wmg, wua, wur, wo, g1, b1, wrh, wrl, rb)]
    out_shape = [jax.ShapeDtypeStruct((M, D), F32),
                 jax.ShapeDtypeStruct((8, M), I32),
                 jax.ShapeDtypeStruct((8, M), F32)]
    out_specs = [pl.BlockSpec((tm, D), lambda i: (i, 0)),
                 pl.BlockSpec((8, tm), lambda i: (0, i)),
                 pl.BlockSpec((8, tm), lambda i: (0, i))]
    return pl.pallas_call(
        functools.partial(_merge_body, alpha), out_shape=out_shape, grid=(M // tm,),
        in_specs=in_specs, out_specs=out_specs,
        compiler_params=_cparams(("parallel",)), name="merge",
    )(x2, oa2, or2, wrg, wmg, wua, wur, wo, g1, b1, wrh, wrl, rb)


def _rank_body(eid_ref, tri_ref, low_ref, pos_ref, cnt_ref, carry_ref, off_ref):
    p = pl.program_id(0)
    j = pl.program_id(1)
    e = eid_ref[...]
    rows = lax.broadcasted_iota(I32, (N_EXPERTS, e.shape[1]), 0)
    hit = rows == e
    per_expert = jnp.sum(jnp.where(hit, 1.0, 0.0), axis=1, keepdims=True)

    @pl.when((p == 0) & (j == 0))
    def _():
        carry_ref[...] = jnp.zeros(carry_ref.shape, F32)

    @pl.when(p == 0)
    def _():
        pos_ref[...] = jnp.zeros(pos_ref.shape, I32)
        carry_ref[...] = carry_ref[...] + per_expert
        cnt_ref[...] = carry_ref[...]

    @pl.when((p == 1) & (j == 0))
    def _():
        tiles = jnp.floor((carry_ref[...] + (EXPERT_TM - 1)) * (1.0 / EXPERT_TM))
        off_ref[...] = _nn(low_ref[...], tiles.astype(BF16)) * EXPERT_TM
        carry_ref[...] = jnp.zeros(carry_ref.shape, F32)

    @pl.when(p == 1)
    def _():
        onehot = jnp.where(hit, 1.0, 0.0).astype(BF16)
        incl = _nn(onehot, tri_ref[...])
        base = carry_ref[:, 0:1] + off_ref[:, 0:1] - 1.0
        pos_ref[...] = jnp.sum(jnp.where(hit, incl + base, 0.0), axis=0, keepdims=True).astype(I32)
        carry_ref[...] = carry_ref[...] + per_expert


def _rank(eid_flat, tri, low):
    n = eid_flat.shape[1]
    tb = RANK_TB
    return pl.pallas_call(
        _rank_body,
        out_shape=[jax.ShapeDtypeStruct((1, n), I32), jax.ShapeDtypeStruct((N_EXPERTS, 128), F32)],
        grid=(2, n // tb),
        in_specs=[pl.BlockSpec((1, tb), lambda p, j: (0, j)), pl.BlockSpec((tb, tb), lambda p, j: (0, 0)),
                  pl.BlockSpec(low.shape, lambda p, j: (0, 0))],
        out_specs=[pl.BlockSpec((1, tb), lambda p, j: (0, j * p)),
                   pl.BlockSpec((N_EXPERTS, 128), lambda p, j: (0, 0))],
        scratch_shapes=[pltpu.VMEM((N_EXPERTS, 128), F32), pltpu.VMEM((N_EXPERTS, 128), F32)],
        compiler_params=_cparams(("arbitrary", "arbitrary")), name="rank",
    )(eid_flat, tri, low)


def _dispatch_body(pos_ref, ztile_ref, x_ref, xs_ref, zbuf_ref, sem, zsem):
    tm = DISPATCH_TM
    m_tok = pos_ref.shape[0] // 2
    base = pl.program_id(0) * tm

    @pl.when(pl.program_id(0) == 0)
    def _():
        zbuf_ref[...] = jnp.zeros(zbuf_ref.shape, F32)

        def zero_copy(e):
            z = pl.multiple_of(jnp.maximum(ztile_ref[e], 0), EXPERT_TM)
            return pltpu.make_async_copy(zbuf_ref, xs_ref.at[pl.ds(z, EXPERT_TM), :], zsem)

        for e in range(N_EXPERTS):
            @pl.when(ztile_ref[e] >= 0)
            def _():
                zero_copy(e).start()
        for e in range(N_EXPERTS):
            @pl.when(ztile_ref[e] >= 0)
            def _():
                zero_copy(e).wait()

        def tail_copy(t):
            return pltpu.make_async_copy(
                zbuf_ref, xs_ref.at[pl.ds(pl.multiple_of(t * EXPERT_TM, EXPERT_TM), EXPERT_TM), :], zsem)

        n_tiles = xs_ref.shape[0] // EXPERT_TM
        lax.fori_loop(ztile_ref[N_EXPERTS], n_tiles, lambda t, c: (tail_copy(t).start(), c)[1], 0)
        lax.fori_loop(ztile_ref[N_EXPERTS], n_tiles, lambda t, c: (tail_copy(t).wait(), c)[1], 0)

    def row_copy(r, p):
        return pltpu.make_async_copy(x_ref.at[pl.ds(r, 1), :], xs_ref.at[pl.ds(p, 1), :], sem)

    for r in range(tm):
        row_copy(r, pos_ref[base + r]).start()
        row_copy(r, pos_ref[m_tok + base + r]).start(priority=1)

    def drain(r, carry):
        row_copy(0, 0).wait()
        row_copy(0, 0).wait()
        return carry

    lax.fori_loop(0, tm, drain, 0, unroll=16)


def _dispatch(pos, ztile, x1, n_pad):
    M, D = x1.shape
    tm = DISPATCH_TM
    grid_spec = pltpu.PrefetchScalarGridSpec(
        num_scalar_prefetch=2, grid=(M // tm,),
        in_specs=[pl.BlockSpec((tm, D), lambda i, pos, zt: (i, 0))],
        out_specs=pl.BlockSpec(memory_space=pl.ANY),
        scratch_shapes=[pltpu.VMEM((EXPERT_TM, D), F32), pltpu.SemaphoreType.DMA, pltpu.SemaphoreType.DMA],
    )
    return pl.pallas_call(
        _dispatch_body, out_shape=jax.ShapeDtypeStruct((n_pad, D), F32), grid_spec=grid_spec,
        compiler_params=_cparams(("arbitrary",), has_side_effects=True, disable_bounds_checks=True),
        name="dispatch",
    )(pos, ztile, x1)


def _experts_body(te_ref, nu_ref, seg_ref, nxt_ref, xs_ref, wg_hbm, wu_hbm, wd_hbm, ys_ref,
                  wgb_ref, wub_ref, wdb_ref, sg_ref, su_ref, sd_ref, sem):
    i = pl.program_id(0)
    live = i < nu_ref[0]

    def fetch(e, slot):
        return (pltpu.make_async_copy(wg_hbm.at[e], sg_ref.at[slot], sem.at[0, slot]),
                pltpu.make_async_copy(wu_hbm.at[e], su_ref.at[slot], sem.at[1, slot]),
                pltpu.make_async_copy(wd_hbm.at[e], sd_ref.at[slot], sem.at[2, slot]))

    @pl.when(live & (i == 0))
    def _():
        for c in fetch(te_ref[0], 0):
            c.start()

    @pl.when(live & (seg_ref[i] >= 0))
    def _():
        slot = seg_ref[i]
        for c in fetch(te_ref[i], slot):
            c.wait()
        wgb_ref[...] = sg_ref[slot].astype(BF16)
        wub_ref[...] = su_ref[slot].astype(BF16)
        wdb_ref[...] = sd_ref[slot].astype(BF16)

        @pl.when(nxt_ref[i] >= 0)
        def _():
            for c in fetch(nxt_ref[i], 1 - slot):
                c.start()

    @pl.when(live)
    def _():
        xb = xs_ref[...].astype(BF16)
        hg = _nn(xb, wgb_ref[...])
        hu = _nn(xb, wub_ref[...])
        h = (jax.nn.silu(hg) * hu).astype(BF16)
        ys_ref[...] = _nn(h, wdb_ref[...])

    @pl.when(i >= nu_ref[0])
    def _():
        ys_ref[...] = jnp.zeros(ys_ref.shape, F32)


def _experts(tile_expert, n_used, seg_slot, next_expert, xs, wg, wu, wd):
    npad, D = xs.shape
    tm = EXPERT_TM
    grid_spec = pltpu.PrefetchScalarGridSpec(
        num_scalar_prefetch=4, grid=(npad // tm,),
        in_specs=[pl.BlockSpec((tm, D), lambda i, te, nu, sg, nx: (jnp.minimum(i, nu[0] - 1), 0)),
                  pl.BlockSpec(memory_space=pl.ANY), pl.BlockSpec(memory_space=pl.ANY),
                  pl.BlockSpec(memory_space=pl.ANY)],
        out_specs=pl.BlockSpec((tm, D), lambda i, te, nu, sg, nx: (i, 0)),
        scratch_shapes=[pltpu.VMEM((D, D_FF), BF16), pltpu.VMEM((D, D_FF), BF16), pltpu.VMEM((D_FF, D), BF16),
                        pltpu.VMEM((2, D, D_FF), F32), pltpu.VMEM((2, D, D_FF), F32),
                        pltpu.VMEM((2, D_FF, D), F32), pltpu.SemaphoreType.DMA((3, 2))],
    )
    return pl.pallas_call(
        _experts_body, out_shape=jax.ShapeDtypeStruct((npad, D), F32), grid_spec=grid_spec,
        compiler_params=_cparams(("arbitrary",)), name="experts",
    )(tile_expert, n_used, seg_slot, next_expert, xs, wg, wu, wd)


def _combine_body(alpha, pos_ref, x1_ref, w_ref, eye_ref, g2_ref, b2_ref, ys_ref, o_ref,
                  buf_a, buf_b, sem_a, sem_b):
    tm = COMBINE_TM
    m_tok = pos_ref.shape[0] // 2
    k = pl.program_id(0)
    n = pl.num_programs(0)

    def row_copies(buf, sem, r, base):
        return (pltpu.make_async_copy(ys_ref.at[pl.ds(pos_ref[base + r], 1), :], buf.at[0, pl.ds(r, 1), :], sem),
                pltpu.make_async_copy(ys_ref.at[pl.ds(pos_ref[m_tok + base + r], 1), :],
                                      buf.at[1, pl.ds(r, 1), :], sem))

    def issue_inline(buf, sem, tile):
        for r in range(tm):
            for queue, c in enumerate(row_copies(buf, sem, r, tile * tm)):
                c.start(priority=queue)

    def drain(buf, sem):
        def body(r, carry):
            for c in row_copies(buf, sem, 0, 0):
                c.wait()
            return carry
        lax.fori_loop(0, tm, body, 0, unroll=16)

    def compute(buf, s):
        rows = slice(s * tm, (s + 1) * tm)
        w = w_ref[:, rows]
        eye = eye_ref[...]
        w_a = w.astype(BF16)
        w_b = (w - w_a.astype(F32)).astype(BF16)
        w_c = (w - w_a.astype(F32) - w_b.astype(F32)).astype(BF16)
        wcol = _nt(eye, w_a) + _nt(eye, w_b) + _nt(eye, w_c)
        moe = buf[0] * wcol[:, 0:1] + buf[1] * wcol[:, 1:2]
        o_ref[rows, :] = _layer_norm(alpha * x1_ref[rows, :] + moe, g2_ref[...], b2_ref[...])

    @pl.when(k == 0)
    def _():
        def body(r, carry):
            for queue, c in enumerate(row_copies(buf_a, sem_a, r, 0)):
                c.start(priority=queue)
            return carry
        lax.fori_loop(0, tm, body, 0, unroll=16)

    drain(buf_a, sem_a)
    issue_inline(buf_b, sem_b, 2 * k + 1)
    compute(buf_a, 0)
    drain(buf_b, sem_b)
    issue_inline(buf_a, sem_a, jnp.minimum(2 * k + 2, 2 * n - 2))
    compute(buf_b, 1)

    @pl.when(k == n - 1)
    def _():
        drain(buf_a, sem_a)


def _combine(alpha, pos, x1, wgt, eye, g2, b2, ys):
    M, D = x1.shape
    tm = COMBINE_TM
    grid_spec = pltpu.PrefetchScalarGridSpec(
        num_scalar_prefetch=1, grid=(M // (2 * tm),),
        in_specs=[pl.BlockSpec((2 * tm, D), lambda i, pos: (i, 0)),
                  pl.BlockSpec((8, 2 * tm), lambda i, pos: (0, i)),
                  pl.BlockSpec((tm, tm), lambda i, pos: (0, 0)),
                  pl.BlockSpec((1, D), lambda i, pos: (0, 0)),
                  pl.BlockSpec((1, D), lambda i, pos: (0, 0)),
                  pl.BlockSpec(memory_space=pl.ANY)],
        out_specs=pl.BlockSpec((2 * tm, D), lambda i, pos: (i, 0)),
        scratch_shapes=[pltpu.VMEM((2, tm, D), F32), pltpu.VMEM((2, tm, D), F32),
                        pltpu.SemaphoreType.DMA, pltpu.SemaphoreType.DMA],
    )
    return pl.pallas_call(
        functools.partial(_combine_body, alpha), out_shape=jax.ShapeDtypeStruct((M, D), F32),
        grid_spec=grid_spec, compiler_params=_cparams(("arbitrary",), disable_bounds_checks=True),
        name="combine",
    )(pos, x1, wgt, eye, g2, b2, ys)


def _rope_angles(pos, dim):
    half = dim // 2
    inv_freq = ROPE_THETA ** (-np.arange(half, dtype=np.float64) * 2.0 / dim)
    return pos.astype(np.float64)[:, None] * inv_freq[None, :]


def _tables(T):
    f32 = lambda a: jnp.asarray(np.ascontiguousarray(a), dtype=F32)
    ang = _rope_angles(np.arange(T), NSA_HD)
    cos, sin = np.cos(ang), np.sin(ang)
    c64 = np.tile(cos, (1, 4))
    s64 = np.tile(np.concatenate([-sin, sin], axis=1), (1, 2))
    ang2 = _rope_angles(np.arange(T), RET_DK)
    cos2, sin2 = np.cos(ang2), np.sin(ang2)
    c128 = np.tile(cos2, (1, 2))
    s128 = np.concatenate([-sin2, sin2], axis=1)
    n_rows = T // CMP_STRIDE
    angc = _rope_angles(np.arange(n_rows) * CMP_STRIDE + CMP_BLOCK - 1, NSA_HD)
    cc = np.tile(np.cos(angc), (1, 2))
    sc = np.tile(np.sin(angc), (1, 2))
    return tuple(f32(a) for a in (c64, s64, c128, s128, cos.T, sin.T, cos2.T, sin2.T)), f32(cc), f32(sc)


def _overlap_matrix(T):
    n_rows = T // CMP_STRIDE
    n_sel = T // SLC_BLOCK
    cmp_start = np.arange(n_rows) * CMP_STRIDE
    sel_start = np.arange(n_sel) * SLC_BLOCK
    ov = np.clip(np.minimum(cmp_start[None, :] + CMP_BLOCK, sel_start[:, None] + SLC_BLOCK)
                 - np.maximum(cmp_start[None, :], sel_start[:, None]), 0, None)
    return jnp.asarray(ov.astype(np.float32) / CMP_STRIDE, dtype=BF16)


def _retention_tables():
    C = RET_CHUNK
    f32 = lambda a: jnp.asarray(np.ascontiguousarray(a), dtype=F32)
    gamma = 1.0 - 2.0 ** (-5.0 - np.arange(RET_HEADS, dtype=np.float64))
    log_g = np.log(gamma)
    i = np.arange(C, dtype=np.float64)
    diff = i[:, None] - i[None, :]
    dec = np.where(diff >= 0, np.exp(np.maximum(diff, 0.0) * log_g[:, None, None]), 0.0)
    xi = np.exp((i + 1.0) * log_g[:, None])
    zeta = np.exp((C - 1.0 - i) * log_g[:, None])
    cd = np.exp(C * log_g)
    xi_b = np.broadcast_to(xi[:, :, None], (RET_HEADS, C, RET_DV))
    cd_b = np.broadcast_to(cd[:, None, None], (RET_HEADS, 1, RET_DV))
    return f32(dec), f32(xi_b), f32(zeta[:, None, :]), f32(cd_b)


def _rot_half_cols(w):
    half = w.shape[-1] // 2
    return jnp.concatenate([-w[..., half:], w[..., :half]], axis=-1)


def kernel(x, w_in, cmp_pos_k, cmp_k_w1, cmp_k_b1, cmp_k_w2, cmp_k_b2, cmp_pos_v, cmp_v_w1, cmp_v_b1, cmp_v_w2, cmp_v_b2, ret_gn_g, ret_gn_b, w_up_attn, w_up_ret, w_out, ln1_g, ln1_b, router_group_w, router_group_b, router_inner_w, router_inner_b, expert_w_gate, expert_w_up, expert_w_down, ln2_g, ln2_b):
    B, T, D = x.shape
    M = B * T
    depth = w_in.shape[0]
    alpha = (2.0 * depth) ** 0.25
    tabs, cc, sc = _tables(T)
    ov = _overlap_matrix(T)
    dec, xi_b, zeta, cd_b = _retention_tables()
    tri = jnp.asarray(np.triu(np.ones((RANK_TB, RANK_TB), np.float32)), dtype=BF16)
    low = jnp.asarray(np.tril(np.ones((N_EXPERTS, N_EXPERTS), np.float32), k=-1), dtype=BF16)
    eye = jnp.asarray(np.eye(COMBINE_TM, dtype=np.float32), dtype=BF16)
    n_pad = 2 * M + N_EXPERTS * EXPERT_TM
    n_tiles = n_pad // EXPERT_TM

    for l in range(depth):
        w = w_in[l]
        col = lambda n: w[:, _OFF[n][0]:_OFF[n][1]]
        wn = jnp.concatenate([col("cmp_k"), col("cmp_v"), col("slc_k"), col("win_k"), col("ret_q"), col("ret_v")],
                             axis=1).astype(BF16)
        wt = jnp.concatenate([col("nsa_q"), col("slc_v"), col("win_v"), col("ret_k"), col("nsa_gate"),
                              jnp.zeros((D, 8), F32)], axis=1).T.astype(BF16)
        qT, ckv, sk, wk, svT, wvT, gT, rq, rkT, rv = _proj(x, wn, wt, tabs)

        kc, vcT = _compress(
            ckv, cmp_pos_k[l].reshape(2, -1), cmp_pos_v[l].reshape(2, -1),
            cmp_k_w1[l], cmp_k_b1[l][None, :], cmp_k_w2[l], _rot_half_cols(cmp_k_w2[l]),
            cmp_k_b2[l][None, :], _rot_half_cols(cmp_k_b2[l])[None, :], cc, sc,
            cmp_v_w1[l], cmp_v_b1[l][None, :], cmp_v_w2[l].T, cmp_v_b2[l][:, None])
        o_attn = _nsa(qT, kc, vcT, ov, sk, svT, wk, wvT, gT)
        o_ret = _retention(rq, rkT, rv, dec, xi_b, zeta, cd_b, ret_gn_g[l][None, :], ret_gn_b[l][None, :])

        wr = jnp.concatenate([router_inner_w[l].transpose(0, 2, 1).reshape(N_EXPERTS, D),
                              router_group_w[l].T, jnp.zeros((4, D), F32)], axis=0)
        wrh = wr.astype(BF16)
        wrl = (wr - wrh.astype(F32)).astype(BF16)
        rb = jnp.concatenate([router_inner_b[l].reshape(-1), router_group_b[l], jnp.zeros((4,), F32)])[:, None]
        x1, eid, wgt = _merge(
            alpha, x.reshape(M, D), o_attn.reshape(M, -1), o_ret.reshape(M, -1),
            col("ret_gate").astype(BF16), col("merge_gate").astype(BF16), w_up_attn[l].astype(BF16),
            w_up_ret[l].astype(BF16), w_out[l].astype(BF16), ln1_g[l][None, :], ln1_b[l][None, :], wrh, wrl, rb)

        eid_flat = eid[:2].reshape(1, 2 * M)
        pos, cnt = _rank(eid_flat, tri, low)
        pos = pos[0]
        counts = cnt[:, 0].astype(I32)
        tiles_per = (counts + EXPERT_TM - 1) // EXPERT_TM
        tile_end = jnp.cumsum(tiles_per)
        tile_ids = jnp.arange(n_tiles, dtype=I32)
        tile_expert = jnp.minimum(jnp.sum((tile_end[None, :] <= tile_ids[:, None]).astype(I32), axis=1),
                                  N_EXPERTS - 1).astype(I32)
        n_used = tile_end[-1:].astype(I32)

        ztile = jnp.concatenate([jnp.where(tiles_per > 0, (tile_end - 1) * EXPERT_TM, -1), tile_end[-1:]]).astype(I32)
        xs = _dispatch(pos, ztile, x1, n_pad)
        prev_expert = jnp.concatenate([jnp.full((1,), -1, I32), tile_expert[:-1]])
        is_first = (tile_ids < n_used[0]) & (tile_expert != prev_expert)
        seg_slot = jnp.where(is_first, (jnp.cumsum(is_first.astype(I32)) - 1) % 2, -1).astype(I32)
        own = tile_expert[:, None] == jnp.arange(N_EXPERTS, dtype=I32)[None, :]
        seg_end = jnp.sum(jnp.where(own, tile_end[None, :], 0), axis=1)
        expert_at_end = jnp.minimum(jnp.sum((tile_end[None, :] <= seg_end[:, None]).astype(I32), axis=1),
                                    N_EXPERTS - 1)
        next_expert = jnp.where(is_first & (seg_end < n_used[0]), expert_at_end, -1).astype(I32)
        ys = _experts(tile_expert, n_used, seg_slot, next_expert, xs,
                      expert_w_gate[l], expert_w_up[l], expert_w_down[l])
        x = _combine(alpha, pos, x1, wgt, eye, ln2_g[l][None, :], ln2_b[l][None, :], ys).reshape(B, T, D)
    return x
```

```python
import functools

import numpy as np
import jax
import jax.numpy as jnp
from jax import lax
from jax.experimental import pallas as pl
from jax.experimental.pallas import tpu as pltpu

F32 = jnp.float32
BF16 = jnp.bfloat16
I32 = jnp.int32

D_MODEL = 1024
NSA_HEADS = 8
NSA_HD = 64
NSA_GROUPS = 2
NSA_HPG = NSA_HEADS // NSA_GROUPS
CMP_BLOCK = 32
CMP_STRIDE = 16
CMP_HIDDEN = 256
SLC_BLOCK = 64
SLC_TOPK = 16
WINDOW = 512
RET_HEADS = 4
RET_DK = 128
RET_DV = 256
RET_CHUNK = 128
MOE_GROUPS = 4
EXPERTS_PER_GROUP = 8
N_EXPERTS = MOE_GROUPS * EXPERTS_PER_GROUP
D_FF = 512
ROPE_THETA = 10000.0
LN_EPS = 1e-5
GN_EPS = 1e-5
NEG_INF = -1e30
LOG2_E = 1.4426950408889634

VMEM_LIMIT_V7X = 56 * 1024 * 1024

_OFF = {}
_o = 0
for _n, _w in (("nsa_q", 512), ("cmp_k", 128), ("cmp_v", 128), ("slc_k", 128), ("slc_v", 128),
               ("win_k", 128), ("win_v", 128), ("nsa_gate", 24), ("ret_q", 512), ("ret_k", 512),
               ("ret_v", 1024), ("ret_gate", 1024), ("merge_gate", 2048)):
    _OFF[_n] = (_o, _o + _w)
    _o += _w

PROJ_TM = 512
NSA_TQ = 256
NSA_KC = 256
NSA_VROWS = 80
RET_STEP_CHUNKS = 4
RET_STEP_BATCH = 2
MERGE_TM = 512
RANK_TB = 1024
DISPATCH_TM = 512
EXPERT_TM = 256
EXPERT_STEP_TILES = 2
COMBINE_TM = 256


def _cparams(sem, **kw):
    return pltpu.CompilerParams(dimension_semantics=sem, vmem_limit_bytes=VMEM_LIMIT_V7X, **kw)


def _nt(a, b):
    return lax.dot_general(a, b, (((1,), (1,)), ((), ())), preferred_element_type=F32)


def _nn(a, b):
    return jnp.dot(a, b, preferred_element_type=F32)


def _proj_body(x_ref, wn_ref, wt_ref, c64_ref, s64_ref, c128_ref, s128_ref,
               ct64_ref, st64_ref, ct128_ref, st128_ref,
               qT_ref, ckv_ref, sk_ref, wk_ref, svT_ref, wvT_ref, gT_ref, rq_ref, rkT_ref, rv_ref, ckv_scr):
    tm = PROJ_TM
    xb = x_ref[0].astype(BF16)

    def nn(a, b):
        return _nn(xb, wn_ref[:, a:b])

    def nt(a, b):
        return _nt(wt_ref[a:b, :], xb)

    ckv = nn(0, 256)
    for c in range(2):
        ckv_scr[c] = ckv[:, 128 * c:128 * (c + 1)]
    for l in range(CMP_STRIDE):
        for c in range(2):
            rows = ckv_scr[c, pl.ds(l, tm // CMP_STRIDE, stride=CMP_STRIDE), :]
            for jj in range(2):
                ckv_ref[0, 2 * c + jj, :, 64 * l:64 * (l + 1)] = rows[:, 64 * jj:64 * (jj + 1)]

    lane = lax.broadcasted_iota(I32, (tm, 128), 1)
    first = (lane % 64) < 32
    c64 = c64_ref[...]
    s64 = s64_ref[...]

    def rope64(k):
        rot = jnp.where(first, pltpu.roll(k, 96, 1), pltpu.roll(k, 32, 1))
        return k * c64 + rot * s64

    sk = rope64(nn(256, 384)).astype(BF16)
    sk_ref[0, 0] = sk[:, :64]
    sk_ref[0, 1] = sk[:, 64:]
    wk = rope64(nn(384, 512)).astype(BF16)
    wk_ref[0, 0] = wk[:, :64]
    wk_ref[0, 1] = wk[:, 64:]

    c128 = c128_ref[...]
    s128 = s128_ref[...]
    rq = nn(512, 1024)
    for h in range(RET_HEADS):
        ch = rq[:, 128 * h:128 * (h + 1)]
        rq_ref[0, :, 128 * h:128 * (h + 1)] = (ch * c128 + pltpu.roll(ch, 64, 1) * s128).astype(BF16)
    rv_ref[0] = nn(1024, 2048).astype(BF16)

    ct = ct64_ref[...]
    st = st64_ref[...]
    qT = nt(0, 512)
    scale_q = NSA_HD ** -0.5 * LOG2_E
    tq = NSA_TQ
    for hh in range(NSA_HEADS):
        g, h = divmod(hh, NSA_HPG)
        x1 = qT[64 * hh:64 * hh + 32]
        x2 = qT[64 * hh + 32:64 * hh + 64]
        o1 = ((x1 * ct - x2 * st) * scale_q).astype(BF16)
        o2 = ((x1 * st + x2 * ct) * scale_q).astype(BF16)
        for it in range(tm // tq):
            qT_ref[0, g, it, 0:32, h * tq:(h + 1) * tq] = o1[:, it * tq:(it + 1) * tq]
            qT_ref[0, g, it, 32:64, h * tq:(h + 1) * tq] = o2[:, it * tq:(it + 1) * tq]

    svT = nt(512, 640).astype(BF16)
    wvT = nt(640, 768).astype(BF16)
    row16 = lax.broadcasted_iota(I32, (NSA_VROWS - NSA_HD, NSA_KC), 0)
    ones_blk = jnp.where(row16 == 0, 1.0, 0.0).astype(BF16)
    for c in range(tm // NSA_KC):
        for g in range(NSA_GROUPS):
            for vT, ref in ((svT, svT_ref), (wvT, wvT_ref)):
                ref[0, c, g, 0:NSA_HD, :] = vT[64 * g:64 * (g + 1), NSA_KC * c:NSA_KC * (c + 1)]
                ref[0, c, g, NSA_HD:NSA_VROWS, :] = ones_blk

    ct2 = ct128_ref[...]
    st2 = st128_ref[...]
    rkT = nt(768, 1280)
    scale_k = RET_DK ** -0.5
    for h in range(RET_HEADS):
        x1 = rkT[128 * h:128 * h + 64]
        x2 = rkT[128 * h + 64:128 * h + 128]
        o1 = (x1 * ct2 - x2 * st2) * scale_k
        o2 = (x1 * st2 + x2 * ct2) * scale_k
        for c in range(tm // RET_CHUNK):
            rkT_ref[0, c, 128 * h:128 * h + 64, :] = o1[:, 128 * c:128 * (c + 1)]
            rkT_ref[0, c, 128 * h + 64:128 * h + 128, :] = o2[:, 128 * c:128 * (c + 1)]

    gT_ref[0] = jax.nn.sigmoid(nt(1280, 1312))


def _proj(x, wn, wt, tabs):
    B, T, D = x.shape
    tm = PROJ_TM
    c64, s64, c128, s128, ct64, st64, ct128, st128 = tabs
    const = lambda b, i: (0, 0)
    in_specs = [
        pl.BlockSpec((1, tm, D), lambda b, i: (b, i, 0)),
        pl.BlockSpec(wn.shape, const),
        pl.BlockSpec(wt.shape, const),
        pl.BlockSpec((tm, 128), lambda b, i: (i, 0)),
        pl.BlockSpec((tm, 128), lambda b, i: (i, 0)),
        pl.BlockSpec((tm, 128), lambda b, i: (i, 0)),
        pl.BlockSpec((tm, 128), lambda b, i: (i, 0)),
        pl.BlockSpec((32, tm), lambda b, i: (0, i)),
        pl.BlockSpec((32, tm), lambda b, i: (0, i)),
        pl.BlockSpec((64, tm), lambda b, i: (0, i)),
        pl.BlockSpec((64, tm), lambda b, i: (0, i)),
    ]
    out_shape = [
        jax.ShapeDtypeStruct((B, NSA_GROUPS, T // NSA_TQ, NSA_HD, NSA_HPG * NSA_TQ), BF16),
        jax.ShapeDtypeStruct((B, 4, T // CMP_STRIDE, CMP_STRIDE * NSA_HD), F32),
        jax.ShapeDtypeStruct((B, 2, T, 64), BF16),
        jax.ShapeDtypeStruct((B, 2, T, 64), BF16),
        jax.ShapeDtypeStruct((B, T // NSA_KC, NSA_GROUPS, NSA_VROWS, NSA_KC), BF16),
        jax.ShapeDtypeStruct((B, T // NSA_KC, NSA_GROUPS, NSA_VROWS, NSA_KC), BF16),
        jax.ShapeDtypeStruct((B, 32, T), F32),
        jax.ShapeDtypeStruct((B, T, 512), BF16),
        jax.ShapeDtypeStruct((B, T // RET_CHUNK, 512, RET_CHUNK), F32),
        jax.ShapeDtypeStruct((B, T, 1024), BF16),
    ]
    out_specs = [
        pl.BlockSpec((1, NSA_GROUPS, tm // NSA_TQ, NSA_HD, NSA_HPG * NSA_TQ), lambda b, i: (b, 0, i, 0, 0)),
        pl.BlockSpec((1, 4, tm // CMP_STRIDE, CMP_STRIDE * NSA_HD), lambda b, i: (b, 0, i, 0)),
        pl.BlockSpec((1, 2, tm, 64), lambda b, i: (b, 0, i, 0)),
        pl.BlockSpec((1, 2, tm, 64), lambda b, i: (b, 0, i, 0)),
        pl.BlockSpec((1, tm // NSA_KC, NSA_GROUPS, NSA_VROWS, NSA_KC), lambda b, i: (b, i, 0, 0, 0)),
        pl.BlockSpec((1, tm // NSA_KC, NSA_GROUPS, NSA_VROWS, NSA_KC), lambda b, i: (b, i, 0, 0, 0)),
        pl.BlockSpec((1, 32, tm), lambda b, i: (b, 0, i)),
        pl.BlockSpec((1, tm, 512), lambda b, i: (b, i, 0)),
        pl.BlockSpec((1, tm // RET_CHUNK, 512, RET_CHUNK), lambda b, i: (b, i, 0, 0)),
        pl.BlockSpec((1, tm, 1024), lambda b, i: (b, i, 0)),
    ]
    return pl.pallas_call(
        _proj_body, out_shape=out_shape, grid=(B, T // tm), in_specs=in_specs, out_specs=out_specs,
        scratch_shapes=[pltpu.VMEM((2, tm, 128), F32)],
        compiler_params=_cparams(("parallel", "parallel")), name="proj",
    )(x, wn, wt, c64, s64, c128, s128, ct64, st64, ct128, st128)


def _compress_body(uk_ref, uv_ref, posk_ref, posv_ref, w1k_ref, b1k_ref, w2k_ref, w2kr_ref, b2k_ref, b2kr_ref,
                   cc_ref, sc_ref, w1v_ref, b1v_ref, w2vT_ref, b2vT_ref, kc_ref, vcT_ref):
    half = CMP_STRIDE * NSA_HD

    def hidden(u_ref, pos_ref, w1_ref, b1_ref):
        u = u_ref[0, 0]
        top = (u + pos_ref[0:1, :]).astype(BF16)
        bot = (u + pos_ref[1:2, :]).astype(BF16)
        a = _nn(top, w1_ref[0:half, :].astype(BF16))
        bm = _nn(bot, w1_ref[half:2 * half, :].astype(BF16))
        n = bm.shape[0]
        return jax.nn.gelu(a + pltpu.roll(bm, n - 1, 0) + b1_ref[...])

    hk = hidden(uk_ref, posk_ref, w1k_ref, b1k_ref).astype(BF16)
    k = _nn(hk, w2k_ref[...].astype(BF16)) + b2k_ref[...]
    kr = _nn(hk, w2kr_ref[...].astype(BF16)) + b2kr_ref[...]
    kc_ref[0, 0] = (k * cc_ref[...] + kr * sc_ref[...]).astype(BF16)

    hv = hidden(uv_ref, posv_ref, w1v_ref, b1v_ref).astype(BF16)
    vcT_ref[0, 0] = (_nt(w2vT_ref[...].astype(BF16), hv) + b2vT_ref[...]).astype(BF16)


def _compress(ckv, posk, posv, w1k, b1k, w2k, w2kr, b2k, b2kr, cc, sc, w1v, b1v, w2vT, b2vT):
    B, _, nr, _ = ckv.shape
    u = ckv
    full = lambda a: pl.BlockSpec(a.shape, lambda b, g: (0,) * a.ndim)
    in_specs = [
        pl.BlockSpec((1, 1, nr, 1024), lambda b, g: (b, g, 0, 0)),
        pl.BlockSpec((1, 1, nr, 1024), lambda b, g: (b, g + 2, 0, 0)),
    ] + [full(a) for a in (posk, posv, w1k, b1k, w2k, w2kr, b2k, b2kr, cc, sc, w1v, b1v, w2vT, b2vT)]
    out_shape = [jax.ShapeDtypeStruct((B, NSA_GROUPS, nr, NSA_HD), BF16),
                 jax.ShapeDtypeStruct((B, NSA_GROUPS, NSA_HD, nr), BF16)]
    out_specs = [pl.BlockSpec((1, 1, nr, NSA_HD), lambda b, g: (b, g, 0, 0)),
                 pl.BlockSpec((1, 1, NSA_HD, nr), lambda b, g: (b, g, 0, 0))]
    return pl.pallas_call(
        _compress_body, out_shape=out_shape, grid=(B, NSA_GROUPS), in_specs=in_specs, out_specs=out_specs,
        compiler_params=_cparams(("parallel", "parallel")), name="compress",
    )(u, u, posk, posv, w1k, b1k, w2k, w2kr, b2k, b2kr, cc, sc, w1v, b1v, w2vT, b2vT)


def _nsa_body(q_ref, kc_ref, vcT_ref, ov_ref, sk_ref, svT_ref, wk_ref, wvT_ref, gT_ref, lo_ref, hi_ref, o_ref,
              score_ref, bias_ref, acc_ref, tot_ref, s_ref):
    tq = NSA_TQ
    kc_n = NSA_KC
    hq = NSA_HPG * tq
    groups = range(NSA_GROUPS)
    i = pl.program_id(1)
    t0 = i * tq
    t_row = t0 + lax.broadcasted_iota(I32, (1, tq), 1)
    t_row4 = t0 + lax.broadcasted_iota(I32, (1, hq), 1) % tq
    n_cmp = kc_ref.shape[2]
    n_sel = ov_ref.shape[0]
    q = [q_ref[0, g, 0] for g in groups]

    def gate4(g, br):
        return jnp.concatenate([gT_ref[0, (g * NSA_HPG + h) * 3 + br:(g * NSA_HPG + h) * 3 + br + 1, :]
                                for h in range(NSA_HPG)], axis=1)

    cmp_end = lax.broadcasted_iota(I32, (n_cmp, 1), 0) * CMP_STRIDE + (CMP_BLOCK - 1)
    cbias = jnp.where(cmp_end <= t_row4, 0.0, NEG_INF)
    any_valid = jnp.where(t_row4 >= CMP_BLOCK - 1, 1.0, 0.0)
    blk_t = t_row // SLC_BLOCK
    sub = lax.broadcasted_iota(I32, (8, 1), 0)

    def select_blocks(g, l_cmp, l_sel):
        s = _nn(kc_ref[0, g, 0:l_cmp, :], q[g]) + cbias[0:l_cmp]
        e = jnp.exp2(s - jnp.max(s, axis=0, keepdims=True))
        p = e * (any_valid / jnp.sum(e, axis=0, keepdims=True))
        tot_ref[g] = gate4(g, 0) * _nn(vcT_ref[0, g, :, 0:l_cmp], p.astype(BF16))
        psum = p[:, 0:tq]
        for h in range(1, NSA_HPG):
            psum = psum + p[:, h * tq:(h + 1) * tq]
        p_hi = psum.astype(BF16)
        p_lo = (psum - p_hi.astype(F32)).astype(BF16)
        ov = ov_ref[0:l_sel, 0:l_cmp]
        imp = _nn(ov, p_hi) + _nn(ov, p_lo)
        jc = lax.broadcasted_iota(I32, (l_sel, 1), 0)
        forced = (jc == 0) | (jc == blk_t) | (jc == blk_t - 1)
        score = jnp.where(forced, 1e9, jnp.where(jc <= blk_t, imp, -1e9)).astype(F32)
        score_ref[g, 0:l_sel, :] = score
        n_slab = l_sel // 8
        slabs = [score[8 * v:8 * (v + 1)] for v in range(n_slab)]
        cnt = [jnp.zeros((8, tq), F32) for _ in range(n_slab)]
        for r in range(l_sel):
            row = jnp.broadcast_to(score_ref[g, r:r + 1, :], (8, tq))
            for v in range(n_slab):
                if r < 8 * v:
                    ahead = jnp.where(row >= slabs[v], 1.0, 0.0)
                elif r >= 8 * (v + 1):
                    ahead = jnp.where(row > slabs[v], 1.0, 0.0)
                else:
                    ahead = jnp.where(sub + 8 * v > r, jnp.where(row >= slabs[v], 1.0, 0.0),
                                      jnp.where(row > slabs[v], 1.0, 0.0))
                cnt[v] = cnt[v] + ahead
        for v in range(n_slab):
            bias_ref[g, 8 * v:8 * (v + 1), :] = jnp.where(
                cnt[v] < float(SLC_TOPK), jnp.where(sub + 8 * v <= blk_t, 0.0, NEG_INF), NEG_INF)
        if l_sel < n_sel:
            bias_ref[g, l_sel:n_sel, :] = jnp.full((n_sel - l_sel, tq), NEG_INF, F32)

    n_var = 4
    tiles_per_var = (n_sel * SLC_BLOCK // tq) // n_var
    for var in range(n_var):
        l_sel = (var + 1) * tiles_per_var * (tq // SLC_BLOCK)
        l_cmp = min(n_cmp, -(-((var + 1) * tiles_per_var * (tq // CMP_STRIDE)) // 128) * 128)

        @pl.when(i // tiles_per_var == var)
        def _():
            for g in groups:
                select_blocks(g, l_cmp, l_sel)

    def qk(k_ref, g, c):
        k_c = k_ref[0, g, pl.ds(pl.multiple_of(c * kc_n, kc_n), kc_n), :]
        return [_nn(k_c, q[g][:, h * tq:(h + 1) * tq]) for h in range(NSA_HPG)]

    def stage_a(scores, add_bias, m):
        biased = [add_bias(scores[h]) for h in range(NSA_HPG)]
        m_new = tuple(jnp.maximum(m[h], jnp.max(biased[h], axis=0, keepdims=True)) for h in range(NSA_HPG))
        return biased, m_new

    def stage_b(g, biased, vT_c, m_old, m_new):
        for h in range(NSA_HPG):
            hs = slice(h * tq, (h + 1) * tq)
            alpha = jnp.exp2(m_old[h] - m_new[h])
            p = jnp.exp2((biased[h] - m_new[h]).astype(BF16))
            acc_ref[g, :, hs] = alpha * acc_ref[g, :, hs] + _nn(vT_c, p)

    nb = kc_n // SLC_BLOCK

    def block_bias(g, c, s):
        return jnp.concatenate([s[SLC_BLOCK * b:SLC_BLOCK * (b + 1)] + bias_ref[g, pl.ds(c * nb + b, 1), :]
                                for b in range(nb)], axis=0)

    def finish(g, br):
        acc = acc_ref[g]
        tot_ref[g] = tot_ref[g] + gate4(g, br) * (acc[0:NSA_HD] / acc[NSA_HD:NSA_HD + 1])
        acc_ref[g] = jnp.zeros(acc.shape, F32)

    def park(g, biased):
        for h in range(NSA_HPG):
            s_ref[g, h] = biased[h]

    def parked(g):
        return [s_ref[g, h] for h in range(NSA_HPG)]

    m0 = tuple(jnp.full((1, tq), NEG_INF, F32) for _ in range(NSA_HPG))

    acc_ref[...] = jnp.zeros(acc_ref.shape, F32)
    m_new = []
    for g in groups:
        biased, m_g = stage_a(qk(sk_ref, g, i), lambda s, g=g: block_bias(g, i, s) + lo_ref[...], m0)
        park(g, biased)
        m_new.append(m_g)

    def slc_chunk(c, carry):
        prev, m_old, m_new = carry
        scores = [qk(sk_ref, g, c) for g in groups]
        for g in groups:
            stage_b(g, parked(g), svT_ref[0, prev, g], m_old[g], m_new[g])
        m_next = []
        for g in groups:
            biased, m_g = stage_a(scores[g], functools.partial(block_bias, g, c), m_new[g])
            park(g, biased)
            m_next.append(m_g)
        return c, m_new, tuple(m_next)

    prev, m_old, m_new = lax.fori_loop(0, i, slc_chunk, (i, (m0,) * NSA_GROUPS, tuple(m_new)))

    c_far = jnp.maximum(i - 2, 0)
    c_near = jnp.maximum(i - 1, 0)
    pen_far = jnp.where(i >= 2, 0.0, NEG_INF).astype(F32)
    pen_near = jnp.where(i >= 1, 0.0, NEG_INF).astype(F32)

    scores = [qk(wk_ref, g, c_far) for g in groups]
    far = []
    for g in groups:
        stage_b(g, parked(g), svT_ref[0, prev, g], m_old[g], m_new[g])
        far.append(stage_a(scores[g], lambda s: s + (hi_ref[...] + pen_far), m0))
        finish(g, 1)

    scores = [qk(wk_ref, g, c_near) for g in groups]
    near = []
    for g in groups:
        stage_b(g, far[g][0], wvT_ref[0, c_far, g], m0, far[g][1])
        near.append(stage_a(scores[g], lambda s: s + pen_near, far[g][1]))
    scores = [qk(wk_ref, g, i) for g in groups]
    for g in groups:
        stage_b(g, near[g][0], wvT_ref[0, c_near, g], far[g][1], near[g][1])
        b_diag, m_diag = stage_a(scores[g], lambda s: s + lo_ref[...], near[g][1])
        stage_b(g, b_diag, wvT_ref[0, i, g], near[g][1], m_diag)
        finish(g, 2)

    o_ref[0] = jnp.concatenate([tot_ref[g, :, h * tq:(h + 1) * tq] for g in groups for h in range(NSA_HPG)],
                               axis=0).T.astype(BF16)


def _nsa(qT, kc, vcT, ov, sk, svT, wk, wvT, gT):
    B, _, T = gT.shape
    tq = NSA_TQ
    assert NSA_KC == tq and WINDOW == 2 * NSA_KC
    nr = kc.shape[2]
    nch = T // NSA_KC
    kk = np.arange(NSA_KC)[:, None]
    tt = np.arange(tq)[None, :]
    lo = jnp.asarray(np.where(kk <= tt, 0.0, NEG_INF), dtype=F32)
    hi = jnp.asarray(np.where(kk > tt, 0.0, NEG_INF), dtype=F32)
    G = NSA_GROUPS
    in_specs = [
        pl.BlockSpec((1, G, 1, NSA_HD, NSA_HPG * tq), lambda b, i: (b, 0, i, 0, 0)),
        pl.BlockSpec((1, G, nr, NSA_HD), lambda b, i: (b, 0, 0, 0)),
        pl.BlockSpec((1, G, NSA_HD, nr), lambda b, i: (b, 0, 0, 0)),
        pl.BlockSpec(ov.shape, lambda b, i: (0, 0)),
        pl.BlockSpec((1, G, T, NSA_HD), lambda b, i: (b, 0, 0, 0)),
        pl.BlockSpec((1, nch, G, NSA_VROWS, NSA_KC), lambda b, i: (b, 0, 0, 0, 0)),
        pl.BlockSpec((1, G, T, NSA_HD), lambda b, i: (b, 0, 0, 0)),
        pl.BlockSpec((1, nch, G, NSA_VROWS, NSA_KC), lambda b, i: (b, 0, 0, 0, 0)),
        pl.BlockSpec((1, 32, tq), lambda b, i: (b, 0, i)),
        pl.BlockSpec(lo.shape, lambda b, i: (0, 0)),
        pl.BlockSpec(hi.shape, lambda b, i: (0, 0)),
    ]
    n_sel = T // SLC_BLOCK
    hq = NSA_HPG * tq
    return pl.pallas_call(
        _nsa_body, out_shape=jax.ShapeDtypeStruct((B, T, NSA_HEADS * NSA_HD), BF16),
        grid=(B, T // tq), in_specs=in_specs,
        out_specs=pl.BlockSpec((1, tq, NSA_HEADS * NSA_HD), lambda b, i: (b, i, 0)),
        scratch_shapes=[pltpu.VMEM((G, n_sel, tq), F32), pltpu.VMEM((G, n_sel, tq), F32),
                        pltpu.VMEM((G, NSA_VROWS, hq), F32), pltpu.VMEM((G, NSA_HD, hq), F32),
                        pltpu.VMEM((G, NSA_HPG, NSA_KC, tq), F32)],
        compiler_params=_cparams(("parallel", "parallel")), name="nsa",
    )(qT, kc, vcT, ov, sk, svT, wk, wvT, gT, lo, hi)


def _ret_body(q_ref, kT_ref, v_ref, dec_ref, xi_ref, zeta_ref, cd_ref, gg_ref, gb_ref, o_ref, r_ref):
    C = RET_CHUNK

    @pl.when(pl.program_id(1) == 0)
    def _():
        r_ref[...] = jnp.zeros(r_ref.shape, F32)

    for bb in range(q_ref.shape[0]):
        for h in range(RET_HEADS):
            dk = slice(h * RET_DK, (h + 1) * RET_DK)
            dv = slice(h * RET_DV, (h + 1) * RET_DV)
            dec = dec_ref[h]
            xi = xi_ref[h]
            zeta = zeta_ref[h]
            cd = cd_ref[h]
            gg = gg_ref[:, dv]
            gb = gb_ref[:, dv]
            r = r_ref[bb, h]
            for n in range(RET_STEP_CHUNKS):
                rows = slice(n * C, (n + 1) * C)
                qc = q_ref[bb, rows, dk]
                kT = kT_ref[bb, n, dk, :]
                vc = v_ref[bb, rows, dv]
                s = _nn(qc, kT.astype(BF16)) * dec
                o = _nn(s.astype(BF16), vc) + _nn(qc, r.astype(BF16)) * xi
                r = r * cd + _nn((kT * zeta).astype(BF16), vc)
                mu = jnp.mean(o, axis=-1, keepdims=True)
                var = jnp.mean(jnp.square(o - mu), axis=-1, keepdims=True)
                o_ref[bb, rows, dv] = (o - mu) * lax.rsqrt(var + GN_EPS) * gg + gb
            r_ref[bb, h] = r


def _retention(rq, rkT, rv, dec, xi, zeta, cd, gn_g, gn_b):
    B, T, _ = rq.shape
    ts = RET_STEP_CHUNKS * RET_CHUNK
    full = lambda a: pl.BlockSpec(a.shape, lambda b, j: (0,) * a.ndim)
    nb = RET_STEP_BATCH if B % RET_STEP_BATCH == 0 else 1
    in_specs = [
        pl.BlockSpec((nb, ts, RET_HEADS * RET_DK), lambda b, j: (b, j, 0)),
        pl.BlockSpec((nb, RET_STEP_CHUNKS, RET_HEADS * RET_DK, RET_CHUNK), lambda b, j: (b, j, 0, 0)),
        pl.BlockSpec((nb, ts, RET_HEADS * RET_DV), lambda b, j: (b, j, 0)),
    ] + [full(a) for a in (dec, xi, zeta, cd, gn_g, gn_b)]
    return pl.pallas_call(
        _ret_body, out_shape=jax.ShapeDtypeStruct((B, T, RET_HEADS * RET_DV), F32),
        grid=(B // nb, T // ts), in_specs=in_specs,
        out_specs=pl.BlockSpec((nb, ts, RET_HEADS * RET_DV), lambda b, j: (b, j, 0)),
        scratch_shapes=[pltpu.VMEM((nb, RET_HEADS, RET_DK, RET_DV), F32)],
        compiler_params=_cparams(("parallel", "arbitrary")), name="retention",
    )(rq, rkT, rv, dec, xi, zeta, cd, gn_g, gn_b)


def _layer_norm(y, g, b):
    mu = jnp.mean(y, axis=-1, keepdims=True)
    var = jnp.mean(jnp.square(y - mu), axis=-1, keepdims=True)
    return (y - mu) * lax.rsqrt(var + LN_EPS) * g + b


def _merge_body(alpha, x_ref, oa_ref, or_ref, wrg_ref, wmg_ref, wua_ref, wur_ref, wo_ref, g1_ref, b1_ref,
                wrh_ref, wrl_ref, rb_ref, x1_ref, eid_ref, wgt_ref):
    x = x_ref[...]
    xb = x.astype(BF16)
    rgate = jax.nn.silu(_nn(xb, wrg_ref[...]))
    o_ret = (or_ref[...] * rgate).astype(BF16)
    a = _nn(oa_ref[...], wua_ref[...])
    r = _nn(o_ret, wur_ref[...])
    mg = jax.nn.sigmoid(_nn(xb, wmg_ref[...]))
    merged = mg[:, :D_MODEL] * a + mg[:, D_MODEL:] * r
    mix = _nn(merged.astype(BF16), wo_ref[...])
    x1 = _layer_norm(alpha * x + mix, g1_ref[...], b1_ref[...])
    x1_ref[...] = x1

    xh = x1.astype(BF16)
    xl = (x1 - xh.astype(F32)).astype(BF16)
    wh = wrh_ref[...]
    lg = _nt(wh, xh) + _nt(wh, xl) + _nt(wrl_ref[...], xh) + rb_ref[...]
    ne = N_EXPERTS
    gl = lg[ne:ne + MOE_GROUPS]
    ge = jnp.exp(gl - jnp.max(gl, axis=0, keepdims=True))
    pg = ge / jnp.sum(ge, axis=0, keepdims=True)
    g_prob = jnp.max(pg, axis=0, keepdims=True)
    gi = lax.broadcasted_iota(I32, pg.shape, 0)
    g_idx = jnp.min(jnp.where(pg == g_prob, gi, MOE_GROUPS), axis=0, keepdims=True)
    inner = jnp.zeros((EXPERTS_PER_GROUP, lg.shape[1]), F32)
    for gq in range(MOE_GROUPS):
        inner = inner + jnp.where(g_idx == gq, lg[8 * gq:8 * (gq + 1)], 0.0)
    ei = lax.broadcasted_iota(I32, inner.shape, 0)
    m1 = jnp.max(inner, axis=0, keepdims=True)
    i1 = jnp.min(jnp.where(inner == m1, ei, EXPERTS_PER_GROUP), axis=0, keepdims=True)
    rest = jnp.where(ei == i1, -jnp.inf, inner)
    m2 = jnp.max(rest, axis=0, keepdims=True)
    i2 = jnp.min(jnp.where(rest == m2, ei, EXPERTS_PER_GROUP), axis=0, keepdims=True)
    e2 = jnp.exp(m2 - m1)
    den = 1.0 + e2
    w1 = (1.0 / den) * g_prob
    w2 = (e2 / den) * g_prob
    zi = jnp.zeros((6, lg.shape[1]), I32)
    eid_ref[...] = jnp.concatenate([g_idx * EXPERTS_PER_GROUP + i1, g_idx * EXPERTS_PER_GROUP + i2, zi], axis=0)
    wgt_ref[...] = jnp.concatenate([w1, w2, jnp.zeros((6, lg.shape[1]), F32)], axis=0)


def _merge(alpha, x2, oa2, or2, wrg, wmg, wua, wur, wo, g1, b1, wrh, wrl, rb):
    M, D = x2.shape
    tm = MERGE_TM
    full = lambda a: pl.BlockSpec(a.shape, lambda i: (0,) * a.ndim)
    in_specs = [pl.BlockSpec((tm, D), lambda i: (i, 0)),
                pl.BlockSpec((tm, oa2.shape[1]), lambda i: (i, 0)),
                pl.BlockSpec((tm, or2.shape[1]), lambda i: (i, 0))] + [
        full(a) for a in (wrg, wmg, wua, wur, wo, g1, b1, wrh, wrl, rb)]
    out_shape = [jax.ShapeDtypeStruct((M, D), F32),
                 jax.ShapeDtypeStruct((8, M), I32),
                 jax.ShapeDtypeStruct((8, M), F32)]
    out_specs = [pl.BlockSpec((tm, D), lambda i: (i, 0)),
                 pl.BlockSpec((8, tm), lambda i: (0, i)),
                 pl.BlockSpec((8, tm), lambda i: (0, i))]
    return pl.pallas_call(
        functools.partial(_merge_body, alpha), out_shape=out_shape, grid=(M // tm,),
        in_specs=in_specs, out_specs=out_specs,
        compiler_params=_cparams(("parallel",)), name="merge",
    )(x2, oa2, or2, wrg, wmg, wua, wur, wo, g1, b1, wrh, wrl, rb)


def _rank_body(eid_ref, tri_ref, low_ref, pos_ref, cnt_ref, carry_ref, off_ref):
    p = pl.program_id(0)
    j = pl.program_id(1)
    e = eid_ref[...]
    rows = lax.broadcasted_iota(I32, (N_EXPERTS, e.shape[1]), 0)
    hit = rows == e
    per_expert = jnp.sum(jnp.where(hit, 1.0, 0.0), axis=1, keepdims=True)

    @pl.when((p == 0) & (j == 0))
    def _():
        carry_ref[...] = jnp.zeros(carry_ref.shape, F32)

    @pl.when(p == 0)
    def _():
        pos_ref[...] = jnp.zeros(pos_ref.shape, I32)
        carry_ref[...] = carry_ref[...] + per_expert
        cnt_ref[...] = carry_ref[...]

    @pl.when((p == 1) & (j == 0))
    def _():
        tiles = jnp.floor((carry_ref[...] + (EXPERT_TM - 1)) * (1.0 / EXPERT_TM))
        off_ref[...] = _nn(low_ref[...], tiles.astype(BF16)) * EXPERT_TM
        carry_ref[...] = jnp.zeros(carry_ref.shape, F32)

    @pl.when(p == 1)
    def _():
        onehot = jnp.where(hit, 1.0, 0.0).astype(BF16)
        incl = _nn(onehot, tri_ref[...])
        base = carry_ref[:, 0:1] + off_ref[:, 0:1] - 1.0
        pos_ref[...] = jnp.sum(jnp.where(hit, incl + base, 0.0), axis=0, keepdims=True).astype(I32)
        carry_ref[...] = carry_ref[...] + per_expert


def _rank(eid_flat, tri, low):
    n = eid_flat.shape[1]
    tb = RANK_TB
    return pl.pallas_call(
        _rank_body,
        out_shape=[jax.ShapeDtypeStruct((1, n), I32), jax.ShapeDtypeStruct((N_EXPERTS, 128), F32)],
        grid=(2, n // tb),
        in_specs=[pl.BlockSpec((1, tb), lambda p, j: (0, j)), pl.BlockSpec((tb, tb), lambda p, j: (0, 0)),
                  pl.BlockSpec(low.shape, lambda p, j: (0, 0))],
        out_specs=[pl.BlockSpec((1, tb), lambda p, j: (0, j * p)),
                   pl.BlockSpec((N_EXPERTS, 128), lambda p, j: (0, 0))],
        scratch_shapes=[pltpu.VMEM((N_EXPERTS, 128), F32), pltpu.VMEM((N_EXPERTS, 128), F32)],
        compiler_params=_cparams(("arbitrary", "arbitrary")), name="rank",
    )(eid_flat, tri, low)


def _dispatch_body(pos_ref, ztile_ref, x_ref, xs_ref, zbuf_ref, sem, zsem):
    tm = DISPATCH_TM
    m_tok = pos_ref.shape[0] // 2
    base = pl.program_id(0) * tm

    @pl.when(pl.program_id(0) == 0)
    def _():
        zbuf_ref[...] = jnp.zeros(zbuf_ref.shape, F32)

        def zero_copy(e):
            z = pl.multiple_of(jnp.maximum(ztile_ref[e], 0), EXPERT_TM)
            return pltpu.make_async_copy(zbuf_ref, xs_ref.at[pl.ds(z, EXPERT_TM), :], zsem)

        for e in range(N_EXPERTS):
            @pl.when(ztile_ref[e] >= 0)
            def _():
                zero_copy(e).start()
        for e in range(N_EXPERTS):
            @pl.when(ztile_ref[e] >= 0)
            def _():
                zero_copy(e).wait()

        def tail_copy(t):
            return pltpu.make_async_copy(
                zbuf_ref, xs_ref.at[pl.ds(pl.multiple_of(t * EXPERT_TM, EXPERT_TM), EXPERT_TM), :], zsem)

        n_tiles = xs_ref.shape[0] // EXPERT_TM
        lax.fori_loop(ztile_ref[N_EXPERTS], n_tiles, lambda t, c: (tail_copy(t).start(), c)[1], 0)
        lax.fori_loop(ztile_ref[N_EXPERTS], n_tiles, lambda t, c: (tail_copy(t).wait(), c)[1], 0)

    def row_copy(r, p):
        return pltpu.make_async_copy(x_ref.at[pl.ds(r, 1), :], xs_ref.at[pl.ds(p, 1), :], sem)

    for r in range(tm):
        row_copy(r, pos_ref[base + r]).start()
        row_copy(r, pos_ref[m_tok + base + r]).start(priority=1)

    def drain(r, carry):
        row_copy(0, 0).wait()
        row_copy(0, 0).wait()
        return carry

    lax.fori_loop(0, tm, drain, 0, unroll=16)


def _dispatch(pos, ztile, x1, n_pad):
    M, D = x1.shape
    tm = DISPATCH_TM
    grid_spec = pltpu.PrefetchScalarGridSpec(
        num_scalar_prefetch=2, grid=(M // tm,),
        in_specs=[pl.BlockSpec((tm, D), lambda i, pos, zt: (i, 0))],
        out_specs=pl.BlockSpec(memory_space=pl.ANY),
        scratch_shapes=[pltpu.VMEM((EXPERT_TM, D), F32), pltpu.SemaphoreType.DMA, pltpu.SemaphoreType.DMA],
    )
    return pl.pallas_call(
        _dispatch_body, out_shape=jax.ShapeDtypeStruct((n_pad, D), F32), grid_spec=grid_spec,
        compiler_params=_cparams(("arbitrary",), has_side_effects=True, disable_bounds_checks=True),
        name="dispatch",
    )(pos, ztile, x1)


def _experts_body(te_ref, nu_ref, seg_ref, nxt_ref, xs_ref, wg_hbm, wu_hbm, wd_hbm, ys_ref,
                  wgb_ref, wub_ref, wdb_ref, sg_ref, su_ref, sd_ref, sem):
    tm = EXPERT_TM
    k = pl.program_id(0)
    last = nu_ref[0] - 1

    def fetch(e, slot):
        return (pltpu.make_async_copy(wg_hbm.at[e], sg_ref.at[slot], sem.at[0, slot]),
                pltpu.make_async_copy(wu_hbm.at[e], su_ref.at[slot], sem.at[1, slot]),
                pltpu.make_async_copy(wd_hbm.at[e], sd_ref.at[slot], sem.at[2, slot]))

    @pl.when(k == 0)
    def _():
        for c in fetch(te_ref[0], 0):
            c.start()

    for s in range(EXPERT_STEP_TILES):
        t = EXPERT_STEP_TILES * k + s
        rows = slice(s * tm, (s + 1) * tm)

        @pl.when((t <= last) & (seg_ref[t] >= 0))
        def _():
            slot = seg_ref[t]
            for c in fetch(te_ref[t], slot):
                c.wait()
            wgb_ref[...] = sg_ref[slot].astype(BF16)
            wub_ref[...] = su_ref[slot].astype(BF16)
            wdb_ref[...] = sd_ref[slot].astype(BF16)

            @pl.when(nxt_ref[t] >= 0)
            def _():
                for c in fetch(nxt_ref[t], 1 - slot):
                    c.start()

        @pl.when(t <= last)
        def _():
            xb = xs_ref[rows, :].astype(BF16)
            hg = _nn(xb, wgb_ref[...])
            hu = _nn(xb, wub_ref[...])
            h = (jax.nn.silu(hg) * hu).astype(BF16)
            ys_ref[rows, :] = _nn(h, wdb_ref[...])

        @pl.when(t > last)
        def _():
            ys_ref[rows, :] = jnp.zeros((tm, ys_ref.shape[1]), F32)


def _experts(tile_expert, n_used, seg_slot, next_expert, xs, wg, wu, wd):
    npad, D = xs.shape
    tm = EXPERT_STEP_TILES * EXPERT_TM
    grid_spec = pltpu.PrefetchScalarGridSpec(
        num_scalar_prefetch=4, grid=(npad // tm,),
        in_specs=[pl.BlockSpec((tm, D), lambda i, te, nu, sg, nx:
                               (jnp.minimum(i, (nu[0] - 1) // EXPERT_STEP_TILES), 0)),
                  pl.BlockSpec(memory_space=pl.ANY), pl.BlockSpec(memory_space=pl.ANY),
                  pl.BlockSpec(memory_space=pl.ANY)],
        out_specs=pl.BlockSpec((tm, D), lambda i, te, nu, sg, nx: (i, 0)),
        scratch_shapes=[pltpu.VMEM((D, D_FF), BF16), pltpu.VMEM((D, D_FF), BF16), pltpu.VMEM((D_FF, D), BF16),
                        pltpu.VMEM((2, D, D_FF), F32), pltpu.VMEM((2, D, D_FF), F32),
                        pltpu.VMEM((2, D_FF, D), F32), pltpu.SemaphoreType.DMA((3, 2))],
    )
    return pl.pallas_call(
        _experts_body, out_shape=jax.ShapeDtypeStruct((npad, D), F32), grid_spec=grid_spec,
        compiler_params=_cparams(("arbitrary",)), name="experts",
    )(tile_expert, n_used, seg_slot, next_expert, xs, wg, wu, wd)


def _combine_body(alpha, pos_ref, x1_ref, w_ref, eye_ref, g2_ref, b2_ref, ys_ref, o_ref,
                  buf_a, buf_b, sem_a, sem_b):
    tm = COMBINE_TM
    m_tok = pos_ref.shape[0] // 2
    k = pl.program_id(0)
    n = pl.num_programs(0)

    def row_copies(buf, sem, r, base):
        return (pltpu.make_async_copy(ys_ref.at[pl.ds(pos_ref[base + r], 1), :], buf.at[0, pl.ds(r, 1), :], sem),
                pltpu.make_async_copy(ys_ref.at[pl.ds(pos_ref[m_tok + base + r], 1), :],
                                      buf.at[1, pl.ds(r, 1), :], sem))

    def issue_inline(buf, sem, tile):
        for r in range(tm):
            for queue, c in enumerate(row_copies(buf, sem, r, tile * tm)):
                c.start(priority=queue)

    def drain(buf, sem):
        def body(r, carry):
            for c in row_copies(buf, sem, 0, 0):
                c.wait()
            return carry
        lax.fori_loop(0, tm, body, 0, unroll=16)

    def compute(buf, s):
        rows = slice(s * tm, (s + 1) * tm)
        w = w_ref[:, rows]
        eye = eye_ref[...]
        w_a = w.astype(BF16)
        w_b = (w - w_a.astype(F32)).astype(BF16)
        w_c = (w - w_a.astype(F32) - w_b.astype(F32)).astype(BF16)
        wcol = _nt(eye, w_a) + _nt(eye, w_b) + _nt(eye, w_c)
        moe = buf[0] * wcol[:, 0:1] + buf[1] * wcol[:, 1:2]
        o_ref[rows, :] = _layer_norm(alpha * x1_ref[rows, :] + moe, g2_ref[...], b2_ref[...])

    @pl.when(k == 0)
    def _():
        def body(r, carry):
            for queue, c in enumerate(row_copies(buf_a, sem_a, r, 0)):
                c.start(priority=queue)
            return carry
        lax.fori_loop(0, tm, body, 0, unroll=16)

    drain(buf_a, sem_a)
    issue_inline(buf_b, sem_b, 2 * k + 1)
    compute(buf_a, 0)
    drain(buf_b, sem_b)
    issue_inline(buf_a, sem_a, jnp.minimum(2 * k + 2, 2 * n - 2))
    compute(buf_b, 1)

    @pl.when(k == n - 1)
    def _():
        drain(buf_a, sem_a)


def _combine(alpha, pos, x1, wgt, eye, g2, b2, ys):
    M, D = x1.shape
    tm = COMBINE_TM
    grid_spec = pltpu.PrefetchScalarGridSpec(
        num_scalar_prefetch=1, grid=(M // (2 * tm),),
        in_specs=[pl.BlockSpec((2 * tm, D), lambda i, pos: (i, 0)),
                  pl.BlockSpec((8, 2 * tm), lambda i, pos: (0, i)),
                  pl.BlockSpec((tm, tm), lambda i, pos: (0, 0)),
                  pl.BlockSpec((1, D), lambda i, pos: (0, 0)),
                  pl.BlockSpec((1, D), lambda i, pos: (0, 0)),
                  pl.BlockSpec(memory_space=pl.ANY)],
        out_specs=pl.BlockSpec((2 * tm, D), lambda i, pos: (i, 0)),
        scratch_shapes=[pltpu.VMEM((2, tm, D), F32), pltpu.VMEM((2, tm, D), F32),
                        pltpu.SemaphoreType.DMA, pltpu.SemaphoreType.DMA],
    )
    return pl.pallas_call(
        functools.partial(_combine_body, alpha), out_shape=jax.ShapeDtypeStruct((M, D), F32),
        grid_spec=grid_spec, compiler_params=_cparams(("arbitrary",), disable_bounds_checks=True),
        name="combine",
    )(pos, x1, wgt, eye, g2, b2, ys)


def _rope_angles(pos, dim):
    half = dim // 2
    inv_freq = ROPE_THETA ** (-np.arange(half, dtype=np.float64) * 2.0 / dim)
    return pos.astype(np.float64)[:, None] * inv_freq[None, :]


def _tables(T):
    f32 = lambda a: jnp.asarray(np.ascontiguousarray(a), dtype=F32)
    ang = _rope_angles(np.arange(T), NSA_HD)
    cos, sin = np.cos(ang), np.sin(ang)
    c64 = np.tile(cos, (1, 4))
    s64 = np.tile(np.concatenate([-sin, sin], axis=1), (1, 2))
    ang2 = _rope_angles(np.arange(T), RET_DK)
    cos2, sin2 = np.cos(ang2), np.sin(ang2)
    c128 = np.tile(cos2, (1, 2))
    s128 = np.concatenate([-sin2, sin2], axis=1)
    n_rows = T // CMP_STRIDE
    angc = _rope_angles(np.arange(n_rows) * CMP_STRIDE + CMP_BLOCK - 1, NSA_HD)
    cc = np.tile(np.cos(angc), (1, 2))
    sc = np.tile(np.sin(angc), (1, 2))
    return tuple(f32(a) for a in (c64, s64, c128, s128, cos.T, sin.T, cos2.T, sin2.T)), f32(cc), f32(sc)


def _overlap_matrix(T):
    n_rows = T // CMP_STRIDE
    n_sel = T // SLC_BLOCK
    cmp_start = np.arange(n_rows) * CMP_STRIDE
    sel_start = np.arange(n_sel) * SLC_BLOCK
    ov = np.clip(np.minimum(cmp_start[None, :] + CMP_BLOCK, sel_start[:, None] + SLC_BLOCK)
                 - np.maximum(cmp_start[None, :], sel_start[:, None]), 0, None)
    return jnp.asarray(ov.astype(np.float32) / CMP_STRIDE, dtype=BF16)


def _retention_tables():
    C = RET_CHUNK
    f32 = lambda a: jnp.asarray(np.ascontiguousarray(a), dtype=F32)
    gamma = 1.0 - 2.0 ** (-5.0 - np.arange(RET_HEADS, dtype=np.float64))
    log_g = np.log(gamma)
    i = np.arange(C, dtype=np.float64)
    diff = i[:, None] - i[None, :]
    dec = np.where(diff >= 0, np.exp(np.maximum(diff, 0.0) * log_g[:, None, None]), 0.0)
    xi = np.exp((i + 1.0) * log_g[:, None])
    zeta = np.exp((C - 1.0 - i) * log_g[:, None])
    cd = np.exp(C * log_g)
    xi_b = np.broadcast_to(xi[:, :, None], (RET_HEADS, C, RET_DV))
    cd_b = np.broadcast_to(cd[:, None, None], (RET_HEADS, 1, RET_DV))
    return f32(dec), f32(xi_b), f32(zeta[:, None, :]), f32(cd_b)


def _rot_half_cols(w):
    half = w.shape[-1] // 2
    return jnp.concatenate([-w[..., half:], w[..., :half]], axis=-1)


def kernel(x, w_in, cmp_pos_k, cmp_k_w1, cmp_k_b1, cmp_k_w2, cmp_k_b2, cmp_pos_v, cmp_v_w1, cmp_v_b1, cmp_v_w2, cmp_v_b2, ret_gn_g, ret_gn_b, w_up_attn, w_up_ret, w_out, ln1_g, ln1_b, router_group_w, router_group_b, router_inner_w, router_inner_b, expert_w_gate, expert_w_up, expert_w_down, ln2_g, ln2_b):
    B, T, D = x.shape
    M = B * T
    depth = w_in.shape[0]
    alpha = (2.0 * depth) ** 0.25
    tabs, cc, sc = _tables(T)
    ov = _overlap_matrix(T)
    dec, xi_b, zeta, cd_b = _retention_tables()
    tri = jnp.asarray(np.triu(np.ones((RANK_TB, RANK_TB), np.float32)), dtype=BF16)
    low = jnp.asarray(np.tril(np.ones((N_EXPERTS, N_EXPERTS), np.float32), k=-1), dtype=BF16)
    eye = jnp.asarray(np.eye(COMBINE_TM, dtype=np.float32), dtype=BF16)
    n_pad = 2 * M + N_EXPERTS * EXPERT_TM
    n_tiles = n_pad // EXPERT_TM

    for l in range(depth):
        w = w_in[l]
        col = lambda n: w[:, _OFF[n][0]:_OFF[n][1]]
        wn = jnp.concatenate([col("cmp_k"), col("cmp_v"), col("slc_k"), col("win_k"), col("ret_q"), col("ret_v")],
                             axis=1).astype(BF16)
        wt = jnp.concatenate([col("nsa_q"), col("slc_v"), col("win_v"), col("ret_k"), col("nsa_gate"),
                              jnp.zeros((D, 8), F32)], axis=1).T.astype(BF16)
        qT, ckv, sk, wk, svT, wvT, gT, rq, rkT, rv = _proj(x, wn, wt, tabs)

        kc, vcT = _compress(
            ckv, cmp_pos_k[l].reshape(2, -1), cmp_pos_v[l].reshape(2, -1),
            cmp_k_w1[l], cmp_k_b1[l][None, :], cmp_k_w2[l], _rot_half_cols(cmp_k_w2[l]),
            cmp_k_b2[l][None, :], _rot_half_cols(cmp_k_b2[l])[None, :], cc, sc,
            cmp_v_w1[l], cmp_v_b1[l][None, :], cmp_v_w2[l].T, cmp_v_b2[l][:, None])
        o_attn = _nsa(qT, kc, vcT, ov, sk, svT, wk, wvT, gT)
        o_ret = _retention(rq, rkT, rv, dec, xi_b, zeta, cd_b, ret_gn_g[l][None, :], ret_gn_b[l][None, :])

        wr = jnp.concatenate([router_inner_w[l].transpose(0, 2, 1).reshape(N_EXPERTS, D),
                              router_group_w[l].T, jnp.zeros((4, D), F32)], axis=0)
        wrh = wr.astype(BF16)
        wrl = (wr - wrh.astype(F32)).astype(BF16)
        rb = jnp.concatenate([router_inner_b[l].reshape(-1), router_group_b[l], jnp.zeros((4,), F32)])[:, None]
        x1, eid, wgt = _merge(
            alpha, x.reshape(M, D), o_attn.reshape(M, -1), o_ret.reshape(M, -1),
            col("ret_gate").astype(BF16), col("merge_gate").astype(BF16), w_up_attn[l].astype(BF16),
            w_up_ret[l].astype(BF16), w_out[l].astype(BF16), ln1_g[l][None, :], ln1_b[l][None, :], wrh, wrl, rb)

        eid_flat = eid[:2].reshape(1, 2 * M)
        pos, cnt = _rank(eid_flat, tri, low)
        pos = pos[0]
        counts = cnt[:, 0].astype(I32)
        tiles_per = (counts + EXPERT_TM - 1) // EXPERT_TM
        tile_end = jnp.cumsum(tiles_per)
        tile_ids = jnp.arange(n_tiles, dtype=I32)
        tile_expert = jnp.minimum(jnp.sum((tile_end[None, :] <= tile_ids[:, None]).astype(I32), axis=1),
                                  N_EXPERTS - 1).astype(I32)
        n_used = tile_end[-1:].astype(I32)

        ztile = jnp.concatenate([jnp.where(tiles_per > 0, (tile_end - 1) * EXPERT_TM, -1), tile_end[-1:]]).astype(I32)
        xs = _dispatch(pos, ztile, x1, n_pad)
        prev_expert = jnp.concatenate([jnp.full((1,), -1, I32), tile_expert[:-1]])
        is_first = (tile_ids < n_used[0]) & (tile_expert != prev_expert)
        seg_slot = jnp.where(is_first, (jnp.cumsum(is_first.astype(I32)) - 1) % 2, -1).astype(I32)
        own = tile_expert[:, None] == jnp.arange(N_EXPERTS, dtype=I32)[None, :]
        seg_end = jnp.sum(jnp.where(own, tile_end[None, :], 0), axis=1)
        expert_at_end = jnp.minimum(jnp.sum((tile_end[None, :] <= seg_end[:, None]).astype(I32), axis=1),
                                    N_EXPERTS - 1)
        next_expert = jnp.where(is_first & (seg_end < n_used[0]), expert_at_end, -1).astype(I32)
        ys = _experts(tile_expert, n_used, seg_slot, next_expert, xs,
                      expert_w_gate[l], expert_w_up[l], expert_w_down[l])
        x = _combine(alpha, pos, x1, wgt, eye, ln2_g[l][None, :], ln2_b[l][None, :], ys).reshape(B, T, D)
    return x
```

```python
import functools

import numpy as np
import jax
import jax.numpy as jnp
from jax import lax
from jax.experimental import pallas as pl
from jax.experimental.pallas import tpu as pltpu

F32 = jnp.float32
BF16 = jnp.bfloat16
I32 = jnp.int32

D_MODEL = 1024
NSA_HEADS = 8
NSA_HD = 64
NSA_GROUPS = 2
NSA_HPG = NSA_HEADS // NSA_GROUPS
CMP_BLOCK = 32
CMP_STRIDE = 16
CMP_HIDDEN = 256
SLC_BLOCK = 64
SLC_TOPK = 16
WINDOW = 512
RET_HEADS = 4
RET_DK = 128
RET_DV = 256
RET_CHUNK = 128
MOE_GROUPS = 4
EXPERTS_PER_GROUP = 8
N_EXPERTS = MOE_GROUPS * EXPERTS_PER_GROUP
D_FF = 512
ROPE_THETA = 10000.0
LN_EPS = 1e-5
GN_EPS = 1e-5
NEG_INF = -1e30
LOG2_E = 1.4426950408889634

VMEM_LIMIT_V7X = 56 * 1024 * 1024

_OFF = {}
_o = 0
for _n, _w in (("nsa_q", 512), ("cmp_k", 128), ("cmp_v", 128), ("slc_k", 128), ("slc_v", 128),
               ("win_k", 128), ("win_v", 128), ("nsa_gate", 24), ("ret_q", 512), ("ret_k", 512),
               ("ret_v", 1024), ("ret_gate", 1024), ("merge_gate", 2048)):
    _OFF[_n] = (_o, _o + _w)
    _o += _w

PROJ_TM = 512
NSA_TQ = 256
NSA_KC = 256
NSA_VROWS = 80
RET_STEP_CHUNKS = 8
RET_STEP_BATCH = 2
MERGE_TM = 512
RANK_TB = 1024
RANK_STEP_BLOCKS = 4
DISPATCH_TM = 1024
EXPERT_TM = 256
EXPERT_STEP_TILES = 4
COMBINE_TM = 256


def _cparams(sem, **kw):
    return pltpu.CompilerParams(dimension_semantics=sem, vmem_limit_bytes=VMEM_LIMIT_V7X, **kw)


def _nt(a, b):
    return lax.dot_general(a, b, (((1,), (1,)), ((), ())), preferred_element_type=F32)


def _nn(a, b):
    return jnp.dot(a, b, preferred_element_type=F32)


def _proj_body(x_ref, wn_ref, wt_ref, c64_ref, s64_ref, c128_ref, s128_ref,
               ct64_ref, st64_ref, ct128_ref, st128_ref,
               qT_ref, ckv_ref, sk_ref, wk_ref, svT_ref, wvT_ref, gT_ref, rq_ref, rkT_ref, rv_ref, ckv_scr):
    tm = PROJ_TM
    xb = x_ref[0].astype(BF16)

    def nn(a, b):
        return _nn(xb, wn_ref[:, a:b])

    def nt(a, b):
        return _nt(wt_ref[a:b, :], xb)

    ckv = nn(0, 256)
    for c in range(2):
        ckv_scr[c] = ckv[:, 128 * c:128 * (c + 1)]
    for l in range(CMP_STRIDE):
        for c in range(2):
            rows = ckv_scr[c, pl.ds(l, tm // CMP_STRIDE, stride=CMP_STRIDE), :]
            for jj in range(2):
                ckv_ref[0, 2 * c + jj, :, 64 * l:64 * (l + 1)] = rows[:, 64 * jj:64 * (jj + 1)]

    lane = lax.broadcasted_iota(I32, (tm, 128), 1)
    first = (lane % 64) < 32
    c64 = c64_ref[...]
    s64 = s64_ref[...]

    def rope64(k):
        rot = jnp.where(first, pltpu.roll(k, 96, 1), pltpu.roll(k, 32, 1))
        return k * c64 + rot * s64

    sk = rope64(nn(256, 384)).astype(BF16)
    sk_ref[0, 0] = sk[:, :64]
    sk_ref[0, 1] = sk[:, 64:]
    wk = rope64(nn(384, 512)).astype(BF16)
    wk_ref[0, 0] = wk[:, :64]
    wk_ref[0, 1] = wk[:, 64:]

    c128 = c128_ref[...]
    s128 = s128_ref[...]
    rq = nn(512, 1024)
    for h in range(RET_HEADS):
        ch = rq[:, 128 * h:128 * (h + 1)]
        rq_ref[0, :, 128 * h:128 * (h + 1)] = (ch * c128 + pltpu.roll(ch, 64, 1) * s128).astype(BF16)
    rv_ref[0] = nn(1024, 2048).astype(BF16)

    ct = ct64_ref[...]
    st = st64_ref[...]
    qT = nt(0, 512)
    scale_q = NSA_HD ** -0.5 * LOG2_E
    tq = NSA_TQ
    for hh in range(NSA_HEADS):
        g, h = divmod(hh, NSA_HPG)
        x1 = qT[64 * hh:64 * hh + 32]
        x2 = qT[64 * hh + 32:64 * hh + 64]
        o1 = ((x1 * ct - x2 * st) * scale_q).astype(BF16)
        o2 = ((x1 * st + x2 * ct) * scale_q).astype(BF16)
        for it in range(tm // tq):
            qT_ref[0, g, it, 0:32, h * tq:(h + 1) * tq] = o1[:, it * tq:(it + 1) * tq]
            qT_ref[0, g, it, 32:64, h * tq:(h + 1) * tq] = o2[:, it * tq:(it + 1) * tq]

    svT = nt(512, 640).astype(BF16)
    wvT = nt(640, 768).astype(BF16)
    row16 = lax.broadcasted_iota(I32, (NSA_VROWS - NSA_HD, NSA_KC), 0)
    ones_blk = jnp.where(row16 == 0, 1.0, 0.0).astype(BF16)
    for c in range(tm // NSA_KC):
        for g in range(NSA_GROUPS):
            for vT, ref in ((svT, svT_ref), (wvT, wvT_ref)):
                ref[0, c, g, 0:NSA_HD, :] = vT[64 * g:64 * (g + 1), NSA_KC * c:NSA_KC * (c + 1)]
                ref[0, c, g, NSA_HD:NSA_VROWS, :] = ones_blk

    ct2 = ct128_ref[...]
    st2 = st128_ref[...]
    rkT = nt(768, 1280)
    scale_k = RET_DK ** -0.5
    for h in range(RET_HEADS):
        x1 = rkT[128 * h:128 * h + 64]
        x2 = rkT[128 * h + 64:128 * h + 128]
        o1 = (x1 * ct2 - x2 * st2) * scale_k
        o2 = (x1 * st2 + x2 * ct2) * scale_k
        for c in range(tm // RET_CHUNK):
            rkT_ref[0, c, 128 * h:128 * h + 64, :] = o1[:, 128 * c:128 * (c + 1)]
            rkT_ref[0, c, 128 * h + 64:128 * h + 128, :] = o2[:, 128 * c:128 * (c + 1)]

    gT_ref[0] = jax.nn.sigmoid(nt(1280, 1312))


def _proj(x, wn, wt, tabs):
    B, T, D = x.shape
    tm = PROJ_TM
    c64, s64, c128, s128, ct64, st64, ct128, st128 = tabs
    const = lambda b, i: (0, 0)
    in_specs = [
        pl.BlockSpec((1, tm, D), lambda b, i: (b, i, 0)),
        pl.BlockSpec(wn.shape, const),
        pl.BlockSpec(wt.shape, const),
        pl.BlockSpec((tm, 128), lambda b, i: (i, 0)),
        pl.BlockSpec((tm, 128), lambda b, i: (i, 0)),
        pl.BlockSpec((tm, 128), lambda b, i: (i, 0)),
        pl.BlockSpec((tm, 128), lambda b, i: (i, 0)),
        pl.BlockSpec((32, tm), lambda b, i: (0, i)),
        pl.BlockSpec((32, tm), lambda b, i: (0, i)),
        pl.BlockSpec((64, tm), lambda b, i: (0, i)),
        pl.BlockSpec((64, tm), lambda b, i: (0, i)),
    ]
    out_shape = [
        jax.ShapeDtypeStruct((B, NSA_GROUPS, T // NSA_TQ, NSA_HD, NSA_HPG * NSA_TQ), BF16),
        jax.ShapeDtypeStruct((B, 4, T // CMP_STRIDE, CMP_STRIDE * NSA_HD), F32),
        jax.ShapeDtypeStruct((B, 2, T, 64), BF16),
        jax.ShapeDtypeStruct((B, 2, T, 64), BF16),
        jax.ShapeDtypeStruct((B, T // NSA_KC, NSA_GROUPS, NSA_VROWS, NSA_KC), BF16),
        jax.ShapeDtypeStruct((B, T // NSA_KC, NSA_GROUPS, NSA_VROWS, NSA_KC), BF16),
        jax.ShapeDtypeStruct((B, 32, T), F32),
        jax.ShapeDtypeStruct((B, T, 512), BF16),
        jax.ShapeDtypeStruct((B, T // RET_CHUNK, 512, RET_CHUNK), F32),
        jax.ShapeDtypeStruct((B, T, 1024), BF16),
    ]
    out_specs = [
        pl.BlockSpec((1, NSA_GROUPS, tm // NSA_TQ, NSA_HD, NSA_HPG * NSA_TQ), lambda b, i: (b, 0, i, 0, 0)),
        pl.BlockSpec((1, 4, tm // CMP_STRIDE, CMP_STRIDE * NSA_HD), lambda b, i: (b, 0, i, 0)),
        pl.BlockSpec((1, 2, tm, 64), lambda b, i: (b, 0, i, 0)),
        pl.BlockSpec((1, 2, tm, 64), lambda b, i: (b, 0, i, 0)),
        pl.BlockSpec((1, tm // NSA_KC, NSA_GROUPS, NSA_VROWS, NSA_KC), lambda b, i: (b, i, 0, 0, 0)),
        pl.BlockSpec((1, tm // NSA_KC, NSA_GROUPS, NSA_VROWS, NSA_KC), lambda b, i: (b, i, 0, 0, 0)),
        pl.BlockSpec((1, 32, tm), lambda b, i: (b, 0, i)),
        pl.BlockSpec((1, tm, 512), lambda b, i: (b, i, 0)),
        pl.BlockSpec((1, tm // RET_CHUNK, 512, RET_CHUNK), lambda b, i: (b, i, 0, 0)),
        pl.BlockSpec((1, tm, 1024), lambda b, i: (b, i, 0)),
    ]
    return pl.pallas_call(
        _proj_body, out_shape=out_shape, grid=(B, T // tm), in_specs=in_specs, out_specs=out_specs,
        scratch_shapes=[pltpu.VMEM((2, tm, 128), F32)],
        compiler_params=_cparams(("parallel", "parallel")), name="proj",
    )(x, wn, wt, c64, s64, c128, s128, ct64, st64, ct128, st128)


def _compress_body(uk_ref, uv_ref, posk_ref, posv_ref, w1k_ref, b1k_ref, w2k_ref, w2kr_ref, b2k_ref, b2kr_ref,
                   cc_ref, sc_ref, w1v_ref, b1v_ref, w2vT_ref, b2vT_ref, kc_ref, vcT_ref):
    half = CMP_STRIDE * NSA_HD

    def hidden(u_ref, pos_ref, w1_ref, b1_ref):
        u = u_ref[0, 0]
        top = (u + pos_ref[0:1, :]).astype(BF16)
        bot = (u + pos_ref[1:2, :]).astype(BF16)
        a = _nn(top, w1_ref[0:half, :].astype(BF16))
        bm = _nn(bot, w1_ref[half:2 * half, :].astype(BF16))
        n = bm.shape[0]
        return jax.nn.gelu(a + pltpu.roll(bm, n - 1, 0) + b1_ref[...])

    hk = hidden(uk_ref, posk_ref, w1k_ref, b1k_ref).astype(BF16)
    k = _nn(hk, w2k_ref[...].astype(BF16)) + b2k_ref[...]
    kr = _nn(hk, w2kr_ref[...].astype(BF16)) + b2kr_ref[...]
    kc_ref[0, 0] = (k * cc_ref[...] + kr * sc_ref[...]).astype(BF16)

    hv = hidden(uv_ref, posv_ref, w1v_ref, b1v_ref).astype(BF16)
    vcT_ref[0, 0] = (_nt(w2vT_ref[...].astype(BF16), hv) + b2vT_ref[...]).astype(BF16)


def _compress(ckv, posk, posv, w1k, b1k, w2k, w2kr, b2k, b2kr, cc, sc, w1v, b1v, w2vT, b2vT):
    B, _, nr, _ = ckv.shape
    u = ckv
    full = lambda a: pl.BlockSpec(a.shape, lambda b, g: (0,) * a.ndim)
    in_specs = [
        pl.BlockSpec((1, 1, nr, 1024), lambda b, g: (b, g, 0, 0)),
        pl.BlockSpec((1, 1, nr, 1024), lambda b, g: (b, g + 2, 0, 0)),
    ] + [full(a) for a in (posk, posv, w1k, b1k, w2k, w2kr, b2k, b2kr, cc, sc, w1v, b1v, w2vT, b2vT)]
    out_shape = [jax.ShapeDtypeStruct((B, NSA_GROUPS, nr, NSA_HD), BF16),
                 jax.ShapeDtypeStruct((B, NSA_GROUPS, NSA_HD, nr), BF16)]
    out_specs = [pl.BlockSpec((1, 1, nr, NSA_HD), lambda b, g: (b, g, 0, 0)),
                 pl.BlockSpec((1, 1, NSA_HD, nr), lambda b, g: (b, g, 0, 0))]
    return pl.pallas_call(
        _compress_body, out_shape=out_shape, grid=(B, NSA_GROUPS), in_specs=in_specs, out_specs=out_specs,
        compiler_params=_cparams(("parallel", "parallel")), name="compress",
    )(u, u, posk, posv, w1k, b1k, w2k, w2kr, b2k, b2kr, cc, sc, w1v, b1v, w2vT, b2vT)


def _nsa_body(q_ref, kc_ref, vcT_ref, ov_ref, sk_ref, svT_ref, wk_ref, wvT_ref, gT_ref, lo_ref, hi_ref, o_ref,
              score_ref, bias_ref, acc_ref, tot_ref, s_ref):
    tq = NSA_TQ
    kc_n = NSA_KC
    hq = NSA_HPG * tq
    groups = range(NSA_GROUPS)
    i = pl.program_id(1)
    t0 = i * tq
    t_row = t0 + lax.broadcasted_iota(I32, (1, tq), 1)
    t_row4 = t0 + lax.broadcasted_iota(I32, (1, hq), 1) % tq
    n_cmp = kc_ref.shape[2]
    n_sel = ov_ref.shape[0]
    q = [q_ref[0, g, 0] for g in groups]

    def gate4(g, br):
        return jnp.concatenate([gT_ref[0, (g * NSA_HPG + h) * 3 + br:(g * NSA_HPG + h) * 3 + br + 1, :]
                                for h in range(NSA_HPG)], axis=1)

    cmp_end = lax.broadcasted_iota(I32, (n_cmp, 1), 0) * CMP_STRIDE + (CMP_BLOCK - 1)
    cbias = jnp.where(cmp_end <= t_row4, 0.0, NEG_INF)
    any_valid = jnp.where(t_row4 >= CMP_BLOCK - 1, 1.0, 0.0)
    blk_t = t_row // SLC_BLOCK
    sub = lax.broadcasted_iota(I32, (8, 1), 0)

    def select_blocks(g, l_cmp, l_sel):
        s = _nn(kc_ref[0, g, 0:l_cmp, :], q[g]) + cbias[0:l_cmp]
        e = jnp.exp2(s - jnp.max(s, axis=0, keepdims=True))
        p = e * (any_valid / jnp.sum(e, axis=0, keepdims=True))
        tot_ref[g] = gate4(g, 0) * _nn(vcT_ref[0, g, :, 0:l_cmp], p.astype(BF16))
        psum = p[:, 0:tq]
        for h in range(1, NSA_HPG):
            psum = psum + p[:, h * tq:(h + 1) * tq]
        p_hi = psum.astype(BF16)
        p_lo = (psum - p_hi.astype(F32)).astype(BF16)
        ov = ov_ref[0:l_sel, 0:l_cmp]
        imp = _nn(ov, p_hi) + _nn(ov, p_lo)
        jc = lax.broadcasted_iota(I32, (l_sel, 1), 0)
        forced = (jc == 0) | (jc == blk_t) | (jc == blk_t - 1)
        score = jnp.where(forced, 1e9, jnp.where(jc <= blk_t, imp, -1e9)).astype(F32)
        score_ref[g, 0:l_sel, :] = score
        n_slab = l_sel // 8
        slabs = [score[8 * v:8 * (v + 1)] for v in range(n_slab)]
        cnt = [jnp.zeros((8, tq), F32) for _ in range(n_slab)]
        for r in range(l_sel):
            row = jnp.broadcast_to(score_ref[g, r:r + 1, :], (8, tq))
            for v in range(n_slab):
                if r < 8 * v:
                    ahead = jnp.where(row >= slabs[v], 1.0, 0.0)
                elif r >= 8 * (v + 1):
                    ahead = jnp.where(row > slabs[v], 1.0, 0.0)
                else:
                    ahead = jnp.where(sub + 8 * v > r, jnp.where(row >= slabs[v], 1.0, 0.0),
                                      jnp.where(row > slabs[v], 1.0, 0.0))
                cnt[v] = cnt[v] + ahead
        for v in range(n_slab):
            bias_ref[g, 8 * v:8 * (v + 1), :] = jnp.where(
                cnt[v] < float(SLC_TOPK), jnp.where(sub + 8 * v <= blk_t, 0.0, NEG_INF), NEG_INF)
        if l_sel < n_sel:
            bias_ref[g, l_sel:n_sel, :] = jnp.full((n_sel - l_sel, tq), NEG_INF, F32)

    n_var = 4
    tiles_per_var = (n_sel * SLC_BLOCK // tq) // n_var
    for var in range(n_var):
        l_sel = (var + 1) * tiles_per_var * (tq // SLC_BLOCK)
        l_cmp = min(n_cmp, -(-((var + 1) * tiles_per_var * (tq // CMP_STRIDE)) // 128) * 128)

        @pl.when(i // tiles_per_var == var)
        def _():
            for g in groups:
                select_blocks(g, l_cmp, l_sel)

    def qk(k_ref, g, c):
        k_c = k_ref[0, g, pl.ds(pl.multiple_of(c * kc_n, kc_n), kc_n), :]
        return [_nn(k_c, q[g][:, h * tq:(h + 1) * tq]) for h in range(NSA_HPG)]

    def stage_a(scores, add_bias, m):
        biased = [add_bias(scores[h]) for h in range(NSA_HPG)]
        m_new = tuple(jnp.maximum(m[h], jnp.max(biased[h], axis=0, keepdims=True)) for h in range(NSA_HPG))
        return biased, m_new

    def stage_b(g, biased, vT_c, m_old, m_new):
        for h in range(NSA_HPG):
            hs = slice(h * tq, (h + 1) * tq)
            alpha = jnp.exp2(m_old[h] - m_new[h])
            p = jnp.exp2((biased[h] - m_new[h]).astype(BF16))
            acc_ref[g, :, hs] = alpha * acc_ref[g, :, hs] + _nn(vT_c, p)

    nb = kc_n // SLC_BLOCK

    def block_bias(g, c, s):
        return jnp.concatenate([s[SLC_BLOCK * b:SLC_BLOCK * (b + 1)] + bias_ref[g, pl.ds(c * nb + b, 1), :]
                                for b in range(nb)], axis=0)

    def finish(g, br):
        acc = acc_ref[g]
        tot_ref[g] = tot_ref[g] + gate4(g, br) * (acc[0:NSA_HD] / acc[NSA_HD:NSA_HD + 1])
        acc_ref[g] = jnp.zeros(acc.shape, F32)

    def park(g, biased):
        for h in range(NSA_HPG):
            s_ref[g, h] = biased[h]

    def parked(g):
        return [s_ref[g, h] for h in range(NSA_HPG)]

    m0 = tuple(jnp.full((1, tq), NEG_INF, F32) for _ in range(NSA_HPG))

    acc_ref[...] = jnp.zeros(acc_ref.shape, F32)
    m_new = []
    for g in groups:
        biased, m_g = stage_a(qk(sk_ref, g, i), lambda s, g=g: block_bias(g, i, s) + lo_ref[...], m0)
        park(g, biased)
        m_new.append(m_g)

    def slc_chunk(c, carry):
        prev, m_old, m_new = carry
        scores = [qk(sk_ref, g, c) for g in groups]
        for g in groups:
            stage_b(g, parked(g), svT_ref[0, prev, g], m_old[g], m_new[g])
        m_next = []
        for g in groups:
            biased, m_g = stage_a(scores[g], functools.partial(block_bias, g, c), m_new[g])
            park(g, biased)
            m_next.append(m_g)
        return c, m_new, tuple(m_next)

    prev, m_old, m_new = lax.fori_loop(0, i, slc_chunk, (i, (m0,) * NSA_GROUPS, tuple(m_new)))

    c_far = jnp.maximum(i - 2, 0)
    c_near = jnp.maximum(i - 1, 0)
    pen_far = jnp.where(i >= 2, 0.0, NEG_INF).astype(F32)
    pen_near = jnp.where(i >= 1, 0.0, NEG_INF).astype(F32)

    scores = [qk(wk_ref, g, c_far) for g in groups]
    far = []
    for g in groups:
        stage_b(g, parked(g), svT_ref[0, prev, g], m_old[g], m_new[g])
        far.append(stage_a(scores[g], lambda s: s + (hi_ref[...] + pen_far), m0))
        finish(g, 1)

    scores = [qk(wk_ref, g, c_near) for g in groups]
    near = []
    for g in groups:
        stage_b(g, far[g][0], wvT_ref[0, c_far, g], m0, far[g][1])
        near.append(stage_a(scores[g], lambda s: s + pen_near, far[g][1]))
    scores = [qk(wk_ref, g, i) for g in groups]
    for g in groups:
        stage_b(g, near[g][0], wvT_ref[0, c_near, g], far[g][1], near[g][1])
        b_diag, m_diag = stage_a(scores[g], lambda s: s + lo_ref[...], near[g][1])
        stage_b(g, b_diag, wvT_ref[0, i, g], near[g][1], m_diag)
        finish(g, 2)

    o_ref[0] = jnp.concatenate([tot_ref[g, :, h * tq:(h + 1) * tq] for g in groups for h in range(NSA_HPG)],
                               axis=0).T.astype(BF16)


def _nsa(qT, kc, vcT, ov, sk, svT, wk, wvT, gT):
    B, _, T = gT.shape
    tq = NSA_TQ
    assert NSA_KC == tq and WINDOW == 2 * NSA_KC
    nr = kc.shape[2]
    nch = T // NSA_KC
    kk = np.arange(NSA_KC)[:, None]
    tt = np.arange(tq)[None, :]
    lo = jnp.asarray(np.where(kk <= tt, 0.0, NEG_INF), dtype=F32)
    hi = jnp.asarray(np.where(kk > tt, 0.0, NEG_INF), dtype=F32)
    G = NSA_GROUPS
    in_specs = [
        pl.BlockSpec((1, G, 1, NSA_HD, NSA_HPG * tq), lambda b, i: (b, 0, i, 0, 0)),
        pl.BlockSpec((1, G, nr, NSA_HD), lambda b, i: (b, 0, 0, 0)),
        pl.BlockSpec((1, G, NSA_HD, nr), lambda b, i: (b, 0, 0, 0)),
        pl.BlockSpec(ov.shape, lambda b, i: (0, 0)),
        pl.BlockSpec((1, G, T, NSA_HD), lambda b, i: (b, 0, 0, 0)),
        pl.BlockSpec((1, nch, G, NSA_VROWS, NSA_KC), lambda b, i: (b, 0, 0, 0, 0)),
        pl.BlockSpec((1, G, T, NSA_HD), lambda b, i: (b, 0, 0, 0)),
        pl.BlockSpec((1, nch, G, NSA_VROWS, NSA_KC), lambda b, i: (b, 0, 0, 0, 0)),
        pl.BlockSpec((1, 32, tq), lambda b, i: (b, 0, i)),
        pl.BlockSpec(lo.shape, lambda b, i: (0, 0)),
        pl.BlockSpec(hi.shape, lambda b, i: (0, 0)),
    ]
    n_sel = T // SLC_BLOCK
    hq = NSA_HPG * tq
    return pl.pallas_call(
        _nsa_body, out_shape=jax.ShapeDtypeStruct((B, T, NSA_HEADS * NSA_HD), BF16),
        grid=(B, T // tq), in_specs=in_specs,
        out_specs=pl.BlockSpec((1, tq, NSA_HEADS * NSA_HD), lambda b, i: (b, i, 0)),
        scratch_shapes=[pltpu.VMEM((G, n_sel, tq), F32), pltpu.VMEM((G, n_sel, tq), F32),
                        pltpu.VMEM((G, NSA_VROWS, hq), F32), pltpu.VMEM((G, NSA_HD, hq), F32),
                        pltpu.VMEM((G, NSA_HPG, NSA_KC, tq), F32)],
        compiler_params=_cparams(("parallel", "parallel")), name="nsa",
    )(qT, kc, vcT, ov, sk, svT, wk, wvT, gT, lo, hi)


def _ret_body(q_ref, kT_ref, v_ref, dec_ref, xi_ref, zeta_ref, cd_ref, gg_ref, gb_ref, o_ref, r_ref):
    C = RET_CHUNK

    @pl.when(pl.program_id(1) == 0)
    def _():
        r_ref[...] = jnp.zeros(r_ref.shape, F32)

    for bb in range(q_ref.shape[0]):
        for h in range(RET_HEADS):
            dk = slice(h * RET_DK, (h + 1) * RET_DK)
            dv = slice(h * RET_DV, (h + 1) * RET_DV)
            dec = dec_ref[h]
            xi = xi_ref[h]
            zeta = zeta_ref[h]
            cd = cd_ref[h]
            gg = gg_ref[:, dv]
            gb = gb_ref[:, dv]
            r = r_ref[bb, h]
            for n in range(RET_STEP_CHUNKS):
                rows = slice(n * C, (n + 1) * C)
                qc = q_ref[bb, rows, dk]
                kT = kT_ref[bb, n, dk, :]
                vc = v_ref[bb, rows, dv]
                s = _nn(qc, kT.astype(BF16)) * dec
                o = _nn(s.astype(BF16), vc) + _nn(qc, r.astype(BF16)) * xi
                r = r * cd + _nn((kT * zeta).astype(BF16), vc)
                mu = jnp.mean(o, axis=-1, keepdims=True)
                var = jnp.mean(jnp.square(o - mu), axis=-1, keepdims=True)
                o_ref[bb, rows, dv] = (o - mu) * lax.rsqrt(var + GN_EPS) * gg + gb
            r_ref[bb, h] = r


def _retention(rq, rkT, rv, dec, xi, zeta, cd, gn_g, gn_b):
    B, T, _ = rq.shape
    ts = RET_STEP_CHUNKS * RET_CHUNK
    full = lambda a: pl.BlockSpec(a.shape, lambda b, j: (0,) * a.ndim)
    nb = RET_STEP_BATCH if B % RET_STEP_BATCH == 0 else 1
    in_specs = [
        pl.BlockSpec((nb, ts, RET_HEADS * RET_DK), lambda b, j: (b, j, 0)),
        pl.BlockSpec((nb, RET_STEP_CHUNKS, RET_HEADS * RET_DK, RET_CHUNK), lambda b, j: (b, j, 0, 0)),
        pl.BlockSpec((nb, ts, RET_HEADS * RET_DV), lambda b, j: (b, j, 0)),
    ] + [full(a) for a in (dec, xi, zeta, cd, gn_g, gn_b)]
    return pl.pallas_call(
        _ret_body, out_shape=jax.ShapeDtypeStruct((B, T, RET_HEADS * RET_DV), F32),
        grid=(B // nb, T // ts), in_specs=in_specs,
        out_specs=pl.BlockSpec((nb, ts, RET_HEADS * RET_DV), lambda b, j: (b, j, 0)),
        scratch_shapes=[pltpu.VMEM((nb, RET_HEADS, RET_DK, RET_DV), F32)],
        compiler_params=_cparams(("parallel", "arbitrary")), name="retention",
    )(rq, rkT, rv, dec, xi, zeta, cd, gn_g, gn_b)


def _layer_norm(y, g, b):
    mu = jnp.mean(y, axis=-1, keepdims=True)
    var = jnp.mean(jnp.square(y - mu), axis=-1, keepdims=True)
    return (y - mu) * lax.rsqrt(var + LN_EPS) * g + b


def _merge_body(alpha, x_ref, oa_ref, or_ref, wrg_ref, wmg_ref, wua_ref, wur_ref, wo_ref, g1_ref, b1_ref,
                wrh_ref, wrl_ref, rb_ref, x1_ref, eid_ref, wgt_ref):
    x = x_ref[...]
    xb = x.astype(BF16)
    rgate = jax.nn.silu(_nn(xb, wrg_ref[...]))
    o_ret = (or_ref[...] * rgate).astype(BF16)
    a = _nn(oa_ref[...], wua_ref[...])
    r = _nn(o_ret, wur_ref[...])
    mg = jax.nn.sigmoid(_nn(xb, wmg_ref[...]))
    merged = mg[:, :D_MODEL] * a + mg[:, D_MODEL:] * r
    mix = _nn(merged.astype(BF16), wo_ref[...])
    x1 = _layer_norm(alpha * x + mix, g1_ref[...], b1_ref[...])
    x1_ref[...] = x1

    xh = x1.astype(BF16)
    xl = (x1 - xh.astype(F32)).astype(BF16)
    wh = wrh_ref[...]
    lg = _nt(wh, xh) + _nt(wh, xl) + _nt(wrl_ref[...], xh) + rb_ref[...]
    ne = N_EXPERTS
    gl = lg[ne:ne + MOE_GROUPS]
    ge = jnp.exp(gl - jnp.max(gl, axis=0, keepdims=True))
    pg = ge / jnp.sum(ge, axis=0, keepdims=True)
    g_prob = jnp.max(pg, axis=0, keepdims=True)
    gi = lax.broadcasted_iota(I32, pg.shape, 0)
    g_idx = jnp.min(jnp.where(pg == g_prob, gi, MOE_GROUPS), axis=0, keepdims=True)
    inner = jnp.zeros((EXPERTS_PER_GROUP, lg.shape[1]), F32)
    for gq in range(MOE_GROUPS):
        inner = inner + jnp.where(g_idx == gq, lg[8 * gq:8 * (gq + 1)], 0.0)
    ei = lax.broadcasted_iota(I32, inner.shape, 0)
    m1 = jnp.max(inner, axis=0, keepdims=True)
    i1 = jnp.min(jnp.where(inner == m1, ei, EXPERTS_PER_GROUP), axis=0, keepdims=True)
    rest = jnp.where(ei == i1, -jnp.inf, inner)
    m2 = jnp.max(rest, axis=0, keepdims=True)
    i2 = jnp.min(jnp.where(rest == m2, ei, EXPERTS_PER_GROUP), axis=0, keepdims=True)
    e2 = jnp.exp(m2 - m1)
    den = 1.0 + e2
    w1 = (1.0 / den) * g_prob
    w2 = (e2 / den) * g_prob
    zi = jnp.zeros((6, lg.shape[1]), I32)
    eid_ref[...] = jnp.concatenate([g_idx * EXPERTS_PER_GROUP + i1, g_idx * EXPERTS_PER_GROUP + i2, zi], axis=0)
    wgt_ref[...] = jnp.concatenate([w1, w2, jnp.zeros((6, lg.shape[1]), F32)], axis=0)


def _merge(alpha, x2, oa2, or2, wrg, wmg, wua, wur, wo, g1, b1, wrh, wrl, rb):
    M, D = x2.shape
    tm = MERGE_TM
    full = lambda a: pl.BlockSpec(a.shape, lambda i: (0,) * a.ndim)
    in_specs = [pl.BlockSpec((tm, D), lambda i: (i, 0)),
                pl.BlockSpec((tm, oa2.shape[1]), lambda i: (i, 0)),
                pl.BlockSpec((tm, or2.shape[1]), lambda i: (i, 0))] + [
        full(a) for a in (wrg, wmg, wua, wur, wo, g1, b1, wrh, wrl, rb)]
    out_shape = [jax.ShapeDtypeStruct((M, D), F32),
                 jax.ShapeDtypeStruct((8, M), I32),
                 jax.ShapeDtypeStruct((8, M), F32)]
    out_specs = [pl.BlockSpec((tm, D), lambda i: (i, 0)),
                 pl.BlockSpec((8, tm), lambda i: (0, i)),
                 pl.BlockSpec((8, tm), lambda i: (0, i))]
    return pl.pallas_call(
        functools.partial(_merge_body, alpha), out_shape=out_shape, grid=(M // tm,),
        in_specs=in_specs, out_specs=out_specs,
        compiler_params=_cparams(("parallel",)), name="merge",
    )(x2, oa2, or2, wrg, wmg, wua, wur, wo, g1, b1, wrh, wrl, rb)


def _rank_body(eid_ref, tri_ref, low_ref, pos_ref, cnt_ref, carry_ref, off_ref):
    p = pl.program_id(0)
    j = pl.program_id(1)
    tb = tri_ref.shape[0]

    @pl.when((p == 0) & (j == 0))
    def _():
        carry_ref[...] = jnp.zeros(carry_ref.shape, F32)

    @pl.when((p == 1) & (j == 0))
    def _():
        tiles = jnp.floor((carry_ref[...] + (EXPERT_TM - 1)) * (1.0 / EXPERT_TM))
        off_ref[...] = _nn(low_ref[...], tiles.astype(BF16)) * EXPERT_TM
        carry_ref[...] = jnp.zeros(carry_ref.shape, F32)

    for s in range(eid_ref.shape[1] // tb):
        cols = slice(s * tb, (s + 1) * tb)
        e = eid_ref[:, cols]
        rows = lax.broadcasted_iota(I32, (N_EXPERTS, tb), 0)
        hit = rows == e
        per_expert = jnp.sum(jnp.where(hit, 1.0, 0.0), axis=1, keepdims=True)

        @pl.when(p == 0)
        def _():
            pos_ref[:, cols] = jnp.zeros((1, tb), I32)
            carry_ref[...] = carry_ref[...] + per_expert
            cnt_ref[...] = carry_ref[...]

        @pl.when(p == 1)
        def _():
            onehot = jnp.where(hit, 1.0, 0.0).astype(BF16)
            incl = _nn(onehot, tri_ref[...])
            base = carry_ref[:, 0:1] + off_ref[:, 0:1] - 1.0
            pos_ref[:, cols] = jnp.sum(jnp.where(hit, incl + base, 0.0), axis=0, keepdims=True).astype(I32)
            carry_ref[...] = carry_ref[...] + per_expert


def _rank(eid_flat, tri, low):
    n = eid_flat.shape[1]
    tb = RANK_STEP_BLOCKS * RANK_TB
    return pl.pallas_call(
        _rank_body,
        out_shape=[jax.ShapeDtypeStruct((1, n), I32), jax.ShapeDtypeStruct((N_EXPERTS, 128), F32)],
        grid=(2, n // tb),
        in_specs=[pl.BlockSpec((1, tb), lambda p, j: (0, j)), pl.BlockSpec(tri.shape, lambda p, j: (0, 0)),
                  pl.BlockSpec(low.shape, lambda p, j: (0, 0))],
        out_specs=[pl.BlockSpec((1, tb), lambda p, j: (0, j * p)),
                   pl.BlockSpec((N_EXPERTS, 128), lambda p, j: (0, 0))],
        scratch_shapes=[pltpu.VMEM((N_EXPERTS, 128), F32), pltpu.VMEM((N_EXPERTS, 128), F32)],
        compiler_params=_cparams(("arbitrary", "arbitrary")), name="rank",
    )(eid_flat, tri, low)


def _dispatch_body(pos_ref, ztile_ref, x_ref, xs_ref, zbuf_ref, sem, zsem):
    tm = DISPATCH_TM
    m_tok = pos_ref.shape[0] // 2
    base = pl.program_id(0) * tm

    @pl.when(pl.program_id(0) == 0)
    def _():
        zbuf_ref[...] = jnp.zeros(zbuf_ref.shape, F32)

        def zero_copy(e):
            z = pl.multiple_of(jnp.maximum(ztile_ref[e], 0), EXPERT_TM)
            return pltpu.make_async_copy(zbuf_ref, xs_ref.at[pl.ds(z, EXPERT_TM), :], zsem)

        for e in range(N_EXPERTS):
            @pl.when(ztile_ref[e] >= 0)
            def _():
                zero_copy(e).start()
        for e in range(N_EXPERTS):
            @pl.when(ztile_ref[e] >= 0)
            def _():
                zero_copy(e).wait()

        def tail_copy(t):
            return pltpu.make_async_copy(
                zbuf_ref, xs_ref.at[pl.ds(pl.multiple_of(t * EXPERT_TM, EXPERT_TM), EXPERT_TM), :], zsem)

        n_tiles = xs_ref.shape[0] // EXPERT_TM
        lax.fori_loop(ztile_ref[N_EXPERTS], n_tiles, lambda t, c: (tail_copy(t).start(), c)[1], 0)
        lax.fori_loop(ztile_ref[N_EXPERTS], n_tiles, lambda t, c: (tail_copy(t).wait(), c)[1], 0)

    def row_copy(r, p):
        return pltpu.make_async_copy(x_ref.at[pl.ds(r, 1), :], xs_ref.at[pl.ds(p, 1), :], sem)

    for r in range(tm):
        row_copy(r, pos_ref[base + r]).start()
        row_copy(r, pos_ref[m_tok + base + r]).start(priority=1)

    def drain(r, carry):
        row_copy(0, 0).wait()
        row_copy(0, 0).wait()
        return carry

    lax.fori_loop(0, tm, drain, 0, unroll=16)


def _dispatch(pos, ztile, x1, n_pad):
    M, D = x1.shape
    tm = DISPATCH_TM
    grid_spec = pltpu.PrefetchScalarGridSpec(
        num_scalar_prefetch=2, grid=(M // tm,),
        in_specs=[pl.BlockSpec((tm, D), lambda i, pos, zt: (i, 0))],
        out_specs=pl.BlockSpec(memory_space=pl.ANY),
        scratch_shapes=[pltpu.VMEM((EXPERT_TM, D), F32), pltpu.SemaphoreType.DMA, pltpu.SemaphoreType.DMA],
    )
    return pl.pallas_call(
        _dispatch_body, out_shape=jax.ShapeDtypeStruct((n_pad, D), F32), grid_spec=grid_spec,
        compiler_params=_cparams(("arbitrary",), has_side_effects=True, disable_bounds_checks=True),
        name="dispatch",
    )(pos, ztile, x1)


def _experts_body(te_ref, nu_ref, seg_ref, nxt_ref, xs_ref, wg_hbm, wu_hbm, wd_hbm, ys_ref,
                  wgb_ref, wub_ref, wdb_ref, sg_ref, su_ref, sd_ref, sem):
    tm = EXPERT_TM
    k = pl.program_id(0)
    last = nu_ref[0] - 1

    def fetch(e, slot):
        return (pltpu.make_async_copy(wg_hbm.at[e], sg_ref.at[slot], sem.at[0, slot]),
                pltpu.make_async_copy(wu_hbm.at[e], su_ref.at[slot], sem.at[1, slot]),
                pltpu.make_async_copy(wd_hbm.at[e], sd_ref.at[slot], sem.at[2, slot]))

    @pl.when(k == 0)
    def _():
        for c in fetch(te_ref[0], 0):
            c.start()

    for s in range(EXPERT_STEP_TILES):
        t = EXPERT_STEP_TILES * k + s
        rows = slice(s * tm, (s + 1) * tm)

        @pl.when((t <= last) & (seg_ref[t] >= 0))
        def _():
            slot = seg_ref[t]
            for c in fetch(te_ref[t], slot):
                c.wait()
            wgb_ref[...] = sg_ref[slot].astype(BF16)
            wub_ref[...] = su_ref[slot].astype(BF16)
            wdb_ref[...] = sd_ref[slot].astype(BF16)

            @pl.when(nxt_ref[t] >= 0)
            def _():
                for c in fetch(nxt_ref[t], 1 - slot):
                    c.start()

        @pl.when(t <= last)
        def _():
            xb = xs_ref[rows, :].astype(BF16)
            hg = _nn(xb, wgb_ref[...])
            hu = _nn(xb, wub_ref[...])
            h = (jax.nn.silu(hg) * hu).astype(BF16)
            ys_ref[rows, :] = _nn(h, wdb_ref[...])

        @pl.when(t > last)
        def _():
            ys_ref[rows, :] = jnp.zeros((tm, ys_ref.shape[1]), F32)


def _experts(tile_expert, n_used, seg_slot, next_expert, xs, wg, wu, wd):
    npad, D = xs.shape
    tm = EXPERT_STEP_TILES * EXPERT_TM
    grid_spec = pltpu.PrefetchScalarGridSpec(
        num_scalar_prefetch=4, grid=(npad // tm,),
        in_specs=[pl.BlockSpec((tm, D), lambda i, te, nu, sg, nx:
                               (jnp.minimum(i, (nu[0] - 1) // EXPERT_STEP_TILES), 0)),
                  pl.BlockSpec(memory_space=pl.ANY), pl.BlockSpec(memory_space=pl.ANY),
                  pl.BlockSpec(memory_space=pl.ANY)],
        out_specs=pl.BlockSpec((tm, D), lambda i, te, nu, sg, nx: (i, 0)),
        scratch_shapes=[pltpu.VMEM((D, D_FF), BF16), pltpu.VMEM((D, D_FF), BF16), pltpu.VMEM((D_FF, D), BF16),
                        pltpu.VMEM((2, D, D_FF), F32), pltpu.VMEM((2, D, D_FF), F32),
                        pltpu.VMEM((2, D_FF, D), F32), pltpu.SemaphoreType.DMA((3, 2))],
    )
    return pl.pallas_call(
        _experts_body, out_shape=jax.ShapeDtypeStruct((npad, D), F32), grid_spec=grid_spec,
        compiler_params=_cparams(("arbitrary",)), name="experts",
    )(tile_expert, n_used, seg_slot, next_expert, xs, wg, wu, wd)


def _combine_body(alpha, pos_ref, x1_ref, w_ref, eye_ref, g2_ref, b2_ref, ys_ref, o_ref,
                  buf_a, buf_b, sem_a, sem_b):
    tm = COMBINE_TM
    m_tok = pos_ref.shape[0] // 2
    k = pl.program_id(0)
    n = pl.num_programs(0)

    def row_copies(buf, sem, r, base):
        return (pltpu.make_async_copy(ys_ref.at[pl.ds(pos_ref[base + r], 1), :], buf.at[0, pl.ds(r, 1), :], sem),
                pltpu.make_async_copy(ys_ref.at[pl.ds(pos_ref[m_tok + base + r], 1), :],
                                      buf.at[1, pl.ds(r, 1), :], sem))

    def issue_inline(buf, sem, tile):
        for r in range(tm):
            for queue, c in enumerate(row_copies(buf, sem, r, tile * tm)):
                c.start(priority=queue)

    def drain(buf, sem):
        def body(r, carry):
            for c in row_copies(buf, sem, 0, 0):
                c.wait()
            return carry
        lax.fori_loop(0, tm, body, 0, unroll=16)

    def compute(buf, s):
        rows = slice(s * tm, (s + 1) * tm)
        w = w_ref[:, rows]
        eye = eye_ref[...]
        w_a = w.astype(BF16)
        w_b = (w - w_a.astype(F32)).astype(BF16)
        w_c = (w - w_a.astype(F32) - w_b.astype(F32)).astype(BF16)
        wcol = _nt(eye, w_a) + _nt(eye, w_b) + _nt(eye, w_c)
        moe = buf[0] * wcol[:, 0:1] + buf[1] * wcol[:, 1:2]
        o_ref[rows, :] = _layer_norm(alpha * x1_ref[rows, :] + moe, g2_ref[...], b2_ref[...])

    @pl.when(k == 0)
    def _():
        def body(r, carry):
            for queue, c in enumerate(row_copies(buf_a, sem_a, r, 0)):
                c.start(priority=queue)
            return carry
        lax.fori_loop(0, tm, body, 0, unroll=16)

    drain(buf_a, sem_a)
    issue_inline(buf_b, sem_b, 2 * k + 1)
    compute(buf_a, 0)
    drain(buf_b, sem_b)
    issue_inline(buf_a, sem_a, jnp.minimum(2 * k + 2, 2 * n - 2))
    compute(buf_b, 1)

    @pl.when(k == n - 1)
    def _():
        drain(buf_a, sem_a)


def _combine(alpha, pos, x1, wgt, eye, g2, b2, ys):
    M, D = x1.shape
    tm = COMBINE_TM
    grid_spec = pltpu.PrefetchScalarGridSpec(
        num_scalar_prefetch=1, grid=(M // (2 * tm),),
        in_specs=[pl.BlockSpec((2 * tm, D), lambda i, pos: (i, 0)),
                  pl.BlockSpec((8, 2 * tm), lambda i, pos: (0, i)),
                  pl.BlockSpec((tm, tm), lambda i, pos: (0, 0)),
                  pl.BlockSpec((1, D), lambda i, pos: (0, 0)),
                  pl.BlockSpec((1, D), lambda i, pos: (0, 0)),
                  pl.BlockSpec(memory_space=pl.ANY)],
        out_specs=pl.BlockSpec((2 * tm, D), lambda i, pos: (i, 0)),
        scratch_shapes=[pltpu.VMEM((2, tm, D), F32), pltpu.VMEM((2, tm, D), F32),
                        pltpu.SemaphoreType.DMA, pltpu.SemaphoreType.DMA],
    )
    return pl.pallas_call(
        functools.partial(_combine_body, alpha), out_shape=jax.ShapeDtypeStruct((M, D), F32),
        grid_spec=grid_spec, compiler_params=_cparams(("arbitrary",), disable_bounds_checks=True),
        name="combine",
    )(pos, x1, wgt, eye, g2, b2, ys)


def _rope_angles(pos, dim):
    half = dim // 2
    inv_freq = ROPE_THETA ** (-np.arange(half, dtype=np.float64) * 2.0 / dim)
    return pos.astype(np.float64)[:, None] * inv_freq[None, :]


def _tables(T):
    f32 = lambda a: jnp.asarray(np.ascontiguousarray(a), dtype=F32)
    ang = _rope_angles(np.arange(T), NSA_HD)
    cos, sin = np.cos(ang), np.sin(ang)
    c64 = np.tile(cos, (1, 4))
    s64 = np.tile(np.concatenate([-sin, sin], axis=1), (1, 2))
    ang2 = _rope_angles(np.arange(T), RET_DK)
    cos2, sin2 = np.cos(ang2), np.sin(ang2)
    c128 = np.tile(cos2, (1, 2))
    s128 = np.concatenate([-sin2, sin2], axis=1)
    n_rows = T // CMP_STRIDE
    angc = _rope_angles(np.arange(n_rows) * CMP_STRIDE + CMP_BLOCK - 1, NSA_HD)
    cc = np.tile(np.cos(angc), (1, 2))
    sc = np.tile(np.sin(angc), (1, 2))
    return tuple(f32(a) for a in (c64, s64, c128, s128, cos.T, sin.T, cos2.T, sin2.T)), f32(cc), f32(sc)


def _overlap_matrix(T):
    n_rows = T // CMP_STRIDE
    n_sel = T // SLC_BLOCK
    cmp_start = np.arange(n_rows) * CMP_STRIDE
    sel_start = np.arange(n_sel) * SLC_BLOCK
    ov = np.clip(np.minimum(cmp_start[None, :] + CMP_BLOCK, sel_start[:, None] + SLC_BLOCK)
                 - np.maximum(cmp_start[None, :], sel_start[:, None]), 0, None)
    return jnp.asarray(ov.astype(np.float32) / CMP_STRIDE, dtype=BF16)


def _retention_tables():
    C = RET_CHUNK
    f32 = lambda a: jnp.asarray(np.ascontiguousarray(a), dtype=F32)
    gamma = 1.0 - 2.0 ** (-5.0 - np.arange(RET_HEADS, dtype=np.float64))
    log_g = np.log(gamma)
    i = np.arange(C, dtype=np.float64)
    diff = i[:, None] - i[None, :]
    dec = np.where(diff >= 0, np.exp(np.maximum(diff, 0.0) * log_g[:, None, None]), 0.0)
    xi = np.exp((i + 1.0) * log_g[:, None])
    zeta = np.exp((C - 1.0 - i) * log_g[:, None])
    cd = np.exp(C * log_g)
    xi_b = np.broadcast_to(xi[:, :, None], (RET_HEADS, C, RET_DV))
    cd_b = np.broadcast_to(cd[:, None, None], (RET_HEADS, 1, RET_DV))
    return f32(dec), f32(xi_b), f32(zeta[:, None, :]), f32(cd_b)


def _rot_half_cols(w):
    half = w.shape[-1] // 2
    return jnp.concatenate([-w[..., half:], w[..., :half]], axis=-1)


def kernel(x, w_in, cmp_pos_k, cmp_k_w1, cmp_k_b1, cmp_k_w2, cmp_k_b2, cmp_pos_v, cmp_v_w1, cmp_v_b1, cmp_v_w2, cmp_v_b2, ret_gn_g, ret_gn_b, w_up_attn, w_up_ret, w_out, ln1_g, ln1_b, router_group_w, router_group_b, router_inner_w, router_inner_b, expert_w_gate, expert_w_up, expert_w_down, ln2_g, ln2_b):
    B, T, D = x.shape
    M = B * T
    depth = w_in.shape[0]
    alpha = (2.0 * depth) ** 0.25
    tabs, cc, sc = _tables(T)
    ov = _overlap_matrix(T)
    dec, xi_b, zeta, cd_b = _retention_tables()
    tri = jnp.asarray(np.triu(np.ones((RANK_TB, RANK_TB), np.float32)), dtype=BF16)
    low = jnp.asarray(np.tril(np.ones((N_EXPERTS, N_EXPERTS), np.float32), k=-1), dtype=BF16)
    eye = jnp.asarray(np.eye(COMBINE_TM, dtype=np.float32), dtype=BF16)
    n_pad = 2 * M + N_EXPERTS * EXPERT_TM
    n_tiles = n_pad // EXPERT_TM

    for l in range(depth):
        w = w_in[l]
        col = lambda n: w[:, _OFF[n][0]:_OFF[n][1]]
        wn = jnp.concatenate([col("cmp_k"), col("cmp_v"), col("slc_k"), col("win_k"), col("ret_q"), col("ret_v")],
                             axis=1).astype(BF16)
        wt = jnp.concatenate([col("nsa_q"), col("slc_v"), col("win_v"), col("ret_k"), col("nsa_gate"),
                              jnp.zeros((D, 8), F32)], axis=1).T.astype(BF16)
        qT, ckv, sk, wk, svT, wvT, gT, rq, rkT, rv = _proj(x, wn, wt, tabs)

        kc, vcT = _compress(
            ckv, cmp_pos_k[l].reshape(2, -1), cmp_pos_v[l].reshape(2, -1),
            cmp_k_w1[l], cmp_k_b1[l][None, :], cmp_k_w2[l], _rot_half_cols(cmp_k_w2[l]),
            cmp_k_b2[l][None, :], _rot_half_cols(cmp_k_b2[l])[None, :], cc, sc,
            cmp_v_w1[l], cmp_v_b1[l][None, :], cmp_v_w2[l].T, cmp_v_b2[l][:, None])
        o_attn = _nsa(qT, kc, vcT, ov, sk, svT, wk, wvT, gT)
        o_ret = _retention(rq, rkT, rv, dec, xi_b, zeta, cd_b, ret_gn_g[l][None, :], ret_gn_b[l][None, :])

        wr = jnp.concatenate([router_inner_w[l].transpose(0, 2, 1).reshape(N_EXPERTS, D),
                              router_group_w[l].T, jnp.zeros((4, D), F32)], axis=0)
        wrh = wr.astype(BF16)
        wrl = (wr - wrh.astype(F32)).astype(BF16)
        rb = jnp.concatenate([router_inner_b[l].reshape(-1), router_group_b[l], jnp.zeros((4,), F32)])[:, None]
        x1, eid, wgt = _merge(
            alpha, x.reshape(M, D), o_attn.reshape(M, -1), o_ret.reshape(M, -1),
            col("ret_gate").astype(BF16), col("merge_gate").astype(BF16), w_up_attn[l].astype(BF16),
            w_up_ret[l].astype(BF16), w_out[l].astype(BF16), ln1_g[l][None, :], ln1_b[l][None, :], wrh, wrl, rb)

        eid_flat = eid[:2].reshape(1, 2 * M)
        pos, cnt = _rank(eid_flat, tri, low)
        pos = pos[0]
        counts = cnt[:, 0].astype(I32)
        tiles_per = (counts + EXPERT_TM - 1) // EXPERT_TM
        tile_end = jnp.cumsum(tiles_per)
        tile_ids = jnp.arange(n_tiles, dtype=I32)
        tile_expert = jnp.minimum(jnp.sum((tile_end[None, :] <= tile_ids[:, None]).astype(I32), axis=1),
                                  N_EXPERTS - 1).astype(I32)
        n_used = tile_end[-1:].astype(I32)

        ztile = jnp.concatenate([jnp.where(tiles_per > 0, (tile_end - 1) * EXPERT_TM, -1), tile_end[-1:]]).astype(I32)
        xs = _dispatch(pos, ztile, x1, n_pad)
        prev_expert = jnp.concatenate([jnp.full((1,), -1, I32), tile_expert[:-1]])
        is_first = (tile_ids < n_used[0]) & (tile_expert != prev_expert)
        seg_slot = jnp.where(is_first, (jnp.cumsum(is_first.astype(I32)) - 1) % 2, -1).astype(I32)
        own = tile_expert[:, None] == jnp.arange(N_EXPERTS, dtype=I32)[None, :]
        seg_end = jnp.sum(jnp.where(own, tile_end[None, :], 0), axis=1)
        expert_at_end = jnp.minimum(jnp.sum((tile_end[None, :] <= seg_end[:, None]).astype(I32), axis=1),
                                    N_EXPERTS - 1)
        next_expert = jnp.where(is_first & (seg_end < n_used[0]), expert_at_end, -1).astype(I32)
        ys = _experts(tile_expert, n_used, seg_slot, next_expert, xs,
                      expert_w_gate[l], expert_w_up[l], expert_w_down[l])
        x = _combine(alpha, pos, x1, wgt, eye, ln2_g[l][None, :], ln2_b[l][None, :], ys).reshape(B, T, D)
    return x
```

```python
import functools

import numpy as np
import jax
import jax.numpy as jnp
from jax import lax
from jax.experimental import pallas as pl
from jax.experimental.pallas import tpu as pltpu

F32 = jnp.float32
BF16 = jnp.bfloat16
I32 = jnp.int32

D_MODEL = 1024
NSA_HEADS = 8
NSA_HD = 64
NSA_GROUPS = 2
NSA_HPG = NSA_HEADS // NSA_GROUPS
CMP_BLOCK = 32
CMP_STRIDE = 16
CMP_HIDDEN = 256
SLC_BLOCK = 64
SLC_TOPK = 16
WINDOW = 512
RET_HEADS = 4
RET_DK = 128
RET_DV = 256
RET_CHUNK = 128
MOE_GROUPS = 4
EXPERTS_PER_GROUP = 8
N_EXPERTS = MOE_GROUPS * EXPERTS_PER_GROUP
D_FF = 512
ROPE_THETA = 10000.0
LN_EPS = 1e-5
GN_EPS = 1e-5
NEG_INF = -1e30
LOG2_E = 1.4426950408889634

VMEM_LIMIT_V7X = 56 * 1024 * 1024

_OFF = {}
_o = 0
for _n, _w in (("nsa_q", 512), ("cmp_k", 128), ("cmp_v", 128), ("slc_k", 128), ("slc_v", 128),
               ("win_k", 128), ("win_v", 128), ("nsa_gate", 24), ("ret_q", 512), ("ret_k", 512),
               ("ret_v", 1024), ("ret_gate", 1024), ("merge_gate", 2048)):
    _OFF[_n] = (_o, _o + _w)
    _o += _w

PROJ_TM = 512
NSA_TQ = 256
NSA_KC = 256
NSA_VROWS = 80
RET_STEP_CHUNKS = 8
RET_STEP_BATCH = 2
MERGE_TM = 512
RANK_TB = 1024
RANK_STEP_BLOCKS = 4
DISPATCH_TM = 1024
EXPERT_TM = 256
EXPERT_STEP_TILES = 4
COMBINE_TM = 256


def _cparams(sem, **kw):
    return pltpu.CompilerParams(dimension_semantics=sem, vmem_limit_bytes=VMEM_LIMIT_V7X, **kw)


def _nt(a, b):
    return lax.dot_general(a, b, (((1,), (1,)), ((), ())), preferred_element_type=F32)


def _nn(a, b):
    return jnp.dot(a, b, preferred_element_type=F32)


def _proj_body(x_ref, wn_ref, wt_ref, c64_ref, s64_ref, c128_ref, s128_ref,
               ct64_ref, st64_ref, ct128_ref, st128_ref,
               qT_ref, ckv_ref, sk_ref, wk_ref, svT_ref, wvT_ref, gT_ref, rq_ref, rkT_ref, rv_ref, ckv_scr):
    tm = PROJ_TM
    xb = x_ref[0].astype(BF16)

    def nn(a, b):
        return _nn(xb, wn_ref[:, a:b])

    def nt(a, b):
        return _nt(wt_ref[a:b, :], xb)

    ckv = nn(0, 256)
    for c in range(2):
        ckv_scr[c] = ckv[:, 128 * c:128 * (c + 1)]
    for l in range(CMP_STRIDE):
        for c in range(2):
            rows = ckv_scr[c, pl.ds(l, tm // CMP_STRIDE, stride=CMP_STRIDE), :]
            for jj in range(2):
                ckv_ref[0, 2 * c + jj, :, 64 * l:64 * (l + 1)] = rows[:, 64 * jj:64 * (jj + 1)]

    lane = lax.broadcasted_iota(I32, (tm, 128), 1)
    first = (lane % 64) < 32
    c64 = c64_ref[...]
    s64 = s64_ref[...]

    def rope64(k):
        rot = jnp.where(first, pltpu.roll(k, 96, 1), pltpu.roll(k, 32, 1))
        return k * c64 + rot * s64

    sk = rope64(nn(256, 384)).astype(BF16)
    sk_ref[0, 0] = sk[:, :64]
    sk_ref[0, 1] = sk[:, 64:]
    wk = rope64(nn(384, 512)).astype(BF16)
    wk_ref[0, 0] = wk[:, :64]
    wk_ref[0, 1] = wk[:, 64:]

    c128 = c128_ref[...]
    s128 = s128_ref[...]
    rq = nn(512, 1024)
    for h in range(RET_HEADS):
        ch = rq[:, 128 * h:128 * (h + 1)]
        rq_ref[0, :, 128 * h:128 * (h + 1)] = (ch * c128 + pltpu.roll(ch, 64, 1) * s128).astype(BF16)
    rv_ref[0] = nn(1024, 2048).astype(BF16)

    ct = ct64_ref[...]
    st = st64_ref[...]
    qT = nt(0, 512)
    scale_q = NSA_HD ** -0.5 * LOG2_E
    tq = NSA_TQ
    for hh in range(NSA_HEADS):
        g, h = divmod(hh, NSA_HPG)
        x1 = qT[64 * hh:64 * hh + 32]
        x2 = qT[64 * hh + 32:64 * hh + 64]
        o1 = ((x1 * ct - x2 * st) * scale_q).astype(BF16)
        o2 = ((x1 * st + x2 * ct) * scale_q).astype(BF16)
        for it in range(tm // tq):
            qT_ref[0, g, it, 0:32, h * tq:(h + 1) * tq] = o1[:, it * tq:(it + 1) * tq]
            qT_ref[0, g, it, 32:64, h * tq:(h + 1) * tq] = o2[:, it * tq:(it + 1) * tq]

    svT = nt(512, 640).astype(BF16)
    wvT = nt(640, 768).astype(BF16)
    row16 = lax.broadcasted_iota(I32, (NSA_VROWS - NSA_HD, NSA_KC), 0)
    ones_blk = jnp.where(row16 == 0, 1.0, 0.0).astype(BF16)
    for c in range(tm // NSA_KC):
        for g in range(NSA_GROUPS):
            for vT, ref in ((svT, svT_ref), (wvT, wvT_ref)):
                ref[0, c, g, 0:NSA_HD, :] = vT[64 * g:64 * (g + 1), NSA_KC * c:NSA_KC * (c + 1)]
                ref[0, c, g, NSA_HD:NSA_VROWS, :] = ones_blk

    ct2 = ct128_ref[...]
    st2 = st128_ref[...]
    rkT = nt(768, 1280)
    scale_k = RET_DK ** -0.5
    for h in range(RET_HEADS):
        x1 = rkT[128 * h:128 * h + 64]
        x2 = rkT[128 * h + 64:128 * h + 128]
        o1 = (x1 * ct2 - x2 * st2) * scale_k
        o2 = (x1 * st2 + x2 * ct2) * scale_k
        for c in range(tm // RET_CHUNK):
            rkT_ref[0, c, 128 * h:128 * h + 64, :] = o1[:, 128 * c:128 * (c + 1)]
            rkT_ref[0, c, 128 * h + 64:128 * h + 128, :] = o2[:, 128 * c:128 * (c + 1)]

    gT_ref[0] = jax.nn.sigmoid(nt(1280, 1312))


def _proj(x, wn, wt, tabs):
    B, T, D = x.shape
    tm = PROJ_TM
    c64, s64, c128, s128, ct64, st64, ct128, st128 = tabs
    const = lambda b, i: (0, 0)
    in_specs = [
        pl.BlockSpec((1, tm, D), lambda b, i: (b, i, 0)),
        pl.BlockSpec(wn.shape, const),
        pl.BlockSpec(wt.shape, const),
        pl.BlockSpec((tm, 128), lambda b, i: (i, 0)),
        pl.BlockSpec((tm, 128), lambda b, i: (i, 0)),
        pl.BlockSpec((tm, 128), lambda b, i: (i, 0)),
        pl.BlockSpec((tm, 128), lambda b, i: (i, 0)),
        pl.BlockSpec((32, tm), lambda b, i: (0, i)),
        pl.BlockSpec((32, tm), lambda b, i: (0, i)),
        pl.BlockSpec((64, tm), lambda b, i: (0, i)),
        pl.BlockSpec((64, tm), lambda b, i: (0, i)),
    ]
    out_shape = [
        jax.ShapeDtypeStruct((B, NSA_GROUPS, T // NSA_TQ, NSA_HD, NSA_HPG * NSA_TQ), BF16),
        jax.ShapeDtypeStruct((B, 4, T // CMP_STRIDE, CMP_STRIDE * NSA_HD), F32),
        jax.ShapeDtypeStruct((B, 2, T, 64), BF16),
        jax.ShapeDtypeStruct((B, 2, T, 64), BF16),
        jax.ShapeDtypeStruct((B, T // NSA_KC, NSA_GROUPS, NSA_VROWS, NSA_KC), BF16),
        jax.ShapeDtypeStruct((B, T // NSA_KC, NSA_GROUPS, NSA_VROWS, NSA_KC), BF16),
        jax.ShapeDtypeStruct((B, 32, T), F32),
        jax.ShapeDtypeStruct((B, T, 512), BF16),
        jax.ShapeDtypeStruct((B, T // RET_CHUNK, 512, RET_CHUNK), F32),
        jax.ShapeDtypeStruct((B, T, 1024), BF16),
    ]
    out_specs = [
        pl.BlockSpec((1, NSA_GROUPS, tm // NSA_TQ, NSA_HD, NSA_HPG * NSA_TQ), lambda b, i: (b, 0, i, 0, 0)),
        pl.BlockSpec((1, 4, tm // CMP_STRIDE, CMP_STRIDE * NSA_HD), lambda b, i: (b, 0, i, 0)),
        pl.BlockSpec((1, 2, tm, 64), lambda b, i: (b, 0, i, 0)),
        pl.BlockSpec((1, 2, tm, 64), lambda b, i: (b, 0, i, 0)),
        pl.BlockSpec((1, tm // NSA_KC, NSA_GROUPS, NSA_VROWS, NSA_KC), lambda b, i: (b, i, 0, 0, 0)),
        pl.BlockSpec((1, tm // NSA_KC, NSA_GROUPS, NSA_VROWS, NSA_KC), lambda b, i: (b, i, 0, 0, 0)),
        pl.BlockSpec((1, 32, tm), lambda b, i: (b, 0, i)),
        pl.BlockSpec((1, tm, 512), lambda b, i: (b, i, 0)),
        pl.BlockSpec((1, tm // RET_CHUNK, 512, RET_CHUNK), lambda b, i: (b, i, 0, 0)),
        pl.BlockSpec((1, tm, 1024), lambda b, i: (b, i, 0)),
    ]
    return pl.pallas_call(
        _proj_body, out_shape=out_shape, grid=(B, T // tm), in_specs=in_specs, out_specs=out_specs,
        scratch_shapes=[pltpu.VMEM((2, tm, 128), F32)],
        compiler_params=_cparams(("parallel", "parallel")), name="proj",
    )(x, wn, wt, c64, s64, c128, s128, ct64, st64, ct128, st128)


def _compress_body(uk_ref, uv_ref, posk_ref, posv_ref, w1k_ref, b1k_ref, w2k_ref, w2kr_ref, b2k_ref, b2kr_ref,
                   cc_ref, sc_ref, w1v_ref, b1v_ref, w2vT_ref, b2vT_ref, kc_ref, vcT_ref):
    half = CMP_STRIDE * NSA_HD

    def hidden(u_ref, pos_ref, w1_ref, b1_ref):
        u = u_ref[0, 0]
        top = (u + pos_ref[0:1, :]).astype(BF16)
        bot = (u + pos_ref[1:2, :]).astype(BF16)
        a = _nn(top, w1_ref[0:half, :].astype(BF16))
        bm = _nn(bot, w1_ref[half:2 * half, :].astype(BF16))
        n = bm.shape[0]
        return jax.nn.gelu(a + pltpu.roll(bm, n - 1, 0) + b1_ref[...])

    hk = hidden(uk_ref, posk_ref, w1k_ref, b1k_ref).astype(BF16)
    k = _nn(hk, w2k_ref[...].astype(BF16)) + b2k_ref[...]
    kr = _nn(hk, w2kr_ref[...].astype(BF16)) + b2kr_ref[...]
    kc_ref[0, 0] = (k * cc_ref[...] + kr * sc_ref[...]).astype(BF16)

    hv = hidden(uv_ref, posv_ref, w1v_ref, b1v_ref).astype(BF16)
    vcT_ref[0, 0] = (_nt(w2vT_ref[...].astype(BF16), hv) + b2vT_ref[...]).astype(BF16)


def _compress(ckv, posk, posv, w1k, b1k, w2k, w2kr, b2k, b2kr, cc, sc, w1v, b1v, w2vT, b2vT):
    B, _, nr, _ = ckv.shape
    u = ckv
    full = lambda a: pl.BlockSpec(a.shape, lambda b, g: (0,) * a.ndim)
    in_specs = [
        pl.BlockSpec((1, 1, nr, 1024), lambda b, g: (b, g, 0, 0)),
        pl.BlockSpec((1, 1, nr, 1024), lambda b, g: (b, g + 2, 0, 0)),
    ] + [full(a) for a in (posk, posv, w1k, b1k, w2k, w2kr, b2k, b2kr, cc, sc, w1v, b1v, w2vT, b2vT)]
    out_shape = [jax.ShapeDtypeStruct((B, NSA_GROUPS, nr, NSA_HD), BF16),
                 jax.ShapeDtypeStruct((B, NSA_GROUPS, NSA_HD, nr), BF16)]
    out_specs = [pl.BlockSpec((1, 1, nr, NSA_HD), lambda b, g: (b, g, 0, 0)),
                 pl.BlockSpec((1, 1, NSA_HD, nr), lambda b, g: (b, g, 0, 0))]
    return pl.pallas_call(
        _compress_body, out_shape=out_shape, grid=(B, NSA_GROUPS), in_specs=in_specs, out_specs=out_specs,
        compiler_params=_cparams(("parallel", "parallel")), name="compress",
    )(u, u, posk, posv, w1k, b1k, w2k, w2kr, b2k, b2kr, cc, sc, w1v, b1v, w2vT, b2vT)


def _nsa_body(q_ref, kc_ref, vcT_ref, ov_ref, sk_ref, svT_ref, wk_ref, wvT_ref, gT_ref, lo_ref, hi_ref, o_ref,
              score_ref, bias_ref, acc_ref, tot_ref, s_ref):
    tq = NSA_TQ
    kc_n = NSA_KC
    hq = NSA_HPG * tq
    groups = range(NSA_GROUPS)
    i = pl.program_id(1)
    t0 = i * tq
    t_row = t0 + lax.broadcasted_iota(I32, (1, tq), 1)
    t_row4 = t0 + lax.broadcasted_iota(I32, (1, hq), 1) % tq
    n_cmp = kc_ref.shape[2]
    n_sel = ov_ref.shape[0]
    q = [q_ref[0, g, 0] for g in groups]

    def gate4(g, br):
        return jnp.concatenate([gT_ref[0, (g * NSA_HPG + h) * 3 + br:(g * NSA_HPG + h) * 3 + br + 1, :]
                                for h in range(NSA_HPG)], axis=1)

    cmp_end = lax.broadcasted_iota(I32, (n_cmp, 1), 0) * CMP_STRIDE + (CMP_BLOCK - 1)
    cbias = jnp.where(cmp_end <= t_row4, 0.0, NEG_INF)
    any_valid = jnp.where(t_row4 >= CMP_BLOCK - 1, 1.0, 0.0)
    blk_t = t_row // SLC_BLOCK
    sub = lax.broadcasted_iota(I32, (8, 1), 0)

    def select_blocks(g, l_cmp, l_sel):
        s = _nn(kc_ref[0, g, 0:l_cmp, :], q[g]) + cbias[0:l_cmp]
        e = jnp.exp2(s - jnp.max(s, axis=0, keepdims=True))
        p = e * (any_valid / jnp.sum(e, axis=0, keepdims=True))
        tot_ref[g] = gate4(g, 0) * _nn(vcT_ref[0, g, :, 0:l_cmp], p.astype(BF16))
        psum = p[:, 0:tq]
        for h in range(1, NSA_HPG):
            psum = psum + p[:, h * tq:(h + 1) * tq]
        p_hi = psum.astype(BF16)
        p_lo = (psum - p_hi.astype(F32)).astype(BF16)
        ov = ov_ref[0:l_sel, 0:l_cmp]
        imp = _nn(ov, p_hi) + _nn(ov, p_lo)
        jc = lax.broadcasted_iota(I32, (l_sel, 1), 0)
        forced = (jc == 0) | (jc == blk_t) | (jc == blk_t - 1)
        score = jnp.where(forced, 1e9, jnp.where(jc <= blk_t, imp, -1e9)).astype(F32)
        score_ref[g, 0:l_sel, :] = score
        n_slab = l_sel // 8
        slabs = [score[8 * v:8 * (v + 1)] for v in range(n_slab)]
        cnt = [jnp.zeros((8, tq), F32) for _ in range(n_slab)]
        for r in range(l_sel):
            row = jnp.broadcast_to(score_ref[g, r:r + 1, :], (8, tq))
            for v in range(n_slab):
                if r < 8 * v:
                    ahead = jnp.where(row >= slabs[v], 1.0, 0.0)
                elif r >= 8 * (v + 1):
                    ahead = jnp.where(row > slabs[v], 1.0, 0.0)
                else:
                    ahead = jnp.where(sub + 8 * v > r, jnp.where(row >= slabs[v], 1.0, 0.0),
                                      jnp.where(row > slabs[v], 1.0, 0.0))
                cnt[v] = cnt[v] + ahead
        for v in range(n_slab):
            bias_ref[g, 8 * v:8 * (v + 1), :] = jnp.where(
                cnt[v] < float(SLC_TOPK), jnp.where(sub + 8 * v <= blk_t, 0.0, NEG_INF), NEG_INF)
        if l_sel < n_sel:
            bias_ref[g, l_sel:n_sel, :] = jnp.full((n_sel - l_sel, tq), NEG_INF, F32)

    n_var = 4
    tiles_per_var = (n_sel * SLC_BLOCK // tq) // n_var
    for var in range(n_var):
        l_sel = (var + 1) * tiles_per_var * (tq // SLC_BLOCK)
        l_cmp = min(n_cmp, -(-((var + 1) * tiles_per_var * (tq // CMP_STRIDE)) // 128) * 128)

        @pl.when(i // tiles_per_var == var)
        def _():
            for g in groups:
                select_blocks(g, l_cmp, l_sel)

    def qk(k_ref, g, c):
        k_c = k_ref[0, g, pl.ds(pl.multiple_of(c * kc_n, kc_n), kc_n), :]
        return [_nn(k_c, q[g][:, h * tq:(h + 1) * tq]) for h in range(NSA_HPG)]

    def stage_a(scores, add_bias, m):
        biased = [add_bias(scores[h]) for h in range(NSA_HPG)]
        m_new = tuple(jnp.maximum(m[h], jnp.max(biased[h], axis=0, keepdims=True)) for h in range(NSA_HPG))
        return biased, m_new

    def stage_b(g, biased, vT_c, m_old, m_new):
        for h in range(NSA_HPG):
            hs = slice(h * tq, (h + 1) * tq)
            alpha = jnp.exp2(m_old[h] - m_new[h])
            p = jnp.exp2((biased[h] - m_new[h]).astype(BF16))
            acc_ref[g, :, hs] = alpha * acc_ref[g, :, hs] + _nn(vT_c, p)

    nb = kc_n // SLC_BLOCK

    def block_bias(g, c, s):
        return jnp.concatenate([s[SLC_BLOCK * b:SLC_BLOCK * (b + 1)] + bias_ref[g, pl.ds(c * nb + b, 1), :]
                                for b in range(nb)], axis=0)

    def finish(g, br):
        acc = acc_ref[g]
        tot_ref[g] = tot_ref[g] + gate4(g, br) * (acc[0:NSA_HD] / acc[NSA_HD:NSA_HD + 1])
        acc_ref[g] = jnp.zeros(acc.shape, F32)

    def park(g, biased):
        for h in range(NSA_HPG):
            s_ref[g, h] = biased[h]

    def parked(g):
        return [s_ref[g, h] for h in range(NSA_HPG)]

    m0 = tuple(jnp.full((1, tq), NEG_INF, F32) for _ in range(NSA_HPG))

    acc_ref[...] = jnp.zeros(acc_ref.shape, F32)
    m_new = []
    for g in groups:
        biased, m_g = stage_a(qk(sk_ref, g, i), lambda s, g=g: block_bias(g, i, s) + lo_ref[...], m0)
        park(g, biased)
        m_new.append(m_g)

    def slc_chunk(c, carry):
        prev, m_old, m_new = carry
        scores = [qk(sk_ref, g, c) for g in groups]
        for g in groups:
            stage_b(g, parked(g), svT_ref[0, prev, g], m_old[g], m_new[g])
        m_next = []
        for g in groups:
            biased, m_g = stage_a(scores[g], functools.partial(block_bias, g, c), m_new[g])
            park(g, biased)
            m_next.append(m_g)
        return c, m_new, tuple(m_next)

    carry = (i, (m0,) * NSA_GROUPS, tuple(m_new))
    odd = i % 2
    carry = lax.cond(odd == 1, lambda c: slc_chunk(jnp.zeros((), I32), c), lambda c: c, carry)

    def slc_pair(pidx, c):
        first = odd + 2 * pidx
        return slc_chunk(first + 1, slc_chunk(first, c))

    prev, m_old, m_new = lax.fori_loop(0, i // 2, slc_pair, carry)

    c_far = jnp.maximum(i - 2, 0)
    c_near = jnp.maximum(i - 1, 0)
    pen_far = jnp.where(i >= 2, 0.0, NEG_INF).astype(F32)
    pen_near = jnp.where(i >= 1, 0.0, NEG_INF).astype(F32)

    scores = [qk(wk_ref, g, c_far) for g in groups]
    far = []
    for g in groups:
        stage_b(g, parked(g), svT_ref[0, prev, g], m_old[g], m_new[g])
        far.append(stage_a(scores[g], lambda s: s + (hi_ref[...] + pen_far), m0))
        finish(g, 1)

    scores = [qk(wk_ref, g, c_near) for g in groups]
    near = []
    for g in groups:
        stage_b(g, far[g][0], wvT_ref[0, c_far, g], m0, far[g][1])
        near.append(stage_a(scores[g], lambda s: s + pen_near, far[g][1]))
    scores = [qk(wk_ref, g, i) for g in groups]
    for g in groups:
        stage_b(g, near[g][0], wvT_ref[0, c_near, g], far[g][1], near[g][1])
        b_diag, m_diag = stage_a(scores[g], lambda s: s + lo_ref[...], near[g][1])
        stage_b(g, b_diag, wvT_ref[0, i, g], near[g][1], m_diag)
        finish(g, 2)

    o_ref[0] = jnp.concatenate([tot_ref[g, :, h * tq:(h + 1) * tq] for g in groups for h in range(NSA_HPG)],
                               axis=0).T.astype(BF16)


def _nsa(qT, kc, vcT, ov, sk, svT, wk, wvT, gT):
    B, _, T = gT.shape
    tq = NSA_TQ
    assert NSA_KC == tq and WINDOW == 2 * NSA_KC
    nr = kc.shape[2]
    nch = T // NSA_KC
    kk = np.arange(NSA_KC)[:, None]
    tt = np.arange(tq)[None, :]
    lo = jnp.asarray(np.where(kk <= tt, 0.0, NEG_INF), dtype=F32)
    hi = jnp.asarray(np.where(kk > tt, 0.0, NEG_INF), dtype=F32)
    G = NSA_GROUPS
    in_specs = [
        pl.BlockSpec((1, G, 1, NSA_HD, NSA_HPG * tq), lambda b, i: (b, 0, i, 0, 0)),
        pl.BlockSpec((1, G, nr, NSA_HD), lambda b, i: (b, 0, 0, 0)),
        pl.BlockSpec((1, G, NSA_HD, nr), lambda b, i: (b, 0, 0, 0)),
        pl.BlockSpec(ov.shape, lambda b, i: (0, 0)),
        pl.BlockSpec((1, G, T, NSA_HD), lambda b, i: (b, 0, 0, 0)),
        pl.BlockSpec((1, nch, G, NSA_VROWS, NSA_KC), lambda b, i: (b, 0, 0, 0, 0)),
        pl.BlockSpec((1, G, T, NSA_HD), lambda b, i: (b, 0, 0, 0)),
        pl.BlockSpec((1, nch, G, NSA_VROWS, NSA_KC), lambda b, i: (b, 0, 0, 0, 0)),
        pl.BlockSpec((1, 32, tq), lambda b, i: (b, 0, i)),
        pl.BlockSpec(lo.shape, lambda b, i: (0, 0)),
        pl.BlockSpec(hi.shape, lambda b, i: (0, 0)),
    ]
    n_sel = T // SLC_BLOCK
    hq = NSA_HPG * tq
    return pl.pallas_call(
        _nsa_body, out_shape=jax.ShapeDtypeStruct((B, T, NSA_HEADS * NSA_HD), BF16),
        grid=(B, T // tq), in_specs=in_specs,
        out_specs=pl.BlockSpec((1, tq, NSA_HEADS * NSA_HD), lambda b, i: (b, i, 0)),
        scratch_shapes=[pltpu.VMEM((G, n_sel, tq), F32), pltpu.VMEM((G, n_sel, tq), F32),
                        pltpu.VMEM((G, NSA_VROWS, hq), F32), pltpu.VMEM((G, NSA_HD, hq), F32),
                        pltpu.VMEM((G, NSA_HPG, NSA_KC, tq), F32)],
        compiler_params=_cparams(("parallel", "parallel")), name="nsa",
    )(qT, kc, vcT, ov, sk, svT, wk, wvT, gT, lo, hi)


def _ret_body(q_ref, kT_ref, v_ref, dec_ref, xi_ref, zeta_ref, cd_ref, gg_ref, gb_ref, o_ref, r_ref):
    C = RET_CHUNK

    @pl.when(pl.program_id(1) == 0)
    def _():
        r_ref[...] = jnp.zeros(r_ref.shape, F32)

    for bb in range(q_ref.shape[0]):
        for h in range(RET_HEADS):
            dk = slice(h * RET_DK, (h + 1) * RET_DK)
            dv = slice(h * RET_DV, (h + 1) * RET_DV)
            dec = dec_ref[h]
            xi = xi_ref[h]
            zeta = zeta_ref[h]
            cd = cd_ref[h]
            gg = gg_ref[:, dv]
            gb = gb_ref[:, dv]
            r = r_ref[bb, h]
            for n in range(RET_STEP_CHUNKS):
                rows = slice(n * C, (n + 1) * C)
                qc = q_ref[bb, rows, dk]
                kT = kT_ref[bb, n, dk, :]
                vc = v_ref[bb, rows, dv]
                s = _nn(qc, kT.astype(BF16)) * dec
                o = _nn(s.astype(BF16), vc) + _nn(qc, r.astype(BF16)) * xi
                r = r * cd + _nn((kT * zeta).astype(BF16), vc)
                mu = jnp.mean(o, axis=-1, keepdims=True)
                var = jnp.mean(jnp.square(o - mu), axis=-1, keepdims=True)
                o_ref[bb, rows, dv] = (o - mu) * lax.rsqrt(var + GN_EPS) * gg + gb
            r_ref[bb, h] = r


def _retention(rq, rkT, rv, dec, xi, zeta, cd, gn_g, gn_b):
    B, T, _ = rq.shape
    ts = RET_STEP_CHUNKS * RET_CHUNK
    full = lambda a: pl.BlockSpec(a.shape, lambda b, j: (0,) * a.ndim)
    nb = RET_STEP_BATCH if B % RET_STEP_BATCH == 0 else 1
    in_specs = [
        pl.BlockSpec((nb, ts, RET_HEADS * RET_DK), lambda b, j: (b, j, 0)),
        pl.BlockSpec((nb, RET_STEP_CHUNKS, RET_HEADS * RET_DK, RET_CHUNK), lambda b, j: (b, j, 0, 0)),
        pl.BlockSpec((nb, ts, RET_HEADS * RET_DV), lambda b, j: (b, j, 0)),
    ] + [full(a) for a in (dec, xi, zeta, cd, gn_g, gn_b)]
    return pl.pallas_call(
        _ret_body, out_shape=jax.ShapeDtypeStruct((B, T, RET_HEADS * RET_DV), F32),
        grid=(B // nb, T // ts), in_specs=in_specs,
        out_specs=pl.BlockSpec((nb, ts, RET_HEADS * RET_DV), lambda b, j: (b, j, 0)),
        scratch_shapes=[pltpu.VMEM((nb, RET_HEADS, RET_DK, RET_DV), F32)],
        compiler_params=_cparams(("parallel", "arbitrary")), name="retention",
    )(rq, rkT, rv, dec, xi, zeta, cd, gn_g, gn_b)


def _layer_norm(y, g, b):
    mu = jnp.mean(y, axis=-1, keepdims=True)
    var = jnp.mean(jnp.square(y - mu), axis=-1, keepdims=True)
    return (y - mu) * lax.rsqrt(var + LN_EPS) * g + b


def _merge_body(alpha, x_ref, oa_ref, or_ref, wrg_ref, wmg_ref, wua_ref, wur_ref, wo_ref, g1_ref, b1_ref,
                wrh_ref, wrl_ref, rb_ref, x1_ref, eid_ref, wgt_ref):
    x = x_ref[...]
    xb = x.astype(BF16)
    rgate = jax.nn.silu(_nn(xb, wrg_ref[...]))
    o_ret = (or_ref[...] * rgate).astype(BF16)
    a = _nn(oa_ref[...], wua_ref[...])
    r = _nn(o_ret, wur_ref[...])
    mg = jax.nn.sigmoid(_nn(xb, wmg_ref[...]))
    merged = mg[:, :D_MODEL] * a + mg[:, D_MODEL:] * r
    mix = _nn(merged.astype(BF16), wo_ref[...])
    x1 = _layer_norm(alpha * x + mix, g1_ref[...], b1_ref[...])
    x1_ref[...] = x1

    xh = x1.astype(BF16)
    xl = (x1 - xh.astype(F32)).astype(BF16)
    wh = wrh_ref[...]
    lg = _nt(wh, xh) + _nt(wh, xl) + _nt(wrl_ref[...], xh) + rb_ref[...]
    ne = N_EXPERTS
    gl = lg[ne:ne + MOE_GROUPS]
    ge = jnp.exp(gl - jnp.max(gl, axis=0, keepdims=True))
    pg = ge / jnp.sum(ge, axis=0, keepdims=True)
    g_prob = jnp.max(pg, axis=0, keepdims=True)
    gi = lax.broadcasted_iota(I32, pg.shape, 0)
    g_idx = jnp.min(jnp.where(pg == g_prob, gi, MOE_GROUPS), axis=0, keepdims=True)
    inner = jnp.zeros((EXPERTS_PER_GROUP, lg.shape[1]), F32)
    for gq in range(MOE_GROUPS):
        inner = inner + jnp.where(g_idx == gq, lg[8 * gq:8 * (gq + 1)], 0.0)
    ei = lax.broadcasted_iota(I32, inner.shape, 0)
    m1 = jnp.max(inner, axis=0, keepdims=True)
    i1 = jnp.min(jnp.where(inner == m1, ei, EXPERTS_PER_GROUP), axis=0, keepdims=True)
    rest = jnp.where(ei == i1, -jnp.inf, inner)
    m2 = jnp.max(rest, axis=0, keepdims=True)
    i2 = jnp.min(jnp.where(rest == m2, ei, EXPERTS_PER_GROUP), axis=0, keepdims=True)
    e2 = jnp.exp(m2 - m1)
    den = 1.0 + e2
    w1 = (1.0 / den) * g_prob
    w2 = (e2 / den) * g_prob
    zi = jnp.zeros((6, lg.shape[1]), I32)
    eid_ref[...] = jnp.concatenate([g_idx * EXPERTS_PER_GROUP + i1, g_idx * EXPERTS_PER_GROUP + i2, zi], axis=0)
    wgt_ref[...] = jnp.concatenate([w1, w2, jnp.zeros((6, lg.shape[1]), F32)], axis=0)


def _merge(alpha, x2, oa2, or2, wrg, wmg, wua, wur, wo, g1, b1, wrh, wrl, rb):
    M, D = x2.shape
    tm = MERGE_TM
    full = lambda a: pl.BlockSpec(a.shape, lambda i: (0,) * a.ndim)
    in_specs = [pl.BlockSpec((tm, D), lambda i: (i, 0)),
                pl.BlockSpec((tm, oa2.shape[1]), lambda i: (i, 0)),
                pl.BlockSpec((tm, or2.shape[1]), lambda i: (i, 0))] + [
        full(a) for a in (wrg, wmg, wua, wur, wo, g1, b1, wrh, wrl, rb)]
    out_shape = [jax.ShapeDtypeStruct((M, D), F32),
                 jax.ShapeDtypeStruct((8, M), I32),
                 jax.ShapeDtypeStruct((8, M), F32)]
    out_specs = [pl.BlockSpec((tm, D), lambda i: (i, 0)),
                 pl.BlockSpec((8, tm), lambda i: (0, i)),
                 pl.BlockSpec((8, tm), lambda i: (0, i))]
    return pl.pallas_call(
        functools.partial(_merge_body, alpha), out_shape=out_shape, grid=(M // tm,),
        in_specs=in_specs, out_specs=out_specs,
        compiler_params=_cparams(("parallel",)), name="merge",
    )(x2, oa2, or2, wrg, wmg, wua, wur, wo, g1, b1, wrh, wrl, rb)


def _rank_body(eid_ref, tri_ref, low_ref, pos_ref, cnt_ref, carry_ref, off_ref):
    p = pl.program_id(0)
    j = pl.program_id(1)
    tb = tri_ref.shape[0]

    @pl.when((p == 0) & (j == 0))
    def _():
        carry_ref[...] = jnp.zeros(carry_ref.shape, F32)

    @pl.when((p == 1) & (j == 0))
    def _():
        tiles = jnp.floor((carry_ref[...] + (EXPERT_TM - 1)) * (1.0 / EXPERT_TM))
        off_ref[...] = _nn(low_ref[...], tiles.astype(BF16)) * EXPERT_TM
        carry_ref[...] = jnp.zeros(carry_ref.shape, F32)

    for s in range(eid_ref.shape[1] // tb):
        cols = slice(s * tb, (s + 1) * tb)
        e = eid_ref[:, cols]
        rows = lax.broadcasted_iota(I32, (N_EXPERTS, tb), 0)
        hit = rows == e
        per_expert = jnp.sum(jnp.where(hit, 1.0, 0.0), axis=1, keepdims=True)

        @pl.when(p == 0)
        def _():
            pos_ref[:, cols] = jnp.zeros((1, tb), I32)
            carry_ref[...] = carry_ref[...] + per_expert
            cnt_ref[...] = carry_ref[...]

        @pl.when(p == 1)
        def _():
            onehot = jnp.where(hit, 1.0, 0.0).astype(BF16)
            incl = _nn(onehot, tri_ref[...])
            base = carry_ref[:, 0:1] + off_ref[:, 0:1] - 1.0
            pos_ref[:, cols] = jnp.sum(jnp.where(hit, incl + base, 0.0), axis=0, keepdims=True).astype(I32)
            carry_ref[...] = carry_ref[...] + per_expert


def _rank(eid_flat, tri, low):
    n = eid_flat.shape[1]
    tb = RANK_STEP_BLOCKS * RANK_TB
    return pl.pallas_call(
        _rank_body,
        out_shape=[jax.ShapeDtypeStruct((1, n), I32), jax.ShapeDtypeStruct((N_EXPERTS, 128), F32)],
        grid=(2, n // tb),
        in_specs=[pl.BlockSpec((1, tb), lambda p, j: (0, j)), pl.BlockSpec(tri.shape, lambda p, j: (0, 0)),
                  pl.BlockSpec(low.shape, lambda p, j: (0, 0))],
        out_specs=[pl.BlockSpec((1, tb), lambda p, j: (0, j * p)),
                   pl.BlockSpec((N_EXPERTS, 128), lambda p, j: (0, 0))],
        scratch_shapes=[pltpu.VMEM((N_EXPERTS, 128), F32), pltpu.VMEM((N_EXPERTS, 128), F32)],
        compiler_params=_cparams(("arbitrary", "arbitrary")), name="rank",
    )(eid_flat, tri, low)


def _dispatch_body(pos_ref, ztile_ref, x_ref, xs_ref, zbuf_ref, sem, zsem):
    tm = DISPATCH_TM
    m_tok = pos_ref.shape[0] // 2
    base = pl.program_id(0) * tm

    @pl.when(pl.program_id(0) == 0)
    def _():
        zbuf_ref[...] = jnp.zeros(zbuf_ref.shape, F32)

        def zero_copy(e):
            z = pl.multiple_of(jnp.maximum(ztile_ref[e], 0), EXPERT_TM)
            return pltpu.make_async_copy(zbuf_ref, xs_ref.at[pl.ds(z, EXPERT_TM), :], zsem)

        for e in range(N_EXPERTS):
            @pl.when(ztile_ref[e] >= 0)
            def _():
                zero_copy(e).start()
        for e in range(N_EXPERTS):
            @pl.when(ztile_ref[e] >= 0)
            def _():
                zero_copy(e).wait()

        def tail_copy(t):
            return pltpu.make_async_copy(
                zbuf_ref, xs_ref.at[pl.ds(pl.multiple_of(t * EXPERT_TM, EXPERT_TM), EXPERT_TM), :], zsem)

        n_tiles = xs_ref.shape[0] // EXPERT_TM
        lax.fori_loop(ztile_ref[N_EXPERTS], n_tiles, lambda t, c: (tail_copy(t).start(), c)[1], 0)
        lax.fori_loop(ztile_ref[N_EXPERTS], n_tiles, lambda t, c: (tail_copy(t).wait(), c)[1], 0)

    def row_copy(r, p):
        return pltpu.make_async_copy(x_ref.at[pl.ds(r, 1), :], xs_ref.at[pl.ds(p, 1), :], sem)

    for r in range(tm):
        row_copy(r, pos_ref[base + r]).start()
        row_copy(r, pos_ref[m_tok + base + r]).start(priority=1)

    def drain(r, carry):
        row_copy(0, 0).wait()
        row_copy(0, 0).wait()
        return carry

    lax.fori_loop(0, tm, drain, 0, unroll=16)


def _dispatch(pos, ztile, x1, n_pad):
    M, D = x1.shape
    tm = DISPATCH_TM
    grid_spec = pltpu.PrefetchScalarGridSpec(
        num_scalar_prefetch=2, grid=(M // tm,),
        in_specs=[pl.BlockSpec((tm, D), lambda i, pos, zt: (i, 0))],
        out_specs=pl.BlockSpec(memory_space=pl.ANY),
        scratch_shapes=[pltpu.VMEM((EXPERT_TM, D), F32), pltpu.SemaphoreType.DMA, pltpu.SemaphoreType.DMA],
    )
    return pl.pallas_call(
        _dispatch_body, out_shape=jax.ShapeDtypeStruct((n_pad, D), F32), grid_spec=grid_spec,
        compiler_params=_cparams(("arbitrary",), has_side_effects=True, disable_bounds_checks=True),
        name="dispatch",
    )(pos, ztile, x1)


def _experts_body(te_ref, nu_ref, seg_ref, nxt_ref, xs_ref, wg_hbm, wu_hbm, wd_hbm, ys_ref,
                  wgb_ref, wub_ref, wdb_ref, sg_ref, su_ref, sd_ref, sem):
    tm = EXPERT_TM
    k = pl.program_id(0)
    last = nu_ref[0] - 1

    def fetch(e, slot):
        return (pltpu.make_async_copy(wg_hbm.at[e], sg_ref.at[slot], sem.at[0, slot]),
                pltpu.make_async_copy(wu_hbm.at[e], su_ref.at[slot], sem.at[1, slot]),
                pltpu.make_async_copy(wd_hbm.at[e], sd_ref.at[slot], sem.at[2, slot]))

    @pl.when(k == 0)
    def _():
        for c in fetch(te_ref[0], 0):
            c.start()

    for s in range(EXPERT_STEP_TILES):
        t = EXPERT_STEP_TILES * k + s
        rows = slice(s * tm, (s + 1) * tm)

        @pl.when((t <= last) & (seg_ref[t] >= 0))
        def _():
            slot = seg_ref[t]
            for c in fetch(te_ref[t], slot):
                c.wait()
            wgb_ref[...] = sg_ref[slot].astype(BF16)
            wub_ref[...] = su_ref[slot].astype(BF16)
            wdb_ref[...] = sd_ref[slot].astype(BF16)

            @pl.when(nxt_ref[t] >= 0)
            def _():
                for c in fetch(nxt_ref[t], 1 - slot):
                    c.start()

        @pl.when(t <= last)
        def _():
            xb = xs_ref[rows, :].astype(BF16)
            hg = _nn(xb, wgb_ref[...])
            hu = _nn(xb, wub_ref[...])
            h = (jax.nn.silu(hg) * hu).astype(BF16)
            ys_ref[rows, :] = _nn(h, wdb_ref[...])

        @pl.when(t > last)
        def _():
            ys_ref[rows, :] = jnp.zeros((tm, ys_ref.shape[1]), F32)


def _experts(tile_expert, n_used, seg_slot, next_expert, xs, wg, wu, wd):
    npad, D = xs.shape
    tm = EXPERT_STEP_TILES * EXPERT_TM
    grid_spec = pltpu.PrefetchScalarGridSpec(
        num_scalar_prefetch=4, grid=(npad // tm,),
        in_specs=[pl.BlockSpec((tm, D), lambda i, te, nu, sg, nx:
                               (jnp.minimum(i, (nu[0] - 1) // EXPERT_STEP_TILES), 0)),
                  pl.BlockSpec(memory_space=pl.ANY), pl.BlockSpec(memory_space=pl.ANY),
                  pl.BlockSpec(memory_space=pl.ANY)],
        out_specs=pl.BlockSpec((tm, D), lambda i, te, nu, sg, nx: (i, 0)),
        scratch_shapes=[pltpu.VMEM((D, D_FF), BF16), pltpu.VMEM((D, D_FF), BF16), pltpu.VMEM((D_FF, D), BF16),
                        pltpu.VMEM((2, D, D_FF), F32), pltpu.VMEM((2, D, D_FF), F32),
                        pltpu.VMEM((2, D_FF, D), F32), pltpu.SemaphoreType.DMA((3, 2))],
    )
    return pl.pallas_call(
        _experts_body, out_shape=jax.ShapeDtypeStruct((npad, D), F32), grid_spec=grid_spec,
        compiler_params=_cparams(("arbitrary",)), name="experts",
    )(tile_expert, n_used, seg_slot, next_expert, xs, wg, wu, wd)


def _combine_body(alpha, pos_ref, x1_ref, w_ref, eye_ref, g2_ref, b2_ref, ys_ref, o_ref,
                  buf_a, buf_b, sem_a, sem_b):
    tm = COMBINE_TM
    m_tok = pos_ref.shape[0] // 2
    k = pl.program_id(0)
    n = pl.num_programs(0)

    def row_copies(buf, sem, r, base):
        return (pltpu.make_async_copy(ys_ref.at[pl.ds(pos_ref[base + r], 1), :], buf.at[0, pl.ds(r, 1), :], sem),
                pltpu.make_async_copy(ys_ref.at[pl.ds(pos_ref[m_tok + base + r], 1), :],
                                      buf.at[1, pl.ds(r, 1), :], sem))

    def issue_inline(buf, sem, tile):
        for r in range(tm):
            for queue, c in enumerate(row_copies(buf, sem, r, tile * tm)):
                c.start(priority=queue)

    def drain(buf, sem):
        def body(r, carry):
            for c in row_copies(buf, sem, 0, 0):
                c.wait()
            return carry
        lax.fori_loop(0, tm, body, 0, unroll=16)

    def compute(buf, s):
        rows = slice(s * tm, (s + 1) * tm)
        w = w_ref[:, rows]
        eye = eye_ref[...]
        w_a = w.astype(BF16)
        w_b = (w - w_a.astype(F32)).astype(BF16)
        w_c = (w - w_a.astype(F32) - w_b.astype(F32)).astype(BF16)
        wcol = _nt(eye, w_a) + _nt(eye, w_b) + _nt(eye, w_c)
        moe = buf[0] * wcol[:, 0:1] + buf[1] * wcol[:, 1:2]
        o_ref[rows, :] = _layer_norm(alpha * x1_ref[rows, :] + moe, g2_ref[...], b2_ref[...])

    @pl.when(k == 0)
    def _():
        def body(r, carry):
            for queue, c in enumerate(row_copies(buf_a, sem_a, r, 0)):
                c.start(priority=queue)
            return carry
        lax.fori_loop(0, tm, body, 0, unroll=16)

    drain(buf_a, sem_a)
    issue_inline(buf_b, sem_b, 2 * k + 1)
    compute(buf_a, 0)
    drain(buf_b, sem_b)
    issue_inline(buf_a, sem_a, jnp.minimum(2 * k + 2, 2 * n - 2))
    compute(buf_b, 1)

    @pl.when(k == n - 1)
    def _():
        drain(buf_a, sem_a)


def _combine(alpha, pos, x1, wgt, eye, g2, b2, ys):
    M, D = x1.shape
    tm = COMBINE_TM
    grid_spec = pltpu.PrefetchScalarGridSpec(
        num_scalar_prefetch=1, grid=(M // (2 * tm),),
        in_specs=[pl.BlockSpec((2 * tm, D), lambda i, pos: (i, 0)),
                  pl.BlockSpec((8, 2 * tm), lambda i, pos: (0, i)),
                  pl.BlockSpec((tm, tm), lambda i, pos: (0, 0)),
                  pl.BlockSpec((1, D), lambda i, pos: (0, 0)),
                  pl.BlockSpec((1, D), lambda i, pos: (0, 0)),
                  pl.BlockSpec(memory_space=pl.ANY)],
        out_specs=pl.BlockSpec((2 * tm, D), lambda i, pos: (i, 0)),
        scratch_shapes=[pltpu.VMEM((2, tm, D), F32), pltpu.VMEM((2, tm, D), F32),
                        pltpu.SemaphoreType.DMA, pltpu.SemaphoreType.DMA],
    )
    return pl.pallas_call(
        functools.partial(_combine_body, alpha), out_shape=jax.ShapeDtypeStruct((M, D), F32),
        grid_spec=grid_spec, compiler_params=_cparams(("arbitrary",), disable_bounds_checks=True),
        name="combine",
    )(pos, x1, wgt, eye, g2, b2, ys)


def _rope_angles(pos, dim):
    half = dim // 2
    inv_freq = ROPE_THETA ** (-np.arange(half, dtype=np.float64) * 2.0 / dim)
    return pos.astype(np.float64)[:, None] * inv_freq[None, :]


def _tables(T):
    f32 = lambda a: jnp.asarray(np.ascontiguousarray(a), dtype=F32)
    ang = _rope_angles(np.arange(T), NSA_HD)
    cos, sin = np.cos(ang), np.sin(ang)
    c64 = np.tile(cos, (1, 4))
    s64 = np.tile(np.concatenate([-sin, sin], axis=1), (1, 2))
    ang2 = _rope_angles(np.arange(T), RET_DK)
    cos2, sin2 = np.cos(ang2), np.sin(ang2)
    c128 = np.tile(cos2, (1, 2))
    s128 = np.concatenate([-sin2, sin2], axis=1)
    n_rows = T // CMP_STRIDE
    angc = _rope_angles(np.arange(n_rows) * CMP_STRIDE + CMP_BLOCK - 1, NSA_HD)
    cc = np.tile(np.cos(angc), (1, 2))
    sc = np.tile(np.sin(angc), (1, 2))
    return tuple(f32(a) for a in (c64, s64, c128, s128, cos.T, sin.T, cos2.T, sin2.T)), f32(cc), f32(sc)


def _overlap_matrix(T):
    n_rows = T // CMP_STRIDE
    n_sel = T // SLC_BLOCK
    cmp_start = np.arange(n_rows) * CMP_STRIDE
    sel_start = np.arange(n_sel) * SLC_BLOCK
    ov = np.clip(np.minimum(cmp_start[None, :] + CMP_BLOCK, sel_start[:, None] + SLC_BLOCK)
                 - np.maximum(cmp_start[None, :], sel_start[:, None]), 0, None)
    return jnp.asarray(ov.astype(np.float32) / CMP_STRIDE, dtype=BF16)


def _retention_tables():
    C = RET_CHUNK
    f32 = lambda a: jnp.asarray(np.ascontiguousarray(a), dtype=F32)
    gamma = 1.0 - 2.0 ** (-5.0 - np.arange(RET_HEADS, dtype=np.float64))
    log_g = np.log(gamma)
    i = np.arange(C, dtype=np.float64)
    diff = i[:, None] - i[None, :]
    dec = np.where(diff >= 0, np.exp(np.maximum(diff, 0.0) * log_g[:, None, None]), 0.0)
    xi = np.exp((i + 1.0) * log_g[:, None])
    zeta = np.exp((C - 1.0 - i) * log_g[:, None])
    cd = np.exp(C * log_g)
    xi_b = np.broadcast_to(xi[:, :, None], (RET_HEADS, C, RET_DV))
    cd_b = np.broadcast_to(cd[:, None, None], (RET_HEADS, 1, RET_DV))
    return f32(dec), f32(xi_b), f32(zeta[:, None, :]), f32(cd_b)


def _rot_half_cols(w):
    half = w.shape[-1] // 2
    return jnp.concatenate([-w[..., half:], w[..., :half]], axis=-1)


def kernel(x, w_in, cmp_pos_k, cmp_k_w1, cmp_k_b1, cmp_k_w2, cmp_k_b2, cmp_pos_v, cmp_v_w1, cmp_v_b1, cmp_v_w2, cmp_v_b2, ret_gn_g, ret_gn_b, w_up_attn, w_up_ret, w_out, ln1_g, ln1_b, router_group_w, router_group_b, router_inner_w, router_inner_b, expert_w_gate, expert_w_up, expert_w_down, ln2_g, ln2_b):
    B, T, D = x.shape
    M = B * T
    depth = w_in.shape[0]
    alpha = (2.0 * depth) ** 0.25
    tabs, cc, sc = _tables(T)
    ov = _overlap_matrix(T)
    dec, xi_b, zeta, cd_b = _retention_tables()
    tri = jnp.asarray(np.triu(np.ones((RANK_TB, RANK_TB), np.float32)), dtype=BF16)
    low = jnp.asarray(np.tril(np.ones((N_EXPERTS, N_EXPERTS), np.float32), k=-1), dtype=BF16)
    eye = jnp.asarray(np.eye(COMBINE_TM, dtype=np.float32), dtype=BF16)
    n_pad = 2 * M + N_EXPERTS * EXPERT_TM
    n_tiles = n_pad // EXPERT_TM

    for l in range(depth):
        w = w_in[l]
        col = lambda n: w[:, _OFF[n][0]:_OFF[n][1]]
        wn = jnp.concatenate([col("cmp_k"), col("cmp_v"), col("slc_k"), col("win_k"), col("ret_q"), col("ret_v")],
                             axis=1).astype(BF16)
        wt = jnp.concatenate([col("nsa_q"), col("slc_v"), col("win_v"), col("ret_k"), col("nsa_gate"),
                              jnp.zeros((D, 8), F32)], axis=1).T.astype(BF16)
        qT, ckv, sk, wk, svT, wvT, gT, rq, rkT, rv = _proj(x, wn, wt, tabs)

        kc, vcT = _compress(
            ckv, cmp_pos_k[l].reshape(2, -1), cmp_pos_v[l].reshape(2, -1),
            cmp_k_w1[l], cmp_k_b1[l][None, :], cmp_k_w2[l], _rot_half_cols(cmp_k_w2[l]),
            cmp_k_b2[l][None, :], _rot_half_cols(cmp_k_b2[l])[None, :], cc, sc,
            cmp_v_w1[l], cmp_v_b1[l][None, :], cmp_v_w2[l].T, cmp_v_b2[l][:, None])
        o_attn = _nsa(qT, kc, vcT, ov, sk, svT, wk, wvT, gT)
        o_ret = _retention(rq, rkT, rv, dec, xi_b, zeta, cd_b, ret_gn_g[l][None, :], ret_gn_b[l][None, :])

        wr = jnp.concatenate([router_inner_w[l].transpose(0, 2, 1).reshape(N_EXPERTS, D),
                              router_group_w[l].T, jnp.zeros((4, D), F32)], axis=0)
        wrh = wr.astype(BF16)
        wrl = (wr - wrh.astype(F32)).astype(BF16)
        rb = jnp.concatenate([router_inner_b[l].reshape(-1), router_group_b[l], jnp.zeros((4,), F32)])[:, None]
        x1, eid, wgt = _merge(
            alpha, x.reshape(M, D), o_attn.reshape(M, -1), o_ret.reshape(M, -1),
            col("ret_gate").astype(BF16), col("merge_gate").astype(BF16), w_up_attn[l].astype(BF16),
            w_up_ret[l].astype(BF16), w_out[l].astype(BF16), ln1_g[l][None, :], ln1_b[l][None, :], wrh, wrl, rb)

        eid_flat = eid[:2].reshape(1, 2 * M)
        pos, cnt = _rank(eid_flat, tri, low)
        pos = pos[0]
        counts = cnt[:, 0].astype(I32)
        tiles_per = (counts + EXPERT_TM - 1) // EXPERT_TM
        tile_end = jnp.cumsum(tiles_per)
        tile_ids = jnp.arange(n_tiles, dtype=I32)
        tile_expert = jnp.minimum(jnp.sum((tile_end[None, :] <= tile_ids[:, None]).astype(I32), axis=1),
                                  N_EXPERTS - 1).astype(I32)
        n_used = tile_end[-1:].astype(I32)

        ztile = jnp.concatenate([jnp.where(tiles_per > 0, (tile_end - 1) * EXPERT_TM, -1), tile_end[-1:]]).astype(I32)
        xs = _dispatch(pos, ztile, x1, n_pad)
        prev_expert = jnp.concatenate([jnp.full((1,), -1, I32), tile_expert[:-1]])
        is_first = (tile_ids < n_used[0]) & (tile_expert != prev_expert)
        seg_slot = jnp.where(is_first, (jnp.cumsum(is_first.astype(I32)) - 1) % 2, -1).astype(I32)
        own = tile_expert[:, None] == jnp.arange(N_EXPERTS, dtype=I32)[None, :]
        seg_end = jnp.sum(jnp.where(own, tile_end[None, :], 0), axis=1)
        expert_at_end = jnp.minimum(jnp.sum((tile_end[None, :] <= seg_end[:, None]).astype(I32), axis=1),
                                    N_EXPERTS - 1)
        next_expert = jnp.where(is_first & (seg_end < n_used[0]), expert_at_end, -1).astype(I32)
        ys = _experts(tile_expert, n_used, seg_slot, next_expert, xs,
                      expert_w_gate[l], expert_w_up[l], expert_w_down[l])
        x = _combine(alpha, pos, x1, wgt, eye, ln2_g[l][None, :], ln2_b[l][None, :], ys).reshape(B, T, D)
    return x
```

```python
import functools

import numpy as np
import jax
import jax.numpy as jnp
from jax import lax
from jax.experimental import pallas as pl
from jax.experimental.pallas import tpu as pltpu

F32 = jnp.float32
BF16 = jnp.bfloat16
I32 = jnp.int32

D_MODEL = 1024
NSA_HEADS = 8
NSA_HD = 64
NSA_GROUPS = 2
NSA_HPG = NSA_HEADS // NSA_GROUPS
CMP_BLOCK = 32
CMP_STRIDE = 16
CMP_HIDDEN = 256
SLC_BLOCK = 64
SLC_TOPK = 16
WINDOW = 512
RET_HEADS = 4
RET_DK = 128
RET_DV = 256
RET_CHUNK = 128
MOE_GROUPS = 4
EXPERTS_PER_GROUP = 8
N_EXPERTS = MOE_GROUPS * EXPERTS_PER_GROUP
D_FF = 512
ROPE_THETA = 10000.0
LN_EPS = 1e-5
GN_EPS = 1e-5
NEG_INF = -1e30
LOG2_E = 1.4426950408889634

VMEM_LIMIT_V7X = 56 * 1024 * 1024

_OFF = {}
_o = 0
for _n, _w in (("nsa_q", 512), ("cmp_k", 128), ("cmp_v", 128), ("slc_k", 128), ("slc_v", 128),
               ("win_k", 128), ("win_v", 128), ("nsa_gate", 24), ("ret_q", 512), ("ret_k", 512),
               ("ret_v", 1024), ("ret_gate", 1024), ("merge_gate", 2048)):
    _OFF[_n] = (_o, _o + _w)
    _o += _w

PROJ_TM = 1024
NSA_TQ = 256
NSA_KC = 256
NSA_VROWS = 80
RET_STEP_CHUNKS = 8
RET_STEP_BATCH = 2
MERGE_TM = 512
RANK_TB = 1024
RANK_STEP_BLOCKS = 4
DISPATCH_TM = 2048
EXPERT_TM = 256
EXPERT_STEP_TILES = 4
COMBINE_TM = 256


def _cparams(sem, **kw):
    return pltpu.CompilerParams(dimension_semantics=sem, vmem_limit_bytes=VMEM_LIMIT_V7X, **kw)


def _nt(a, b):
    return lax.dot_general(a, b, (((1,), (1,)), ((), ())), preferred_element_type=F32)


def _nn(a, b):
    return jnp.dot(a, b, preferred_element_type=F32)


def _proj_body(x_ref, wn_ref, wt_ref, c64_ref, s64_ref, c128_ref, s128_ref,
               ct64_ref, st64_ref, ct128_ref, st128_ref,
               qT_ref, ckv_ref, sk_ref, wk_ref, svT_ref, wvT_ref, gT_ref, rq_ref, rkT_ref, rv_ref, ckv_scr):
    tm = PROJ_TM
    xb = x_ref[0].astype(BF16)

    def nn(a, b):
        return _nn(xb, wn_ref[:, a:b])

    def nt(a, b):
        return _nt(wt_ref[a:b, :], xb)

    ckv = nn(0, 256)
    for c in range(2):
        ckv_scr[c] = ckv[:, 128 * c:128 * (c + 1)]
    for l in range(CMP_STRIDE):
        for c in range(2):
            rows = ckv_scr[c, pl.ds(l, tm // CMP_STRIDE, stride=CMP_STRIDE), :]
            for jj in range(2):
                ckv_ref[0, 2 * c + jj, :, 64 * l:64 * (l + 1)] = rows[:, 64 * jj:64 * (jj + 1)]

    lane = lax.broadcasted_iota(I32, (tm, 128), 1)
    first = (lane % 64) < 32
    c64 = c64_ref[...]
    s64 = s64_ref[...]

    def rope64(k):
        rot = jnp.where(first, pltpu.roll(k, 96, 1), pltpu.roll(k, 32, 1))
        return k * c64 + rot * s64

    sk = rope64(nn(256, 384)).astype(BF16)
    sk_ref[0, 0] = sk[:, :64]
    sk_ref[0, 1] = sk[:, 64:]
    wk = rope64(nn(384, 512)).astype(BF16)
    wk_ref[0, 0] = wk[:, :64]
    wk_ref[0, 1] = wk[:, 64:]

    c128 = c128_ref[...]
    s128 = s128_ref[...]
    rq = nn(512, 1024)
    for h in range(RET_HEADS):
        ch = rq[:, 128 * h:128 * (h + 1)]
        rq_ref[0, :, 128 * h:128 * (h + 1)] = (ch * c128 + pltpu.roll(ch, 64, 1) * s128).astype(BF16)
    rv_ref[0] = nn(1024, 2048).astype(BF16)

    ct = ct64_ref[...]
    st = st64_ref[...]
    qT = nt(0, 512)
    scale_q = NSA_HD ** -0.5 * LOG2_E
    tq = NSA_TQ
    for hh in range(NSA_HEADS):
        g, h = divmod(hh, NSA_HPG)
        x1 = qT[64 * hh:64 * hh + 32]
        x2 = qT[64 * hh + 32:64 * hh + 64]
        o1 = ((x1 * ct - x2 * st) * scale_q).astype(BF16)
        o2 = ((x1 * st + x2 * ct) * scale_q).astype(BF16)
        for it in range(tm // tq):
            qT_ref[0, g, it, 0:32, h * tq:(h + 1) * tq] = o1[:, it * tq:(it + 1) * tq]
            qT_ref[0, g, it, 32:64, h * tq:(h + 1) * tq] = o2[:, it * tq:(it + 1) * tq]

    svT = nt(512, 640).astype(BF16)
    wvT = nt(640, 768).astype(BF16)
    row16 = lax.broadcasted_iota(I32, (NSA_VROWS - NSA_HD, NSA_KC), 0)
    ones_blk = jnp.where(row16 == 0, 1.0, 0.0).astype(BF16)
    for c in range(tm // NSA_KC):
        for g in range(NSA_GROUPS):
            for vT, ref in ((svT, svT_ref), (wvT, wvT_ref)):
                ref[0, c, g, 0:NSA_HD, :] = vT[64 * g:64 * (g + 1), NSA_KC * c:NSA_KC * (c + 1)]
                ref[0, c, g, NSA_HD:NSA_VROWS, :] = ones_blk

    ct2 = ct128_ref[...]
    st2 = st128_ref[...]
    rkT = nt(768, 1280)
    scale_k = RET_DK ** -0.5
    for h in range(RET_HEADS):
        x1 = rkT[128 * h:128 * h + 64]
        x2 = rkT[128 * h + 64:128 * h + 128]
        o1 = (x1 * ct2 - x2 * st2) * scale_k
        o2 = (x1 * st2 + x2 * ct2) * scale_k
        for c in range(tm // RET_CHUNK):
            rkT_ref[0, c, 128 * h:128 * h + 64, :] = o1[:, 128 * c:128 * (c + 1)]
            rkT_ref[0, c, 128 * h + 64:128 * h + 128, :] = o2[:, 128 * c:128 * (c + 1)]

    gT_ref[0] = jax.nn.sigmoid(nt(1280, 1312))


def _proj(x, wn, wt, tabs):
    B, T, D = x.shape
    tm = PROJ_TM
    c64, s64, c128, s128, ct64, st64, ct128, st128 = tabs
    const = lambda b, i: (0, 0)
    in_specs = [
        pl.BlockSpec((1, tm, D), lambda b, i: (b, i, 0)),
        pl.BlockSpec(wn.shape, const),
        pl.BlockSpec(wt.shape, const),
        pl.BlockSpec((tm, 128), lambda b, i: (i, 0)),
        pl.BlockSpec((tm, 128), lambda b, i: (i, 0)),
        pl.BlockSpec((tm, 128), lambda b, i: (i, 0)),
        pl.BlockSpec((tm, 128), lambda b, i: (i, 0)),
        pl.BlockSpec((32, tm), lambda b, i: (0, i)),
        pl.BlockSpec((32, tm), lambda b, i: (0, i)),
        pl.BlockSpec((64, tm), lambda b, i: (0, i)),
        pl.BlockSpec((64, tm), lambda b, i: (0, i)),
    ]
    out_shape = [
        jax.ShapeDtypeStruct((B, NSA_GROUPS, T // NSA_TQ, NSA_HD, NSA_HPG * NSA_TQ), BF16),
        jax.ShapeDtypeStruct((B, 4, T // CMP_STRIDE, CMP_STRIDE * NSA_HD), F32),
        jax.ShapeDtypeStruct((B, 2, T, 64), BF16),
        jax.ShapeDtypeStruct((B, 2, T, 64), BF16),
        jax.ShapeDtypeStruct((B, T // NSA_KC, NSA_GROUPS, NSA_VROWS, NSA_KC), BF16),
        jax.ShapeDtypeStruct((B, T // NSA_KC, NSA_GROUPS, NSA_VROWS, NSA_KC), BF16),
        jax.ShapeDtypeStruct((B, 32, T), F32),
        jax.ShapeDtypeStruct((B, T, 512), BF16),
        jax.ShapeDtypeStruct((B, T // RET_CHUNK, 512, RET_CHUNK), F32),
        jax.ShapeDtypeStruct((B, T, 1024), BF16),
    ]
    out_specs = [
        pl.BlockSpec((1, NSA_GROUPS, tm // NSA_TQ, NSA_HD, NSA_HPG * NSA_TQ), lambda b, i: (b, 0, i, 0, 0)),
        pl.BlockSpec((1, 4, tm // CMP_STRIDE, CMP_STRIDE * NSA_HD), lambda b, i: (b, 0, i, 0)),
        pl.BlockSpec((1, 2, tm, 64), lambda b, i: (b, 0, i, 0)),
        pl.BlockSpec((1, 2, tm, 64), lambda b, i: (b, 0, i, 0)),
        pl.BlockSpec((1, tm // NSA_KC, NSA_GROUPS, NSA_VROWS, NSA_KC), lambda b, i: (b, i, 0, 0, 0)),
        pl.BlockSpec((1, tm // NSA_KC, NSA_GROUPS, NSA_VROWS, NSA_KC), lambda b, i: (b, i, 0, 0, 0)),
        pl.BlockSpec((1, 32, tm), lambda b, i: (b, 0, i)),
        pl.BlockSpec((1, tm, 512), lambda b, i: (b, i, 0)),
        pl.BlockSpec((1, tm // RET_CHUNK, 512, RET_CHUNK), lambda b, i: (b, i, 0, 0)),
        pl.BlockSpec((1, tm, 1024), lambda b, i: (b, i, 0)),
    ]
    return pl.pallas_call(
        _proj_body, out_shape=out_shape, grid=(B, T // tm), in_specs=in_specs, out_specs=out_specs,
        scratch_shapes=[pltpu.VMEM((2, tm, 128), F32)],
        compiler_params=_cparams(("parallel", "parallel")), name="proj",
    )(x, wn, wt, c64, s64, c128, s128, ct64, st64, ct128, st128)


def _compress_body(uk_ref, uv_ref, posk_ref, posv_ref, w1k_ref, b1k_ref, w2k_ref, w2kr_ref, b2k_ref, b2kr_ref,
                   cc_ref, sc_ref, w1v_ref, b1v_ref, w2vT_ref, b2vT_ref, kc_ref, vcT_ref):
    half = CMP_STRIDE * NSA_HD

    def hidden(u_ref, pos_ref, w1_ref, b1_ref):
        u = u_ref[0, 0]
        top = (u + pos_ref[0:1, :]).astype(BF16)
        bot = (u + pos_ref[1:2, :]).astype(BF16)
        a = _nn(top, w1_ref[0:half, :].astype(BF16))
        bm = _nn(bot, w1_ref[half:2 * half, :].astype(BF16))
        n = bm.shape[0]
        return jax.nn.gelu(a + pltpu.roll(bm, n - 1, 0) + b1_ref[...])

    hk = hidden(uk_ref, posk_ref, w1k_ref, b1k_ref).astype(BF16)
    k = _nn(hk, w2k_ref[...].astype(BF16)) + b2k_ref[...]
    kr = _nn(hk, w2kr_ref[...].astype(BF16)) + b2kr_ref[...]
    kc_ref[0, 0] = (k * cc_ref[...] + kr * sc_ref[...]).astype(BF16)

    hv = hidden(uv_ref, posv_ref, w1v_ref, b1v_ref).astype(BF16)
    vcT_ref[0, 0] = (_nt(w2vT_ref[...].astype(BF16), hv) + b2vT_ref[...]).astype(BF16)


def _compress(ckv, posk, posv, w1k, b1k, w2k, w2kr, b2k, b2kr, cc, sc, w1v, b1v, w2vT, b2vT):
    B, _, nr, _ = ckv.shape
    u = ckv
    full = lambda a: pl.BlockSpec(a.shape, lambda b, g: (0,) * a.ndim)
    in_specs = [
        pl.BlockSpec((1, 1, nr, 1024), lambda b, g: (b, g, 0, 0)),
        pl.BlockSpec((1, 1, nr, 1024), lambda b, g: (b, g + 2, 0, 0)),
    ] + [full(a) for a in (posk, posv, w1k, b1k, w2k, w2kr, b2k, b2kr, cc, sc, w1v, b1v, w2vT, b2vT)]
    out_shape = [jax.ShapeDtypeStruct((B, NSA_GROUPS, nr, NSA_HD), BF16),
                 jax.ShapeDtypeStruct((B, NSA_GROUPS, NSA_HD, nr), BF16)]
    out_specs = [pl.BlockSpec((1, 1, nr, NSA_HD), lambda b, g: (b, g, 0, 0)),
                 pl.BlockSpec((1, 1, NSA_HD, nr), lambda b, g: (b, g, 0, 0))]
    return pl.pallas_call(
        _compress_body, out_shape=out_shape, grid=(B, NSA_GROUPS), in_specs=in_specs, out_specs=out_specs,
        compiler_params=_cparams(("parallel", "parallel")), name="compress",
    )(u, u, posk, posv, w1k, b1k, w2k, w2kr, b2k, b2kr, cc, sc, w1v, b1v, w2vT, b2vT)


def _nsa_body(q_ref, kc_ref, vcT_ref, ov_ref, sk_ref, svT_ref, wk_ref, wvT_ref, gT_ref, lo_ref, hi_ref, o_ref,
              score_ref, bias_ref, acc_ref, tot_ref, s_ref):
    tq = NSA_TQ
    kc_n = NSA_KC
    hq = NSA_HPG * tq
    groups = range(NSA_GROUPS)
    i = pl.program_id(1)
    t0 = i * tq
    t_row = t0 + lax.broadcasted_iota(I32, (1, tq), 1)
    t_row4 = t0 + lax.broadcasted_iota(I32, (1, hq), 1) % tq
    n_cmp = kc_ref.shape[2]
    n_sel = ov_ref.shape[0]
    q = [q_ref[0, g, 0] for g in groups]

    def gate4(g, br):
        return jnp.concatenate([gT_ref[0, (g * NSA_HPG + h) * 3 + br:(g * NSA_HPG + h) * 3 + br + 1, :]
                                for h in range(NSA_HPG)], axis=1)

    cmp_end = lax.broadcasted_iota(I32, (n_cmp, 1), 0) * CMP_STRIDE + (CMP_BLOCK - 1)
    cbias = jnp.where(cmp_end <= t_row4, 0.0, NEG_INF)
    any_valid = jnp.where(t_row4 >= CMP_BLOCK - 1, 1.0, 0.0)
    blk_t = t_row // SLC_BLOCK
    sub = lax.broadcasted_iota(I32, (8, 1), 0)

    def select_blocks(g, l_cmp, l_sel):
        s = _nn(kc_ref[0, g, 0:l_cmp, :], q[g]) + cbias[0:l_cmp]
        e = jnp.exp2(s - jnp.max(s, axis=0, keepdims=True))
        p = e * (any_valid / jnp.sum(e, axis=0, keepdims=True))
        tot_ref[g] = gate4(g, 0) * _nn(vcT_ref[0, g, :, 0:l_cmp], p.astype(BF16))
        psum = p[:, 0:tq]
        for h in range(1, NSA_HPG):
            psum = psum + p[:, h * tq:(h + 1) * tq]
        p_hi = psum.astype(BF16)
        p_lo = (psum - p_hi.astype(F32)).astype(BF16)
        ov = ov_ref[0:l_sel, 0:l_cmp]
        imp = _nn(ov, p_hi) + _nn(ov, p_lo)
        jc = lax.broadcasted_iota(I32, (l_sel, 1), 0)
        forced = (jc == 0) | (jc == blk_t) | (jc == blk_t - 1)
        score = jnp.where(forced, 1e9, jnp.where(jc <= blk_t, imp, -1e9)).astype(F32)
        score_ref[g, 0:l_sel, :] = score
        n_slab = l_sel // 8
        slabs = [score[8 * v:8 * (v + 1)] for v in range(n_slab)]
        cnt = [jnp.zeros((8, tq), F32) for _ in range(n_slab)]
        for r in range(l_sel):
            row = jnp.broadcast_to(score_ref[g, r:r + 1, :], (8, tq))
            for v in range(n_slab):
                if r < 8 * v:
                    ahead = jnp.where(row >= slabs[v], 1.0, 0.0)
                elif r >= 8 * (v + 1):
                    ahead = jnp.where(row > slabs[v], 1.0, 0.0)
                else:
                    ahead = jnp.where(sub + 8 * v > r, jnp.where(row >= slabs[v], 1.0, 0.0),
                                      jnp.where(row > slabs[v], 1.0, 0.0))
                cnt[v] = cnt[v] + ahead
        for v in range(n_slab):
            bias_ref[g, 8 * v:8 * (v + 1), :] = jnp.where(
                cnt[v] < float(SLC_TOPK), jnp.where(sub + 8 * v <= blk_t, 0.0, NEG_INF), NEG_INF)
        if l_sel < n_sel:
            bias_ref[g, l_sel:n_sel, :] = jnp.full((n_sel - l_sel, tq), NEG_INF, F32)

    n_var = 4
    tiles_per_var = (n_sel * SLC_BLOCK // tq) // n_var
    for var in range(n_var):
        l_sel = (var + 1) * tiles_per_var * (tq // SLC_BLOCK)
        l_cmp = min(n_cmp, -(-((var + 1) * tiles_per_var * (tq // CMP_STRIDE)) // 128) * 128)

        @pl.when(i // tiles_per_var == var)
        def _():
            for g in groups:
                select_blocks(g, l_cmp, l_sel)

    def qk(k_ref, g, c):
        k_c = k_ref[0, g, pl.ds(pl.multiple_of(c * kc_n, kc_n), kc_n), :]
        return [_nn(k_c, q[g][:, h * tq:(h + 1) * tq]) for h in range(NSA_HPG)]

    def stage_a(scores, add_bias, m):
        biased = [add_bias(scores[h]) for h in range(NSA_HPG)]
        m_new = tuple(jnp.maximum(m[h], jnp.max(biased[h], axis=0, keepdims=True)) for h in range(NSA_HPG))
        return biased, m_new

    def stage_b(g, biased, vT_c, m_old, m_new):
        for h in range(NSA_HPG):
            hs = slice(h * tq, (h + 1) * tq)
            alpha = jnp.exp2(m_old[h] - m_new[h])
            p = jnp.exp2((biased[h] - m_new[h]).astype(BF16))
            acc_ref[g, :, hs] = alpha * acc_ref[g, :, hs] + _nn(vT_c, p)

    nb = kc_n // SLC_BLOCK

    def block_bias(g, c, s):
        return jnp.concatenate([s[SLC_BLOCK * b:SLC_BLOCK * (b + 1)] + bias_ref[g, pl.ds(c * nb + b, 1), :]
                                for b in range(nb)], axis=0)

    def finish(g, br):
        acc = acc_ref[g]
        tot_ref[g] = tot_ref[g] + gate4(g, br) * (acc[0:NSA_HD] / acc[NSA_HD:NSA_HD + 1])
        acc_ref[g] = jnp.zeros(acc.shape, F32)

    def park(g, biased):
        for h in range(NSA_HPG):
            s_ref[g, h] = biased[h]

    def parked(g):
        return [s_ref[g, h] for h in range(NSA_HPG)]

    m0 = tuple(jnp.full((1, tq), NEG_INF, F32) for _ in range(NSA_HPG))

    acc_ref[...] = jnp.zeros(acc_ref.shape, F32)
    m_new = []
    for g in groups:
        biased, m_g = stage_a(qk(sk_ref, g, i), lambda s, g=g: block_bias(g, i, s) + lo_ref[...], m0)
        park(g, biased)
        m_new.append(m_g)

    def slc_chunk(c, carry):
        prev, m_old, m_new = carry
        scores = [qk(sk_ref, g, c) for g in groups]
        for g in groups:
            stage_b(g, parked(g), svT_ref[0, prev, g], m_old[g], m_new[g])
        m_next = []
        for g in groups:
            biased, m_g = stage_a(scores[g], functools.partial(block_bias, g, c), m_new[g])
            park(g, biased)
            m_next.append(m_g)
        return c, m_new, tuple(m_next)

    carry = (i, (m0,) * NSA_GROUPS, tuple(m_new))
    odd = i % 2
    carry = lax.cond(odd == 1, lambda c: slc_chunk(jnp.zeros((), I32), c), lambda c: c, carry)

    def slc_pair(pidx, c):
        first = odd + 2 * pidx
        return slc_chunk(first + 1, slc_chunk(first, c))

    prev, m_old, m_new = lax.fori_loop(0, i // 2, slc_pair, carry)

    c_far = jnp.maximum(i - 2, 0)
    c_near = jnp.maximum(i - 1, 0)
    pen_far = jnp.where(i >= 2, 0.0, NEG_INF).astype(F32)
    pen_near = jnp.where(i >= 1, 0.0, NEG_INF).astype(F32)

    scores = [qk(wk_ref, g, c_far) for g in groups]
    far = []
    for g in groups:
        stage_b(g, parked(g), svT_ref[0, prev, g], m_old[g], m_new[g])
        far.append(stage_a(scores[g], lambda s: s + (hi_ref[...] + pen_far), m0))
        finish(g, 1)

    scores = [qk(wk_ref, g, c_near) for g in groups]
    near = []
    for g in groups:
        stage_b(g, far[g][0], wvT_ref[0, c_far, g], m0, far[g][1])
        near.append(stage_a(scores[g], lambda s: s + pen_near, far[g][1]))
    scores = [qk(wk_ref, g, i) for g in groups]
    for g in groups:
        stage_b(g, near[g][0], wvT_ref[0, c_near, g], far[g][1], near[g][1])
        b_diag, m_diag = stage_a(scores[g], lambda s: s + lo_ref[...], near[g][1])
        stage_b(g, b_diag, wvT_ref[0, i, g], near[g][1], m_diag)
        finish(g, 2)

    o_ref[0] = jnp.concatenate([tot_ref[g, :, h * tq:(h + 1) * tq] for g in groups for h in range(NSA_HPG)],
                               axis=0).T.astype(BF16)


def _nsa(qT, kc, vcT, ov, sk, svT, wk, wvT, gT):
    B, _, T = gT.shape
    tq = NSA_TQ
    assert NSA_KC == tq and WINDOW == 2 * NSA_KC
    nr = kc.shape[2]
    nch = T // NSA_KC
    kk = np.arange(NSA_KC)[:, None]
    tt = np.arange(tq)[None, :]
    lo = jnp.asarray(np.where(kk <= tt, 0.0, NEG_INF), dtype=F32)
    hi = jnp.asarray(np.where(kk > tt, 0.0, NEG_INF), dtype=F32)
    G = NSA_GROUPS
    in_specs = [
        pl.BlockSpec((1, G, 1, NSA_HD, NSA_HPG * tq), lambda b, i: (b, 0, i, 0, 0)),
        pl.BlockSpec((1, G, nr, NSA_HD), lambda b, i: (b, 0, 0, 0)),
        pl.BlockSpec((1, G, NSA_HD, nr), lambda b, i: (b, 0, 0, 0)),
        pl.BlockSpec(ov.shape, lambda b, i: (0, 0)),
        pl.BlockSpec((1, G, T, NSA_HD), lambda b, i: (b, 0, 0, 0)),
        pl.BlockSpec((1, nch, G, NSA_VROWS, NSA_KC), lambda b, i: (b, 0, 0, 0, 0)),
        pl.BlockSpec((1, G, T, NSA_HD), lambda b, i: (b, 0, 0, 0)),
        pl.BlockSpec((1, nch, G, NSA_VROWS, NSA_KC), lambda b, i: (b, 0, 0, 0, 0)),
        pl.BlockSpec((1, 32, tq), lambda b, i: (b, 0, i)),
        pl.BlockSpec(lo.shape, lambda b, i: (0, 0)),
        pl.BlockSpec(hi.shape, lambda b, i: (0, 0)),
    ]
    n_sel = T // SLC_BLOCK
    hq = NSA_HPG * tq
    return pl.pallas_call(
        _nsa_body, out_shape=jax.ShapeDtypeStruct((B, T, NSA_HEADS * NSA_HD), BF16),
        grid=(B, T // tq), in_specs=in_specs,
        out_specs=pl.BlockSpec((1, tq, NSA_HEADS * NSA_HD), lambda b, i: (b, i, 0)),
        scratch_shapes=[pltpu.VMEM((G, n_sel, tq), F32), pltpu.VMEM((G, n_sel, tq), F32),
                        pltpu.VMEM((G, NSA_VROWS, hq), F32), pltpu.VMEM((G, NSA_HD, hq), F32),
                        pltpu.VMEM((G, NSA_HPG, NSA_KC, tq), F32)],
        compiler_params=_cparams(("parallel", "parallel")), name="nsa",
    )(qT, kc, vcT, ov, sk, svT, wk, wvT, gT, lo, hi)


def _ret_body(q_ref, kT_ref, v_ref, dec_ref, xi_ref, zeta_ref, cd_ref, gg_ref, gb_ref, o_ref, r_ref):
    C = RET_CHUNK

    @pl.when(pl.program_id(1) == 0)
    def _():
        r_ref[...] = jnp.zeros(r_ref.shape, F32)

    for bb in range(q_ref.shape[0]):
        for h in range(RET_HEADS):
            dk = slice(h * RET_DK, (h + 1) * RET_DK)
            dv = slice(h * RET_DV, (h + 1) * RET_DV)
            dec = dec_ref[h]
            xi = xi_ref[h]
            zeta = zeta_ref[h]
            cd = cd_ref[h]
            gg = gg_ref[:, dv]
            gb = gb_ref[:, dv]
            r = r_ref[bb, h]
            for n in range(RET_STEP_CHUNKS):
                rows = slice(n * C, (n + 1) * C)
                qc = q_ref[bb, rows, dk]
                kT = kT_ref[bb, n, dk, :]
                vc = v_ref[bb, rows, dv]
                s = _nn(qc, kT.astype(BF16)) * dec
                o = _nn(s.astype(BF16), vc) + _nn(qc, r.astype(BF16)) * xi
                r = r * cd + _nn((kT * zeta).astype(BF16), vc)
                mu = jnp.mean(o, axis=-1, keepdims=True)
                var = jnp.mean(jnp.square(o - mu), axis=-1, keepdims=True)
                o_ref[bb, rows, dv] = (o - mu) * lax.rsqrt(var + GN_EPS) * gg + gb
            r_ref[bb, h] = r


def _retention(rq, rkT, rv, dec, xi, zeta, cd, gn_g, gn_b):
    B, T, _ = rq.shape
    ts = RET_STEP_CHUNKS * RET_CHUNK
    full = lambda a: pl.BlockSpec(a.shape, lambda b, j: (0,) * a.ndim)
    nb = RET_STEP_BATCH if B % RET_STEP_BATCH == 0 else 1
    in_specs = [
        pl.BlockSpec((nb, ts, RET_HEADS * RET_DK), lambda b, j: (b, j, 0)),
        pl.BlockSpec((nb, RET_STEP_CHUNKS, RET_HEADS * RET_DK, RET_CHUNK), lambda b, j: (b, j, 0, 0)),
        pl.BlockSpec((nb, ts, RET_HEADS * RET_DV), lambda b, j: (b, j, 0)),
    ] + [full(a) for a in (dec, xi, zeta, cd, gn_g, gn_b)]
    return pl.pallas_call(
        _ret_body, out_shape=jax.ShapeDtypeStruct((B, T, RET_HEADS * RET_DV), F32),
        grid=(B // nb, T // ts), in_specs=in_specs,
        out_specs=pl.BlockSpec((nb, ts, RET_HEADS * RET_DV), lambda b, j: (b, j, 0)),
        scratch_shapes=[pltpu.VMEM((nb, RET_HEADS, RET_DK, RET_DV), F32)],
        compiler_params=_cparams(("parallel", "arbitrary")), name="retention",
    )(rq, rkT, rv, dec, xi, zeta, cd, gn_g, gn_b)


def _layer_norm(y, g, b):
    mu = jnp.mean(y, axis=-1, keepdims=True)
    var = jnp.mean(jnp.square(y - mu), axis=-1, keepdims=True)
    return (y - mu) * lax.rsqrt(var + LN_EPS) * g + b


def _merge_body(alpha, x_ref, oa_ref, or_ref, wrg_ref, wmg_ref, wua_ref, wur_ref, wo_ref, g1_ref, b1_ref,
                wrh_ref, wrl_ref, rb_ref, x1_ref, eid_ref, wgt_ref):
    x = x_ref[...]
    xb = x.astype(BF16)
    rgate = jax.nn.silu(_nn(xb, wrg_ref[...]))
    o_ret = (or_ref[...] * rgate).astype(BF16)
    a = _nn(oa_ref[...], wua_ref[...])
    r = _nn(o_ret, wur_ref[...])
    mg = jax.nn.sigmoid(_nn(xb, wmg_ref[...]))
    merged = mg[:, :D_MODEL] * a + mg[:, D_MODEL:] * r
    mix = _nn(merged.astype(BF16), wo_ref[...])
    x1 = _layer_norm(alpha * x + mix, g1_ref[...], b1_ref[...])
    x1_ref[...] = x1

    xh = x1.astype(BF16)
    xl = (x1 - xh.astype(F32)).astype(BF16)
    wh = wrh_ref[...]
    lg = _nt(wh, xh) + _nt(wh, xl) + _nt(wrl_ref[...], xh) + rb_ref[...]
    ne = N_EXPERTS
    gl = lg[ne:ne + MOE_GROUPS]
    ge = jnp.exp(gl - jnp.max(gl, axis=0, keepdims=True))
    pg = ge / jnp.sum(ge, axis=0, keepdims=True)
    g_prob = jnp.max(pg, axis=0, keepdims=True)
    gi = lax.broadcasted_iota(I32, pg.shape, 0)
    g_idx = jnp.min(jnp.where(pg == g_prob, gi, MOE_GROUPS), axis=0, keepdims=True)
    inner = jnp.zeros((EXPERTS_PER_GROUP, lg.shape[1]), F32)
    for gq in range(MOE_GROUPS):
        inner = inner + jnp.where(g_idx == gq, lg[8 * gq:8 * (gq + 1)], 0.0)
    ei = lax.broadcasted_iota(I32, inner.shape, 0)
    m1 = jnp.max(inner, axis=0, keepdims=True)
    i1 = jnp.min(jnp.where(inner == m1, ei, EXPERTS_PER_GROUP), axis=0, keepdims=True)
    rest = jnp.where(ei == i1, -jnp.inf, inner)
    m2 = jnp.max(rest, axis=0, keepdims=True)
    i2 = jnp.min(jnp.where(rest == m2, ei, EXPERTS_PER_GROUP), axis=0, keepdims=True)
    e2 = jnp.exp(m2 - m1)
    den = 1.0 + e2
    w1 = (1.0 / den) * g_prob
    w2 = (e2 / den) * g_prob
    zi = jnp.zeros((6, lg.shape[1]), I32)
    eid_ref[...] = jnp.concatenate([g_idx * EXPERTS_PER_GROUP + i1, g_idx * EXPERTS_PER_GROUP + i2, zi], axis=0)
    wgt_ref[...] = jnp.concatenate([w1, w2, jnp.zeros((6, lg.shape[1]), F32)], axis=0)


def _merge(alpha, x2, oa2, or2, wrg, wmg, wua, wur, wo, g1, b1, wrh, wrl, rb):
    M, D = x2.shape
    tm = MERGE_TM
    full = lambda a: pl.BlockSpec(a.shape, lambda i: (0,) * a.ndim)
    in_specs = [pl.BlockSpec((tm, D), lambda i: (i, 0)),
                pl.BlockSpec((tm, oa2.shape[1]), lambda i: (i, 0)),
                pl.BlockSpec((tm, or2.shape[1]), lambda i: (i, 0))] + [
        full(a) for a in (wrg, wmg, wua, wur, wo, g1, b1, wrh, wrl, rb)]
    out_shape = [jax.ShapeDtypeStruct((M, D), F32),
                 jax.ShapeDtypeStruct((8, M), I32),
                 jax.ShapeDtypeStruct((8, M), F32)]
    out_specs = [pl.BlockSpec((tm, D), lambda i: (i, 0)),
                 pl.BlockSpec((8, tm), lambda i: (0, i)),
                 pl.BlockSpec((8, tm), lambda i: (0, i))]
    return pl.pallas_call(
        functools.partial(_merge_body, alpha), out_shape=out_shape, grid=(M // tm,),
        in_specs=in_specs, out_specs=out_specs,
        compiler_params=_cparams(("parallel",)), name="merge",
    )(x2, oa2, or2, wrg, wmg, wua, wur, wo, g1, b1, wrh, wrl, rb)


def _rank_body(eid_ref, tri_ref, low_ref, pos_ref, cnt_ref, carry_ref, off_ref):
    p = pl.program_id(0)
    j = pl.program_id(1)
    tb = tri_ref.shape[0]

    @pl.when((p == 0) & (j == 0))
    def _():
        carry_ref[...] = jnp.zeros(carry_ref.shape, F32)

    @pl.when((p == 1) & (j == 0))
    def _():
        tiles = jnp.floor((carry_ref[...] + (EXPERT_TM - 1)) * (1.0 / EXPERT_TM))
        off_ref[...] = _nn(low_ref[...], tiles.astype(BF16)) * EXPERT_TM
        carry_ref[...] = jnp.zeros(carry_ref.shape, F32)

    for s in range(eid_ref.shape[1] // tb):
        cols = slice(s * tb, (s + 1) * tb)
        e = eid_ref[:, cols]
        rows = lax.broadcasted_iota(I32, (N_EXPERTS, tb), 0)
        hit = rows == e
        per_expert = jnp.sum(jnp.where(hit, 1.0, 0.0), axis=1, keepdims=True)

        @pl.when(p == 0)
        def _():
            pos_ref[:, cols] = jnp.zeros((1, tb), I32)
            carry_ref[...] = carry_ref[...] + per_expert
            cnt_ref[...] = carry_ref[...]

        @pl.when(p == 1)
        def _():
            onehot = jnp.where(hit, 1.0, 0.0).astype(BF16)
            incl = _nn(onehot, tri_ref[...])
            base = carry_ref[:, 0:1] + off_ref[:, 0:1] - 1.0
            pos_ref[:, cols] = jnp.sum(jnp.where(hit, incl + base, 0.0), axis=0, keepdims=True).astype(I32)
            carry_ref[...] = carry_ref[...] + per_expert


def _rank(eid_flat, tri, low):
    n = eid_flat.shape[1]
    tb = RANK_STEP_BLOCKS * RANK_TB
    return pl.pallas_call(
        _rank_body,
        out_shape=[jax.ShapeDtypeStruct((1, n), I32), jax.ShapeDtypeStruct((N_EXPERTS, 128), F32)],
        grid=(2, n // tb),
        in_specs=[pl.BlockSpec((1, tb), lambda p, j: (0, j)), pl.BlockSpec(tri.shape, lambda p, j: (0, 0)),
                  pl.BlockSpec(low.shape, lambda p, j: (0, 0))],
        out_specs=[pl.BlockSpec((1, tb), lambda p, j: (0, j * p)),
                   pl.BlockSpec((N_EXPERTS, 128), lambda p, j: (0, 0))],
        scratch_shapes=[pltpu.VMEM((N_EXPERTS, 128), F32), pltpu.VMEM((N_EXPERTS, 128), F32)],
        compiler_params=_cparams(("arbitrary", "arbitrary")), name="rank",
    )(eid_flat, tri, low)


def _dispatch_body(pos_ref, ztile_ref, x_ref, xs_ref, zbuf_ref, sem, zsem):
    tm = DISPATCH_TM
    m_tok = pos_ref.shape[0] // 2
    base = pl.program_id(0) * tm

    @pl.when(pl.program_id(0) == 0)
    def _():
        zbuf_ref[...] = jnp.zeros(zbuf_ref.shape, F32)

        def zero_copy(e):
            z = pl.multiple_of(jnp.maximum(ztile_ref[e], 0), EXPERT_TM)
            return pltpu.make_async_copy(zbuf_ref, xs_ref.at[pl.ds(z, EXPERT_TM), :], zsem)

        for e in range(N_EXPERTS):
            @pl.when(ztile_ref[e] >= 0)
            def _():
                zero_copy(e).start()
        for e in range(N_EXPERTS):
            @pl.when(ztile_ref[e] >= 0)
            def _():
                zero_copy(e).wait()

        def tail_copy(t):
            return pltpu.make_async_copy(
                zbuf_ref, xs_ref.at[pl.ds(pl.multiple_of(t * EXPERT_TM, EXPERT_TM), EXPERT_TM), :], zsem)

        n_tiles = xs_ref.shape[0] // EXPERT_TM
        lax.fori_loop(ztile_ref[N_EXPERTS], n_tiles, lambda t, c: (tail_copy(t).start(), c)[1], 0)
        lax.fori_loop(ztile_ref[N_EXPERTS], n_tiles, lambda t, c: (tail_copy(t).wait(), c)[1], 0)

    def row_copy(r, p):
        return pltpu.make_async_copy(x_ref.at[pl.ds(r, 1), :], xs_ref.at[pl.ds(p, 1), :], sem)

    for r in range(tm):
        row_copy(r, pos_ref[base + r]).start()
        row_copy(r, pos_ref[m_tok + base + r]).start(priority=1)

    def drain(r, carry):
        row_copy(0, 0).wait()
        row_copy(0, 0).wait()
        return carry

    lax.fori_loop(0, tm, drain, 0, unroll=16)


def _dispatch(pos, ztile, x1, n_pad):
    M, D = x1.shape
    tm = DISPATCH_TM
    grid_spec = pltpu.PrefetchScalarGridSpec(
        num_scalar_prefetch=2, grid=(M // tm,),
        in_specs=[pl.BlockSpec((tm, D), lambda i, pos, zt: (i, 0))],
        out_specs=pl.BlockSpec(memory_space=pl.ANY),
        scratch_shapes=[pltpu.VMEM((EXPERT_TM, D), F32), pltpu.SemaphoreType.DMA, pltpu.SemaphoreType.DMA],
    )
    return pl.pallas_call(
        _dispatch_body, out_shape=jax.ShapeDtypeStruct((n_pad, D), F32), grid_spec=grid_spec,
        compiler_params=_cparams(("arbitrary",), has_side_effects=True, disable_bounds_checks=True),
        name="dispatch",
    )(pos, ztile, x1)


def _experts_body(te_ref, nu_ref, seg_ref, nxt_ref, xs_ref, wg_hbm, wu_hbm, wd_hbm, ys_ref,
                  wgb_ref, wub_ref, wdb_ref, sg_ref, su_ref, sd_ref, sem):
    tm = EXPERT_TM
    k = pl.program_id(0)
    last = nu_ref[0] - 1

    def fetch(e, slot):
        return (pltpu.make_async_copy(wg_hbm.at[e], sg_ref.at[slot], sem.at[0, slot]),
                pltpu.make_async_copy(wu_hbm.at[e], su_ref.at[slot], sem.at[1, slot]),
                pltpu.make_async_copy(wd_hbm.at[e], sd_ref.at[slot], sem.at[2, slot]))

    @pl.when(k == 0)
    def _():
        for c in fetch(te_ref[0], 0):
            c.start()

    for s in range(EXPERT_STEP_TILES):
        t = EXPERT_STEP_TILES * k + s
        rows = slice(s * tm, (s + 1) * tm)

        @pl.when((t <= last) & (seg_ref[t] >= 0))
        def _():
            slot = seg_ref[t]
            for c in fetch(te_ref[t], slot):
                c.wait()
            wgb_ref[...] = sg_ref[slot].astype(BF16)
            wub_ref[...] = su_ref[slot].astype(BF16)
            wdb_ref[...] = sd_ref[slot].astype(BF16)

            @pl.when(nxt_ref[t] >= 0)
            def _():
                for c in fetch(nxt_ref[t], 1 - slot):
                    c.start()

        @pl.when(t <= last)
        def _():
            xb = xs_ref[rows, :].astype(BF16)
            hg = _nn(xb, wgb_ref[...])
            hu = _nn(xb, wub_ref[...])
            h = (jax.nn.silu(hg) * hu).astype(BF16)
            ys_ref[rows, :] = _nn(h, wdb_ref[...])

        @pl.when(t > last)
        def _():
            ys_ref[rows, :] = jnp.zeros((tm, ys_ref.shape[1]), F32)


def _experts(tile_expert, n_used, seg_slot, next_expert, xs, wg, wu, wd):
    npad, D = xs.shape
    tm = EXPERT_STEP_TILES * EXPERT_TM
    grid_spec = pltpu.PrefetchScalarGridSpec(
        num_scalar_prefetch=4, grid=(npad // tm,),
        in_specs=[pl.BlockSpec((tm, D), lambda i, te, nu, sg, nx:
                               (jnp.minimum(i, (nu[0] - 1) // EXPERT_STEP_TILES), 0)),
                  pl.BlockSpec(memory_space=pl.ANY), pl.BlockSpec(memory_space=pl.ANY),
                  pl.BlockSpec(memory_space=pl.ANY)],
        out_specs=pl.BlockSpec((tm, D), lambda i, te, nu, sg, nx: (i, 0)),
        scratch_shapes=[pltpu.VMEM((D, D_FF), BF16), pltpu.VMEM((D, D_FF), BF16), pltpu.VMEM((D_FF, D), BF16),
                        pltpu.VMEM((2, D, D_FF), F32), pltpu.VMEM((2, D, D_FF), F32),
                        pltpu.VMEM((2, D_FF, D), F32), pltpu.SemaphoreType.DMA((3, 2))],
    )
    return pl.pallas_call(
        _experts_body, out_shape=jax.ShapeDtypeStruct((npad, D), F32), grid_spec=grid_spec,
        compiler_params=_cparams(("arbitrary",)), name="experts",
    )(tile_expert, n_used, seg_slot, next_expert, xs, wg, wu, wd)


def _combine_body(alpha, pos_ref, x1_ref, w_ref, eye_ref, g2_ref, b2_ref, ys_ref, o_ref,
                  buf_a, buf_b, sem_a, sem_b):
    tm = COMBINE_TM
    m_tok = pos_ref.shape[0] // 2
    k = pl.program_id(0)
    n = pl.num_programs(0)

    def row_copies(buf, sem, r, base):
        return (pltpu.make_async_copy(ys_ref.at[pl.ds(pos_ref[base + r], 1), :], buf.at[0, pl.ds(r, 1), :], sem),
                pltpu.make_async_copy(ys_ref.at[pl.ds(pos_ref[m_tok + base + r], 1), :],
                                      buf.at[1, pl.ds(r, 1), :], sem))

    def issue_inline(buf, sem, tile):
        for r in range(tm):
            for queue, c in enumerate(row_copies(buf, sem, r, tile * tm)):
                c.start(priority=queue)

    def drain(buf, sem):
        def body(r, carry):
            for c in row_copies(buf, sem, 0, 0):
                c.wait()
            return carry
        lax.fori_loop(0, tm, body, 0, unroll=16)

    def compute(buf, s):
        rows = slice(s * tm, (s + 1) * tm)
        w = w_ref[:, rows]
        eye = eye_ref[...]
        w_a = w.astype(BF16)
        w_b = (w - w_a.astype(F32)).astype(BF16)
        w_c = (w - w_a.astype(F32) - w_b.astype(F32)).astype(BF16)
        wcol = _nt(eye, w_a) + _nt(eye, w_b) + _nt(eye, w_c)
        moe = buf[0] * wcol[:, 0:1] + buf[1] * wcol[:, 1:2]
        o_ref[rows, :] = _layer_norm(alpha * x1_ref[rows, :] + moe, g2_ref[...], b2_ref[...])

    @pl.when(k == 0)
    def _():
        def body(r, carry):
            for queue, c in enumerate(row_copies(buf_a, sem_a, r, 0)):
                c.start(priority=queue)
            return carry
        lax.fori_loop(0, tm, body, 0, unroll=16)

    drain(buf_a, sem_a)
    issue_inline(buf_b, sem_b, 2 * k + 1)
    compute(buf_a, 0)
    drain(buf_b, sem_b)
    issue_inline(buf_a, sem_a, jnp.minimum(2 * k + 2, 2 * n - 2))
    compute(buf_b, 1)

    @pl.when(k == n - 1)
    def _():
        drain(buf_a, sem_a)


def _combine(alpha, pos, x1, wgt, eye, g2, b2, ys):
    M, D = x1.shape
    tm = COMBINE_TM
    grid_spec = pltpu.PrefetchScalarGridSpec(
        num_scalar_prefetch=1, grid=(M // (2 * tm),),
        in_specs=[pl.BlockSpec((2 * tm, D), lambda i, pos: (i, 0)),
                  pl.BlockSpec((8, 2 * tm), lambda i, pos: (0, i)),
                  pl.BlockSpec((tm, tm), lambda i, pos: (0, 0)),
                  pl.BlockSpec((1, D), lambda i, pos: (0, 0)),
                  pl.BlockSpec((1, D), lambda i, pos: (0, 0)),
                  pl.BlockSpec(memory_space=pl.ANY)],
        out_specs=pl.BlockSpec((2 * tm, D), lambda i, pos: (i, 0)),
        scratch_shapes=[pltpu.VMEM((2, tm, D), F32), pltpu.VMEM((2, tm, D), F32),
                        pltpu.SemaphoreType.DMA, pltpu.SemaphoreType.DMA],
    )
    return pl.pallas_call(
        functools.partial(_combine_body, alpha), out_shape=jax.ShapeDtypeStruct((M, D), F32),
        grid_spec=grid_spec, compiler_params=_cparams(("arbitrary",), disable_bounds_checks=True),
        name="combine",
    )(pos, x1, wgt, eye, g2, b2, ys)


def _rope_angles(pos, dim):
    half = dim // 2
    inv_freq = ROPE_THETA ** (-np.arange(half, dtype=np.float64) * 2.0 / dim)
    return pos.astype(np.float64)[:, None] * inv_freq[None, :]


def _tables(T):
    f32 = lambda a: jnp.asarray(np.ascontiguousarray(a), dtype=F32)
    ang = _rope_angles(np.arange(T), NSA_HD)
    cos, sin = np.cos(ang), np.sin(ang)
    c64 = np.tile(cos, (1, 4))
    s64 = np.tile(np.concatenate([-sin, sin], axis=1), (1, 2))
    ang2 = _rope_angles(np.arange(T), RET_DK)
    cos2, sin2 = np.cos(ang2), np.sin(ang2)
    c128 = np.tile(cos2, (1, 2))
    s128 = np.concatenate([-sin2, sin2], axis=1)
    n_rows = T // CMP_STRIDE
    angc = _rope_angles(np.arange(n_rows) * CMP_STRIDE + CMP_BLOCK - 1, NSA_HD)
    cc = np.tile(np.cos(angc), (1, 2))
    sc = np.tile(np.sin(angc), (1, 2))
    return tuple(f32(a) for a in (c64, s64, c128, s128, cos.T, sin.T, cos2.T, sin2.T)), f32(cc), f32(sc)


def _overlap_matrix(T):
    n_rows = T // CMP_STRIDE
    n_sel = T // SLC_BLOCK
    cmp_start = np.arange(n_rows) * CMP_STRIDE
    sel_start = np.arange(n_sel) * SLC_BLOCK
    ov = np.clip(np.minimum(cmp_start[None, :] + CMP_BLOCK, sel_start[:, None] + SLC_BLOCK)
                 - np.maximum(cmp_start[None, :], sel_start[:, None]), 0, None)
    return jnp.asarray(ov.astype(np.float32) / CMP_STRIDE, dtype=BF16)


def _retention_tables():
    C = RET_CHUNK
    f32 = lambda a: jnp.asarray(np.ascontiguousarray(a), dtype=F32)
    gamma = 1.0 - 2.0 ** (-5.0 - np.arange(RET_HEADS, dtype=np.float64))
    log_g = np.log(gamma)
    i = np.arange(C, dtype=np.float64)
    diff = i[:, None] - i[None, :]
    dec = np.where(diff >= 0, np.exp(np.maximum(diff, 0.0) * log_g[:, None, None]), 0.0)
    xi = np.exp((i + 1.0) * log_g[:, None])
    zeta = np.exp((C - 1.0 - i) * log_g[:, None])
    cd = np.exp(C * log_g)
    xi_b = np.broadcast_to(xi[:, :, None], (RET_HEADS, C, RET_DV))
    cd_b = np.broadcast_to(cd[:, None, None], (RET_HEADS, 1, RET_DV))
    return f32(dec), f32(xi_b), f32(zeta[:, None, :]), f32(cd_b)


def _rot_half_cols(w):
    half = w.shape[-1] // 2
    return jnp.concatenate([-w[..., half:], w[..., :half]], axis=-1)


def kernel(x, w_in, cmp_pos_k, cmp_k_w1, cmp_k_b1, cmp_k_w2, cmp_k_b2, cmp_pos_v, cmp_v_w1, cmp_v_b1, cmp_v_w2, cmp_v_b2, ret_gn_g, ret_gn_b, w_up_attn, w_up_ret, w_out, ln1_g, ln1_b, router_group_w, router_group_b, router_inner_w, router_inner_b, expert_w_gate, expert_w_up, expert_w_down, ln2_g, ln2_b):
    B, T, D = x.shape
    M = B * T
    depth = w_in.shape[0]
    alpha = (2.0 * depth) ** 0.25
    tabs, cc, sc = _tables(T)
    ov = _overlap_matrix(T)
    dec, xi_b, zeta, cd_b = _retention_tables()
    tri = jnp.asarray(np.triu(np.ones((RANK_TB, RANK_TB), np.float32)), dtype=BF16)
    low = jnp.asarray(np.tril(np.ones((N_EXPERTS, N_EXPERTS), np.float32), k=-1), dtype=BF16)
    eye = jnp.asarray(np.eye(COMBINE_TM, dtype=np.float32), dtype=BF16)
    n_pad = 2 * M + N_EXPERTS * EXPERT_TM
    n_tiles = n_pad // EXPERT_TM

    for l in range(depth):
        w = w_in[l]
        col = lambda n: w[:, _OFF[n][0]:_OFF[n][1]]
        wn = jnp.concatenate([col("cmp_k"), col("cmp_v"), col("slc_k"), col("win_k"), col("ret_q"), col("ret_v")],
                             axis=1).astype(BF16)
        wt = jnp.concatenate([col("nsa_q"), col("slc_v"), col("win_v"), col("ret_k"), col("nsa_gate"),
                              jnp.zeros((D, 8), F32)], axis=1).T.astype(BF16)
        qT, ckv, sk, wk, svT, wvT, gT, rq, rkT, rv = _proj(x, wn, wt, tabs)

        kc, vcT = _compress(
            ckv, cmp_pos_k[l].reshape(2, -1), cmp_pos_v[l].reshape(2, -1),
            cmp_k_w1[l], cmp_k_b1[l][None, :], cmp_k_w2[l], _rot_half_cols(cmp_k_w2[l]),
            cmp_k_b2[l][None, :], _rot_half_cols(cmp_k_b2[l])[None, :], cc, sc,
            cmp_v_w1[l], cmp_v_b1[l][None, :], cmp_v_w2[l].T, cmp_v_b2[l][:, None])
        o_attn = _nsa(qT, kc, vcT, ov, sk, svT, wk, wvT, gT)
        o_ret = _retention(rq, rkT, rv, dec, xi_b, zeta, cd_b, ret_gn_g[l][None, :], ret_gn_b[l][None, :])

        wr = jnp.concatenate([router_inner_w[l].transpose(0, 2, 1).reshape(N_EXPERTS, D),
                              router_group_w[l].T, jnp.zeros((4, D), F32)], axis=0)
        wrh = wr.astype(BF16)
        wrl = (wr - wrh.astype(F32)).astype(BF16)
        rb = jnp.concatenate([router_inner_b[l].reshape(-1), router_group_b[l], jnp.zeros((4,), F32)])[:, None]
        x1, eid, wgt = _merge(
            alpha, x.reshape(M, D), o_attn.reshape(M, -1), o_ret.reshape(M, -1),
            col("ret_gate").astype(BF16), col("merge_gate").astype(BF16), w_up_attn[l].astype(BF16),
            w_up_ret[l].astype(BF16), w_out[l].astype(BF16), ln1_g[l][None, :], ln1_b[l][None, :], wrh, wrl, rb)

        eid_flat = eid[:2].reshape(1, 2 * M)
        pos, cnt = _rank(eid_flat, tri, low)
        pos = pos[0]
        counts = cnt[:, 0].astype(I32)
        tiles_per = (counts + EXPERT_TM - 1) // EXPERT_TM
        tile_end = jnp.cumsum(tiles_per)
        tile_ids = jnp.arange(n_tiles, dtype=I32)
        tile_expert = jnp.minimum(jnp.sum((tile_end[None, :] <= tile_ids[:, None]).astype(I32), axis=1),
                                  N_EXPERTS - 1).astype(I32)
        n_used = tile_end[-1:].astype(I32)

        ztile = jnp.concatenate([jnp.where(tiles_per > 0, (tile_end - 1) * EXPERT_TM, -1), tile_end[-1:]]).astype(I32)
        xs = _dispatch(pos, ztile, x1, n_pad)
        prev_expert = jnp.concatenate([jnp.full((1,), -1, I32), tile_expert[:-1]])
        is_first = (tile_ids < n_used[0]) & (tile_expert != prev_expert)
        seg_slot = jnp.where(is_first, (jnp.cumsum(is_first.astype(I32)) - 1) % 2, -1).astype(I32)
        own = tile_expert[:, None] == jnp.arange(N_EXPERTS, dtype=I32)[None, :]
        seg_end = jnp.sum(jnp.where(own, tile_end[None, :], 0), axis=1)
        expert_at_end = jnp.minimum(jnp.sum((tile_end[None, :] <= seg_end[:, None]).astype(I32), axis=1),
                                    N_EXPERTS - 1)
        next_expert = jnp.where(is_first & (seg_end < n_used[0]), expert_at_end, -1).astype(I32)
        ys = _experts(tile_expert, n_used, seg_slot, next_expert, xs,
                      expert_w_gate[l], expert_w_up[l], expert_w_down[l])
        x = _combine(alpha, pos, x1, wgt, eye, ln2_g[l][None, :], ln2_b[l][None, :], ys).reshape(B, T, D)
    return x
```
